```python
import jax, jax.numpy as jnp
from jax import lax
import numpy as np

D_MODEL = 1024
BATCH = 8
SEQ = 2048
DEPTH = 1
DEC_BATCH = 8
DEC_SEQ = 64
PAST_LEN = 4096

CHUNK = 64
D_MIX = D_MODEL
D_CONV = D_MIX // 2
D_ATT = D_MIX - D_CONV
HEAD_DIM = 64
N_HEADS = D_ATT // HEAD_DIM
CONV_WIDTH = 31
LEFT_CHUNKS = 8
BAND = (LEFT_CHUNKS + 1) * CHUNK
MAX_REL = 128
D_FF = 4 * D_MODEL
D_PLE = 256
D_IN = 2 * D_CONV + 3 * D_ATT
EPS = 1e-6
NEG = -1e30

kernel_name = "hymba_conv_chunkband_attn_stream_step"


def rms_norm(x, g):
    xf = x.astype(jnp.float32)
    y = xf * lax.rsqrt(jnp.mean(xf * xf, axis=-1, keepdims=True) + EPS)
    return (y * g.astype(jnp.float32)).astype(x.dtype)


def layer_norm(x, g, b):
    xf = x.astype(jnp.float32)
    mu = jnp.mean(xf, axis=-1, keepdims=True)
    var = jnp.mean(jnp.square(xf - mu), axis=-1, keepdims=True)
    y = (xf - mu) * lax.rsqrt(var + EPS)
    return (y * g.astype(jnp.float32) + b.astype(jnp.float32)).astype(x.dtype)


def in_proj(x, g_mix, w_in, g_q, g_k):
    z = rms_norm(x, g_mix) @ w_in
    b, t, _ = z.shape
    a = z[..., :D_CONV]
    gate = z[..., D_CONV:2 * D_CONV]
    u = a * jax.nn.sigmoid(gate)
    o = 2 * D_CONV
    q = z[..., o:o + D_ATT].reshape(b, t, N_HEADS, HEAD_DIM)
    k = z[..., o + D_ATT:o + 2 * D_ATT].reshape(b, t, N_HEADS, HEAD_DIM)
    v = z[..., o + 2 * D_ATT:].reshape(b, t, N_HEADS, HEAD_DIM)
    return u, rms_norm(q, g_q), rms_norm(k, g_k), v


def conv_branch(u_full, w_dw, b_dw, g_ln, b_ln):
    y = lax.conv_general_dilated(
        u_full, w_dw[:, None, :], window_strides=(1,), padding='VALID',
        dimension_numbers=('NWC', 'WIO', 'NWC'), feature_group_count=D_CONV)
    y = layer_norm(y + b_dw, g_ln, b_ln)
    return jax.nn.silu(y)


def bias_lookup(rel_bias, rel):
    idx = jnp.clip(rel, -MAX_REL, MAX_REL) + MAX_REL
    return rel_bias[:, idx].astype(jnp.float32)


def chunk_attn_prompt(q, k, v, rel_bias):
    b, t, h, dh = q.shape
    nc = t // CHUNK
    qc = q.reshape(b, nc, CHUNK, h, dh)
    pad = ((0, 0), (LEFT_CHUNKS, 0), (0, 0), (0, 0), (0, 0))
    kp = jnp.pad(k.reshape(b, nc, CHUNK, h, dh), pad)
    vp = jnp.pad(v.reshape(b, nc, CHUNK, h, dh), pad)
    kb = jnp.concatenate([kp[:, o:o + nc] for o in range(LEFT_CHUNKS + 1)], axis=2)
    vb = jnp.concatenate([vp[:, o:o + nc] for o in range(LEFT_CHUNKS + 1)], axis=2)
    s = jnp.einsum('bnqhd,bnkhd->bnhqk', qc, kb).astype(jnp.float32) * (HEAD_DIM ** -0.5)
    qi = jnp.arange(CHUNK)
    kk = jnp.arange(BAND)
    rel = kk[None, :] - LEFT_CHUNKS * CHUNK - qi[:, None]
    s = s + bias_lookup(rel_bias, rel)[None, None]
    kpos = (jnp.arange(nc)[:, None] - LEFT_CHUNKS) * CHUNK + kk[None, :]
    s = jnp.where((kpos >= 0)[None, :, None, None, :], s, NEG)
    p = jax.nn.softmax(s, axis=-1).astype(v.dtype)
    o = jnp.einsum('bnhqk,bnkhd->bnqhd', p, vb)
    return o.reshape(b, t, h * dh)


def chunk_attn_sample(q, k_new, v_new, k_cache, v_cache, rel_bias):
    b, s_len, h, dh = q.shape
    l = k_cache.shape[1]
    kk = jnp.concatenate([k_cache, k_new], axis=1)
    vv = jnp.concatenate([v_cache, v_new], axis=1)
    s = jnp.einsum('bqhd,bkhd->bhqk', q, kk).astype(jnp.float32) * (HEAD_DIM ** -0.5)
    kpos = jnp.concatenate([jnp.arange(l) - l, jnp.arange(s_len)])
    rel = kpos[None, :] - jnp.arange(s_len)[:, None]
    s = s + bias_lookup(rel_bias, rel)[None]
    p = jax.nn.softmax(s, axis=-1).astype(vv.dtype)
    o = jnp.einsum('bhqk,bkhd->bqhd', p, vv)
    return o.reshape(b, s_len, h * dh)


def merge_ffn_ple(x, p, c_out, a_out, g_conv_out, g_att_out, w_out, g_ffn, w_ff1, w_ff2,
                  g_ple, w_gate, w_ple):
    merged = jnp.concatenate([rms_norm(c_out, g_conv_out), rms_norm(a_out, g_att_out)], axis=-1)
    x = x + merged @ w_out
    h = rms_norm(x, g_ffn)
    x = x + jnp.square(jax.nn.relu(h @ w_ff1)) @ w_ff2
    gate = jax.nn.sigmoid(rms_norm(x, g_ple) @ w_gate)
    return x + (p @ w_ple) * gate


def setup_inputs(seed: int = 0) -> dict:
    key = jax.random.key(seed)
    ks = jax.random.split(key, 32)
    f32 = jnp.float32
    att_len = min(LEFT_CHUNKS * CHUNK, PAST_LEN)

    def nrm(k, shape, scale):
        return jax.random.normal(k, shape, f32) * scale

    def gain(k, shape):
        return 1.0 + 0.01 * jax.random.normal(k, shape, f32)

    return {
        "x_prompt": nrm(ks[0], (BATCH, SEQ, D_MODEL), 1.0),
        "x_sample": nrm(ks[1], (DEC_BATCH, DEC_SEQ, D_MODEL), 1.0),
        "p_prompt": nrm(ks[2], (DEPTH, BATCH, SEQ, D_PLE), 1.0),
        "p_sample": nrm(ks[3], (DEPTH, DEC_BATCH, DEC_SEQ, D_PLE), 1.0),
        "cache_att_k": nrm(ks[4], (DEPTH, DEC_BATCH, att_len, N_HEADS, HEAD_DIM), 1.0),
        "cache_att_v": nrm(ks[5], (DEPTH, DEC_BATCH, att_len, N_HEADS, HEAD_DIM), 1.0),
        "state_conv": nrm(ks[6], (DEPTH, DEC_BATCH, CONV_WIDTH - 1, D_CONV), 0.5),
        "g_mix": gain(ks[7], (DEPTH, D_MODEL)),
        "w_in": nrm(ks[8], (DEPTH, D_MODEL, D_IN), D_MODEL ** -0.5),
        "w_dw": nrm(ks[9], (DEPTH, CONV_WIDTH, D_CONV), CONV_WIDTH ** -0.5),
        "b_dw": nrm(ks[10], (DEPTH, D_CONV), 0.01),
        "g_conv_ln": gain(ks[11], (DEPTH, D_CONV)),
        "b_conv_ln": nrm(ks[12], (DEPTH, D_CONV), 0.01),
        "g_q": gain(ks[13], (DEPTH, HEAD_DIM)),
        "g_k": gain(ks[14], (DEPTH, HEAD_DIM)),
        "rel_bias": nrm(ks[15], (DEPTH, N_HEADS, 2 * MAX_REL + 1), 0.1),
        "g_conv_out": gain(ks[16], (DEPTH, D_CONV)),
        "g_att_out": gain(ks[17], (DEPTH, D_ATT)),
        "w_out": nrm(ks[18], (DEPTH, D_MIX, D_MODEL), D_MIX ** -0.5),
        "g_ffn": gain(ks[19], (DEPTH, D_MODEL)),
        "w_ff1": nrm(ks[20], (DEPTH, D_MODEL, D_FF), D_MODEL ** -0.5),
        "w_ff2": nrm(ks[21], (DEPTH, D_FF, D_MODEL), D_FF ** -0.5),
        "g_ple": gain(ks[22], (DEPTH, D_MODEL)),
        "w_gate": nrm(ks[23], (DEPTH, D_MODEL, D_MODEL), D_MODEL ** -0.5),
        "w_ple": nrm(ks[24], (DEPTH, D_PLE, D_MODEL), D_PLE ** -0.5),
    }


def reference(x_prompt, x_sample, p_prompt, p_sample, cache_att_k, cache_att_v, state_conv,
              g_mix, w_in, w_dw, b_dw, g_conv_ln, b_conv_ln, g_q, g_k, rel_bias,
              g_conv_out, g_att_out, w_out, g_ffn, w_ff1, w_ff2, g_ple, w_gate, w_ple):
    xp, xs = x_prompt, x_sample
    kp_l, vp_l, cp_l, ks_l, vs_l, cs_l = [], [], [], [], [], []
    for i in range(DEPTH):
        u_p, q_p, k_p, v_p = in_proj(xp, g_mix[i], w_in[i], g_q[i], g_k[i])
        c_p = conv_branch(jnp.pad(u_p, ((0, 0), (CONV_WIDTH - 1, 0), (0, 0))),
                          w_dw[i], b_dw[i], g_conv_ln[i], b_conv_ln[i])
        a_p = chunk_attn_prompt(q_p, k_p, v_p, rel_bias[i])
        keep = min(LEFT_CHUNKS * CHUNK, xp.shape[1])
        kp_l.append(k_p[:, -keep:])
        vp_l.append(v_p[:, -keep:])
        cp_l.append(u_p[:, -(CONV_WIDTH - 1):])
        xp = merge_ffn_ple(xp, p_prompt[i], c_p, a_p, g_conv_out[i], g_att_out[i], w_out[i],
                           g_ffn[i], w_ff1[i], w_ff2[i], g_ple[i], w_gate[i], w_ple[i])
        u_s, q_s, k_s, v_s = in_proj(xs, g_mix[i], w_in[i], g_q[i], g_k[i])
        conv_in = jnp.concatenate([state_conv[i], u_s], axis=1)
        c_s = conv_branch(conv_in, w_dw[i], b_dw[i], g_conv_ln[i], b_conv_ln[i])
        a_s = chunk_attn_sample(q_s, k_s, v_s, cache_att_k[i], cache_att_v[i], rel_bias[i])
        ks_l.append(k_s)
        vs_l.append(v_s)
        cs_l.append(conv_in[:, -(CONV_WIDTH - 1):])
        xs = merge_ffn_ple(xs, p_sample[i], c_s, a_s, g_conv_out[i], g_att_out[i], w_out[i],
                           g_ffn[i], w_ff1[i], w_ff2[i], g_ple[i], w_gate[i], w_ple[i])
    new_k_prompt = jnp.stack(kp_l)
    new_v_prompt = jnp.stack(vp_l)
    new_conv_prompt = jnp.stack(cp_l)
    new_k_sample = jnp.stack(ks_l)
    new_v_sample = jnp.stack(vs_l)
    new_conv_sample = jnp.stack(cs_l)
    return (xp, xs, new_k_prompt, new_v_prompt, new_conv_prompt, new_k_sample, new_v_sample, new_conv_sample)
```

```python
import functools

import jax
import jax.numpy as jnp
from jax import lax
from jax.experimental import pallas as pl
from jax.experimental.pallas import tpu as pltpu

F32 = jnp.float32
BF16 = jnp.bfloat16

CHUNK = 64
LEFT_CHUNKS = 8
HEAD_DIM = 64
CONV_WIDTH = 31
MAX_REL = 128
EPS = 1e-6
NEG = -1e30

LANES = 128
PAIR = 2 * CHUNK
WIN = (LEFT_CHUNKS + 2) * CHUNK
WIN_SLABS = WIN // LANES
BIAS_SLABS = 2 * WIN_SLABS - 1
HALO = 32
VMEM_LIMIT = 56 * 1024 * 1024


def _rms(x, g):
    return x * lax.rsqrt(jnp.mean(x * x, axis=-1, keepdims=True) + EPS) * g


def _const_spec(shape):
    zeros = (0,) * len(shape)
    return pl.BlockSpec(shape, lambda *_: zeros, pipeline_mode=pl.Buffered(1))


def _bias_kernel(rb_ref, o_ref):
    n_heads = rb_ref.shape[0]
    rb = rb_ref[...]
    lane = lax.broadcasted_iota(jnp.int32, (n_heads, LANES), 1)
    b0 = jnp.broadcast_to(rb[:, 0:1], (n_heads, LANES))
    t4 = jnp.where(lane < CHUNK, rb[:, LANES:2 * LANES], b0)
    t = jnp.concatenate([b0, b0, b0, rb[:, 0:LANES], t4, b0], axis=1)
    r_i = lax.broadcasted_iota(jnp.int32, (PAIR, LANES), 0)
    c_i = lax.broadcasted_iota(jnp.int32, (PAIR, LANES), 1)
    for h in range(n_heads):
        row = jnp.broadcast_to(t[h:h + 1, :], (PAIR, t.shape[1]))
        toe = pltpu.roll(row, 0, 1, stride=1, stride_axis=0)
        for c in range(WIN_SLABS):
            jj = c_i + c * LANES
            valid = ((r_i < CHUNK) & (jj < WIN - CHUNK)) | ((r_i >= CHUNK) & (jj >= CHUNK))
            o_ref[h, c] = jnp.where(valid, toe[:, c * LANES:(c + 1) * LANES], NEG)
        for c in range(WIN_SLABS, BIAS_SLABS):
            o_ref[h, c] = jnp.full((PAIR, LANES), NEG, F32)


def _bias_tiles(rel_bias):
    n_heads = rel_bias.shape[0]
    return pl.pallas_call(
        _bias_kernel,
        out_shape=jax.ShapeDtypeStruct((n_heads, BIAS_SLABS, PAIR, LANES), F32),
        name="bias_tiles",
    )(rel_bias)


def _inproj_kernel(x_ref, gmix_ref, win_ref, gq_ref, gk_ref, *outs, d_conv, d_att, tail0, with_kt):
    if with_kt:
        u_ref, q_ref, kt_ref, vb_ref, k32_ref, v32_ref = outs
    else:
        u_ref, q_ref, k32_ref, v32_ref = outs
    tm = x_ref.shape[1]
    h = _rms(x_ref[0], gmix_ref[...]).astype(BF16)

    def proj(lo, width):
        return jnp.dot(h, win_ref[:, lo:lo + width], preferred_element_type=F32)

    u_ref[0] = proj(0, d_conv) * jax.nn.sigmoid(proj(d_conv, d_conv))

    r_i = lax.shift_right_logical(lax.broadcasted_iota(jnp.int32, (d_att, d_att), 0), 6)
    c_i = lax.shift_right_logical(lax.broadcasted_iota(jnp.int32, (d_att, d_att), 1), 6)
    avg = jnp.where(r_i == c_i, 1.0 / HEAD_DIM, 0.0).astype(BF16)

    def head_rms(z, g):
        ms = jnp.dot((z * z).astype(BF16), avg, preferred_element_type=F32)
        return z * lax.rsqrt(ms + EPS) * g

    o = 2 * d_conv
    qn = head_rms(proj(o, d_att), gq_ref[...])
    q_ref[0] = (qn * (HEAD_DIM ** -0.5)).astype(BF16)
    kn = head_rms(proj(o + d_att, d_att), gk_ref[...])
    v = proj(o + 2 * d_att, d_att)
    if with_kt:
        for s in range(tm // LANES):
            kt_ref[0, s] = kn[s * LANES:(s + 1) * LANES, :].T.astype(BF16)
        vb_ref[0] = v.astype(BF16)

    @pl.when(pl.program_id(1) >= tail0)
    def _():
        k32_ref[0] = kn
        v32_ref[0] = v


def _in_proj(x, g_mix, w_in, g_q, g_k, *, tm, keep, with_kt):
    nb, t_len, d_model = x.shape
    d_att = g_q.shape[-1]
    d_in = w_in.shape[-1]
    d_conv = (d_in - 3 * d_att) // 2
    n_t = t_len // tm
    assert t_len % tm == 0 and (t_len - keep) % tm == 0 and tm % LANES == 0
    tail0 = (t_len - keep) // tm
    row = lambda b, t: (b, t, 0)
    tail = lambda b, t: (b, jnp.maximum(t - tail0, 0), 0)
    out_shape = [jax.ShapeDtypeStruct((nb, t_len, d_conv), F32),
                 jax.ShapeDtypeStruct((nb, t_len, d_att), BF16)]
    out_specs = [pl.BlockSpec((1, tm, d_conv), row), pl.BlockSpec((1, tm, d_att), row)]
    if with_kt:
        out_shape += [jax.ShapeDtypeStruct((nb, t_len // LANES, d_att, LANES), BF16),
                      jax.ShapeDtypeStruct((nb, t_len, d_att), BF16)]
        out_specs += [pl.BlockSpec((1, tm // LANES, d_att, LANES), lambda b, t: (b, t, 0, 0)),
                      pl.BlockSpec((1, tm, d_att), row)]
    out_shape += [jax.ShapeDtypeStruct((nb, keep, d_att), F32)] * 2
    out_specs += [pl.BlockSpec((1, tm, d_att), tail)] * 2
    kern = functools.partial(_inproj_kernel, d_conv=d_conv, d_att=d_att, tail0=tail0, with_kt=with_kt)
    return pl.pallas_call(
        kern,
        grid=(nb, n_t),
        in_specs=[pl.BlockSpec((1, tm, d_model), row),
                  _const_spec((1, d_model)), _const_spec((d_model, d_in)),
                  _const_spec((1, d_att)), _const_spec((1, d_att))],
        out_specs=out_specs,
        out_shape=out_shape,
        compiler_params=pltpu.CompilerParams(
            dimension_semantics=("arbitrary", "arbitrary"), vmem_limit_bytes=VMEM_LIMIT),
        name="in_proj",
    )(x, g_mix, w_in, g_q, g_k)


def _attend(q, get_k, get_v, bias_ref, off, n_heads):
    rows = q.shape[0]
    lane = lax.broadcasted_iota(jnp.int32, (rows, LANES), 1)
    outs = []
    for hp in range(n_heads // 2):
        v2 = get_v(hp)
        pair = []
        for h in (2 * hp, 2 * hp + 1):
            s = jnp.dot(q[:, h * HEAD_DIM:(h + 1) * HEAD_DIM], get_k(h), preferred_element_type=F32)
            s = s + jnp.concatenate([bias_ref[h, off + c, 0:rows, :] for c in range(WIN_SLABS)], axis=1)
            e = jnp.exp(s - jnp.max(s, axis=-1, keepdims=True))
            inv = 1.0 / jnp.sum(e, axis=-1, keepdims=True)
            pair.append(jnp.dot(e.astype(BF16), v2, preferred_element_type=F32) * inv)
        outs.append(jnp.where(lane < HEAD_DIM, pair[0], pair[1]))
    return jnp.concatenate(outs, axis=1)


def _attn_prompt_kernel(q_ref, kt_ref, v_ref, bias_ref, g_ref, o_ref, *, n_heads):
    pairs = q_ref.shape[1] // PAIR
    t = pl.program_id(1)

    def body(i, carry):
        p = t * pairs + i
        ws = jnp.maximum(p - (WIN_SLABS - 1), 0)
        off = ws + (WIN_SLABS - 1) - p
        r0 = pl.multiple_of(i * PAIR, PAIR)
        k0 = pl.multiple_of(ws * LANES, LANES)

        def get_k(h):
            return jnp.concatenate(
                [kt_ref[0, ws + c, h * HEAD_DIM:(h + 1) * HEAD_DIM, :] for c in range(WIN_SLABS)], axis=1)

        def get_v(hp):
            return v_ref[0, pl.ds(k0, WIN), hp * LANES:(hp + 1) * LANES]

        a = _attend(q_ref[0, pl.ds(r0, PAIR), :], get_k, get_v, bias_ref, off, n_heads)
        o_ref[0, pl.ds(r0, PAIR), :] = _rms(a, g_ref[...]).astype(BF16)
        return carry

    lax.fori_loop(0, pairs, body, 0)


def _attn_prompt(q, kt, v, bias, g_att, *, tq):
    nb, t_len, d_att = q.shape
    n_heads = d_att // HEAD_DIM
    assert t_len % tq == 0 and tq % PAIR == 0 and t_len >= WIN
    return pl.pallas_call(
        functools.partial(_attn_prompt_kernel, n_heads=n_heads),
        grid=(nb, t_len // tq),
        in_specs=[pl.BlockSpec((1, tq, d_att), lambda b, t: (b, t, 0)),
                  pl.BlockSpec((1, t_len // LANES, d_att, LANES), lambda b, t: (b, 0, 0, 0)),
                  pl.BlockSpec((1, t_len, d_att), lambda b, t: (b, 0, 0)),
                  _const_spec(bias.shape), _const_spec((1, d_att))],
        out_specs=pl.BlockSpec((1, tq, d_att), lambda b, t: (b, t, 0)),
        out_shape=jax.ShapeDtypeStruct((nb, t_len, d_att), BF16),
        compiler_params=pltpu.CompilerParams(
            dimension_semantics=("arbitrary", "arbitrary"), vmem_limit_bytes=VMEM_LIMIT),
        name="attn_prompt",
    )(q, kt, v, bias, g_att)


def _attn_sample_kernel(q_ref, ck_ref, cv_ref, kn_ref, vn_ref, bias_ref, g_ref, o_ref, kt_s, v_s, *, n_heads):
    d_att = q_ref.shape[-1]
    pad = jnp.zeros((WIN - ck_ref.shape[1] - kn_ref.shape[1], d_att), F32)
    k_all = jnp.concatenate([ck_ref[0], kn_ref[0], pad], axis=0)
    for c in range(WIN_SLABS):
        kt_s[c] = k_all[c * LANES:(c + 1) * LANES, :].T.astype(BF16)
    v_s[...] = jnp.concatenate([cv_ref[0], vn_ref[0], pad], axis=0).astype(BF16)

    def get_k(h):
        return jnp.concatenate(
            [kt_s[c, h * HEAD_DIM:(h + 1) * HEAD_DIM, :] for c in range(WIN_SLABS)], axis=1)

    def get_v(hp):
        return v_s[:, hp * LANES:(hp + 1) * LANES]

    a = _attend(q_ref[0], get_k, get_v, bias_ref, 0, n_heads)
    o_ref[0] = _rms(a, g_ref[...]).astype(BF16)


def _attn_sample(q, cache_k, cache_v, k_new, v_new, bias, g_att):
    nb, s_len, d_att = q.shape
    n_heads = d_att // HEAD_DIM
    l_cache = cache_k.shape[1]
    assert s_len == CHUNK and l_cache == LEFT_CHUNKS * CHUNK
    per_b = lambda b: (b, 0, 0)
    return pl.pallas_call(
        functools.partial(_attn_sample_kernel, n_heads=n_heads),
        grid=(nb,),
        in_specs=[pl.BlockSpec((1, s_len, d_att), per_b),
                  pl.BlockSpec((1, l_cache, d_att), per_b), pl.BlockSpec((1, l_cache, d_att), per_b),
                  pl.BlockSpec((1, s_len, d_att), per_b), pl.BlockSpec((1, s_len, d_att), per_b),
                  _const_spec(bias.shape), _const_spec((1, d_att))],
        out_specs=pl.BlockSpec((1, s_len, d_att), per_b),
        out_shape=jax.ShapeDtypeStruct((nb, s_len, d_att), BF16),
        scratch_shapes=[pltpu.VMEM((WIN_SLABS, d_att, LANES), BF16), pltpu.VMEM((WIN, d_att), BF16)],
        compiler_params=pltpu.CompilerParams(
            dimension_semantics=("arbitrary",), vmem_limit_bytes=VMEM_LIMIT),
        name="attn_sample",
    )(q, cache_k, cache_v, k_new, v_new, bias, g_att)


def _mix_kernel(x_ref, p_ref, an_ref, u_ref, halo_ref, wdw_ref, bdw_ref, gln_ref, bln_ref, gco_ref,
                wout_ref, gffn_ref, w1_ref, w2_ref, gple_ref, wg_ref, wple_ref, o_ref, ubuf,
                *, halo_from_prev_rows, ff_chunk):
    n_seg, seg, d_conv = u_ref.shape
    hr = halo_ref.shape[1]
    first = pl.program_id(1) == 0
    convs = []
    for s in range(n_seg):
        hal = halo_ref[s]
        if halo_from_prev_rows:
            hal = jnp.where(first, 0.0, hal)
        ubuf[s, HALO - hr:HALO, :] = hal
        ubuf[s, HALO:HALO + seg, :] = u_ref[s]
        acc = jnp.zeros((seg, d_conv), F32)
        for j in range(CONV_WIDTH):
            lo = HALO - (CONV_WIDTH - 1) + j
            acc = acc + wdw_ref[j:j + 1, :] * ubuf[s, lo:lo + seg, :]
        convs.append(acc)
    y = (convs[0] if n_seg == 1 else jnp.concatenate(convs, axis=0)) + bdw_ref[...]
    mu = jnp.mean(y, axis=-1, keepdims=True)
    yc = y - mu
    yn = yc * lax.rsqrt(jnp.mean(yc * yc, axis=-1, keepdims=True) + EPS) * gln_ref[...] + bln_ref[...]
    c = yn * jax.nn.sigmoid(yn)
    cn = _rms(c, gco_ref[...]).astype(BF16)

    x1 = (x_ref[0]
          + jnp.dot(cn, wout_ref[0:d_conv, :], preferred_element_type=F32)
          + jnp.dot(an_ref[0], wout_ref[d_conv:, :], preferred_element_type=F32))
    hb = _rms(x1, gffn_ref[...]).astype(BF16)
    d_ff = w1_ref.shape[1]
    ffn = jnp.zeros_like(x1)
    for lo in range(0, d_ff, ff_chunk):
        a = jnp.maximum(jnp.dot(hb, w1_ref[:, lo:lo + ff_chunk], preferred_element_type=F32), 0.0)
        ffn = ffn + jnp.dot((a * a).astype(BF16), w2_ref[lo:lo + ff_chunk, :], preferred_element_type=F32)
    x2 = x1 + ffn
    gate = jax.nn.sigmoid(jnp.dot(_rms(x2, gple_ref[...]).astype(BF16), wg_ref[...],
                                  preferred_element_type=F32))
    ple = jnp.dot(p_ref[0].astype(BF16), wple_ref[...], preferred_element_type=F32)
    o_ref[0] = x2 + ple * gate


def _mix(x, p, an, u, halo_src, w_dw, b_dw, g_ln, b_ln, g_co, w_out, g_ffn, w1, w2, g_ple, w_gate, w_ple,
         *, tm, halo_from_prev_rows):
    ng, rows, d_model = x.shape
    d_conv = u.shape[-1]
    d_ple = p.shape[-1]
    n_t = rows // tm
    assert rows % tm == 0 and tm % HALO == 0
    row = lambda g, t: (g, t, 0)
    if halo_from_prev_rows:
        n_seg, seg = 1, tm
        u_spec = pl.BlockSpec((1, tm, d_conv), row)
        per = tm // HALO
        halo_spec = pl.BlockSpec((1, HALO, d_conv), lambda g, t: (g, jnp.maximum(t * per - 1, 0), 0))
    else:
        n_seg, seg = u.shape[0], u.shape[1]
        assert ng == 1 and n_t == 1 and n_seg * seg == tm and halo_src.shape[1] == CONV_WIDTH - 1
        u_spec = pl.BlockSpec(u.shape, lambda g, t: (0, 0, 0))
        halo_spec = pl.BlockSpec(halo_src.shape, lambda g, t: (0, 0, 0))
    consts = [w_dw, b_dw, g_ln, b_ln, g_co, w_out, g_ffn, w1, w2, g_ple, w_gate, w_ple]
    kern = functools.partial(_mix_kernel, halo_from_prev_rows=halo_from_prev_rows, ff_chunk=1024)
    return pl.pallas_call(
        kern,
        grid=(ng, n_t),
        in_specs=[pl.BlockSpec((1, tm, d_model), row), pl.BlockSpec((1, tm, d_ple), row),
                  pl.BlockSpec((1, tm, an.shape[-1]), row), u_spec, halo_spec]
                 + [_const_spec(c.shape) for c in consts],
        out_specs=pl.BlockSpec((1, tm, d_model), row),
        out_shape=jax.ShapeDtypeStruct(x.shape, F32),
        scratch_shapes=[pltpu.VMEM((n_seg, HALO + seg, d_conv), F32)],
        compiler_params=pltpu.CompilerParams(
            dimension_semantics=("arbitrary", "arbitrary"), vmem_limit_bytes=VMEM_LIMIT),
        name="mix",
    )(x, p, an, u, halo_src, *consts)


def kernel(x_prompt, x_sample, p_prompt, p_sample, cache_att_k, cache_att_v, state_conv, g_mix, w_in, w_dw,
           b_dw, g_conv_ln, b_conv_ln, g_q, g_k, rel_bias, g_conv_out, g_att_out, w_out, g_ffn, w_ff1, w_ff2,
           g_ple, w_gate, w_ple):
    depth = w_in.shape[0]
    nb, t_len, d_model = x_prompt.shape
    sb, s_len, _ = x_sample.shape
    d_conv = w_dw.shape[-1]
    d_att = g_att_out.shape[-1]
    n_heads = d_att // HEAD_DIM
    keep = min(LEFT_CHUNKS * CHUNK, t_len)
    tile = 512

    xp = x_prompt
    xs = x_sample.reshape(1, sb * s_len, d_model)
    outs = [[] for _ in range(6)]
    for i in range(depth):
        vec = lambda a: a[i].reshape(1, -1)
        gq = jnp.tile(g_q[i], n_heads).reshape(1, d_att)
        gk = jnp.tile(g_k[i], n_heads).reshape(1, d_att)
        w_in_b = w_in[i].astype(BF16)
        mix_w = (w_dw[i], vec(b_dw), vec(g_conv_ln), vec(b_conv_ln), vec(g_conv_out), w_out[i].astype(BF16),
                 vec(g_ffn), w_ff1[i].astype(BF16), w_ff2[i].astype(BF16), vec(g_ple),
                 w_gate[i].astype(BF16), w_ple[i].astype(BF16))
        bias = _bias_tiles(rel_bias[i])

        u_p, q_p, kt_p, vb_p, k_tail, v_tail = _in_proj(xp, vec(g_mix), w_in_b, gq, gk,
                                                        tm=tile, keep=keep, with_kt=True)
        an_p = _attn_prompt(q_p, kt_p, vb_p, bias, vec(g_att_out), tq=tile)
        xp = _mix(xp, p_prompt[i], an_p, u_p, u_p, *mix_w, tm=tile, halo_from_prev_rows=True)
        outs[0].append(k_tail.reshape(nb, keep, n_heads, HEAD_DIM))
        outs[1].append(v_tail.reshape(nb, keep, n_heads, HEAD_DIM))
        outs[2].append(u_p[:, t_len - (CONV_WIDTH - 1):])

        rows = sb * s_len
        u_s, q_s, k_s, v_s = _in_proj(xs, vec(g_mix), w_in_b, gq, gk, tm=rows, keep=rows, with_kt=False)
        per_b = lambda a: a.reshape(sb, s_len, a.shape[-1])
        ck = cache_att_k[i].reshape(sb, -1, d_att)
        cv = cache_att_v[i].reshape(sb, -1, d_att)
        an_s = _attn_sample(per_b(q_s), ck, cv, per_b(k_s), per_b(v_s), bias, vec(g_att_out))
        xs = _mix(xs, p_sample[i].reshape(1, rows, -1), an_s.reshape(1, rows, d_att), per_b(u_s),
                  state_conv[i], *mix_w, tm=rows, halo_from_prev_rows=False)
        outs[3].append(k_s.reshape(sb, s_len, n_heads, HEAD_DIM))
        outs[4].append(v_s.reshape(sb, s_len, n_heads, HEAD_DIM))
        conv_in_tail = jnp.concatenate([state_conv[i], per_b(u_s)], axis=1)[:, -(CONV_WIDTH - 1):]
        outs[5].append(conv_in_tail)

    return (xp, xs.reshape(sb, s_len, d_model)) + tuple(jnp.stack(o) for o in outs)
```

```python
import functools

import jax
import jax.numpy as jnp
from jax import lax
from jax.experimental import pallas as pl
from jax.experimental.pallas import tpu as pltpu

F32 = jnp.float32
BF16 = jnp.bfloat16

CHUNK = 64
LEFT_CHUNKS = 8
HEAD_DIM = 64
CONV_WIDTH = 31
MAX_REL = 128
EPS = 1e-6
NEG = -1e30

LANES = 128
PAIR = 2 * CHUNK
WIN = (LEFT_CHUNKS + 2) * CHUNK
WIN_SLABS = WIN // LANES
BIAS_SLABS = 2 * WIN_SLABS - 1
HALO = 32
VMEM_LIMIT = 56 * 1024 * 1024


def _rms(x, g):
    return x * lax.rsqrt(jnp.mean(x * x, axis=-1, keepdims=True) + EPS) * g


def _const_spec(shape):
    zeros = (0,) * len(shape)
    return pl.BlockSpec(shape, lambda *_: zeros, pipeline_mode=pl.Buffered(1))


def _bias_kernel(rb_ref, o_ref):
    n_heads = rb_ref.shape[0]
    rb = rb_ref[...]
    lane = lax.broadcasted_iota(jnp.int32, (n_heads, LANES), 1)
    b0 = jnp.broadcast_to(rb[:, 0:1], (n_heads, LANES))
    t4 = jnp.where(lane < CHUNK, rb[:, LANES:2 * LANES], b0)
    t = jnp.concatenate([b0, b0, b0, rb[:, 0:LANES], t4, b0], axis=1)
    r_i = lax.broadcasted_iota(jnp.int32, (PAIR, LANES), 0)
    c_i = lax.broadcasted_iota(jnp.int32, (PAIR, LANES), 1)
    for h in range(n_heads):
        row = jnp.broadcast_to(t[h:h + 1, :], (PAIR, t.shape[1]))
        toe = pltpu.roll(row, 0, 1, stride=1, stride_axis=0)
        for c in range(WIN_SLABS):
            jj = c_i + c * LANES
            valid = ((r_i < CHUNK) & (jj < WIN - CHUNK)) | ((r_i >= CHUNK) & (jj >= CHUNK))
            o_ref[h, c] = jnp.where(valid, toe[:, c * LANES:(c + 1) * LANES], NEG)
        for c in range(WIN_SLABS, BIAS_SLABS):
            o_ref[h, c] = jnp.full((PAIR, LANES), NEG, F32)


def _bias_tiles(rel_bias):
    n_heads = rel_bias.shape[0]
    return pl.pallas_call(
        _bias_kernel,
        out_shape=jax.ShapeDtypeStruct((n_heads, BIAS_SLABS, PAIR, LANES), F32),
        name="bias_tiles",
    )(rel_bias)


def _inproj_kernel(x_ref, gmix_ref, win_ref, gq_ref, gk_ref, *outs, d_conv, d_att, tail0, with_kt):
    if with_kt:
        u_ref, q_ref, kt_ref, vb_ref, k32_ref, v32_ref = outs
    else:
        u_ref, q_ref, k32_ref, v32_ref = outs
    tm = x_ref.shape[1]
    h = _rms(x_ref[0], gmix_ref[...]).astype(BF16)

    def proj(lo, width):
        return jnp.dot(h, win_ref[:, lo:lo + width], preferred_element_type=F32)

    u_ref[0] = proj(0, d_conv) * jax.nn.sigmoid(proj(d_conv, d_conv))

    r_i = lax.shift_right_logical(lax.broadcasted_iota(jnp.int32, (d_att, d_att), 0), 6)
    c_i = lax.shift_right_logical(lax.broadcasted_iota(jnp.int32, (d_att, d_att), 1), 6)
    avg = jnp.where(r_i == c_i, 1.0 / HEAD_DIM, 0.0).astype(BF16)

    def head_rms(z, g):
        ms = jnp.dot((z * z).astype(BF16), avg, preferred_element_type=F32)
        return z * lax.rsqrt(ms + EPS) * g

    o = 2 * d_conv
    qn = head_rms(proj(o, d_att), gq_ref[...])
    q_ref[0] = (qn * (HEAD_DIM ** -0.5)).astype(BF16)
    kn = head_rms(proj(o + d_att, d_att), gk_ref[...])
    v = proj(o + 2 * d_att, d_att)
    if with_kt:
        for s in range(tm // LANES):
            kt_ref[0, s] = kn[s * LANES:(s + 1) * LANES, :].T.astype(BF16)
        vb_ref[0] = v.astype(BF16)

    @pl.when(pl.program_id(1) >= tail0)
    def _():
        k32_ref[0] = kn
        v32_ref[0] = v


def _in_proj(x, g_mix, w_in, g_q, g_k, *, tm, keep, with_kt):
    nb, t_len, d_model = x.shape
    d_att = g_q.shape[-1]
    d_in = w_in.shape[-1]
    d_conv = (d_in - 3 * d_att) // 2
    n_t = t_len // tm
    assert t_len % tm == 0 and (t_len - keep) % tm == 0 and tm % LANES == 0
    tail0 = (t_len - keep) // tm
    row = lambda b, t: (b, t, 0)
    tail = lambda b, t: (b, jnp.maximum(t - tail0, 0), 0)
    out_shape = [jax.ShapeDtypeStruct((nb, t_len, d_conv), F32),
                 jax.ShapeDtypeStruct((nb, t_len, d_att), BF16)]
    out_specs = [pl.BlockSpec((1, tm, d_conv), row), pl.BlockSpec((1, tm, d_att), row)]
    if with_kt:
        out_shape += [jax.ShapeDtypeStruct((nb, t_len // LANES, d_att, LANES), BF16),
                      jax.ShapeDtypeStruct((nb, t_len, d_att), BF16)]
        out_specs += [pl.BlockSpec((1, tm // LANES, d_att, LANES), lambda b, t: (b, t, 0, 0)),
                      pl.BlockSpec((1, tm, d_att), row)]
    out_shape += [jax.ShapeDtypeStruct((nb, keep, d_att), F32)] * 2
    out_specs += [pl.BlockSpec((1, tm, d_att), tail)] * 2
    kern = functools.partial(_inproj_kernel, d_conv=d_conv, d_att=d_att, tail0=tail0, with_kt=with_kt)
    return pl.pallas_call(
        kern,
        grid=(nb, n_t),
        in_specs=[pl.BlockSpec((1, tm, d_model), row),
                  _const_spec((1, d_model)), _const_spec((d_model, d_in)),
                  _const_spec((1, d_att)), _const_spec((1, d_att))],
        out_specs=out_specs,
        out_shape=out_shape,
        compiler_params=pltpu.CompilerParams(
            dimension_semantics=("arbitrary", "arbitrary"), vmem_limit_bytes=VMEM_LIMIT),
        name="in_proj",
    )(x, g_mix, w_in, g_q, g_k)


def _attend(q, get_k, get_v, bias_ref, off, n_heads):
    rows = q.shape[0]
    lane = lax.broadcasted_iota(jnp.int32, (rows, LANES), 1)
    outs = []
    for hp in range(n_heads // 2):
        v2 = get_v(hp)
        pair = []
        for h in (2 * hp, 2 * hp + 1):
            s = jnp.dot(q[:, h * HEAD_DIM:(h + 1) * HEAD_DIM], get_k(h), preferred_element_type=F32)
            s = s + jnp.concatenate([bias_ref[h, off + c, 0:rows, :] for c in range(WIN_SLABS)], axis=1)
            e = jnp.exp(s - jnp.max(s, axis=-1, keepdims=True))
            inv = 1.0 / jnp.sum(e, axis=-1, keepdims=True)
            pair.append(jnp.dot(e.astype(BF16), v2, preferred_element_type=F32) * inv)
        outs.append(jnp.where(lane < HEAD_DIM, pair[0], pair[1]))
    return jnp.concatenate(outs, axis=1)


def _attn_prompt_kernel(q_ref, kt_ref, v_ref, bias_ref, g_ref, o_ref, *, n_heads):
    pairs = q_ref.shape[1] // PAIR
    t = pl.program_id(1)

    def body(i, carry):
        p = t * pairs + i
        ws = jnp.maximum(p - (WIN_SLABS - 1), 0)
        off = ws + (WIN_SLABS - 1) - p
        r0 = pl.multiple_of(i * PAIR, PAIR)
        k0 = pl.multiple_of(ws * LANES, LANES)

        def get_k(h):
            return jnp.concatenate(
                [kt_ref[0, ws + c, h * HEAD_DIM:(h + 1) * HEAD_DIM, :] for c in range(WIN_SLABS)], axis=1)

        def get_v(hp):
            return v_ref[0, pl.ds(k0, WIN), hp * LANES:(hp + 1) * LANES]

        a = _attend(q_ref[0, pl.ds(r0, PAIR), :], get_k, get_v, bias_ref, off, n_heads)
        o_ref[0, pl.ds(r0, PAIR), :] = _rms(a, g_ref[...]).astype(BF16)
        return carry

    lax.fori_loop(0, pairs, body, 0)


def _attn_prompt(q, kt, v, bias, g_att, *, tq):
    nb, t_len, d_att = q.shape
    n_heads = d_att // HEAD_DIM
    assert t_len % tq == 0 and tq % PAIR == 0 and t_len >= WIN
    return pl.pallas_call(
        functools.partial(_attn_prompt_kernel, n_heads=n_heads),
        grid=(nb, t_len // tq),
        in_specs=[pl.BlockSpec((1, tq, d_att), lambda b, t: (b, t, 0)),
                  pl.BlockSpec((1, t_len // LANES, d_att, LANES), lambda b, t: (b, 0, 0, 0)),
                  pl.BlockSpec((1, t_len, d_att), lambda b, t: (b, 0, 0)),
                  _const_spec(bias.shape), _const_spec((1, d_att))],
        out_specs=pl.BlockSpec((1, tq, d_att), lambda b, t: (b, t, 0)),
        out_shape=jax.ShapeDtypeStruct((nb, t_len, d_att), BF16),
        compiler_params=pltpu.CompilerParams(
            dimension_semantics=("arbitrary", "arbitrary"), vmem_limit_bytes=VMEM_LIMIT),
        name="attn_prompt",
    )(q, kt, v, bias, g_att)


def _attn_sample_kernel(q_ref, ck_ref, cv_ref, kn_ref, vn_ref, bias_ref, g_ref, o_ref, kt_s, v_s, *, n_heads):
    d_att = q_ref.shape[-1]
    pad = jnp.zeros((WIN - ck_ref.shape[1] - kn_ref.shape[1], d_att), F32)
    k_all = jnp.concatenate([ck_ref[0], kn_ref[0], pad], axis=0)
    for c in range(WIN_SLABS):
        kt_s[c] = k_all[c * LANES:(c + 1) * LANES, :].T.astype(BF16)
    v_s[...] = jnp.concatenate([cv_ref[0], vn_ref[0], pad], axis=0).astype(BF16)

    def get_k(h):
        return jnp.concatenate(
            [kt_s[c, h * HEAD_DIM:(h + 1) * HEAD_DIM, :] for c in range(WIN_SLABS)], axis=1)

    def get_v(hp):
        return v_s[:, hp * LANES:(hp + 1) * LANES]

    a = _attend(q_ref[0], get_k, get_v, bias_ref, 0, n_heads)
    o_ref[0] = _rms(a, g_ref[...]).astype(BF16)


def _attn_sample(q, cache_k, cache_v, k_new, v_new, bias, g_att):
    nb, s_len, d_att = q.shape
    n_heads = d_att // HEAD_DIM
    l_cache = cache_k.shape[1]
    assert s_len == CHUNK and l_cache == LEFT_CHUNKS * CHUNK
    per_b = lambda b: (b, 0, 0)
    return pl.pallas_call(
        functools.partial(_attn_sample_kernel, n_heads=n_heads),
        grid=(nb,),
        in_specs=[pl.BlockSpec((1, s_len, d_att), per_b),
                  pl.BlockSpec((1, l_cache, d_att), per_b), pl.BlockSpec((1, l_cache, d_att), per_b),
                  pl.BlockSpec((1, s_len, d_att), per_b), pl.BlockSpec((1, s_len, d_att), per_b),
                  _const_spec(bias.shape), _const_spec((1, d_att))],
        out_specs=pl.BlockSpec((1, s_len, d_att), per_b),
        out_shape=jax.ShapeDtypeStruct((nb, s_len, d_att), BF16),
        scratch_shapes=[pltpu.VMEM((WIN_SLABS, d_att, LANES), BF16), pltpu.VMEM((WIN, d_att), BF16)],
        compiler_params=pltpu.CompilerParams(
            dimension_semantics=("arbitrary",), vmem_limit_bytes=VMEM_LIMIT),
        name="attn_sample",
    )(q, cache_k, cache_v, k_new, v_new, bias, g_att)


def _conv_stage(u_ref, halo_ref, no_history, ubuf, wdw_ref, bdw_ref, gln_ref, bln_ref, gco_ref):
    n_seg, seg, d_conv = u_ref.shape
    hr = halo_ref.shape[1]
    off = HALO - (CONV_WIDTH - 1)
    ys = []
    for s in range(n_seg):
        hal = halo_ref[s]
        if no_history is not None:
            hal = jnp.where(no_history, 0.0, hal)
        if hr < HALO:
            ubuf[s, 0:HALO - hr, :] = jnp.zeros((HALO - hr, d_conv), F32)
        ubuf[s, HALO - hr:HALO, :] = hal
        ubuf[s, HALO:HALO + seg, :] = u_ref[s]
        ubuf[s, HALO + seg:HALO + seg + 8, :] = jnp.zeros((8, d_conv), F32)
        y = None
        for b in range(8):
            zb = None
            for a in range(HALO // 8 + 1):
                j = 8 * a + b - off
                if 0 <= j < CONV_WIDTH:
                    term = wdw_ref[j:j + 1, :] * ubuf[s, 8 * a:8 * a + seg + 8, :]
                    zb = term if zb is None else zb + term
            zb = zb[b:b + seg]
            y = zb if y is None else y + zb
        ys.append(y)
    y = (ys[0] if n_seg == 1 else jnp.concatenate(ys, axis=0)) + bdw_ref[...]
    mu = jnp.mean(y, axis=-1, keepdims=True)
    yc = y - mu
    yn = yc * lax.rsqrt(jnp.mean(yc * yc, axis=-1, keepdims=True) + EPS) * gln_ref[...] + bln_ref[...]
    c = yn * jax.nn.sigmoid(yn)
    return _rms(c, gco_ref[...]).astype(BF16)


def _dense_stage(x, cn, an, p, wout_ref, gffn_ref, w1_ref, w2_ref, gple_ref, wg_ref, wple_ref, ff_chunk):
    d_conv = cn.shape[-1]
    x1 = (x + jnp.dot(cn, wout_ref[0:d_conv, :], preferred_element_type=F32)
          + jnp.dot(an, wout_ref[d_conv:, :], preferred_element_type=F32))
    hb = _rms(x1, gffn_ref[...]).astype(BF16)
    ffn = None
    for lo in range(0, w1_ref.shape[1], ff_chunk):
        a = jnp.maximum(jnp.dot(hb, w1_ref[:, lo:lo + ff_chunk], preferred_element_type=F32), 0.0)
        part = jnp.dot((a * a).astype(BF16), w2_ref[lo:lo + ff_chunk, :], preferred_element_type=F32)
        ffn = part if ffn is None else ffn + part
    x2 = x1 + ffn
    gate = jax.nn.sigmoid(jnp.dot(_rms(x2, gple_ref[...]).astype(BF16), wg_ref[...],
                                  preferred_element_type=F32))
    ple = jnp.dot(p.astype(BF16), wple_ref[...], preferred_element_type=F32)
    return x2 + ple * gate


def _mix_kernel(*refs, lookahead, tiles_per_seq, n_split, ff_chunk):
    if lookahead:
        x_ref, p_ref, an_ref, u0_ref, un_ref, hn_ref = refs[:6]
        conv_w, dense_w = refs[6:11], refs[11:18]
        o_ref, ubuf, cn_scr = refs[18:]
        i = pl.program_id(0)

        @pl.when(i == 0)
        def _():
            cn_scr[...] = _conv_stage(u0_ref, hn_ref, True, ubuf, *conv_w)
    else:
        x_ref, p_ref, an_ref, u_ref, halo_ref = refs[:5]
        conv_w, dense_w = refs[5:10], refs[10:17]
        o_ref, ubuf, cn_scr = refs[17:]
        cn_scr[...] = _conv_stage(u_ref, halo_ref, None, ubuf, *conv_w)

    tm = x_ref.shape[1]
    hm = tm // n_split
    for h in range(n_split):
        r = slice(h * hm, (h + 1) * hm)
        o_ref[0, r, :] = _dense_stage(x_ref[0, r, :], cn_scr[r, :], an_ref[0, r, :], p_ref[0, r, :],
                                      *dense_w, ff_chunk)

    if lookahead:
        nxt_starts_seq = lax.rem(i + 1, tiles_per_seq) == 0
        cn_scr[...] = _conv_stage(un_ref, hn_ref, nxt_starts_seq, ubuf, *conv_w)


def _mix(x, p, an, u, halo_src, w_dw, b_dw, g_ln, b_ln, g_co, w_out, g_ffn, w1, w2, g_ple, w_gate, w_ple,
         *, tm, lookahead):
    ng, rows, d_model = x.shape
    d_conv = u.shape[-1]
    n_t = rows // tm
    n_steps = ng * n_t
    assert rows % tm == 0 and tm % HALO == 0
    cur = lambda i: (i // n_t, i % n_t, 0)
    if lookahead:
        n_seg, seg = 1, tm
        per = tm // HALO
        nxt = lambda i: jnp.minimum(i + 1, n_steps - 1)
        conv_in = [u, u, halo_src]
        conv_specs = [pl.BlockSpec((1, tm, d_conv), lambda i: (0, 0, 0), pipeline_mode=pl.Buffered(1)),
                      pl.BlockSpec((1, tm, d_conv), lambda i: (nxt(i) // n_t, nxt(i) % n_t, 0)),
                      pl.BlockSpec((1, HALO, d_conv),
                                   lambda i: (nxt(i) // n_t, jnp.maximum((nxt(i) % n_t) * per - 1, 0), 0))]
    else:
        n_seg, seg = u.shape[0], u.shape[1]
        assert n_steps == 1 and n_seg * seg == tm and halo_src.shape[1] == CONV_WIDTH - 1
        conv_in = [u, halo_src]
        conv_specs = [pl.BlockSpec(u.shape, lambda i: (0, 0, 0)),
                      pl.BlockSpec(halo_src.shape, lambda i: (0, 0, 0))]
    consts = [w_dw, b_dw, g_ln, b_ln, g_co, w_out, g_ffn, w1, w2, g_ple, w_gate, w_ple]
    kern = functools.partial(_mix_kernel, lookahead=lookahead, tiles_per_seq=n_t, n_split=2, ff_chunk=1024)
    return pl.pallas_call(
        kern,
        grid=(n_steps,),
        in_specs=[pl.BlockSpec((1, tm, d_model), cur), pl.BlockSpec((1, tm, p.shape[-1]), cur),
                  pl.BlockSpec((1, tm, an.shape[-1]), cur)] + conv_specs
                 + [_const_spec(c.shape) for c in consts],
        out_specs=pl.BlockSpec((1, tm, d_model), cur),
        out_shape=jax.ShapeDtypeStruct(x.shape, F32),
        scratch_shapes=[pltpu.VMEM((n_seg, HALO + seg + 8, d_conv), F32), pltpu.VMEM((tm, d_conv), BF16)],
        compiler_params=pltpu.CompilerParams(
            dimension_semantics=("arbitrary",), vmem_limit_bytes=VMEM_LIMIT),
        name="mix",
    )(x, p, an, *conv_in, *consts)


def kernel(x_prompt, x_sample, p_prompt, p_sample, cache_att_k, cache_att_v, state_conv, g_mix, w_in, w_dw,
           b_dw, g_conv_ln, b_conv_ln, g_q, g_k, rel_bias, g_conv_out, g_att_out, w_out, g_ffn, w_ff1, w_ff2,
           g_ple, w_gate, w_ple):
    depth = w_in.shape[0]
    nb, t_len, d_model = x_prompt.shape
    sb, s_len, _ = x_sample.shape
    d_conv = w_dw.shape[-1]
    d_att = g_att_out.shape[-1]
    n_heads = d_att // HEAD_DIM
    keep = min(LEFT_CHUNKS * CHUNK, t_len)
    tile = 512

    xp = x_prompt
    xs = x_sample.reshape(1, sb * s_len, d_model)
    outs = [[] for _ in range(6)]
    for i in range(depth):
        vec = lambda a: a[i].reshape(1, -1)
        gq = jnp.tile(g_q[i], n_heads).reshape(1, d_att)
        gk = jnp.tile(g_k[i], n_heads).reshape(1, d_att)
        w_in_b = w_in[i].astype(BF16)
        mix_w = (w_dw[i], vec(b_dw), vec(g_conv_ln), vec(b_conv_ln), vec(g_conv_out), w_out[i].astype(BF16),
                 vec(g_ffn), w_ff1[i].astype(BF16), w_ff2[i].astype(BF16), vec(g_ple),
                 w_gate[i].astype(BF16), w_ple[i].astype(BF16))
        bias = _bias_tiles(rel_bias[i])

        u_p, q_p, kt_p, vb_p, k_tail, v_tail = _in_proj(xp, vec(g_mix), w_in_b, gq, gk,
                                                        tm=tile, keep=keep, with_kt=True)
        an_p = _attn_prompt(q_p, kt_p, vb_p, bias, vec(g_att_out), tq=tile)
        xp = _mix(xp, p_prompt[i], an_p, u_p, u_p, *mix_w, tm=tile, lookahead=True)
        outs[0].append(k_tail.reshape(nb, keep, n_heads, HEAD_DIM))
        outs[1].append(v_tail.reshape(nb, keep, n_heads, HEAD_DIM))
        outs[2].append(u_p[:, t_len - (CONV_WIDTH - 1):])

        rows = sb * s_len
        u_s, q_s, k_s, v_s = _in_proj(xs, vec(g_mix), w_in_b, gq, gk, tm=rows, keep=rows, with_kt=False)
        per_b = lambda a: a.reshape(sb, s_len, a.shape[-1])
        ck = cache_att_k[i].reshape(sb, -1, d_att)
        cv = cache_att_v[i].reshape(sb, -1, d_att)
        an_s = _attn_sample(per_b(q_s), ck, cv, per_b(k_s), per_b(v_s), bias, vec(g_att_out))
        xs = _mix(xs, p_sample[i].reshape(1, rows, -1), an_s.reshape(1, rows, d_att), per_b(u_s),
                  state_conv[i], *mix_w, tm=rows, lookahead=False)
        outs[3].append(k_s.reshape(sb, s_len, n_heads, HEAD_DIM))
        outs[4].append(v_s.reshape(sb, s_len, n_heads, HEAD_DIM))
        conv_in_tail = jnp.concatenate([state_conv[i], per_b(u_s)], axis=1)[:, -(CONV_WIDTH - 1):]
        outs[5].append(conv_in_tail)

    return (xp, xs.reshape(sb, s_len, d_model)) + tuple(jnp.stack(o) for o in outs)
```

```python
import functools

import jax
import jax.numpy as jnp
from jax import lax
from jax.experimental import pallas as pl
from jax.experimental.pallas import tpu as pltpu

F32 = jnp.float32
BF16 = jnp.bfloat16

CHUNK = 64
LEFT_CHUNKS = 8
HEAD_DIM = 64
CONV_WIDTH = 31
MAX_REL = 128
EPS = 1e-6
NEG = -1e30

LANES = 128
SUBLANES = 8
PAIR = 2 * CHUNK
WIN = (LEFT_CHUNKS + 2) * CHUNK
WIN_SLABS = WIN // LANES
BIAS_SLABS = 2 * WIN_SLABS - 1
HALO = 32
NORM_ROWS = 32
SOFTMAX_ROWS = 16
VMEM_LIMIT = 56 * 1024 * 1024


def _rms(x, g):
    return x * lax.rsqrt(jnp.mean(x * x, axis=-1, keepdims=True) + EPS) * g


def _const_spec(shape):
    zeros = (0,) * len(shape)
    return pl.BlockSpec(shape, lambda *_: zeros, pipeline_mode=pl.Buffered(1))


def _bias_kernel(rb_ref, o_ref):
    n_heads = rb_ref.shape[0]
    rb = rb_ref[...]
    lane = lax.broadcasted_iota(jnp.int32, (n_heads, LANES), 1)
    b0 = jnp.broadcast_to(rb[:, 0:1], (n_heads, LANES))
    t4 = jnp.where(lane < CHUNK, rb[:, LANES:2 * LANES], b0)
    t = jnp.concatenate([b0, b0, b0, rb[:, 0:LANES], t4, b0], axis=1)
    r_i = lax.broadcasted_iota(jnp.int32, (PAIR, LANES), 0)
    c_i = lax.broadcasted_iota(jnp.int32, (PAIR, LANES), 1)
    for h in range(n_heads):
        row = jnp.broadcast_to(t[h:h + 1, :], (PAIR, t.shape[1]))
        toe = pltpu.roll(row, 0, 1, stride=1, stride_axis=0)
        for c in range(WIN_SLABS):
            jj = c_i + c * LANES
            valid = ((r_i < CHUNK) & (jj < WIN - CHUNK)) | ((r_i >= CHUNK) & (jj >= CHUNK))
            o_ref[h, c] = jnp.where(valid, toe[:, c * LANES:(c + 1) * LANES], NEG)
        for c in range(WIN_SLABS, BIAS_SLABS):
            o_ref[h, c] = jnp.full((PAIR, LANES), NEG, F32)


def _bias_tiles(rel_bias):
    n_heads = rel_bias.shape[0]
    return pl.pallas_call(
        _bias_kernel,
        out_shape=jax.ShapeDtypeStruct((n_heads, BIAS_SLABS, PAIR, LANES), F32),
        name="bias_tiles",
    )(rel_bias)


def _inproj_kernel(x_ref, gmix_ref, win_ref, gq_ref, gk_ref, *outs, d_conv, d_att, tail0, with_kt):
    if with_kt:
        u_ref, q_ref, kt_ref, vb_ref, k32_ref, v32_ref = outs
    else:
        u_ref, q_ref, k32_ref, v32_ref = outs
    tm = x_ref.shape[1]
    h = _rms(x_ref[0], gmix_ref[...]).astype(BF16)

    def proj(lo, width):
        return jnp.dot(h, win_ref[:, lo:lo + width], preferred_element_type=F32)

    u_ref[0] = proj(0, d_conv) * jax.nn.sigmoid(proj(d_conv, d_conv))

    r_i = lax.shift_right_logical(lax.broadcasted_iota(jnp.int32, (d_att, d_att), 0), 6)
    c_i = lax.shift_right_logical(lax.broadcasted_iota(jnp.int32, (d_att, d_att), 1), 6)
    avg = jnp.where(r_i == c_i, 1.0 / HEAD_DIM, 0.0).astype(BF16)

    def head_rms(z, g):
        ms = jnp.dot((z * z).astype(BF16), avg, preferred_element_type=F32)
        return z * lax.rsqrt(ms + EPS) * g

    o = 2 * d_conv
    qn = head_rms(proj(o, d_att), gq_ref[...])
    q_ref[0] = (qn * (HEAD_DIM ** -0.5)).astype(BF16)
    kn = head_rms(proj(o + d_att, d_att), gk_ref[...])
    v = proj(o + 2 * d_att, d_att)
    if with_kt:
        for s in range(tm // LANES):
            kt_ref[0, s] = kn[s * LANES:(s + 1) * LANES, :].T.astype(BF16)
        vb_ref[0] = v.astype(BF16)

    @pl.when(pl.program_id(1) >= tail0)
    def _():
        k32_ref[0] = kn
        v32_ref[0] = v


def _in_proj(x, g_mix, w_in, g_q, g_k, *, tm, keep, with_kt):
    nb, t_len, d_model = x.shape
    d_att = g_q.shape[-1]
    d_in = w_in.shape[-1]
    d_conv = (d_in - 3 * d_att) // 2
    n_t = t_len // tm
    assert t_len % tm == 0 and (t_len - keep) % tm == 0 and tm % LANES == 0
    tail0 = (t_len - keep) // tm
    row = lambda b, t: (b, t, 0)
    tail = lambda b, t: (b, jnp.maximum(t - tail0, 0), 0)
    out_shape = [jax.ShapeDtypeStruct((nb, t_len, d_conv), F32),
                 jax.ShapeDtypeStruct((nb, t_len, d_att), BF16)]
    out_specs = [pl.BlockSpec((1, tm, d_conv), row), pl.BlockSpec((1, tm, d_att), row)]
    if with_kt:
        out_shape += [jax.ShapeDtypeStruct((nb, t_len // LANES, d_att, LANES), BF16),
                      jax.ShapeDtypeStruct((nb, t_len, d_att), BF16)]
        out_specs += [pl.BlockSpec((1, tm // LANES, d_att, LANES), lambda b, t: (b, t, 0, 0)),
                      pl.BlockSpec((1, tm, d_att), row)]
    out_shape += [jax.ShapeDtypeStruct((nb, keep, d_att), F32)] * 2
    out_specs += [pl.BlockSpec((1, tm, d_att), tail)] * 2
    kern = functools.partial(_inproj_kernel, d_conv=d_conv, d_att=d_att, tail0=tail0, with_kt=with_kt)
    return pl.pallas_call(
        kern,
        grid=(nb, n_t),
        in_specs=[pl.BlockSpec((1, tm, d_model), row),
                  _const_spec((1, d_model)), _const_spec((d_model, d_in)),
                  _const_spec((1, d_att)), _const_spec((1, d_att))],
        out_specs=out_specs,
        out_shape=out_shape,
        compiler_params=pltpu.CompilerParams(
            dimension_semantics=("arbitrary", "arbitrary"), vmem_limit_bytes=VMEM_LIMIT),
        name="in_proj",
    )(x, g_mix, w_in, g_q, g_k)


def _attn_scratch(n_heads, slots):
    return [pltpu.VMEM((slots, n_heads, PAIR, WIN), F32), pltpu.VMEM((slots, n_heads, PAIR, WIN), BF16),
            pltpu.VMEM((slots, n_heads, PAIR, 1), F32)]


def _attend(q, get_k, get_v, bias_ref, off, s_scr, p_scr, linv_scr):
    rows = q.shape[0]
    n_heads = s_scr.shape[0]
    for h in range(n_heads):
        s_scr[h, 0:rows, :] = jnp.dot(q[:, h * HEAD_DIM:(h + 1) * HEAD_DIM], get_k(h),
                                      preferred_element_type=F32)
    for h in range(n_heads):
        for r0 in range(0, rows, SOFTMAX_ROWS):
            r = slice(r0, r0 + SOFTMAX_ROWS)
            s = s_scr[h, r, :] + jnp.concatenate([bias_ref[h, off + c, r, :] for c in range(WIN_SLABS)], axis=1)
            e = jnp.exp(s - jnp.max(s, axis=-1, keepdims=True))
            p_scr[h, r, :] = e.astype(BF16)
            linv_scr[h, r, :] = 1.0 / jnp.sum(e, axis=-1, keepdims=True)
    lane = lax.broadcasted_iota(jnp.int32, (rows, LANES), 1)
    outs = []
    for hp in range(n_heads // 2):
        v2 = get_v(hp)
        pair = [jnp.dot(p_scr[h, 0:rows, :], v2, preferred_element_type=F32) * linv_scr[h, 0:rows, :]
                for h in (2 * hp, 2 * hp + 1)]
        outs.append(jnp.where(lane < HEAD_DIM, pair[0], pair[1]))
    return jnp.concatenate(outs, axis=1)


def _attn_prompt_kernel(q_ref, kt_ref, v_ref, bias_ref, g_ref, o_ref, s_scr, p_scr, linv_scr):
    pairs = q_ref.shape[1] // PAIR
    slots = s_scr.shape[0]
    t = pl.program_id(1)
    for i in range(pairs):
        p = t * pairs + i
        ws = jnp.maximum(p - (WIN_SLABS - 1), 0)
        off = ws + (WIN_SLABS - 1) - p
        k0 = pl.multiple_of(ws * LANES, LANES)

        def get_k(h, ws=ws):
            return jnp.concatenate(
                [kt_ref[0, ws + c, h * HEAD_DIM:(h + 1) * HEAD_DIM, :] for c in range(WIN_SLABS)], axis=1)

        def get_v(hp, k0=k0):
            return v_ref[0, pl.ds(k0, WIN), hp * LANES:(hp + 1) * LANES]

        rows = slice(i * PAIR, (i + 1) * PAIR)
        k = i % slots
        a = _attend(q_ref[0, rows, :], get_k, get_v, bias_ref, off, s_scr.at[k], p_scr.at[k], linv_scr.at[k])
        o_ref[0, rows, :] = _rms(a, g_ref[...]).astype(BF16)


def _attn_prompt(q, kt, v, bias, g_att, *, tq):
    nb, t_len, d_att = q.shape
    n_heads = d_att // HEAD_DIM
    assert t_len % tq == 0 and tq % PAIR == 0 and t_len >= WIN
    return pl.pallas_call(
        _attn_prompt_kernel,
        grid=(nb, t_len // tq),
        in_specs=[pl.BlockSpec((1, tq, d_att), lambda b, t: (b, t, 0)),
                  pl.BlockSpec((1, t_len // LANES, d_att, LANES), lambda b, t: (b, 0, 0, 0)),
                  pl.BlockSpec((1, t_len, d_att), lambda b, t: (b, 0, 0)),
                  _const_spec(bias.shape), _const_spec((1, d_att))],
        out_specs=pl.BlockSpec((1, tq, d_att), lambda b, t: (b, t, 0)),
        out_shape=jax.ShapeDtypeStruct((nb, t_len, d_att), BF16),
        scratch_shapes=_attn_scratch(n_heads, slots=2),
        compiler_params=pltpu.CompilerParams(
            dimension_semantics=("arbitrary", "arbitrary"), vmem_limit_bytes=VMEM_LIMIT),
        name="attn_prompt",
    )(q, kt, v, bias, g_att)


def _attn_sample_kernel(q_ref, ck_ref, cv_ref, kn_ref, vn_ref, bias_ref, g_ref, o_ref, kt_s, v_s,
                        s_scr, p_scr, linv_scr):
    d_att = q_ref.shape[-1]
    pad = jnp.zeros((WIN - ck_ref.shape[1] - kn_ref.shape[1], d_att), F32)
    k_all = jnp.concatenate([ck_ref[0], kn_ref[0], pad], axis=0)
    for c in range(WIN_SLABS):
        kt_s[c] = k_all[c * LANES:(c + 1) * LANES, :].T.astype(BF16)
    v_s[...] = jnp.concatenate([cv_ref[0], vn_ref[0], pad], axis=0).astype(BF16)

    def get_k(h):
        return jnp.concatenate(
            [kt_s[c, h * HEAD_DIM:(h + 1) * HEAD_DIM, :] for c in range(WIN_SLABS)], axis=1)

    def get_v(hp):
        return v_s[:, hp * LANES:(hp + 1) * LANES]

    a = _attend(q_ref[0], get_k, get_v, bias_ref, 0, s_scr.at[0], p_scr.at[0], linv_scr.at[0])
    o_ref[0] = _rms(a, g_ref[...]).astype(BF16)


def _attn_sample(q, cache_k, cache_v, k_new, v_new, bias, g_att):
    nb, s_len, d_att = q.shape
    n_heads = d_att // HEAD_DIM
    l_cache = cache_k.shape[1]
    assert s_len == CHUNK and l_cache == LEFT_CHUNKS * CHUNK
    per_b = lambda b: (b, 0, 0)
    return pl.pallas_call(
        _attn_sample_kernel,
        grid=(nb,),
        in_specs=[pl.BlockSpec((1, s_len, d_att), per_b),
                  pl.BlockSpec((1, l_cache, d_att), per_b), pl.BlockSpec((1, l_cache, d_att), per_b),
                  pl.BlockSpec((1, s_len, d_att), per_b), pl.BlockSpec((1, s_len, d_att), per_b),
                  _const_spec(bias.shape), _const_spec((1, d_att))],
        out_specs=pl.BlockSpec((1, s_len, d_att), per_b),
        out_shape=jax.ShapeDtypeStruct((nb, s_len, d_att), BF16),
        scratch_shapes=[pltpu.VMEM((WIN_SLABS, d_att, LANES), BF16), pltpu.VMEM((WIN, d_att), BF16)]
                       + _attn_scratch(n_heads, slots=1),
        compiler_params=pltpu.CompilerParams(
            dimension_semantics=("arbitrary",), vmem_limit_bytes=VMEM_LIMIT),
        name="attn_sample",
    )(q, cache_k, cache_v, k_new, v_new, bias, g_att)


def _conv_stage(u_ref, halo_ref, no_history, ubuf, y_scr, cn_ref, wdw_ref, bdw_ref, gln_ref, bln_ref, gco_ref):
    n_seg, seg, d_conv = u_ref.shape
    hr = halo_ref.shape[1]
    off = HALO - (CONV_WIDTH - 1)
    n_a = HALO // SUBLANES + 1
    for s in range(n_seg):
        hal = halo_ref[s]
        if no_history is not None:
            hal = jnp.where(no_history, 0.0, hal)
        if hr < HALO:
            ubuf[s, 0:HALO - hr, :] = jnp.zeros((HALO - hr, d_conv), F32)
        ubuf[s, HALO - hr:HALO, :] = hal
        ubuf[s, HALO:HALO + seg, :] = u_ref[s]
        ubuf[s, HALO + seg:HALO + seg + SUBLANES, :] = jnp.zeros((SUBLANES, d_conv), F32)

    sub = lax.broadcasted_iota(jnp.int32, (SUBLANES, LANES), 0)
    from_lower = [sub < SUBLANES - b for b in range(SUBLANES)]
    for lt in range(d_conv // LANES):
        lanes = slice(lt * LANES, (lt + 1) * LANES)
        w = [jnp.broadcast_to(wdw_ref[j:j + 1, lanes], (SUBLANES, LANES)) for j in range(CONV_WIDTH)]

        def rotated_groups(s, blk):
            rows = [ubuf[s, SUBLANES * (blk + a):SUBLANES * (blk + a + 1), lanes] for a in range(n_a)]
            out = []
            for b in range(SUBLANES):
                z = None
                for a in range(n_a):
                    j = SUBLANES * a + b - off
                    if 0 <= j < CONV_WIDTH:
                        t = w[j] * rows[a]
                        z = t if z is None else z + t
                out.append(z if b == 0 else pltpu.roll(z, SUBLANES - b, 0))
            return out

        for s in range(n_seg):
            lower = rotated_groups(s, 0)
            for blk in range(seg // SUBLANES):
                upper = rotated_groups(s, blk + 1)
                y = lower[0]
                for b in range(1, SUBLANES):
                    y = y + jnp.where(from_lower[b], lower[b], upper[b])
                y_scr[s * seg + blk * SUBLANES:s * seg + (blk + 1) * SUBLANES, lanes] = y
                lower = upper

    for r0 in range(0, n_seg * seg, NORM_ROWS):
        y = y_scr[r0:r0 + NORM_ROWS, :] + bdw_ref[...]
        mu = jnp.mean(y, axis=-1, keepdims=True)
        yc = y - mu
        yn = yc * lax.rsqrt(jnp.mean(yc * yc, axis=-1, keepdims=True) + EPS) * gln_ref[...] + bln_ref[...]
        c = yn * jax.nn.sigmoid(yn)
        cn_ref[r0:r0 + NORM_ROWS, :] = _rms(c, gco_ref[...]).astype(BF16)


def _dense_stage(x, cn, an, p, wout_ref, gffn_ref, w1_ref, w2_ref, gple_ref, wg_ref, wple_ref, ff_chunk):
    d_conv = cn.shape[-1]
    x1 = (x + jnp.dot(cn, wout_ref[0:d_conv, :], preferred_element_type=F32)
          + jnp.dot(an, wout_ref[d_conv:, :], preferred_element_type=F32))
    hb = _rms(x1, gffn_ref[...]).astype(BF16)
    ffn = None
    for lo in range(0, w1_ref.shape[1], ff_chunk):
        a = jnp.maximum(jnp.dot(hb, w1_ref[:, lo:lo + ff_chunk], preferred_element_type=F32), 0.0)
        part = jnp.dot((a * a).astype(BF16), w2_ref[lo:lo + ff_chunk, :], preferred_element_type=F32)
        ffn = part if ffn is None else ffn + part
    x2 = x1 + ffn
    gate = jax.nn.sigmoid(jnp.dot(_rms(x2, gple_ref[...]).astype(BF16), wg_ref[...],
                                  preferred_element_type=F32))
    ple = jnp.dot(p.astype(BF16), wple_ref[...], preferred_element_type=F32)
    return x2 + ple * gate


def _mix_kernel(*refs, lookahead, tiles_per_seq, n_split, ff_chunk):
    if lookahead:
        x_ref, p_ref, an_ref, u0_ref, un_ref, hn_ref = refs[:6]
        conv_w, dense_w = refs[6:11], refs[11:18]
        o_ref, ubuf, y_scr, cn_scr = refs[18:]
        i = pl.program_id(0)

        @pl.when(i == 0)
        def _():
            _conv_stage(u0_ref, hn_ref, True, ubuf, y_scr, cn_scr, *conv_w)
    else:
        x_ref, p_ref, an_ref, u_ref, halo_ref = refs[:5]
        conv_w, dense_w = refs[5:10], refs[10:17]
        o_ref, ubuf, y_scr, cn_scr = refs[17:]
        _conv_stage(u_ref, halo_ref, None, ubuf, y_scr, cn_scr, *conv_w)

    tm = x_ref.shape[1]
    hm = tm // n_split
    for h in range(n_split):
        r = slice(h * hm, (h + 1) * hm)
        o_ref[0, r, :] = _dense_stage(x_ref[0, r, :], cn_scr[r, :], an_ref[0, r, :], p_ref[0, r, :],
                                      *dense_w, ff_chunk)

    if lookahead:
        nxt_starts_seq = lax.rem(i + 1, tiles_per_seq) == 0
        _conv_stage(un_ref, hn_ref, nxt_starts_seq, ubuf, y_scr, cn_scr, *conv_w)


def _mix(x, p, an, u, halo_src, w_dw, b_dw, g_ln, b_ln, g_co, w_out, g_ffn, w1, w2, g_ple, w_gate, w_ple,
         *, tm, lookahead):
    ng, rows, d_model = x.shape
    d_conv = u.shape[-1]
    n_t = rows // tm
    n_steps = ng * n_t
    assert rows % tm == 0 and tm % HALO == 0
    cur = lambda i: (i // n_t, i % n_t, 0)
    if lookahead:
        n_seg, seg = 1, tm
        per = tm // HALO
        nxt = lambda i: jnp.minimum(i + 1, n_steps - 1)
        conv_in = [u, u, halo_src]
        conv_specs = [pl.BlockSpec((1, tm, d_conv), lambda i: (0, 0, 0), pipeline_mode=pl.Buffered(1)),
                      pl.BlockSpec((1, tm, d_conv), lambda i: (nxt(i) // n_t, nxt(i) % n_t, 0)),
                      pl.BlockSpec((1, HALO, d_conv),
                                   lambda i: (nxt(i) // n_t, jnp.maximum((nxt(i) % n_t) * per - 1, 0), 0))]
    else:
        n_seg, seg = u.shape[0], u.shape[1]
        assert n_steps == 1 and n_seg * seg == tm and halo_src.shape[1] == CONV_WIDTH - 1
        conv_in = [u, halo_src]
        conv_specs = [pl.BlockSpec(u.shape, lambda i: (0, 0, 0)),
                      pl.BlockSpec(halo_src.shape, lambda i: (0, 0, 0))]
    consts = [w_dw, b_dw, g_ln, b_ln, g_co, w_out, g_ffn, w1, w2, g_ple, w_gate, w_ple]
    kern = functools.partial(_mix_kernel, lookahead=lookahead, tiles_per_seq=n_t, n_split=2, ff_chunk=1024)
    return pl.pallas_call(
        kern,
        grid=(n_steps,),
        in_specs=[pl.BlockSpec((1, tm, d_model), cur), pl.BlockSpec((1, tm, p.shape[-1]), cur),
                  pl.BlockSpec((1, tm, an.shape[-1]), cur)] + conv_specs
                 + [_const_spec(c.shape) for c in consts],
        out_specs=pl.BlockSpec((1, tm, d_model), cur),
        out_shape=jax.ShapeDtypeStruct(x.shape, F32),
        scratch_shapes=[pltpu.VMEM((n_seg, HALO + seg + SUBLANES, d_conv), F32),
                        pltpu.VMEM((tm, d_conv), F32), pltpu.VMEM((tm, d_conv), BF16)],
        compiler_params=pltpu.CompilerParams(
            dimension_semantics=("arbitrary",), vmem_limit_bytes=VMEM_LIMIT),
        name="mix",
    )(x, p, an, *conv_in, *consts)


def kernel(x_prompt, x_sample, p_prompt, p_sample, cache_att_k, cache_att_v, state_conv, g_mix, w_in, w_dw,
           b_dw, g_conv_ln, b_conv_ln, g_q, g_k, rel_bias, g_conv_out, g_att_out, w_out, g_ffn, w_ff1, w_ff2,
           g_ple, w_gate, w_ple):
    depth = w_in.shape[0]
    nb, t_len, d_model = x_prompt.shape
    sb, s_len, _ = x_sample.shape
    d_conv = w_dw.shape[-1]
    d_att = g_att_out.shape[-1]
    n_heads = d_att // HEAD_DIM
    keep = min(LEFT_CHUNKS * CHUNK, t_len)
    tile = 512

    xp = x_prompt
    xs = x_sample.reshape(1, sb * s_len, d_model)
    outs = [[] for _ in range(6)]
    for i in range(depth):
        vec = lambda a: a[i].reshape(1, -1)
        gq = jnp.tile(g_q[i], n_heads).reshape(1, d_att)
        gk = jnp.tile(g_k[i], n_heads).reshape(1, d_att)
        w_in_b = w_in[i].astype(BF16)
        mix_w = (w_dw[i], vec(b_dw), vec(g_conv_ln), vec(b_conv_ln), vec(g_conv_out), w_out[i].astype(BF16),
                 vec(g_ffn), w_ff1[i].astype(BF16), w_ff2[i].astype(BF16), vec(g_ple),
                 w_gate[i].astype(BF16), w_ple[i].astype(BF16))
        bias = _bias_tiles(rel_bias[i])

        u_p, q_p, kt_p, vb_p, k_tail, v_tail = _in_proj(xp, vec(g_mix), w_in_b, gq, gk,
                                                        tm=tile, keep=keep, with_kt=True)
        an_p = _attn_prompt(q_p, kt_p, vb_p, bias, vec(g_att_out), tq=tile)
        xp = _mix(xp, p_prompt[i], an_p, u_p, u_p, *mix_w, tm=tile, lookahead=True)
        outs[0].append(k_tail.reshape(nb, keep, n_heads, HEAD_DIM))
        outs[1].append(v_tail.reshape(nb, keep, n_heads, HEAD_DIM))
        outs[2].append(u_p[:, t_len - (CONV_WIDTH - 1):])

        rows = sb * s_len
        u_s, q_s, k_s, v_s = _in_proj(xs, vec(g_mix), w_in_b, gq, gk, tm=rows, keep=rows, with_kt=False)
        per_b = lambda a: a.reshape(sb, s_len, a.shape[-1])
        ck = cache_att_k[i].reshape(sb, -1, d_att)
        cv = cache_att_v[i].reshape(sb, -1, d_att)
        an_s = _attn_sample(per_b(q_s), ck, cv, per_b(k_s), per_b(v_s), bias, vec(g_att_out))
        xs = _mix(xs, p_sample[i].reshape(1, rows, -1), an_s.reshape(1, rows, d_att), per_b(u_s),
                  state_conv[i], *mix_w, tm=rows, lookahead=False)
        outs[3].append(k_s.reshape(sb, s_len, n_heads, HEAD_DIM))
        outs[4].append(v_s.reshape(sb, s_len, n_heads, HEAD_DIM))
        conv_in_tail = jnp.concatenate([state_conv[i], per_b(u_s)], axis=1)[:, -(CONV_WIDTH - 1):]
        outs[5].append(conv_in_tail)

    return (xp, xs.reshape(sb, s_len, d_model)) + tuple(jnp.stack(o) for o in outs)
```

```python
import functools

import jax
import jax.numpy as jnp
from jax import lax
from jax.experimental import pallas as pl
from jax.experimental.pallas import tpu as pltpu

F32 = jnp.float32
BF16 = jnp.bfloat16

CHUNK = 64
LEFT_CHUNKS = 8
HEAD_DIM = 64
CONV_WIDTH = 31
MAX_REL = 128
EPS = 1e-6
NEG = -1e30

LANES = 128
SUBLANES = 8
PAIR = 2 * CHUNK
WIN = (LEFT_CHUNKS + 2) * CHUNK
WIN_SLABS = WIN // LANES
BIAS_SLABS = 2 * WIN_SLABS - 1
HALO = 32
NORM_ROWS = 32
SOFTMAX_ROWS = 16
VMEM_LIMIT = 56 * 1024 * 1024


def _rms(x, g):
    return x * lax.rsqrt(jnp.mean(x * x, axis=-1, keepdims=True) + EPS) * g


def _const_spec(shape):
    zeros = (0,) * len(shape)
    return pl.BlockSpec(shape, lambda *_: zeros, pipeline_mode=pl.Buffered(1))


def _bias_kernel(rb_ref, o_ref):
    n_heads = rb_ref.shape[0]
    rb = rb_ref[...]
    lane = lax.broadcasted_iota(jnp.int32, (n_heads, LANES), 1)
    b0 = jnp.broadcast_to(rb[:, 0:1], (n_heads, LANES))
    t4 = jnp.where(lane < CHUNK, rb[:, LANES:2 * LANES], b0)
    t = jnp.concatenate([b0, b0, b0, rb[:, 0:LANES], t4, b0], axis=1)
    r_i = lax.broadcasted_iota(jnp.int32, (PAIR, LANES), 0)
    c_i = lax.broadcasted_iota(jnp.int32, (PAIR, LANES), 1)
    for h in range(n_heads):
        row = jnp.broadcast_to(t[h:h + 1, :], (PAIR, t.shape[1]))
        toe = pltpu.roll(row, 0, 1, stride=1, stride_axis=0)
        for c in range(WIN_SLABS):
            jj = c_i + c * LANES
            valid = ((r_i < CHUNK) & (jj < WIN - CHUNK)) | ((r_i >= CHUNK) & (jj >= CHUNK))
            o_ref[h, c] = jnp.where(valid, toe[:, c * LANES:(c + 1) * LANES], NEG)
        for c in range(WIN_SLABS, BIAS_SLABS):
            o_ref[h, c] = jnp.full((PAIR, LANES), NEG, F32)


def _bias_tiles(rel_bias):
    n_heads = rel_bias.shape[0]
    return pl.pallas_call(
        _bias_kernel,
        out_shape=jax.ShapeDtypeStruct((n_heads, BIAS_SLABS, PAIR, LANES), F32),
        name="bias_tiles",
    )(rel_bias)


def _inproj_kernel(x_ref, gmix_ref, win_ref, gq_ref, gk_ref, *outs, d_conv, d_att, tail0, with_kt):
    if with_kt:
        u_ref, q_ref, kt_ref, vb_ref, k32_ref, v32_ref = outs
    else:
        u_ref, q_ref, k32_ref, v32_ref = outs
    tm = x_ref.shape[1]
    h = _rms(x_ref[0], gmix_ref[...]).astype(BF16)

    def proj(lo, width):
        return jnp.dot(h, win_ref[:, lo:lo + width], preferred_element_type=F32)

    u_ref[0] = proj(0, d_conv) * jax.nn.sigmoid(proj(d_conv, d_conv))

    r_i = lax.shift_right_logical(lax.broadcasted_iota(jnp.int32, (d_att, d_att), 0), 6)
    c_i = lax.shift_right_logical(lax.broadcasted_iota(jnp.int32, (d_att, d_att), 1), 6)
    avg = jnp.where(r_i == c_i, 1.0 / HEAD_DIM, 0.0).astype(BF16)

    def head_rms(z, g):
        ms = jnp.dot((z * z).astype(BF16), avg, preferred_element_type=F32)
        return z * lax.rsqrt(ms + EPS) * g

    o = 2 * d_conv
    even = (lax.broadcasted_iota(jnp.int32, (tm, d_att), 1) & HEAD_DIM) == 0

    def split_heads(ref, z):
        ref[0, 0] = jnp.where(even, z, 0.0).astype(BF16)
        ref[0, 1] = jnp.where(even, 0.0, z).astype(BF16)

    qn = head_rms(proj(o, d_att), gq_ref[...])
    split_heads(q_ref, qn * (HEAD_DIM ** -0.5))
    kn = head_rms(proj(o + d_att, d_att), gk_ref[...])
    v = proj(o + 2 * d_att, d_att)
    if with_kt:
        for s in range(tm // LANES):
            kt_ref[0, s] = kn[s * LANES:(s + 1) * LANES, :].T.astype(BF16)
        split_heads(vb_ref, v)

    @pl.when(pl.program_id(1) >= tail0)
    def _():
        k32_ref[0] = kn
        v32_ref[0] = v


def _in_proj(x, g_mix, w_in, g_q, g_k, *, tm, keep, with_kt):
    nb, t_len, d_model = x.shape
    d_att = g_q.shape[-1]
    d_in = w_in.shape[-1]
    d_conv = (d_in - 3 * d_att) // 2
    n_t = t_len // tm
    assert t_len % tm == 0 and (t_len - keep) % tm == 0 and tm % LANES == 0
    tail0 = (t_len - keep) // tm
    row = lambda b, t: (b, t, 0)
    tail = lambda b, t: (b, jnp.maximum(t - tail0, 0), 0)
    halves = pl.BlockSpec((1, 2, tm, d_att), lambda b, t: (b, 0, t, 0))
    out_shape = [jax.ShapeDtypeStruct((nb, t_len, d_conv), F32),
                 jax.ShapeDtypeStruct((nb, 2, t_len, d_att), BF16)]
    out_specs = [pl.BlockSpec((1, tm, d_conv), row), halves]
    if with_kt:
        out_shape += [jax.ShapeDtypeStruct((nb, t_len // LANES, d_att, LANES), BF16),
                      jax.ShapeDtypeStruct((nb, 2, t_len, d_att), BF16)]
        out_specs += [pl.BlockSpec((1, tm // LANES, d_att, LANES), lambda b, t: (b, t, 0, 0)), halves]
    out_shape += [jax.ShapeDtypeStruct((nb, keep, d_att), F32)] * 2
    out_specs += [pl.BlockSpec((1, tm, d_att), tail)] * 2
    kern = functools.partial(_inproj_kernel, d_conv=d_conv, d_att=d_att, tail0=tail0, with_kt=with_kt)
    return pl.pallas_call(
        kern,
        grid=(nb, n_t),
        in_specs=[pl.BlockSpec((1, tm, d_model), row),
                  _const_spec((1, d_model)), _const_spec((d_model, d_in)),
                  _const_spec((1, d_att)), _const_spec((1, d_att))],
        out_specs=out_specs,
        out_shape=out_shape,
        compiler_params=pltpu.CompilerParams(
            dimension_semantics=("arbitrary", "arbitrary"), vmem_limit_bytes=VMEM_LIMIT),
        name="in_proj",
    )(x, g_mix, w_in, g_q, g_k)


def _attn_scratch(n_heads, slots):
    return [pltpu.VMEM((slots, n_heads // 2, 2 * PAIR, WIN), F32),
            pltpu.VMEM((slots, n_heads // 2, PAIR, 2 * WIN), BF16)]


def _attend(rows, get_q, get_k, get_v, bias_ref, off, s_scr, p_scr):
    n_pairs = s_scr.shape[0]
    for hp in range(n_pairs):
        s_scr[hp, 0:2 * rows, :] = jnp.dot(get_q(hp), get_k(hp), preferred_element_type=F32)
    for hp in range(n_pairs):
        for sub in range(2):
            h = 2 * hp + sub
            for r0 in range(0, rows, SOFTMAX_ROWS):
                r = slice(r0, r0 + SOFTMAX_ROWS)
                s = s_scr[hp, sub * rows + r0:sub * rows + r0 + SOFTMAX_ROWS, :]
                s = s + jnp.concatenate([bias_ref[h, off + c, r, :] for c in range(WIN_SLABS)], axis=1)
                e = jnp.exp(s - jnp.max(s, axis=-1, keepdims=True))
                p_scr[hp, r, sub * WIN:(sub + 1) * WIN] = e.astype(BF16)
    first = lax.broadcasted_iota(jnp.int32, (WIN, LANES), 1) < HEAD_DIM
    sums = jnp.concatenate([jnp.where(first, 1.0, 0.0), jnp.where(first, 0.0, 1.0)], axis=0).astype(BF16)
    outs = []
    for hp in range(n_pairs):
        o2 = jnp.dot(p_scr[hp, 0:rows, :], jnp.concatenate([get_v(hp), sums], axis=1),
                     preferred_element_type=F32)
        outs.append(o2[:, 0:LANES] * (1.0 / o2[:, LANES:2 * LANES]))
    return jnp.concatenate(outs, axis=1)


def _attn_prompt_kernel(q_ref, kt_ref, v_ref, bias_ref, g_ref, o_ref, s_scr, p_scr):
    pairs = q_ref.shape[2] // PAIR
    slots = s_scr.shape[0]
    t = pl.program_id(1)
    for i in range(pairs):
        p = t * pairs + i
        ws = jnp.maximum(p - (WIN_SLABS - 1), 0)
        off = ws + (WIN_SLABS - 1) - p
        k0 = pl.multiple_of(ws * LANES, LANES)

        def get_k(hp, ws=ws):
            return jnp.concatenate(
                [kt_ref[0, ws + c, hp * LANES:(hp + 1) * LANES, :] for c in range(WIN_SLABS)], axis=1)

        def get_v(hp, k0=k0):
            lanes = slice(hp * LANES, (hp + 1) * LANES)
            return jnp.concatenate([v_ref[0, 0, pl.ds(k0, WIN), lanes], v_ref[0, 1, pl.ds(k0, WIN), lanes]],
                                   axis=0)

        rows = slice(i * PAIR, (i + 1) * PAIR)

        def get_q(hp, rows=rows):
            lanes = slice(hp * LANES, (hp + 1) * LANES)
            return jnp.concatenate([q_ref[0, 0, rows, lanes], q_ref[0, 1, rows, lanes]], axis=0)

        k = i % slots
        a = _attend(PAIR, get_q, get_k, get_v, bias_ref, off, s_scr.at[k], p_scr.at[k])
        o_ref[0, rows, :] = _rms(a, g_ref[...]).astype(BF16)


def _attn_prompt(q, kt, v, bias, g_att, *, tq):
    nb, _, t_len, d_att = q.shape
    n_heads = d_att // HEAD_DIM
    assert t_len % tq == 0 and tq % PAIR == 0 and t_len >= WIN
    return pl.pallas_call(
        _attn_prompt_kernel,
        grid=(nb, t_len // tq),
        in_specs=[pl.BlockSpec((1, 2, tq, d_att), lambda b, t: (b, 0, t, 0)),
                  pl.BlockSpec((1, t_len // LANES, d_att, LANES), lambda b, t: (b, 0, 0, 0)),
                  pl.BlockSpec((1, 2, t_len, d_att), lambda b, t: (b, 0, 0, 0)),
                  _const_spec(bias.shape), _const_spec((1, d_att))],
        out_specs=pl.BlockSpec((1, tq, d_att), lambda b, t: (b, t, 0)),
        out_shape=jax.ShapeDtypeStruct((nb, t_len, d_att), BF16),
        scratch_shapes=_attn_scratch(n_heads, slots=2),
        compiler_params=pltpu.CompilerParams(
            dimension_semantics=("arbitrary", "arbitrary"), vmem_limit_bytes=VMEM_LIMIT),
        name="attn_prompt",
    )(q, kt, v, bias, g_att)


def _attn_sample_kernel(q_ref, ck_ref, cv_ref, kn_ref, vn_ref, bias_ref, g_ref, o_ref, kt_s, v_s,
                        s_scr, p_scr):
    d_att = q_ref.shape[-1]
    pad = jnp.zeros((WIN - ck_ref.shape[1] - kn_ref.shape[1], d_att), F32)
    k_all = jnp.concatenate([ck_ref[0], kn_ref[0], pad], axis=0)
    for c in range(WIN_SLABS):
        kt_s[c] = k_all[c * LANES:(c + 1) * LANES, :].T.astype(BF16)
    v_all = jnp.concatenate([cv_ref[0], vn_ref[0], pad], axis=0)
    even = (lax.broadcasted_iota(jnp.int32, v_all.shape, 1) & HEAD_DIM) == 0
    v_s[0] = jnp.where(even, v_all, 0.0).astype(BF16)
    v_s[1] = jnp.where(even, 0.0, v_all).astype(BF16)

    def get_k(hp):
        return jnp.concatenate(
            [kt_s[c, hp * LANES:(hp + 1) * LANES, :] for c in range(WIN_SLABS)], axis=1)

    def get_v(hp):
        lanes = slice(hp * LANES, (hp + 1) * LANES)
        return jnp.concatenate([v_s[0, :, lanes], v_s[1, :, lanes]], axis=0)

    def get_q(hp):
        lanes = slice(hp * LANES, (hp + 1) * LANES)
        return jnp.concatenate([q_ref[0, 0, :, lanes], q_ref[0, 1, :, lanes]], axis=0)

    a = _attend(q_ref.shape[2], get_q, get_k, get_v, bias_ref, 0, s_scr.at[0], p_scr.at[0])
    o_ref[0] = _rms(a, g_ref[...]).astype(BF16)


def _attn_sample(q, cache_k, cache_v, k_new, v_new, bias, g_att):
    nb, s_len, d_att = k_new.shape
    n_heads = d_att // HEAD_DIM
    l_cache = cache_k.shape[1]
    assert s_len == CHUNK and l_cache == LEFT_CHUNKS * CHUNK
    per_b = lambda b: (b, 0, 0)
    return pl.pallas_call(
        _attn_sample_kernel,
        grid=(nb,),
        in_specs=[pl.BlockSpec((1, 2, s_len, d_att), lambda b: (0, 0, b, 0)),
                  pl.BlockSpec((1, l_cache, d_att), per_b), pl.BlockSpec((1, l_cache, d_att), per_b),
                  pl.BlockSpec((1, s_len, d_att), per_b), pl.BlockSpec((1, s_len, d_att), per_b),
                  _const_spec(bias.shape), _const_spec((1, d_att))],
        out_specs=pl.BlockSpec((1, s_len, d_att), per_b),
        out_shape=jax.ShapeDtypeStruct((nb, s_len, d_att), BF16),
        scratch_shapes=[pltpu.VMEM((WIN_SLABS, d_att, LANES), BF16), pltpu.VMEM((2, WIN, d_att), BF16)]
                       + _attn_scratch(n_heads, slots=1),
        compiler_params=pltpu.CompilerParams(
            dimension_semantics=("arbitrary",), vmem_limit_bytes=VMEM_LIMIT),
        name="attn_sample",
    )(q, cache_k, cache_v, k_new, v_new, bias, g_att)


def _conv_stage(u_ref, halo_ref, no_history, ubuf, y_scr, cn_ref, wdw_ref, bdw_ref, gln_ref, bln_ref, gco_ref):
    n_seg, seg, d_conv = u_ref.shape
    hr = halo_ref.shape[1]
    off = HALO - (CONV_WIDTH - 1)
    n_a = HALO // SUBLANES + 1
    for s in range(n_seg):
        hal = halo_ref[s]
        if no_history is not None:
            hal = jnp.where(no_history, 0.0, hal)
        if hr < HALO:
            ubuf[s, 0:HALO - hr, :] = jnp.zeros((HALO - hr, d_conv), F32)
        ubuf[s, HALO - hr:HALO, :] = hal
        ubuf[s, HALO:HALO + seg, :] = u_ref[s]
        ubuf[s, HALO + seg:HALO + seg + SUBLANES, :] = jnp.zeros((SUBLANES, d_conv), F32)

    sub = lax.broadcasted_iota(jnp.int32, (SUBLANES, LANES), 0)
    from_lower = [sub < SUBLANES - b for b in range(SUBLANES)]
    for lt in range(d_conv // LANES):
        lanes = slice(lt * LANES, (lt + 1) * LANES)
        w = [jnp.broadcast_to(wdw_ref[j:j + 1, lanes], (SUBLANES, LANES)) for j in range(CONV_WIDTH)]

        def rotated_groups(s, blk):
            rows = [ubuf[s, SUBLANES * (blk + a):SUBLANES * (blk + a + 1), lanes] for a in range(n_a)]
            out = []
            for b in range(SUBLANES):
                z = None
                for a in range(n_a):
                    j = SUBLANES * a + b - off
                    if 0 <= j < CONV_WIDTH:
                        t = w[j] * rows[a]
                        z = t if z is None else z + t
                out.append(z if b == 0 else pltpu.roll(z, SUBLANES - b, 0))
            return out

        for s in range(n_seg):
            lower = rotated_groups(s, 0)
            for blk in range(seg // SUBLANES):
                upper = rotated_groups(s, blk + 1)
                y = lower[0]
                for b in range(1, SUBLANES):
                    y = y + jnp.where(from_lower[b], lower[b], upper[b])
                y_scr[s * seg + blk * SUBLANES:s * seg + (blk + 1) * SUBLANES, lanes] = y
                lower = upper

    for r0 in range(0, n_seg * seg, NORM_ROWS):
        y = y_scr[r0:r0 + NORM_ROWS, :] + bdw_ref[...]
        mu = jnp.mean(y, axis=-1, keepdims=True)
        yc = y - mu
        yn = yc * lax.rsqrt(jnp.mean(yc * yc, axis=-1, keepdims=True) + EPS) * gln_ref[...] + bln_ref[...]
        c = yn * jax.nn.sigmoid(yn)
        cn_ref[r0:r0 + NORM_ROWS, :] = _rms(c, gco_ref[...]).astype(BF16)


def _dense_stage(x, cn, an, p, wout_ref, gffn_ref, w1_ref, w2_ref, gple_ref, wg_ref, wple_ref, ff_chunk):
    d_conv = cn.shape[-1]
    x1 = (x + jnp.dot(cn, wout_ref[0:d_conv, :], preferred_element_type=F32)
          + jnp.dot(an, wout_ref[d_conv:, :], preferred_element_type=F32))
    hb = _rms(x1, gffn_ref[...]).astype(BF16)
    ffn = None
    for lo in range(0, w1_ref.shape[1], ff_chunk):
        a = jnp.maximum(jnp.dot(hb, w1_ref[:, lo:lo + ff_chunk], preferred_element_type=F32), 0.0)
        part = jnp.dot((a * a).astype(BF16), w2_ref[lo:lo + ff_chunk, :], preferred_element_type=F32)
        ffn = part if ffn is None else ffn + part
    x2 = x1 + ffn
    gate = jax.nn.sigmoid(jnp.dot(_rms(x2, gple_ref[...]).astype(BF16), wg_ref[...],
                                  preferred_element_type=F32))
    ple = jnp.dot(p.astype(BF16), wple_ref[...], preferred_element_type=F32)
    return x2 + ple * gate


def _mix_kernel(*refs, lookahead, tiles_per_seq, n_split, ff_chunk):
    if lookahead:
        x_ref, p_ref, an_ref, u0_ref, un_ref, hn_ref = refs[:6]
        conv_w, dense_w = refs[6:11], refs[11:18]
        o_ref, ubuf, y_scr, cn_scr = refs[18:]
        i = pl.program_id(0)

        @pl.when(i == 0)
        def _():
            _conv_stage(u0_ref, hn_ref, True, ubuf, y_scr, cn_scr, *conv_w)
    else:
        x_ref, p_ref, an_ref, u_ref, halo_ref = refs[:5]
        conv_w, dense_w = refs[5:10], refs[10:17]
        o_ref, ubuf, y_scr, cn_scr = refs[17:]
        _conv_stage(u_ref, halo_ref, None, ubuf, y_scr, cn_scr, *conv_w)

    tm = x_ref.shape[1]
    hm = tm // n_split
    for h in range(n_split):
        r = slice(h * hm, (h + 1) * hm)
        o_ref[0, r, :] = _dense_stage(x_ref[0, r, :], cn_scr[r, :], an_ref[0, r, :], p_ref[0, r, :],
                                      *dense_w, ff_chunk)

    if lookahead:
        nxt_starts_seq = lax.rem(i + 1, tiles_per_seq) == 0
        _conv_stage(un_ref, hn_ref, nxt_starts_seq, ubuf, y_scr, cn_scr, *conv_w)


def _mix(x, p, an, u, halo_src, w_dw, b_dw, g_ln, b_ln, g_co, w_out, g_ffn, w1, w2, g_ple, w_gate, w_ple,
         *, tm, lookahead):
    ng, rows, d_model = x.shape
    d_conv = u.shape[-1]
    n_t = rows // tm
    n_steps = ng * n_t
    assert rows % tm == 0 and tm % HALO == 0
    cur = lambda i: (i // n_t, i % n_t, 0)
    if lookahead:
        n_seg, seg = 1, tm
        per = tm // HALO
        nxt = lambda i: jnp.minimum(i + 1, n_steps - 1)
        conv_in = [u, u, halo_src]
        conv_specs = [pl.BlockSpec((1, tm, d_conv), lambda i: (0, 0, 0), pipeline_mode=pl.Buffered(1)),
                      pl.BlockSpec((1, tm, d_conv), lambda i: (nxt(i) // n_t, nxt(i) % n_t, 0)),
                      pl.BlockSpec((1, HALO, d_conv),
                                   lambda i: (nxt(i) // n_t, jnp.maximum((nxt(i) % n_t) * per - 1, 0), 0))]
    else:
        n_seg, seg = u.shape[0], u.shape[1]
        assert n_steps == 1 and n_seg * seg == tm and halo_src.shape[1] == CONV_WIDTH - 1
        conv_in = [u, halo_src]
        conv_specs = [pl.BlockSpec(u.shape, lambda i: (0, 0, 0)),
                      pl.BlockSpec(halo_src.shape, lambda i: (0, 0, 0))]
    consts = [w_dw, b_dw, g_ln, b_ln, g_co, w_out, g_ffn, w1, w2, g_ple, w_gate, w_ple]
    kern = functools.partial(_mix_kernel, lookahead=lookahead, tiles_per_seq=n_t, n_split=2, ff_chunk=1024)
    return pl.pallas_call(
        kern,
        grid=(n_steps,),
        in_specs=[pl.BlockSpec((1, tm, d_model), cur), pl.BlockSpec((1, tm, p.shape[-1]), cur),
                  pl.BlockSpec((1, tm, an.shape[-1]), cur)] + conv_specs
                 + [_const_spec(c.shape) for c in consts],
        out_specs=pl.BlockSpec((1, tm, d_model), cur),
        out_shape=jax.ShapeDtypeStruct(x.shape, F32),
        scratch_shapes=[pltpu.VMEM((n_seg, HALO + seg + SUBLANES, d_conv), F32),
                        pltpu.VMEM((tm, d_conv), F32), pltpu.VMEM((tm, d_conv), BF16)],
        compiler_params=pltpu.CompilerParams(
            dimension_semantics=("arbitrary",), vmem_limit_bytes=VMEM_LIMIT),
        name="mix",
    )(x, p, an, *conv_in, *consts)


def kernel(x_prompt, x_sample, p_prompt, p_sample, cache_att_k, cache_att_v, state_conv, g_mix, w_in, w_dw,
           b_dw, g_conv_ln, b_conv_ln, g_q, g_k, rel_bias, g_conv_out, g_att_out, w_out, g_ffn, w_ff1, w_ff2,
           g_ple, w_gate, w_ple):
    depth = w_in.shape[0]
    nb, t_len, d_model = x_prompt.shape
    sb, s_len, _ = x_sample.shape
    d_conv = w_dw.shape[-1]
    d_att = g_att_out.shape[-1]
    n_heads = d_att // HEAD_DIM
    keep = min(LEFT_CHUNKS * CHUNK, t_len)
    tile = 512

    xp = x_prompt
    xs = x_sample.reshape(1, sb * s_len, d_model)
    outs = [[] for _ in range(6)]
    for i in range(depth):
        vec = lambda a: a[i].reshape(1, -1)
        gq = jnp.tile(g_q[i], n_heads).reshape(1, d_att)
        gk = jnp.tile(g_k[i], n_heads).reshape(1, d_att)
        w_in_b = w_in[i].astype(BF16)
        mix_w = (w_dw[i], vec(b_dw), vec(g_conv_ln), vec(b_conv_ln), vec(g_conv_out), w_out[i].astype(BF16),
                 vec(g_ffn), w_ff1[i].astype(BF16), w_ff2[i].astype(BF16), vec(g_ple),
                 w_gate[i].astype(BF16), w_ple[i].astype(BF16))
        bias = _bias_tiles(rel_bias[i])

        u_p, q_p, kt_p, vb_p, k_tail, v_tail = _in_proj(xp, vec(g_mix), w_in_b, gq, gk,
                                                        tm=tile, keep=keep, with_kt=True)
        an_p = _attn_prompt(q_p, kt_p, vb_p, bias, vec(g_att_out), tq=tile)
        xp = _mix(xp, p_prompt[i], an_p, u_p, u_p, *mix_w, tm=tile, lookahead=True)
        outs[0].append(k_tail.reshape(nb, keep, n_heads, HEAD_DIM))
        outs[1].append(v_tail.reshape(nb, keep, n_heads, HEAD_DIM))
        outs[2].append(u_p[:, t_len - (CONV_WIDTH - 1):])

        rows = sb * s_len
        u_s, q_s, k_s, v_s = _in_proj(xs, vec(g_mix), w_in_b, gq, gk, tm=rows, keep=rows, with_kt=False)
        per_b = lambda a: a.reshape(sb, s_len, a.shape[-1])
        ck = cache_att_k[i].reshape(sb, -1, d_att)
        cv = cache_att_v[i].reshape(sb, -1, d_att)
        an_s = _attn_sample(q_s, ck, cv, per_b(k_s), per_b(v_s), bias, vec(g_att_out))
        xs = _mix(xs, p_sample[i].reshape(1, rows, -1), an_s.reshape(1, rows, d_att), per_b(u_s),
                  state_conv[i], *mix_w, tm=rows, lookahead=False)
        outs[3].append(k_s.reshape(sb, s_len, n_heads, HEAD_DIM))
        outs[4].append(v_s.reshape(sb, s_len, n_heads, HEAD_DIM))
        conv_in_tail = jnp.concatenate([state_conv[i], per_b(u_s)], axis=1)[:, -(CONV_WIDTH - 1):]
        outs[5].append(conv_in_tail)

    return (xp, xs.reshape(sb, s_len, d_model)) + tuple(jnp.stack(o) for o in outs)
```

```python
import functools

import jax
import jax.numpy as jnp
from jax import lax
from jax.experimental import pallas as pl
from jax.experimental.pallas import tpu as pltpu

F32 = jnp.float32
BF16 = jnp.bfloat16

CHUNK = 64
LEFT_CHUNKS = 8
HEAD_DIM = 64
CONV_WIDTH = 31
MAX_REL = 128
EPS = 1e-6
NEG = -1e30

LANES = 128
SUBLANES = 8
PAIR = 2 * CHUNK
WIN = (LEFT_CHUNKS + 2) * CHUNK
WIN_SLABS = WIN // LANES
BIAS_SLABS = 2 * WIN_SLABS - 1
HALO = 32
NORM_ROWS = 32
SOFTMAX_ROWS = 16
VMEM_LIMIT = 56 * 1024 * 1024


def _rms(x, g):
    return x * lax.rsqrt(jnp.mean(x * x, axis=-1, keepdims=True) + EPS) * g


def _const_spec(shape):
    zeros = (0,) * len(shape)
    return pl.BlockSpec(shape, lambda *_: zeros, pipeline_mode=pl.Buffered(1))


def _bias_kernel(rb_ref, o_ref):
    n_heads = rb_ref.shape[0]
    rb = rb_ref[...]
    lane = lax.broadcasted_iota(jnp.int32, (n_heads, LANES), 1)
    b0 = jnp.broadcast_to(rb[:, 0:1], (n_heads, LANES))
    t4 = jnp.where(lane < CHUNK, rb[:, LANES:2 * LANES], b0)
    t = jnp.concatenate([b0, b0, b0, rb[:, 0:LANES], t4, b0], axis=1)
    r_i = lax.broadcasted_iota(jnp.int32, (PAIR, LANES), 0)
    c_i = lax.broadcasted_iota(jnp.int32, (PAIR, LANES), 1)
    for h in range(n_heads):
        row = jnp.broadcast_to(t[h:h + 1, :], (PAIR, t.shape[1]))
        toe = pltpu.roll(row, 0, 1, stride=1, stride_axis=0)
        for c in range(WIN_SLABS):
            jj = c_i + c * LANES
            valid = ((r_i < CHUNK) & (jj < WIN - CHUNK)) | ((r_i >= CHUNK) & (jj >= CHUNK))
            o_ref[h, c] = jnp.where(valid, toe[:, c * LANES:(c + 1) * LANES], NEG)
        for c in range(WIN_SLABS, BIAS_SLABS):
            o_ref[h, c] = jnp.full((PAIR, LANES), NEG, F32)


def _bias_tiles(rel_bias):
    n_heads = rel_bias.shape[0]
    return pl.pallas_call(
        _bias_kernel,
        out_shape=jax.ShapeDtypeStruct((n_heads, BIAS_SLABS, PAIR, LANES), F32),
        name="bias_tiles",
    )(rel_bias)


def _conv_stage(u, n_seg, halo_ref, ubuf, y_scr, cn_ref, wdw_ref, bdw_ref, gln_ref, bln_ref, gco_ref):
    d_conv = u.shape[-1]
    seg = u.shape[0] // n_seg
    off = HALO - (CONV_WIDTH - 1)
    n_a = HALO // SUBLANES + 1
    for s in range(n_seg):
        if halo_ref is not None:
            hr = halo_ref.shape[1]
            if hr < HALO:
                ubuf[s, 0:HALO - hr, :] = jnp.zeros((HALO - hr, d_conv), F32)
            ubuf[s, HALO - hr:HALO, :] = halo_ref[s]
        ubuf[s, HALO:HALO + seg, :] = u[s * seg:(s + 1) * seg, :]
        ubuf[s, HALO + seg:HALO + seg + SUBLANES, :] = jnp.zeros((SUBLANES, d_conv), F32)

    sub = lax.broadcasted_iota(jnp.int32, (SUBLANES, LANES), 0)
    from_lower = [sub < SUBLANES - b for b in range(SUBLANES)]
    for lt in range(d_conv // LANES):
        lanes = slice(lt * LANES, (lt + 1) * LANES)
        w = [jnp.broadcast_to(wdw_ref[j:j + 1, lanes], (SUBLANES, LANES)) for j in range(CONV_WIDTH)]

        def rotated_groups(s, blk):
            rows = [ubuf[s, SUBLANES * (blk + a):SUBLANES * (blk + a + 1), lanes] for a in range(n_a)]
            out = []
            for b in range(SUBLANES):
                z = None
                for a in range(n_a):
                    j = SUBLANES * a + b - off
                    if 0 <= j < CONV_WIDTH:
                        t = w[j] * rows[a]
                        z = t if z is None else z + t
                out.append(z if b == 0 else pltpu.roll(z, SUBLANES - b, 0))
            return out

        for s in range(n_seg):
            lower = rotated_groups(s, 0)
            for blk in range(seg // SUBLANES):
                upper = rotated_groups(s, blk + 1)
                y = lower[0]
                for b in range(1, SUBLANES):
                    y = y + jnp.where(from_lower[b], lower[b], upper[b])
                y_scr[s * seg + blk * SUBLANES:s * seg + (blk + 1) * SUBLANES, lanes] = y
                lower = upper

    for r0 in range(0, n_seg * seg, NORM_ROWS):
        y = y_scr[r0:r0 + NORM_ROWS, :] + bdw_ref[...]
        mu = jnp.mean(y, axis=-1, keepdims=True)
        yc = y - mu
        yn = yc * lax.rsqrt(jnp.mean(yc * yc, axis=-1, keepdims=True) + EPS) * gln_ref[...] + bln_ref[...]
        c = yn * jax.nn.sigmoid(yn)
        cn_ref[r0:r0 + NORM_ROWS, :] = _rms(c, gco_ref[...]).astype(BF16)


def _inproj_kernel(*refs, d_conv, d_att, tail0, prompt):
    if prompt:
        x_ref, gmix_ref, win_ref, gq_ref, gk_ref = refs[:5]
        halo_ref, conv_w = None, refs[5:10]
        cn_ref, ut_ref, q_ref, kt_ref, vb_ref, k32_ref, v32_ref, ubuf, y_scr = refs[10:]
        t = pl.program_id(1)
        tm = x_ref.shape[1]

        @pl.when(t == 0)
        def _():
            ubuf[0, 0:HALO, :] = jnp.zeros((HALO, d_conv), F32)

        @pl.when(t > 0)
        def _():
            ubuf[0, 0:HALO, :] = ubuf[0, tm:tm + HALO, :]
    else:
        x_ref, gmix_ref, win_ref, gq_ref, gk_ref, halo_ref = refs[:6]
        conv_w = refs[6:11]
        cn_ref, u_ref, q_ref, k32_ref, v32_ref, ubuf, y_scr = refs[11:]
        tm = x_ref.shape[1]
    h = _rms(x_ref[0], gmix_ref[...]).astype(BF16)

    def proj(lo, width):
        return jnp.dot(h, win_ref[:, lo:lo + width], preferred_element_type=F32)

    u = proj(0, d_conv) * jax.nn.sigmoid(proj(d_conv, d_conv))
    if prompt:
        @pl.when(t == pl.num_programs(1) - 1)
        def _():
            ut_ref[0] = u[tm - HALO:tm, :]
    else:
        u_ref[0] = u
    _conv_stage(u, ubuf.shape[0], halo_ref, ubuf, y_scr, cn_ref.at[0], *conv_w)

    r_i = lax.shift_right_logical(lax.broadcasted_iota(jnp.int32, (d_att, d_att), 0), 6)
    c_i = lax.shift_right_logical(lax.broadcasted_iota(jnp.int32, (d_att, d_att), 1), 6)
    avg = jnp.where(r_i == c_i, 1.0 / HEAD_DIM, 0.0).astype(BF16)

    def head_rms(z, g):
        ms = jnp.dot((z * z).astype(BF16), avg, preferred_element_type=F32)
        return z * lax.rsqrt(ms + EPS) * g

    o = 2 * d_conv
    even = (lax.broadcasted_iota(jnp.int32, (tm, d_att), 1) & HEAD_DIM) == 0

    def split_heads(ref, z):
        ref[0, 0] = jnp.where(even, z, 0.0).astype(BF16)
        ref[0, 1] = jnp.where(even, 0.0, z).astype(BF16)

    qn = head_rms(proj(o, d_att), gq_ref[...])
    split_heads(q_ref, qn * (HEAD_DIM ** -0.5))
    kn = head_rms(proj(o + d_att, d_att), gk_ref[...])
    v = proj(o + 2 * d_att, d_att)
    if prompt:
        for s in range(tm // LANES):
            kt_ref[0, s] = kn[s * LANES:(s + 1) * LANES, :].T.astype(BF16)
        split_heads(vb_ref, v)

    @pl.when(pl.program_id(1) >= tail0)
    def _():
        k32_ref[0] = kn
        v32_ref[0] = v


def _in_proj(x, g_mix, w_in, g_q, g_k, conv_w, *, tm, keep, state=None):
    nb, t_len, d_model = x.shape
    d_att = g_q.shape[-1]
    d_in = w_in.shape[-1]
    d_conv = (d_in - 3 * d_att) // 2
    n_t = t_len // tm
    prompt = state is None
    assert t_len % tm == 0 and (t_len - keep) % tm == 0 and tm % LANES == 0 and tm >= HALO
    tail0 = (t_len - keep) // tm
    row = lambda b, t: (b, t, 0)
    tail = lambda b, t: (b, jnp.maximum(t - tail0, 0), 0)
    halves = pl.BlockSpec((1, 2, tm, d_att), lambda b, t: (b, 0, t, 0))
    out_shape = [jax.ShapeDtypeStruct((nb, t_len, d_conv), BF16)]
    out_specs = [pl.BlockSpec((1, tm, d_conv), row)]
    if prompt:
        n_seg = 1
        out_shape += [jax.ShapeDtypeStruct((nb, HALO, d_conv), F32)]
        out_specs += [pl.BlockSpec((1, HALO, d_conv), lambda b, t: (b, 0, 0))]
    else:
        n_seg = state.shape[0]
        assert nb == 1 and n_t == 1 and tm % n_seg == 0 and state.shape[1] == CONV_WIDTH - 1
        out_shape += [jax.ShapeDtypeStruct((nb, t_len, d_conv), F32)]
        out_specs += [pl.BlockSpec((1, tm, d_conv), row)]
    out_shape += [jax.ShapeDtypeStruct((nb, 2, t_len, d_att), BF16)]
    out_specs += [halves]
    if prompt:
        out_shape += [jax.ShapeDtypeStruct((nb, t_len // LANES, d_att, LANES), BF16),
                      jax.ShapeDtypeStruct((nb, 2, t_len, d_att), BF16)]
        out_specs += [pl.BlockSpec((1, tm // LANES, d_att, LANES), lambda b, t: (b, t, 0, 0)), halves]
    out_shape += [jax.ShapeDtypeStruct((nb, keep, d_att), F32)] * 2
    out_specs += [pl.BlockSpec((1, tm, d_att), tail)] * 2
    kern = functools.partial(_inproj_kernel, d_conv=d_conv, d_att=d_att, tail0=tail0, prompt=prompt)
    extra = [] if prompt else [state]
    return pl.pallas_call(
        kern,
        grid=(nb, n_t),
        in_specs=[pl.BlockSpec((1, tm, d_model), row),
                  _const_spec((1, d_model)), _const_spec((d_model, d_in)),
                  _const_spec((1, d_att)), _const_spec((1, d_att))]
                 + [_const_spec(a.shape) for a in extra + list(conv_w)],
        out_specs=out_specs,
        out_shape=out_shape,
        scratch_shapes=[pltpu.VMEM((n_seg, HALO + tm // n_seg + SUBLANES, d_conv), F32),
                        pltpu.VMEM((tm, d_conv), F32)],
        compiler_params=pltpu.CompilerParams(
            dimension_semantics=("arbitrary", "arbitrary"), vmem_limit_bytes=VMEM_LIMIT),
        name="in_proj",
    )(x, g_mix, w_in, g_q, g_k, *extra, *conv_w)


def _attn_scratch(n_heads, slots):
    return [pltpu.VMEM((slots, n_heads // 2, 2 * PAIR, WIN), F32),
            pltpu.VMEM((slots, n_heads // 2, PAIR, 2 * WIN), BF16)]


def _attend(rows, get_q, get_k, get_v, bias_ref, off, s_scr, p_scr):
    n_pairs = s_scr.shape[0]
    for hp in range(n_pairs):
        s_scr[hp, 0:2 * rows, :] = jnp.dot(get_q(hp), get_k(hp), preferred_element_type=F32)
    for hp in range(n_pairs):
        for sub in range(2):
            h = 2 * hp + sub
            for r0 in range(0, rows, SOFTMAX_ROWS):
                r = slice(r0, r0 + SOFTMAX_ROWS)
                s = s_scr[hp, sub * rows + r0:sub * rows + r0 + SOFTMAX_ROWS, :]
                s = s + jnp.concatenate([bias_ref[h, off + c, r, :] for c in range(WIN_SLABS)], axis=1)
                e = jnp.exp(s - jnp.max(s, axis=-1, keepdims=True))
                p_scr[hp, r, sub * WIN:(sub + 1) * WIN] = e.astype(BF16)
    first = lax.broadcasted_iota(jnp.int32, (WIN, LANES), 1) < HEAD_DIM
    sums = jnp.concatenate([jnp.where(first, 1.0, 0.0), jnp.where(first, 0.0, 1.0)], axis=0).astype(BF16)
    outs = []
    for hp in range(n_pairs):
        o2 = jnp.dot(p_scr[hp, 0:rows, :], jnp.concatenate([get_v(hp), sums], axis=1),
                     preferred_element_type=F32)
        outs.append(o2[:, 0:LANES] * (1.0 / o2[:, LANES:2 * LANES]))
    return jnp.concatenate(outs, axis=1)


def _attn_prompt_kernel(q_ref, kt_ref, v_ref, bias_ref, g_ref, o_ref, s_scr, p_scr):
    pairs = q_ref.shape[2] // PAIR
    slots = s_scr.shape[0]
    t = pl.program_id(1)
    for i in range(pairs):
        p = t * pairs + i
        ws = jnp.maximum(p - (WIN_SLABS - 1), 0)
        off = ws + (WIN_SLABS - 1) - p
        k0 = pl.multiple_of(ws * LANES, LANES)

        def get_k(hp, ws=ws):
            return jnp.concatenate(
                [kt_ref[0, ws + c, hp * LANES:(hp + 1) * LANES, :] for c in range(WIN_SLABS)], axis=1)

        def get_v(hp, k0=k0):
            lanes = slice(hp * LANES, (hp + 1) * LANES)
            return jnp.concatenate([v_ref[0, 0, pl.ds(k0, WIN), lanes], v_ref[0, 1, pl.ds(k0, WIN), lanes]],
                                   axis=0)

        rows = slice(i * PAIR, (i + 1) * PAIR)

        def get_q(hp, rows=rows):
            lanes = slice(hp * LANES, (hp + 1) * LANES)
            return jnp.concatenate([q_ref[0, 0, rows, lanes], q_ref[0, 1, rows, lanes]], axis=0)

        k = i % slots
        a = _attend(PAIR, get_q, get_k, get_v, bias_ref, off, s_scr.at[k], p_scr.at[k])
        o_ref[0, rows, :] = _rms(a, g_ref[...]).astype(BF16)


def _attn_prompt(q, kt, v, bias, g_att, *, tq):
    nb, _, t_len, d_att = q.shape
    n_heads = d_att // HEAD_DIM
    assert t_len % tq == 0 and tq % PAIR == 0 and t_len >= WIN
    return pl.pallas_call(
        _attn_prompt_kernel,
        grid=(nb, t_len // tq),
        in_specs=[pl.BlockSpec((1, 2, tq, d_att), lambda b, t: (b, 0, t, 0)),
                  pl.BlockSpec((1, t_len // LANES, d_att, LANES), lambda b, t: (b, 0, 0, 0)),
                  pl.BlockSpec((1, 2, t_len, d_att), lambda b, t: (b, 0, 0, 0)),
                  _const_spec(bias.shape), _const_spec((1, d_att))],
        out_specs=pl.BlockSpec((1, tq, d_att), lambda b, t: (b, t, 0)),
        out_shape=jax.ShapeDtypeStruct((nb, t_len, d_att), BF16),
        scratch_shapes=_attn_scratch(n_heads, slots=2),
        compiler_params=pltpu.CompilerParams(
            dimension_semantics=("arbitrary", "arbitrary"), vmem_limit_bytes=VMEM_LIMIT),
        name="attn_prompt",
    )(q, kt, v, bias, g_att)


def _attn_sample_kernel(q_ref, ck_ref, cv_ref, kn_ref, vn_ref, bias_ref, g_ref, o_ref, kt_s, v_s,
                        s_scr, p_scr):
    d_att = q_ref.shape[-1]
    pad = jnp.zeros((WIN - ck_ref.shape[1] - kn_ref.shape[1], d_att), F32)
    k_all = jnp.concatenate([ck_ref[0], kn_ref[0], pad], axis=0)
    for c in range(WIN_SLABS):
        kt_s[c] = k_all[c * LANES:(c + 1) * LANES, :].T.astype(BF16)
    v_all = jnp.concatenate([cv_ref[0], vn_ref[0], pad], axis=0)
    even = (lax.broadcasted_iota(jnp.int32, v_all.shape, 1) & HEAD_DIM) == 0
    v_s[0] = jnp.where(even, v_all, 0.0).astype(BF16)
    v_s[1] = jnp.where(even, 0.0, v_all).astype(BF16)

    def get_k(hp):
        return jnp.concatenate(
            [kt_s[c, hp * LANES:(hp + 1) * LANES, :] for c in range(WIN_SLABS)], axis=1)

    def get_v(hp):
        lanes = slice(hp * LANES, (hp + 1) * LANES)
        return jnp.concatenate([v_s[0, :, lanes], v_s[1, :, lanes]], axis=0)

    def get_q(hp):
        lanes = slice(hp * LANES, (hp + 1) * LANES)
        return jnp.concatenate([q_ref[0, 0, :, lanes], q_ref[0, 1, :, lanes]], axis=0)

    a = _attend(q_ref.shape[2], get_q, get_k, get_v, bias_ref, 0, s_scr.at[0], p_scr.at[0])
    o_ref[0] = _rms(a, g_ref[...]).astype(BF16)


def _attn_sample(q, cache_k, cache_v, k_new, v_new, bias, g_att):
    nb, s_len, d_att = k_new.shape
    n_heads = d_att // HEAD_DIM
    l_cache = cache_k.shape[1]
    assert s_len == CHUNK and l_cache == LEFT_CHUNKS * CHUNK
    per_b = lambda b: (b, 0, 0)
    return pl.pallas_call(
        _attn_sample_kernel,
        grid=(nb,),
        in_specs=[pl.BlockSpec((1, 2, s_len, d_att), lambda b: (0, 0, b, 0)),
                  pl.BlockSpec((1, l_cache, d_att), per_b), pl.BlockSpec((1, l_cache, d_att), per_b),
                  pl.BlockSpec((1, s_len, d_att), per_b), pl.BlockSpec((1, s_len, d_att), per_b),
                  _const_spec(bias.shape), _const_spec((1, d_att))],
        out_specs=pl.BlockSpec((1, s_len, d_att), per_b),
        out_shape=jax.ShapeDtypeStruct((nb, s_len, d_att), BF16),
        scratch_shapes=[pltpu.VMEM((WIN_SLABS, d_att, LANES), BF16), pltpu.VMEM((2, WIN, d_att), BF16)]
                       + _attn_scratch(n_heads, slots=1),
        compiler_params=pltpu.CompilerParams(
            dimension_semantics=("arbitrary",), vmem_limit_bytes=VMEM_LIMIT),
        name="attn_sample",
    )(q, cache_k, cache_v, k_new, v_new, bias, g_att)


def _dense_stage(x, cn, an, p, wout_ref, gffn_ref, w1_ref, w2_ref, gple_ref, wg_ref, wple_ref, ff_chunk):
    d_conv = cn.shape[-1]
    x1 = (x + jnp.dot(cn, wout_ref[0:d_conv, :], preferred_element_type=F32)
          + jnp.dot(an, wout_ref[d_conv:, :], preferred_element_type=F32))
    hb = _rms(x1, gffn_ref[...]).astype(BF16)
    ffn = None
    for lo in range(0, w1_ref.shape[1], ff_chunk):
        a = jnp.maximum(jnp.dot(hb, w1_ref[:, lo:lo + ff_chunk], preferred_element_type=F32), 0.0)
        part = jnp.dot((a * a).astype(BF16), w2_ref[lo:lo + ff_chunk, :], preferred_element_type=F32)
        ffn = part if ffn is None else ffn + part
    x2 = x1 + ffn
    gate = jax.nn.sigmoid(jnp.dot(_rms(x2, gple_ref[...]).astype(BF16), wg_ref[...],
                                  preferred_element_type=F32))
    ple = jnp.dot(p.astype(BF16), wple_ref[...], preferred_element_type=F32)
    return x2 + ple * gate


def _mix_kernel(x_ref, p_ref, an_ref, cn_ref, *refs, n_split, ff_chunk):
    dense_w, o_ref = refs[:-1], refs[-1]
    hm = x_ref.shape[1] // n_split
    for h in range(n_split):
        r = slice(h * hm, (h + 1) * hm)
        o_ref[0, r, :] = _dense_stage(x_ref[0, r, :], cn_ref[0, r, :], an_ref[0, r, :], p_ref[0, r, :],
                                      *dense_w, ff_chunk)


def _mix(x, p, an, cn, w_out, g_ffn, w1, w2, g_ple, w_gate, w_ple, *, tm):
    ng, rows, d_model = x.shape
    assert rows % tm == 0
    row = lambda g, t: (g, t, 0)
    consts = [w_out, g_ffn, w1, w2, g_ple, w_gate, w_ple]
    return pl.pallas_call(
        functools.partial(_mix_kernel, n_split=2, ff_chunk=1024),
        grid=(ng, rows // tm),
        in_specs=[pl.BlockSpec((1, tm, a.shape[-1]), row) for a in (x, p, an, cn)]
                 + [_const_spec(c.shape) for c in consts],
        out_specs=pl.BlockSpec((1, tm, d_model), row),
        out_shape=jax.ShapeDtypeStruct(x.shape, F32),
        compiler_params=pltpu.CompilerParams(
            dimension_semantics=("arbitrary", "arbitrary"), vmem_limit_bytes=VMEM_LIMIT),
        name="mix",
    )(x, p, an, cn, *consts)


def kernel(x_prompt, x_sample, p_prompt, p_sample, cache_att_k, cache_att_v, state_conv, g_mix, w_in, w_dw,
           b_dw, g_conv_ln, b_conv_ln, g_q, g_k, rel_bias, g_conv_out, g_att_out, w_out, g_ffn, w_ff1, w_ff2,
           g_ple, w_gate, w_ple):
    depth = w_in.shape[0]
    nb, t_len, d_model = x_prompt.shape
    sb, s_len, _ = x_sample.shape
    d_conv = w_dw.shape[-1]
    d_att = g_att_out.shape[-1]
    n_heads = d_att // HEAD_DIM
    keep = min(LEFT_CHUNKS * CHUNK, t_len)
    tile = 512

    xp = x_prompt
    xs = x_sample.reshape(1, sb * s_len, d_model)
    outs = [[] for _ in range(6)]
    for i in range(depth):
        vec = lambda a: a[i].reshape(1, -1)
        gq = jnp.tile(g_q[i], n_heads).reshape(1, d_att)
        gk = jnp.tile(g_k[i], n_heads).reshape(1, d_att)
        w_in_b = w_in[i].astype(BF16)
        conv_w = (w_dw[i], vec(b_dw), vec(g_conv_ln), vec(b_conv_ln), vec(g_conv_out))
        mix_w = (w_out[i].astype(BF16), vec(g_ffn), w_ff1[i].astype(BF16), w_ff2[i].astype(BF16), vec(g_ple),
                 w_gate[i].astype(BF16), w_ple[i].astype(BF16))
        bias = _bias_tiles(rel_bias[i])

        cn_p, u_tail, q_p, kt_p, vb_p, k_tail, v_tail = _in_proj(xp, vec(g_mix), w_in_b, gq, gk, conv_w,
                                                                 tm=tile, keep=keep)
        an_p = _attn_prompt(q_p, kt_p, vb_p, bias, vec(g_att_out), tq=tile)
        xp = _mix(xp, p_prompt[i], an_p, cn_p, *mix_w, tm=tile)
        outs[0].append(k_tail.reshape(nb, keep, n_heads, HEAD_DIM))
        outs[1].append(v_tail.reshape(nb, keep, n_heads, HEAD_DIM))
        outs[2].append(u_tail[:, HALO - (CONV_WIDTH - 1):])

        rows = sb * s_len
        cn_s, u_s, q_s, k_s, v_s = _in_proj(xs, vec(g_mix), w_in_b, gq, gk, conv_w, tm=rows, keep=rows,
                                            state=state_conv[i])
        per_b = lambda a: a.reshape(sb, s_len, a.shape[-1])
        ck = cache_att_k[i].reshape(sb, -1, d_att)
        cv = cache_att_v[i].reshape(sb, -1, d_att)
        an_s = _attn_sample(q_s, ck, cv, per_b(k_s), per_b(v_s), bias, vec(g_att_out))
        xs = _mix(xs, p_sample[i].reshape(1, rows, -1), an_s.reshape(1, rows, d_att), cn_s, *mix_w, tm=rows)
        outs[3].append(k_s.reshape(sb, s_len, n_heads, HEAD_DIM))
        outs[4].append(v_s.reshape(sb, s_len, n_heads, HEAD_DIM))
        conv_in_tail = jnp.concatenate([state_conv[i], per_b(u_s)], axis=1)[:, -(CONV_WIDTH - 1):]
        outs[5].append(conv_in_tail)

    return (xp, xs.reshape(sb, s_len, d_model)) + tuple(jnp.stack(o) for o in outs)
```

```python
import functools

import jax
import jax.numpy as jnp
from jax import lax
from jax.experimental import pallas as pl
from jax.experimental.pallas import tpu as pltpu

F32 = jnp.float32
BF16 = jnp.bfloat16

CHUNK = 64
LEFT_CHUNKS = 8
HEAD_DIM = 64
CONV_WIDTH = 31
MAX_REL = 128
EPS = 1e-6
NEG = -1e30

LANES = 128
SUBLANES = 8
PAIR = 2 * CHUNK
WIN = (LEFT_CHUNKS + 2) * CHUNK
WIN_SLABS = WIN // LANES
BIAS_SLABS = 2 * WIN_SLABS - 1
HALO = 32
NORM_ROWS = 32
SOFTMAX_ROWS = 16
VMEM_LIMIT = 56 * 1024 * 1024


def _rms(x, g):
    return x * lax.rsqrt(jnp.mean(x * x, axis=-1, keepdims=True) + EPS) * g


def _const_spec(shape):
    zeros = (0,) * len(shape)
    return pl.BlockSpec(shape, lambda *_: zeros, pipeline_mode=pl.Buffered(1))


def _bias_kernel(rb_ref, o_ref):
    n_heads = rb_ref.shape[0]
    rb = rb_ref[...]
    lane = lax.broadcasted_iota(jnp.int32, (n_heads, LANES), 1)
    b0 = jnp.broadcast_to(rb[:, 0:1], (n_heads, LANES))
    t4 = jnp.where(lane < CHUNK, rb[:, LANES:2 * LANES], b0)
    t = jnp.concatenate([b0, b0, b0, rb[:, 0:LANES], t4, b0], axis=1)
    r_i = lax.broadcasted_iota(jnp.int32, (PAIR, LANES), 0)
    c_i = lax.broadcasted_iota(jnp.int32, (PAIR, LANES), 1)
    for h in range(n_heads):
        row = jnp.broadcast_to(t[h:h + 1, :], (PAIR, t.shape[1]))
        toe = pltpu.roll(row, 0, 1, stride=1, stride_axis=0)
        for c in range(WIN_SLABS):
            jj = c_i + c * LANES
            valid = ((r_i < CHUNK) & (jj < WIN - CHUNK)) | ((r_i >= CHUNK) & (jj >= CHUNK))
            o_ref[h, c] = jnp.where(valid, toe[:, c * LANES:(c + 1) * LANES], NEG)
        for c in range(WIN_SLABS, BIAS_SLABS):
            o_ref[h, c] = jnp.full((PAIR, LANES), NEG, F32)


def _bias_tiles(rel_bias):
    n_heads = rel_bias.shape[0]
    return pl.pallas_call(
        _bias_kernel,
        out_shape=jax.ShapeDtypeStruct((n_heads, BIAS_SLABS, PAIR, LANES), F32),
        name="bias_tiles",
    )(rel_bias)


def _conv_stage(u, n_seg, halo_ref, ubuf, y_scr, cn_ref, wdw_ref, bdw_ref, gln_ref, bln_ref, gco_ref):
    d_conv = u.shape[-1]
    seg = u.shape[0] // n_seg
    off = HALO - (CONV_WIDTH - 1)
    n_a = HALO // SUBLANES + 1
    for s in range(n_seg):
        if halo_ref is not None:
            hr = halo_ref.shape[1]
            if hr < HALO:
                ubuf[s, 0:HALO - hr, :] = jnp.zeros((HALO - hr, d_conv), F32)
            ubuf[s, HALO - hr:HALO, :] = halo_ref[s]
        ubuf[s, HALO:HALO + seg, :] = u[s * seg:(s + 1) * seg, :]
        ubuf[s, HALO + seg:HALO + seg + SUBLANES, :] = jnp.zeros((SUBLANES, d_conv), F32)

    sub = lax.broadcasted_iota(jnp.int32, (SUBLANES, LANES), 0)
    from_lower = [sub < SUBLANES - b for b in range(SUBLANES)]
    for lt in range(d_conv // LANES):
        lanes = slice(lt * LANES, (lt + 1) * LANES)

        w = [jnp.broadcast_to(wdw_ref[j:j + 1, lanes], (SUBLANES, LANES)) for j in range(CONV_WIDTH)]

        def rotated_groups(s, blk):
            rows = [ubuf[s, SUBLANES * (blk + a):SUBLANES * (blk + a + 1), lanes] for a in range(n_a)]
            out = []
            for b in range(SUBLANES):
                z = None
                for a in range(n_a):
                    j = SUBLANES * a + b - off
                    if 0 <= j < CONV_WIDTH:
                        t = w[j] * rows[a]
                        z = t if z is None else z + t
                out.append(z if b == 0 else pltpu.roll(z, SUBLANES - b, 0))
            return out

        for s in range(n_seg):
            lower = rotated_groups(s, 0)
            for blk in range(seg // SUBLANES):
                upper = rotated_groups(s, blk + 1)
                y = lower[0]
                for b in range(1, SUBLANES):
                    y = y + jnp.where(from_lower[b], lower[b], upper[b])
                y_scr[s * seg + blk * SUBLANES:s * seg + (blk + 1) * SUBLANES, lanes] = y
                lower = upper

    for r0 in range(0, n_seg * seg, NORM_ROWS):
        y = y_scr[r0:r0 + NORM_ROWS, :] + bdw_ref[...]
        mu = jnp.mean(y, axis=-1, keepdims=True)
        yc = y - mu
        yn = yc * lax.rsqrt(jnp.mean(yc * yc, axis=-1, keepdims=True) + EPS) * gln_ref[...] + bln_ref[...]
        c = yn * jax.nn.sigmoid(yn)
        cn_ref[r0:r0 + NORM_ROWS, :] = _rms(c, gco_ref[...]).astype(BF16)


def _inproj_kernel(*refs, d_conv, d_att, tail0, prompt, n_cast):
    if prompt:
        x_ref, gmix_ref, win_ref, gq_ref, gk_ref = refs[:5]
        halo_ref, conv_w = None, refs[5:10]
        cast_in = refs[10:10 + n_cast]
        cn_ref, ut_ref, q_ref, kt_ref, vb_ref, k32_ref, v32_ref = refs[10 + n_cast:17 + n_cast]
        cast_out = refs[17 + n_cast:17 + 2 * n_cast]
        ubuf, y_scr = refs[17 + 2 * n_cast:]
        t = pl.program_id(1)
        tm = x_ref.shape[1]

        for src, dst in zip(cast_in, cast_out):
            dst[...] = src[...].astype(BF16)

        @pl.when(t == 0)
        def _():
            ubuf[0, 0:HALO, :] = jnp.zeros((HALO, d_conv), F32)

        @pl.when(t > 0)
        def _():
            ubuf[0, 0:HALO, :] = ubuf[0, tm:tm + HALO, :]
    else:
        x_ref, gmix_ref, win_ref, gq_ref, gk_ref, halo_ref = refs[:6]
        conv_w = refs[6:11]
        cn_ref, u_ref, q_ref, k32_ref, v32_ref, ubuf, y_scr = refs[11:]
        tm = x_ref.shape[1]
    h = _rms(x_ref[0], gmix_ref[...]).astype(BF16)

    def proj(lo, width):
        return jnp.dot(h, win_ref[:, lo:lo + width], preferred_element_type=F32)

    u = proj(0, d_conv) * jax.nn.sigmoid(proj(d_conv, d_conv))
    if prompt:
        @pl.when(t == pl.num_programs(1) - 1)
        def _():
            ut_ref[0] = u[tm - HALO:tm, :]
    else:
        u_ref[0] = u
    _conv_stage(u, ubuf.shape[0], halo_ref, ubuf, y_scr, cn_ref.at[0], *conv_w)

    r_i = lax.shift_right_logical(lax.broadcasted_iota(jnp.int32, (d_att, d_att), 0), 6)
    c_i = lax.shift_right_logical(lax.broadcasted_iota(jnp.int32, (d_att, d_att), 1), 6)
    avg = jnp.where(r_i == c_i, 1.0 / HEAD_DIM, 0.0).astype(BF16)

    def head_rms(z, g):
        ms = jnp.dot((z * z).astype(BF16), avg, preferred_element_type=F32)
        return z * lax.rsqrt(ms + EPS) * g

    o = 2 * d_conv
    even = (lax.broadcasted_iota(jnp.int32, (tm, d_att), 1) & HEAD_DIM) == 0

    def split_heads(ref, z):
        ref[0, 0] = jnp.where(even, z, 0.0).astype(BF16)
        ref[0, 1] = jnp.where(even, 0.0, z).astype(BF16)

    qn = head_rms(proj(o, d_att), gq_ref[...])
    split_heads(q_ref, qn * (HEAD_DIM ** -0.5))
    kn = head_rms(proj(o + d_att, d_att), gk_ref[...])
    v = proj(o + 2 * d_att, d_att)
    if prompt:
        for s in range(tm // LANES):
            kt_ref[0, s] = kn[s * LANES:(s + 1) * LANES, :].T.astype(BF16)
        split_heads(vb_ref, v)

    @pl.when(pl.program_id(1) >= tail0)
    def _():
        k32_ref[0] = kn
        v32_ref[0] = v


def _in_proj(x, g_mix, w_in, g_q, g_k, conv_w, *, tm, keep, state=None, cast=()):
    nb, t_len, d_model = x.shape
    d_att = g_q.shape[-1]
    d_in = w_in.shape[-1]
    d_conv = (d_in - 3 * d_att) // 2
    n_t = t_len // tm
    prompt = state is None
    assert t_len % tm == 0 and (t_len - keep) % tm == 0 and tm % LANES == 0 and tm >= HALO
    tail0 = (t_len - keep) // tm
    row = lambda b, t: (b, t, 0)
    tail = lambda b, t: (b, jnp.maximum(t - tail0, 0), 0)
    halves = pl.BlockSpec((1, 2, tm, d_att), lambda b, t: (b, 0, t, 0))
    out_shape = [jax.ShapeDtypeStruct((nb, t_len, d_conv), BF16)]
    out_specs = [pl.BlockSpec((1, tm, d_conv), row)]
    if prompt:
        n_seg = 1
        out_shape += [jax.ShapeDtypeStruct((nb, HALO, d_conv), F32)]
        out_specs += [pl.BlockSpec((1, HALO, d_conv), lambda b, t: (b, 0, 0))]
    else:
        n_seg = state.shape[0]
        assert nb == 1 and n_t == 1 and tm % n_seg == 0 and state.shape[1] == CONV_WIDTH - 1
        out_shape += [jax.ShapeDtypeStruct((nb, t_len, d_conv), F32)]
        out_specs += [pl.BlockSpec((1, tm, d_conv), row)]
    out_shape += [jax.ShapeDtypeStruct((nb, 2, t_len, d_att), BF16)]
    out_specs += [halves]
    if prompt:
        out_shape += [jax.ShapeDtypeStruct((nb, t_len // LANES, d_att, LANES), BF16),
                      jax.ShapeDtypeStruct((nb, 2, t_len, d_att), BF16)]
        out_specs += [pl.BlockSpec((1, tm // LANES, d_att, LANES), lambda b, t: (b, t, 0, 0)), halves]
    out_shape += [jax.ShapeDtypeStruct((nb, keep, d_att), F32)] * 2
    out_specs += [pl.BlockSpec((1, tm, d_att), tail)] * 2
    cast_specs = []
    for w in cast:
        rb = w.shape[0] // (nb * n_t)
        assert prompt and w.ndim == 2 and rb * nb * n_t == w.shape[0] and rb % (2 * SUBLANES) == 0
        cast_specs.append(pl.BlockSpec((rb, w.shape[1]), lambda b, t: (b * n_t + t, 0)))
        out_shape.append(jax.ShapeDtypeStruct(w.shape, BF16))
    out_specs += cast_specs
    kern = functools.partial(_inproj_kernel, d_conv=d_conv, d_att=d_att, tail0=tail0, prompt=prompt,
                             n_cast=len(cast))
    extra = [] if prompt else [state]
    return pl.pallas_call(
        kern,
        grid=(nb, n_t),
        in_specs=[pl.BlockSpec((1, tm, d_model), row),
                  _const_spec((1, d_model)), _const_spec((d_model, d_in)),
                  _const_spec((1, d_att)), _const_spec((1, d_att))]
                 + [_const_spec(a.shape) for a in extra + list(conv_w)] + cast_specs,
        out_specs=out_specs,
        out_shape=out_shape,
        scratch_shapes=[pltpu.VMEM((n_seg, HALO + tm // n_seg + SUBLANES, d_conv), F32),
                        pltpu.VMEM((tm, d_conv), F32)],
        compiler_params=pltpu.CompilerParams(
            dimension_semantics=("arbitrary", "arbitrary"), vmem_limit_bytes=VMEM_LIMIT),
        name="in_proj",
    )(x, g_mix, w_in, g_q, g_k, *extra, *conv_w, *cast)


def _attn_scratch(n_heads, slots):
    return [pltpu.VMEM((slots, n_heads // 2, 2 * PAIR, WIN), F32),
            pltpu.VMEM((slots, n_heads // 2, PAIR, 2 * WIN), BF16)]


def _attend(rows, get_q, get_k, get_v, bias_ref, off, s_scr, p_scr):
    n_pairs = s_scr.shape[0]
    for hp in range(n_pairs):
        s_scr[hp, 0:2 * rows, :] = jnp.dot(get_q(hp), get_k(hp), preferred_element_type=F32)
    for hp in range(n_pairs):
        for sub in range(2):
            h = 2 * hp + sub
            for r0 in range(0, rows, SOFTMAX_ROWS):
                r = slice(r0, r0 + SOFTMAX_ROWS)
                s = s_scr[hp, sub * rows + r0:sub * rows + r0 + SOFTMAX_ROWS, :]
                s = s + jnp.concatenate([bias_ref[h, off + c, r, :] for c in range(WIN_SLABS)], axis=1)
                e = jnp.exp(s - jnp.max(s, axis=-1, keepdims=True))
                p_scr[hp, r, sub * WIN:(sub + 1) * WIN] = e.astype(BF16)
    first = lax.broadcasted_iota(jnp.int32, (WIN, LANES), 1) < HEAD_DIM
    sums = jnp.concatenate([jnp.where(first, 1.0, 0.0), jnp.where(first, 0.0, 1.0)], axis=0).astype(BF16)
    outs = []
    for hp in range(n_pairs):
        o2 = jnp.dot(p_scr[hp, 0:rows, :], jnp.concatenate([get_v(hp), sums], axis=1),
                     preferred_element_type=F32)
        outs.append(o2[:, 0:LANES] * (1.0 / o2[:, LANES:2 * LANES]))
    return jnp.concatenate(outs, axis=1)


def _attn_prompt_kernel(q_ref, kt_ref, v_ref, bias_ref, g_ref, o_ref, s_scr, p_scr):
    pairs = q_ref.shape[2] // PAIR
    slots = s_scr.shape[0]
    t = pl.program_id(1)
    for i in range(pairs):
        p = t * pairs + i
        ws = jnp.maximum(p - (WIN_SLABS - 1), 0)
        off = ws + (WIN_SLABS - 1) - p
        k0 = pl.multiple_of(ws * LANES, LANES)

        def get_k(hp, ws=ws):
            return jnp.concatenate(
                [kt_ref[0, ws + c, hp * LANES:(hp + 1) * LANES, :] for c in range(WIN_SLABS)], axis=1)

        def get_v(hp, k0=k0):
            lanes = slice(hp * LANES, (hp + 1) * LANES)
            return jnp.concatenate([v_ref[0, 0, pl.ds(k0, WIN), lanes], v_ref[0, 1, pl.ds(k0, WIN), lanes]],
                                   axis=0)

        rows = slice(i * PAIR, (i + 1) * PAIR)

        def get_q(hp, rows=rows):
            lanes = slice(hp * LANES, (hp + 1) * LANES)
            return jnp.concatenate([q_ref[0, 0, rows, lanes], q_ref[0, 1, rows, lanes]], axis=0)

        k = i % slots
        a = _attend(PAIR, get_q, get_k, get_v, bias_ref, off, s_scr.at[k], p_scr.at[k])
        o_ref[0, rows, :] = _rms(a, g_ref[...]).astype(BF16)


def _attn_prompt(q, kt, v, bias, g_att, *, tq):
    nb, _, t_len, d_att = q.shape
    n_heads = d_att // HEAD_DIM
    assert t_len % tq == 0 and tq % PAIR == 0 and t_len >= WIN
    return pl.pallas_call(
        _attn_prompt_kernel,
        grid=(nb, t_len // tq),
        in_specs=[pl.BlockSpec((1, 2, tq, d_att), lambda b, t: (b, 0, t, 0)),
                  pl.BlockSpec((1, t_len // LANES, d_att, LANES), lambda b, t: (b, 0, 0, 0)),
                  pl.BlockSpec((1, 2, t_len, d_att), lambda b, t: (b, 0, 0, 0)),
                  _const_spec(bias.shape), _const_spec((1, d_att))],
        out_specs=pl.BlockSpec((1, tq, d_att), lambda b, t: (b, t, 0)),
        out_shape=jax.ShapeDtypeStruct((nb, t_len, d_att), BF16),
        scratch_shapes=_attn_scratch(n_heads, slots=2),
        compiler_params=pltpu.CompilerParams(
            dimension_semantics=("arbitrary", "arbitrary"), vmem_limit_bytes=VMEM_LIMIT),
        name="attn_prompt",
    )(q, kt, v, bias, g_att)


def _attn_sample_kernel(q_ref, ck_ref, cv_ref, kn_ref, vn_ref, bias_ref, g_ref, o_ref, kt_s, v_s,
                        s_scr, p_scr):
    d_att = q_ref.shape[-1]
    pad = jnp.zeros((WIN - ck_ref.shape[1] - kn_ref.shape[1], d_att), F32)
    k_all = jnp.concatenate([ck_ref[0].astype(F32), kn_ref[0], pad], axis=0)
    for c in range(WIN_SLABS):
        kt_s[c] = k_all[c * LANES:(c + 1) * LANES, :].T.astype(BF16)
    v_all = jnp.concatenate([cv_ref[0].astype(F32), vn_ref[0], pad], axis=0)
    even = (lax.broadcasted_iota(jnp.int32, v_all.shape, 1) & HEAD_DIM) == 0
    v_s[0] = jnp.where(even, v_all, 0.0).astype(BF16)
    v_s[1] = jnp.where(even, 0.0, v_all).astype(BF16)

    def get_k(hp):
        return jnp.concatenate(
            [kt_s[c, hp * LANES:(hp + 1) * LANES, :] for c in range(WIN_SLABS)], axis=1)

    def get_v(hp):
        lanes = slice(hp * LANES, (hp + 1) * LANES)
        return jnp.concatenate([v_s[0, :, lanes], v_s[1, :, lanes]], axis=0)

    def get_q(hp):
        lanes = slice(hp * LANES, (hp + 1) * LANES)
        return jnp.concatenate([q_ref[0, 0, :, lanes], q_ref[0, 1, :, lanes]], axis=0)

    a = _attend(q_ref.shape[2], get_q, get_k, get_v, bias_ref, 0, s_scr.at[0], p_scr.at[0])
    o_ref[0] = _rms(a, g_ref[...]).astype(BF16)


def _attn_sample(q, cache_k, cache_v, k_new, v_new, bias, g_att):
    nb, s_len, d_att = k_new.shape
    n_heads = d_att // HEAD_DIM
    l_cache = cache_k.shape[1]
    assert s_len == CHUNK and l_cache == LEFT_CHUNKS * CHUNK
    per_b = lambda b: (b, 0, 0)
    return pl.pallas_call(
        _attn_sample_kernel,
        grid=(nb,),
        in_specs=[pl.BlockSpec((1, 2, s_len, d_att), lambda b: (0, 0, b, 0)),
                  pl.BlockSpec((1, l_cache, d_att), per_b), pl.BlockSpec((1, l_cache, d_att), per_b),
                  pl.BlockSpec((1, s_len, d_att), per_b), pl.BlockSpec((1, s_len, d_att), per_b),
                  _const_spec(bias.shape), _const_spec((1, d_att))],
        out_specs=pl.BlockSpec((1, s_len, d_att), per_b),
        out_shape=jax.ShapeDtypeStruct((nb, s_len, d_att), BF16),
        scratch_shapes=[pltpu.VMEM((WIN_SLABS, d_att, LANES), BF16), pltpu.VMEM((2, WIN, d_att), BF16)]
                       + _attn_scratch(n_heads, slots=1),
        compiler_params=pltpu.CompilerParams(
            dimension_semantics=("arbitrary",), vmem_limit_bytes=VMEM_LIMIT),
        name="attn_sample",
    )(q, cache_k, cache_v, k_new, v_new, bias, g_att)


def _dense_stage(x, cn, an, p, wout_ref, gffn_ref, w1_ref, w2_ref, gple_ref, wg_ref, wple_ref, ff_chunk):
    d_conv = cn.shape[-1]
    x1 = (x + jnp.dot(cn, wout_ref[0:d_conv, :], preferred_element_type=F32)
          + jnp.dot(an, wout_ref[d_conv:, :], preferred_element_type=F32))
    hb = _rms(x1, gffn_ref[...]).astype(BF16)
    ffn = None
    for lo in range(0, w1_ref.shape[1], ff_chunk):
        a = jnp.maximum(jnp.dot(hb, w1_ref[:, lo:lo + ff_chunk], preferred_element_type=F32), 0.0)
        part = jnp.dot((a * a).astype(BF16), w2_ref[lo:lo + ff_chunk, :], preferred_element_type=F32)
        ffn = part if ffn is None else ffn + part
    x2 = x1 + ffn
    gate = jax.nn.sigmoid(jnp.dot(_rms(x2, gple_ref[...]).astype(BF16), wg_ref[...],
                                  preferred_element_type=F32))
    ple = jnp.dot(p.astype(BF16), wple_ref[...], preferred_element_type=F32)
    return x2 + ple * gate


def _mix_kernel(x_ref, p_ref, an_ref, cn_ref, *refs, n_split, ff_chunk):
    dense_w, o_ref = refs[:-1], refs[-1]
    hm = x_ref.shape[1] // n_split
    for h in range(n_split):
        r = slice(h * hm, (h + 1) * hm)
        o_ref[0, r, :] = _dense_stage(x_ref[0, r, :], cn_ref[0, r, :], an_ref[0, r, :], p_ref[0, r, :],
                                      *dense_w, ff_chunk)


def _mix(x, p, an, cn, w_out, g_ffn, w1, w2, g_ple, w_gate, w_ple, *, tm):
    ng, rows, d_model = x.shape
    assert rows % tm == 0
    row = lambda g, t: (g, t, 0)
    consts = [w_out, g_ffn, w1, w2, g_ple, w_gate, w_ple]
    return pl.pallas_call(
        functools.partial(_mix_kernel, n_split=2, ff_chunk=1024),
        grid=(ng, rows // tm),
        in_specs=[pl.BlockSpec((1, tm, a.shape[-1]), row) for a in (x, p, an, cn)]
                 + [_const_spec(c.shape) for c in consts],
        out_specs=pl.BlockSpec((1, tm, d_model), row),
        out_shape=jax.ShapeDtypeStruct(x.shape, F32),
        compiler_params=pltpu.CompilerParams(
            dimension_semantics=("arbitrary", "arbitrary"), vmem_limit_bytes=VMEM_LIMIT),
        name="mix",
    )(x, p, an, cn, *consts)


def kernel(x_prompt, x_sample, p_prompt, p_sample, cache_att_k, cache_att_v, state_conv, g_mix, w_in, w_dw,
           b_dw, g_conv_ln, b_conv_ln, g_q, g_k, rel_bias, g_conv_out, g_att_out, w_out, g_ffn, w_ff1, w_ff2,
           g_ple, w_gate, w_ple):
    depth = w_in.shape[0]
    nb, t_len, d_model = x_prompt.shape
    sb, s_len, _ = x_sample.shape
    d_conv = w_dw.shape[-1]
    d_att = g_att_out.shape[-1]
    n_heads = d_att // HEAD_DIM
    keep = min(LEFT_CHUNKS * CHUNK, t_len)
    tile = 512

    xp = x_prompt
    xs = x_sample.reshape(1, sb * s_len, d_model)
    outs = [[] for _ in range(6)]
    for i in range(depth):
        vec = lambda a: a[i].reshape(1, -1)
        gq = jnp.tile(g_q[i], n_heads).reshape(1, d_att)
        gk = jnp.tile(g_k[i], n_heads).reshape(1, d_att)
        w_in_b = w_in[i].astype(BF16)
        conv_w = (w_dw[i], vec(b_dw), vec(g_conv_ln), vec(b_conv_ln), vec(g_conv_out))
        bias = _bias_tiles(rel_bias[i])

        (cn_p, u_tail, q_p, kt_p, vb_p, k_tail, v_tail, w_out_b, w_ff1_b, w_ff2_b, w_gate_b) = _in_proj(
            xp, vec(g_mix), w_in_b, gq, gk, conv_w, tm=tile, keep=keep,
            cast=(w_out[i], w_ff1[i], w_ff2[i], w_gate[i]))
        mix_w = (w_out_b, vec(g_ffn), w_ff1_b, w_ff2_b, vec(g_ple), w_gate_b, w_ple[i].astype(BF16))
        an_p = _attn_prompt(q_p, kt_p, vb_p, bias, vec(g_att_out), tq=tile)
        xp = _mix(xp, p_prompt[i], an_p, cn_p, *mix_w, tm=tile)
        outs[0].append(k_tail.reshape(nb, keep, n_heads, HEAD_DIM))
        outs[1].append(v_tail.reshape(nb, keep, n_heads, HEAD_DIM))
        outs[2].append(u_tail[:, HALO - (CONV_WIDTH - 1):])

        rows = sb * s_len
        cn_s, u_s, q_s, k_s, v_s = _in_proj(xs, vec(g_mix), w_in_b, gq, gk, conv_w, tm=rows, keep=rows,
                                            state=state_conv[i])
        per_b = lambda a: a.reshape(sb, s_len, a.shape[-1])
        ck = cache_att_k[i].reshape(sb, -1, d_att).astype(BF16)
        cv = cache_att_v[i].reshape(sb, -1, d_att).astype(BF16)
        an_s = _attn_sample(q_s, ck, cv, per_b(k_s), per_b(v_s), bias, vec(g_att_out))
        xs = _mix(xs, p_sample[i].reshape(1, rows, -1), an_s.reshape(1, rows, d_att), cn_s, *mix_w, tm=rows)
        outs[3].append(k_s.reshape(sb, s_len, n_heads, HEAD_DIM))
        outs[4].append(v_s.reshape(sb, s_len, n_heads, HEAD_DIM))
        conv_in_tail = jnp.concatenate([state_conv[i], per_b(u_s)], axis=1)[:, -(CONV_WIDTH - 1):]
        outs[5].append(conv_in_tail)

    return (xp, xs.reshape(sb, s_len, d_model)) + tuple(jnp.stack(o) for o in outs)
```

```python
import functools

import jax
import jax.numpy as jnp
from jax import lax
from jax.experimental import pallas as pl
from jax.experimental.pallas import tpu as pltpu

F32 = jnp.float32
BF16 = jnp.bfloat16

CHUNK = 64
LEFT_CHUNKS = 8
HEAD_DIM = 64
CONV_WIDTH = 31
MAX_REL = 128
EPS = 1e-6
NEG = -1e30

LANES = 128
SUBLANES = 8
PAIR = 2 * CHUNK
WIN = (LEFT_CHUNKS + 2) * CHUNK
WIN_SLABS = WIN // LANES
BIAS_SLABS = 2 * WIN_SLABS - 1
HALO = 32
NORM_ROWS = 32
SOFTMAX_ROWS = 16
VMEM_LIMIT = 56 * 1024 * 1024


def _rms(x, g):
    return x * lax.rsqrt(jnp.mean(x * x, axis=-1, keepdims=True) + EPS) * g


def _const_spec(shape):
    zeros = (0,) * len(shape)
    return pl.BlockSpec(shape, lambda *_: zeros, pipeline_mode=pl.Buffered(1))


def _bias_kernel(rb_ref, o_ref):
    n_heads = rb_ref.shape[0]
    rb = rb_ref[...]
    lane = lax.broadcasted_iota(jnp.int32, (n_heads, LANES), 1)
    b0 = jnp.broadcast_to(rb[:, 0:1], (n_heads, LANES))
    t4 = jnp.where(lane < CHUNK, rb[:, LANES:2 * LANES], b0)
    t = jnp.concatenate([b0, b0, b0, rb[:, 0:LANES], t4, b0], axis=1)
    r_i = lax.broadcasted_iota(jnp.int32, (PAIR, LANES), 0)
    c_i = lax.broadcasted_iota(jnp.int32, (PAIR, LANES), 1)
    for h in range(n_heads):
        row = jnp.broadcast_to(t[h:h + 1, :], (PAIR, t.shape[1]))
        toe = pltpu.roll(row, 0, 1, stride=1, stride_axis=0)
        for c in range(WIN_SLABS):
            jj = c_i + c * LANES
            valid = ((r_i < CHUNK) & (jj < WIN - CHUNK)) | ((r_i >= CHUNK) & (jj >= CHUNK))
            o_ref[h, c] = jnp.where(valid, toe[:, c * LANES:(c + 1) * LANES], NEG)
        for c in range(WIN_SLABS, BIAS_SLABS):
            o_ref[h, c] = jnp.full((PAIR, LANES), NEG, F32)


def _bias_tiles(rel_bias):
    n_heads = rel_bias.shape[0]
    return pl.pallas_call(
        _bias_kernel,
        out_shape=jax.ShapeDtypeStruct((n_heads, BIAS_SLABS, PAIR, LANES), F32),
        name="bias_tiles",
    )(rel_bias)


def _conv_stage(u, n_seg, halo_ref, ubuf, y_scr, cn_ref, wdw_ref, bdw_ref, gln_ref, bln_ref, gco_ref):
    d_conv = u.shape[-1]
    seg = u.shape[0] // n_seg
    off = HALO - (CONV_WIDTH - 1)
    n_a = HALO // SUBLANES + 1
    for s in range(n_seg):
        if halo_ref is not None:
            hr = halo_ref.shape[1]
            if hr < HALO:
                ubuf[s, 0:HALO - hr, :] = jnp.zeros((HALO - hr, d_conv), F32)
            ubuf[s, HALO - hr:HALO, :] = halo_ref[s]
        ubuf[s, HALO:HALO + seg, :] = u[s * seg:(s + 1) * seg, :]
        ubuf[s, HALO + seg:HALO + seg + SUBLANES, :] = jnp.zeros((SUBLANES, d_conv), F32)

    sub = lax.broadcasted_iota(jnp.int32, (SUBLANES, LANES), 0)
    from_lower = [sub < SUBLANES - b for b in range(SUBLANES)]
    for lt in range(d_conv // LANES):
        lanes = slice(lt * LANES, (lt + 1) * LANES)

        w = [jnp.broadcast_to(wdw_ref[j:j + 1, lanes], (SUBLANES, LANES)) for j in range(CONV_WIDTH)]

        def rotated_groups(s, blk):
            rows = [ubuf[s, SUBLANES * (blk + a):SUBLANES * (blk + a + 1), lanes] for a in range(n_a)]
            out = []
            for b in range(SUBLANES):
                z = None
                for a in range(n_a):
                    j = SUBLANES * a + b - off
                    if 0 <= j < CONV_WIDTH:
                        t = w[j] * rows[a]
                        z = t if z is None else z + t
                out.append(z if b == 0 else pltpu.roll(z, SUBLANES - b, 0))
            return out

        for s in range(n_seg):
            lower = rotated_groups(s, 0)
            for blk in range(seg // SUBLANES):
                upper = rotated_groups(s, blk + 1)
                y = lower[0]
                for b in range(1, SUBLANES):
                    y = y + jnp.where(from_lower[b], lower[b], upper[b])
                y_scr[s * seg + blk * SUBLANES:s * seg + (blk + 1) * SUBLANES, lanes] = y
                lower = upper

    for r0 in range(0, n_seg * seg, NORM_ROWS):
        y = y_scr[r0:r0 + NORM_ROWS, :] + bdw_ref[...]
        mu = jnp.mean(y, axis=-1, keepdims=True)
        yc = y - mu
        yn = yc * lax.rsqrt(jnp.mean(yc * yc, axis=-1, keepdims=True) + EPS) * gln_ref[...] + bln_ref[...]
        c = yn * jax.nn.sigmoid(yn)
        cn_ref[r0:r0 + NORM_ROWS, :] = _rms(c, gco_ref[...]).astype(BF16)


def _inproj_kernel(*refs, d_conv, d_att, tail0, prompt, n_cast):
    if prompt:
        x_ref, gmix_ref, win_ref, gq_ref, gk_ref = refs[:5]
        halo_ref, conv_w = None, refs[5:10]
        cast_in = refs[10:10 + n_cast]
        cn_ref, ut_ref, q_ref, kt_ref, vb_ref, k32_ref, v32_ref = refs[10 + n_cast:17 + n_cast]
        cast_out = refs[17 + n_cast:17 + 2 * n_cast]
        ubuf, y_scr = refs[17 + 2 * n_cast:]
        t = pl.program_id(1)
        tm = x_ref.shape[1]

        for src, dst in zip(cast_in, cast_out):
            dst[...] = src[...].astype(BF16)

        @pl.when(t == 0)
        def _():
            ubuf[0, 0:HALO, :] = jnp.zeros((HALO, d_conv), F32)

        @pl.when(t > 0)
        def _():
            ubuf[0, 0:HALO, :] = ubuf[0, tm:tm + HALO, :]
    else:
        x_ref, gmix_ref, win_ref, gq_ref, gk_ref, halo_ref = refs[:6]
        conv_w = refs[6:11]
        cn_ref, u_ref, q_ref, k32_ref, v32_ref, ubuf, y_scr = refs[11:]
        tm = x_ref.shape[1]
    h = _rms(x_ref[0], gmix_ref[...]).astype(BF16)

    def proj(lo, width):
        return jnp.dot(h, win_ref[:, lo:lo + width], preferred_element_type=F32)

    u = proj(0, d_conv) * jax.nn.sigmoid(proj(d_conv, d_conv))
    if prompt:
        @pl.when(t == pl.num_programs(1) - 1)
        def _():
            ut_ref[0] = u[tm - HALO:tm, :]
    else:
        u_ref[0] = u
    _conv_stage(u, ubuf.shape[0], halo_ref, ubuf, y_scr, cn_ref.at[0], *conv_w)

    r_i = lax.shift_right_logical(lax.broadcasted_iota(jnp.int32, (d_att, d_att), 0), 6)
    c_i = lax.shift_right_logical(lax.broadcasted_iota(jnp.int32, (d_att, d_att), 1), 6)
    avg = jnp.where(r_i == c_i, 1.0 / HEAD_DIM, 0.0).astype(BF16)

    def head_rms(z, g):
        ms = jnp.dot((z * z).astype(BF16), avg, preferred_element_type=F32)
        return z * lax.rsqrt(ms + EPS) * g

    o = 2 * d_conv
    even = (lax.broadcasted_iota(jnp.int32, (tm, d_att), 1) & HEAD_DIM) == 0

    def split_heads(ref, z):
        ref[0, 0] = jnp.where(even, z, 0.0).astype(BF16)
        ref[0, 1] = jnp.where(even, 0.0, z).astype(BF16)

    qn = head_rms(proj(o, d_att), gq_ref[...])
    split_heads(q_ref, qn * (HEAD_DIM ** -0.5))
    kn = head_rms(proj(o + d_att, d_att), gk_ref[...])
    v = proj(o + 2 * d_att, d_att)
    if prompt:
        for s in range(tm // LANES):
            kt_ref[0, s] = kn[s * LANES:(s + 1) * LANES, :].T.astype(BF16)
        split_heads(vb_ref, v)

    @pl.when(pl.program_id(1) >= tail0)
    def _():
        k32_ref[0] = kn
        v32_ref[0] = v


def _in_proj(x, g_mix, w_in, g_q, g_k, conv_w, *, tm, keep, state=None, cast=()):
    nb, t_len, d_model = x.shape
    d_att = g_q.shape[-1]
    d_in = w_in.shape[-1]
    d_conv = (d_in - 3 * d_att) // 2
    n_t = t_len // tm
    prompt = state is None
    assert t_len % tm == 0 and (t_len - keep) % tm == 0 and tm % LANES == 0 and tm >= HALO
    tail0 = (t_len - keep) // tm
    row = lambda b, t: (b, t, 0)
    tail = lambda b, t: (b, jnp.maximum(t - tail0, 0), 0)
    halves = pl.BlockSpec((1, 2, tm, d_att), lambda b, t: (b, 0, t, 0))
    out_shape = [jax.ShapeDtypeStruct((nb, t_len, d_conv), BF16)]
    out_specs = [pl.BlockSpec((1, tm, d_conv), row)]
    if prompt:
        n_seg = 1
        out_shape += [jax.ShapeDtypeStruct((nb, HALO, d_conv), F32)]
        out_specs += [pl.BlockSpec((1, HALO, d_conv), lambda b, t: (b, 0, 0))]
    else:
        n_seg = state.shape[0]
        assert nb == 1 and n_t == 1 and tm % n_seg == 0 and state.shape[1] == CONV_WIDTH - 1
        out_shape += [jax.ShapeDtypeStruct((nb, t_len, d_conv), F32)]
        out_specs += [pl.BlockSpec((1, tm, d_conv), row)]
    out_shape += [jax.ShapeDtypeStruct((nb, 2, t_len, d_att), BF16)]
    out_specs += [halves]
    if prompt:
        out_shape += [jax.ShapeDtypeStruct((nb, t_len // LANES, d_att, LANES), BF16),
                      jax.ShapeDtypeStruct((nb, 2, t_len, d_att), BF16)]
        out_specs += [pl.BlockSpec((1, tm // LANES, d_att, LANES), lambda b, t: (b, t, 0, 0)), halves]
    out_shape += [jax.ShapeDtypeStruct((nb, keep, d_att), F32)] * 2
    out_specs += [pl.BlockSpec((1, tm, d_att), tail)] * 2
    cast_specs = []
    for w in cast:
        rb = w.shape[0] // (nb * n_t)
        assert prompt and w.ndim == 2 and rb * nb * n_t == w.shape[0] and rb % (2 * SUBLANES) == 0
        cast_specs.append(pl.BlockSpec((rb, w.shape[1]), lambda b, t: (b * n_t + t, 0)))
        out_shape.append(jax.ShapeDtypeStruct(w.shape, BF16))
    out_specs += cast_specs
    kern = functools.partial(_inproj_kernel, d_conv=d_conv, d_att=d_att, tail0=tail0, prompt=prompt,
                             n_cast=len(cast))
    extra = [] if prompt else [state]
    return pl.pallas_call(
        kern,
        grid=(nb, n_t),
        in_specs=[pl.BlockSpec((1, tm, d_model), row),
                  _const_spec((1, d_model)), _const_spec((d_model, d_in)),
                  _const_spec((1, d_att)), _const_spec((1, d_att))]
                 + [_const_spec(a.shape) for a in extra + list(conv_w)] + cast_specs,
        out_specs=out_specs,
        out_shape=out_shape,
        scratch_shapes=[pltpu.VMEM((n_seg, HALO + tm // n_seg + SUBLANES, d_conv), F32),
                        pltpu.VMEM((tm, d_conv), F32)],
        compiler_params=pltpu.CompilerParams(
            dimension_semantics=("arbitrary", "arbitrary"), vmem_limit_bytes=VMEM_LIMIT),
        name="in_proj",
    )(x, g_mix, w_in, g_q, g_k, *extra, *conv_w, *cast)


def _attn_scratch(n_heads, slots):
    return [pltpu.VMEM((slots, n_heads // 2, 2 * PAIR, WIN), F32),
            pltpu.VMEM((slots, n_heads // 2, PAIR, 2 * WIN), BF16)]


def _attend(rows, n_slabs, get_q, get_k, get_v, bias_ref, off, s_scr, p_scr):
    n_pairs = s_scr.shape[0]
    width = n_slabs * LANES
    for hp in range(n_pairs):
        s_scr[hp, 0:2 * rows, 0:width] = jnp.dot(get_q(hp), get_k(hp), preferred_element_type=F32)
    for hp in range(n_pairs):
        for sub in range(2):
            h = 2 * hp + sub
            for r0 in range(0, rows, SOFTMAX_ROWS):
                r = slice(r0, r0 + SOFTMAX_ROWS)
                s = s_scr[hp, sub * rows + r0:sub * rows + r0 + SOFTMAX_ROWS, 0:width]
                s = s + jnp.concatenate([bias_ref[h, off + c, r, :] for c in range(n_slabs)], axis=1)
                e = jnp.exp(s - jnp.max(s, axis=-1, keepdims=True))
                p_scr[hp, r, sub * width:(sub + 1) * width] = e.astype(BF16)
    first = lax.broadcasted_iota(jnp.int32, (width, LANES), 1) < HEAD_DIM
    sums = jnp.concatenate([jnp.where(first, 1.0, 0.0), jnp.where(first, 0.0, 1.0)], axis=0).astype(BF16)
    outs = []
    for hp in range(n_pairs):
        o2 = jnp.dot(p_scr[hp, 0:rows, 0:2 * width], jnp.concatenate([get_v(hp), sums], axis=1),
                     preferred_element_type=F32)
        outs.append(o2[:, 0:LANES] * (1.0 / o2[:, LANES:2 * LANES]))
    return jnp.concatenate(outs, axis=1)


def _attn_prompt_kernel(q_ref, kt_ref, v_ref, bias_ref, g_ref, o_ref, s_scr, p_scr):
    pairs = q_ref.shape[2] // PAIR
    slots = s_scr.shape[0]
    t = pl.program_id(1)

    def run_pairs(first_step):
        for i in range(pairs):
            p = i if first_step else t * pairs + i
            n_slabs = min(i + 1, WIN_SLABS) if first_step else WIN_SLABS
            ws = max(p - (WIN_SLABS - 1), 0) if first_step else p - (WIN_SLABS - 1)
            off = ws + (WIN_SLABS - 1) - p
            k0 = ws * LANES if first_step else pl.multiple_of(ws * LANES, LANES)
            width = n_slabs * LANES

            def get_k(hp, ws=ws, n_slabs=n_slabs):
                return jnp.concatenate(
                    [kt_ref[0, ws + c, hp * LANES:(hp + 1) * LANES, :] for c in range(n_slabs)], axis=1)

            def get_v(hp, k0=k0, width=width):
                lanes = slice(hp * LANES, (hp + 1) * LANES)
                return jnp.concatenate([v_ref[0, 0, pl.ds(k0, width), lanes],
                                        v_ref[0, 1, pl.ds(k0, width), lanes]], axis=0)

            rows = slice(i * PAIR, (i + 1) * PAIR)

            def get_q(hp, rows=rows):
                lanes = slice(hp * LANES, (hp + 1) * LANES)
                return jnp.concatenate([q_ref[0, 0, rows, lanes], q_ref[0, 1, rows, lanes]], axis=0)

            k = i % slots
            a = _attend(PAIR, n_slabs, get_q, get_k, get_v, bias_ref, off, s_scr.at[k], p_scr.at[k])
            o_ref[0, rows, :] = _rms(a, g_ref[...]).astype(BF16)

    pl.when(t == 0)(functools.partial(run_pairs, True))
    pl.when(t > 0)(functools.partial(run_pairs, False))


def _attn_prompt(q, kt, v, bias, g_att, *, tq):
    nb, _, t_len, d_att = q.shape
    n_heads = d_att // HEAD_DIM
    assert t_len % tq == 0 and tq % PAIR == 0 and tq // PAIR >= WIN_SLABS - 1
    return pl.pallas_call(
        _attn_prompt_kernel,
        grid=(nb, t_len // tq),
        in_specs=[pl.BlockSpec((1, 2, tq, d_att), lambda b, t: (b, 0, t, 0)),
                  pl.BlockSpec((1, t_len // LANES, d_att, LANES), lambda b, t: (b, 0, 0, 0)),
                  pl.BlockSpec((1, 2, t_len, d_att), lambda b, t: (b, 0, 0, 0)),
                  _const_spec(bias.shape), _const_spec((1, d_att))],
        out_specs=pl.BlockSpec((1, tq, d_att), lambda b, t: (b, t, 0)),
        out_shape=jax.ShapeDtypeStruct((nb, t_len, d_att), BF16),
        scratch_shapes=_attn_scratch(n_heads, slots=2),
        compiler_params=pltpu.CompilerParams(
            dimension_semantics=("arbitrary", "arbitrary"), vmem_limit_bytes=VMEM_LIMIT),
        name="attn_prompt",
    )(q, kt, v, bias, g_att)


def _attn_sample_kernel(q_ref, ck_ref, cv_ref, kn_ref, vn_ref, bias_ref, g_ref, o_ref, kt_s, v_s,
                        s_scr, p_scr):
    d_att = q_ref.shape[-1]
    pad = jnp.zeros((WIN - ck_ref.shape[1] - kn_ref.shape[1], d_att), F32)
    k_all = jnp.concatenate([ck_ref[0].astype(F32), kn_ref[0], pad], axis=0)
    for c in range(WIN_SLABS):
        kt_s[c] = k_all[c * LANES:(c + 1) * LANES, :].T.astype(BF16)
    v_all = jnp.concatenate([cv_ref[0].astype(F32), vn_ref[0], pad], axis=0)
    even = (lax.broadcasted_iota(jnp.int32, v_all.shape, 1) & HEAD_DIM) == 0
    v_s[0] = jnp.where(even, v_all, 0.0).astype(BF16)
    v_s[1] = jnp.where(even, 0.0, v_all).astype(BF16)

    def get_k(hp):
        return jnp.concatenate(
            [kt_s[c, hp * LANES:(hp + 1) * LANES, :] for c in range(WIN_SLABS)], axis=1)

    def get_v(hp):
        lanes = slice(hp * LANES, (hp + 1) * LANES)
        return jnp.concatenate([v_s[0, :, lanes], v_s[1, :, lanes]], axis=0)

    def get_q(hp):
        lanes = slice(hp * LANES, (hp + 1) * LANES)
        return jnp.concatenate([q_ref[0, 0, :, lanes], q_ref[0, 1, :, lanes]], axis=0)

    a = _attend(q_ref.shape[2], WIN_SLABS, get_q, get_k, get_v, bias_ref, 0, s_scr.at[0], p_scr.at[0])
    o_ref[0] = _rms(a, g_ref[...]).astype(BF16)


def _attn_sample(q, cache_k, cache_v, k_new, v_new, bias, g_att):
    nb, s_len, d_att = k_new.shape
    n_heads = d_att // HEAD_DIM
    l_cache = cache_k.shape[1]
    assert s_len == CHUNK and l_cache == LEFT_CHUNKS * CHUNK
    per_b = lambda b: (b, 0, 0)
    return pl.pallas_call(
        _attn_sample_kernel,
        grid=(nb,),
        in_specs=[pl.BlockSpec((1, 2, s_len, d_att), lambda b: (0, 0, b, 0)),
                  pl.BlockSpec((1, l_cache, d_att), per_b), pl.BlockSpec((1, l_cache, d_att), per_b),
                  pl.BlockSpec((1, s_len, d_att), per_b), pl.BlockSpec((1, s_len, d_att), per_b),
                  _const_spec(bias.shape), _const_spec((1, d_att))],
        out_specs=pl.BlockSpec((1, s_len, d_att), per_b),
        out_shape=jax.ShapeDtypeStruct((nb, s_len, d_att), BF16),
        scratch_shapes=[pltpu.VMEM((WIN_SLABS, d_att, LANES), BF16), pltpu.VMEM((2, WIN, d_att), BF16)]
                       + _attn_scratch(n_heads, slots=1),
        compiler_params=pltpu.CompilerParams(
            dimension_semantics=("arbitrary",), vmem_limit_bytes=VMEM_LIMIT),
        name="attn_sample",
    )(q, cache_k, cache_v, k_new, v_new, bias, g_att)


def _dense_stage(x, cn, an, p, wout_ref, gffn_ref, w1_ref, w2_ref, gple_ref, wg_ref, wple_ref, ff_chunk):
    d_conv = cn.shape[-1]
    x1 = (x + jnp.dot(cn, wout_ref[0:d_conv, :], preferred_element_type=F32)
          + jnp.dot(an, wout_ref[d_conv:, :], preferred_element_type=F32))
    hb = _rms(x1, gffn_ref[...]).astype(BF16)
    ffn = None
    for lo in range(0, w1_ref.shape[1], ff_chunk):
        a = jnp.maximum(jnp.dot(hb, w1_ref[:, lo:lo + ff_chunk], preferred_element_type=F32), 0.0)
        part = jnp.dot((a * a).astype(BF16), w2_ref[lo:lo + ff_chunk, :], preferred_element_type=F32)
        ffn = part if ffn is None else ffn + part
    x2 = x1 + ffn
    gate = jax.nn.sigmoid(jnp.dot(_rms(x2, gple_ref[...]).astype(BF16), wg_ref[...],
                                  preferred_element_type=F32))
    ple = jnp.dot(p.astype(BF16), wple_ref[...], preferred_element_type=F32)
    return x2 + ple * gate


def _mix_kernel(x_ref, p_ref, an_ref, cn_ref, *refs, n_split, ff_chunk):
    dense_w, o_ref = refs[:-1], refs[-1]
    hm = x_ref.shape[1] // n_split
    for h in range(n_split):
        r = slice(h * hm, (h + 1) * hm)
        o_ref[0, r, :] = _dense_stage(x_ref[0, r, :], cn_ref[0, r, :], an_ref[0, r, :], p_ref[0, r, :],
                                      *dense_w, ff_chunk)


def _mix(x, p, an, cn, w_out, g_ffn, w1, w2, g_ple, w_gate, w_ple, *, tm):
    ng, rows, d_model = x.shape
    assert rows % tm == 0
    row = lambda g, t: (g, t, 0)
    consts = [w_out, g_ffn, w1, w2, g_ple, w_gate, w_ple]
    return pl.pallas_call(
        functools.partial(_mix_kernel, n_split=2, ff_chunk=1024),
        grid=(ng, rows // tm),
        in_specs=[pl.BlockSpec((1, tm, a.shape[-1]), row) for a in (x, p, an, cn)]
                 + [_const_spec(c.shape) for c in consts],
        out_specs=pl.BlockSpec((1, tm, d_model), row),
        out_shape=jax.ShapeDtypeStruct(x.shape, F32),
        compiler_params=pltpu.CompilerParams(
            dimension_semantics=("arbitrary", "arbitrary"), vmem_limit_bytes=VMEM_LIMIT),
        name="mix",
    )(x, p, an, cn, *consts)


def kernel(x_prompt, x_sample, p_prompt, p_sample, cache_att_k, cache_att_v, state_conv, g_mix, w_in, w_dw,
           b_dw, g_conv_ln, b_conv_ln, g_q, g_k, rel_bias, g_conv_out, g_att_out, w_out, g_ffn, w_ff1, w_ff2,
           g_ple, w_gate, w_ple):
    depth = w_in.shape[0]
    nb, t_len, d_model = x_prompt.shape
    sb, s_len, _ = x_sample.shape
    d_conv = w_dw.shape[-1]
    d_att = g_att_out.shape[-1]
    n_heads = d_att // HEAD_DIM
    keep = min(LEFT_CHUNKS * CHUNK, t_len)
    tile = 512

    xp = x_prompt
    xs = x_sample.reshape(1, sb * s_len, d_model)
    outs = [[] for _ in range(6)]
    for i in range(depth):
        vec = lambda a: a[i].reshape(1, -1)
        gq = jnp.tile(g_q[i], n_heads).reshape(1, d_att)
        gk = jnp.tile(g_k[i], n_heads).reshape(1, d_att)
        w_in_b = w_in[i].astype(BF16)
        conv_w = (w_dw[i], vec(b_dw), vec(g_conv_ln), vec(b_conv_ln), vec(g_conv_out))
        bias = _bias_tiles(rel_bias[i])

        (cn_p, u_tail, q_p, kt_p, vb_p, k_tail, v_tail, w_out_b, w_ff1_b, w_ff2_b, w_gate_b) = _in_proj(
            xp, vec(g_mix), w_in_b, gq, gk, conv_w, tm=tile, keep=keep,
            cast=(w_out[i], w_ff1[i], w_ff2[i], w_gate[i]))
        mix_w = (w_out_b, vec(g_ffn), w_ff1_b, w_ff2_b, vec(g_ple), w_gate_b, w_ple[i].astype(BF16))
        an_p = _attn_prompt(q_p, kt_p, vb_p, bias, vec(g_att_out), tq=tile)
        xp = _mix(xp, p_prompt[i], an_p, cn_p, *mix_w, tm=tile)
        outs[0].append(k_tail.reshape(nb, keep, n_heads, HEAD_DIM))
        outs[1].append(v_tail.reshape(nb, keep, n_heads, HEAD_DIM))
        outs[2].append(u_tail[:, HALO - (CONV_WIDTH - 1):])

        rows = sb * s_len
        cn_s, u_s, q_s, k_s, v_s = _in_proj(xs, vec(g_mix), w_in_b, gq, gk, conv_w, tm=rows, keep=rows,
                                            state=state_conv[i])
        per_b = lambda a: a.reshape(sb, s_len, a.shape[-1])
        ck = cache_att_k[i].astype(BF16).reshape(sb, -1, d_att)
        cv = cache_att_v[i].astype(BF16).reshape(sb, -1, d_att)
        an_s = _attn_sample(q_s, ck, cv, per_b(k_s), per_b(v_s), bias, vec(g_att_out))
        xs = _mix(xs, p_sample[i].reshape(1, rows, -1), an_s.reshape(1, rows, d_att), cn_s, *mix_w, tm=rows)
        outs[3].append(k_s.reshape(sb, s_len, n_heads, HEAD_DIM))
        outs[4].append(v_s.reshape(sb, s_len, n_heads, HEAD_DIM))
        conv_in_tail = jnp.concatenate([state_conv[i], per_b(u_s)], axis=1)[:, -(CONV_WIDTH - 1):]
        outs[5].append(conv_in_tail)

    return (xp, xs.reshape(sb, s_len, d_model)) + tuple(jnp.stack(o) for o in outs)
```

```python
import functools

import jax
import jax.numpy as jnp
from jax import lax
from jax.experimental import pallas as pl
from jax.experimental.pallas import tpu as pltpu

F32 = jnp.float32
BF16 = jnp.bfloat16

CHUNK = 64
LEFT_CHUNKS = 8
HEAD_DIM = 64
CONV_WIDTH = 31
MAX_REL = 128
EPS = 1e-6
NEG = -1e30

LANES = 128
SUBLANES = 8
MXU_TILE = 256
PAIR = 2 * CHUNK
WIN = (LEFT_CHUNKS + 2) * CHUNK
WIN_SLABS = WIN // LANES
HALO = 32
NORM_ROWS = 32
SOFTMAX_ROWS = 16
VMEM_LIMIT = 56 * 1024 * 1024


def _rms(x, g):
    return x * lax.rsqrt(jnp.mean(x * x, axis=-1, keepdims=True) + EPS) * g


def _const_spec(shape):
    zeros = (0,) * len(shape)
    return pl.BlockSpec(shape, lambda *_: zeros, pipeline_mode=pl.Buffered(1))


def _bias_kernel(rb_ref, o_ref):
    n_heads = rb_ref.shape[0]
    rb = rb_ref[...]
    lane = lax.broadcasted_iota(jnp.int32, (n_heads, LANES), 1)
    b0 = jnp.broadcast_to(rb[:, 0:1], (n_heads, LANES))
    t4 = jnp.where(lane < CHUNK, rb[:, LANES:2 * LANES], b0)
    t = jnp.concatenate([b0, b0, b0, rb[:, 0:LANES], t4, b0], axis=1)
    r_i = lax.broadcasted_iota(jnp.int32, (PAIR, LANES), 0)
    c_i = lax.broadcasted_iota(jnp.int32, (PAIR, LANES), 1)
    for h in range(n_heads):
        row = jnp.broadcast_to(t[h:h + 1, :], (PAIR, t.shape[1]))
        toe = pltpu.roll(row, 0, 1, stride=1, stride_axis=0)
        for c in range(WIN_SLABS):
            jj = c_i + c * LANES
            valid = ((r_i < CHUNK) & (jj < WIN - CHUNK)) | ((r_i >= CHUNK) & (jj >= CHUNK))
            o_ref[h, c] = jnp.where(valid, toe[:, c * LANES:(c + 1) * LANES], NEG)


def _bias_tiles(rel_bias):
    n_heads = rel_bias.shape[0]
    return pl.pallas_call(
        _bias_kernel,
        out_shape=jax.ShapeDtypeStruct((n_heads, WIN_SLABS, PAIR, LANES), F32),
        name="bias_tiles",
    )(rel_bias)


def _conv_stage(u, n_seg, halo_ref, ubuf, y_scr, cn_ref, wdw_ref, bdw_ref, gln_ref, bln_ref, gco_ref):
    d_conv = u.shape[-1]
    seg = u.shape[0] // n_seg
    off = HALO - (CONV_WIDTH - 1)
    n_a = HALO // SUBLANES + 1
    for s in range(n_seg):
        if halo_ref is not None:
            hr = halo_ref.shape[1]
            if hr < HALO:
                ubuf[s, 0:HALO - hr, :] = jnp.zeros((HALO - hr, d_conv), F32)
            ubuf[s, HALO - hr:HALO, :] = halo_ref[s]
        ubuf[s, HALO:HALO + seg, :] = u[s * seg:(s + 1) * seg, :]
        ubuf[s, HALO + seg:HALO + seg + SUBLANES, :] = jnp.zeros((SUBLANES, d_conv), F32)

    sub = lax.broadcasted_iota(jnp.int32, (SUBLANES, LANES), 0)
    from_lower = [sub < SUBLANES - b for b in range(SUBLANES)]
    for lt in range(d_conv // LANES):
        lanes = slice(lt * LANES, (lt + 1) * LANES)

        w = [jnp.broadcast_to(wdw_ref[j:j + 1, lanes], (SUBLANES, LANES)) for j in range(CONV_WIDTH)]

        def rotated_groups(s, blk):
            rows = [ubuf[s, SUBLANES * (blk + a):SUBLANES * (blk + a + 1), lanes] for a in range(n_a)]
            out = []
            for b in range(SUBLANES):
                z = None
                for a in range(n_a):
                    j = SUBLANES * a + b - off
                    if 0 <= j < CONV_WIDTH:
                        t = w[j] * rows[a]
                        z = t if z is None else z + t
                out.append(z if b == 0 else pltpu.roll(z, SUBLANES - b, 0))
            return out

        for s in range(n_seg):
            lower = rotated_groups(s, 0)
            for blk in range(seg // SUBLANES):
                upper = rotated_groups(s, blk + 1)
                y = lower[0]
                for b in range(1, SUBLANES):
                    y = y + jnp.where(from_lower[b], lower[b], upper[b])
                y_scr[s * seg + blk * SUBLANES:s * seg + (blk + 1) * SUBLANES, lanes] = y
                lower = upper

    for r0 in range(0, n_seg * seg, NORM_ROWS):
        y = y_scr[r0:r0 + NORM_ROWS, :] + bdw_ref[...]
        mu = jnp.mean(y, axis=-1, keepdims=True)
        yc = y - mu
        yn = yc * lax.rsqrt(jnp.mean(yc * yc, axis=-1, keepdims=True) + EPS) * gln_ref[...] + bln_ref[...]
        c = yn * jax.nn.sigmoid(yn)
        cn_ref[r0:r0 + NORM_ROWS, :] = _rms(c, gco_ref[...]).astype(BF16)


def _inproj_kernel(*refs, d_conv, d_att, tail0, prompt, n_cast):
    if prompt:
        x_ref, gmix_ref, win_ref, gq_ref, gk_ref = refs[:5]
        halo_ref, conv_w = None, refs[5:10]
        cast_in = refs[10:10 + n_cast]
        cn_ref, ut_ref, q_ref, kt_ref, vb_ref, k32_ref, v32_ref = refs[10 + n_cast:17 + n_cast]
        cast_out = refs[17 + n_cast:17 + 2 * n_cast]
        ubuf, y_scr = refs[17 + 2 * n_cast:]
        t = pl.program_id(1)
        tm = x_ref.shape[1]

        for src, dst in zip(cast_in, cast_out):
            dst[...] = src[...].astype(BF16)

        @pl.when(t == 0)
        def _():
            ubuf[0, 0:HALO, :] = jnp.zeros((HALO, d_conv), F32)

        @pl.when(t > 0)
        def _():
            ubuf[0, 0:HALO, :] = ubuf[0, tm:tm + HALO, :]
    else:
        x_ref, gmix_ref, win_ref, gq_ref, gk_ref, halo_ref = refs[:6]
        conv_w = refs[6:11]
        cn_ref, u_ref, q_ref, k32_ref, v32_ref, ubuf, y_scr = refs[11:]
        tm = x_ref.shape[1]
    h = _rms(x_ref[0], gmix_ref[...]).astype(BF16)

    def proj(lo, width):
        return jnp.dot(h, win_ref[:, lo:lo + width], preferred_element_type=F32)

    u = proj(0, d_conv) * jax.nn.sigmoid(proj(d_conv, d_conv))
    if prompt:
        @pl.when(t == pl.num_programs(1) - 1)
        def _():
            ut_ref[0] = u[tm - HALO:tm, :]
    else:
        u_ref[0] = u
    _conv_stage(u, ubuf.shape[0], halo_ref, ubuf, y_scr, cn_ref.at[0], *conv_w)

    r_i = lax.shift_right_logical(lax.broadcasted_iota(jnp.int32, (MXU_TILE, MXU_TILE), 0), 6)
    c_i = lax.shift_right_logical(lax.broadcasted_iota(jnp.int32, (MXU_TILE, MXU_TILE), 1), 6)
    avg = jnp.where(r_i == c_i, 1.0 / HEAD_DIM, 0.0).astype(BF16)

    def head_rms(z, g):
        sq = (z * z).astype(BF16)
        ms = jnp.concatenate([jnp.dot(sq[:, lo:lo + MXU_TILE], avg, preferred_element_type=F32)
                              for lo in range(0, d_att, MXU_TILE)], axis=1)
        return z * lax.rsqrt(ms + EPS) * g

    o = 2 * d_conv
    even = (lax.broadcasted_iota(jnp.int32, (tm, d_att), 1) & HEAD_DIM) == 0

    def split_heads(ref, z):
        ref[0, 0] = jnp.where(even, z, 0.0).astype(BF16)
        ref[0, 1] = jnp.where(even, 0.0, z).astype(BF16)

    qn = head_rms(proj(o, d_att), gq_ref[...])
    split_heads(q_ref, qn * (HEAD_DIM ** -0.5))
    kn = head_rms(proj(o + d_att, d_att), gk_ref[...])
    v = proj(o + 2 * d_att, d_att)
    if prompt:
        for s in range(tm // LANES):
            kt_ref[0, s] = kn[s * LANES:(s + 1) * LANES, :].T.astype(BF16)
        split_heads(vb_ref, v)

    @pl.when(pl.program_id(1) >= tail0)
    def _():
        k32_ref[0] = kn
        v32_ref[0] = v


def _in_proj(x, g_mix, w_in, g_q, g_k, conv_w, *, tm, keep, state=None, cast=()):
    nb, t_len, d_model = x.shape
    d_att = g_q.shape[-1]
    d_in = w_in.shape[-1]
    d_conv = (d_in - 3 * d_att) // 2
    n_t = t_len // tm
    prompt = state is None
    assert t_len % tm == 0 and (t_len - keep) % tm == 0 and tm % LANES == 0 and tm >= HALO
    tail0 = (t_len - keep) // tm
    row = lambda b, t: (b, t, 0)
    tail = lambda b, t: (b, jnp.maximum(t - tail0, 0), 0)
    halves = pl.BlockSpec((1, 2, tm, d_att), lambda b, t: (b, 0, t, 0))
    out_shape = [jax.ShapeDtypeStruct((nb, t_len, d_conv), BF16)]
    out_specs = [pl.BlockSpec((1, tm, d_conv), row)]
    if prompt:
        n_seg = 1
        out_shape += [jax.ShapeDtypeStruct((nb, HALO, d_conv), F32)]
        out_specs += [pl.BlockSpec((1, HALO, d_conv), lambda b, t: (b, 0, 0))]
    else:
        n_seg = state.shape[0]
        assert nb == 1 and n_t == 1 and tm % n_seg == 0 and state.shape[1] == CONV_WIDTH - 1
        out_shape += [jax.ShapeDtypeStruct((nb, t_len, d_conv), F32)]
        out_specs += [pl.BlockSpec((1, tm, d_conv), row)]
    out_shape += [jax.ShapeDtypeStruct((nb, 2, t_len, d_att), BF16)]
    out_specs += [halves]
    if prompt:
        out_shape += [jax.ShapeDtypeStruct((nb, t_len // LANES, d_att, LANES), BF16),
                      jax.ShapeDtypeStruct((nb, 2, t_len, d_att), BF16)]
        out_specs += [pl.BlockSpec((1, tm // LANES, d_att, LANES), lambda b, t: (b, t, 0, 0)), halves]
    out_shape += [jax.ShapeDtypeStruct((nb, keep, d_att), F32)] * 2
    out_specs += [pl.BlockSpec((1, tm, d_att), tail)] * 2
    cast_specs = []
    for w in cast:
        rb = w.shape[0] // (nb * n_t)
        assert prompt and w.ndim == 2 and rb * nb * n_t == w.shape[0] and rb % (2 * SUBLANES) == 0
        cast_specs.append(pl.BlockSpec((rb, w.shape[1]), lambda b, t: (b * n_t + t, 0)))
        out_shape.append(jax.ShapeDtypeStruct(w.shape, BF16))
    out_specs += cast_specs
    kern = functools.partial(_inproj_kernel, d_conv=d_conv, d_att=d_att, tail0=tail0, prompt=prompt,
                             n_cast=len(cast))
    extra = [] if prompt else [state]
    return pl.pallas_call(
        kern,
        grid=(nb, n_t),
        in_specs=[pl.BlockSpec((1, tm, d_model), row),
                  _const_spec((1, d_model)), _const_spec((d_model, d_in)),
                  _const_spec((1, d_att)), _const_spec((1, d_att))]
                 + [_const_spec(a.shape) for a in extra + list(conv_w)] + cast_specs,
        out_specs=out_specs,
        out_shape=out_shape,
        scratch_shapes=[pltpu.VMEM((n_seg, HALO + tm // n_seg + SUBLANES, d_conv), F32),
                        pltpu.VMEM((tm, d_conv), F32)],
        compiler_params=pltpu.CompilerParams(
            dimension_semantics=("arbitrary", "arbitrary"), vmem_limit_bytes=VMEM_LIMIT),
        name="in_proj",
    )(x, g_mix, w_in, g_q, g_k, *extra, *conv_w, *cast)


def _attn_scratch(n_heads, slots):
    return [pltpu.VMEM((slots, n_heads // 2, 2 * PAIR, WIN), F32),
            pltpu.VMEM((slots, n_heads // 2, PAIR, 2 * WIN), BF16)]


def _attend(rows, n_slabs, get_q, get_k, get_v, bias_ref, off, s_scr, p_scr):
    n_pairs = s_scr.shape[0]
    width = n_slabs * LANES
    for hp in range(n_pairs):
        s_scr[hp, 0:2 * rows, 0:width] = jnp.dot(get_q(hp), get_k(hp), preferred_element_type=F32)
    for hp in range(n_pairs):
        for sub in range(2):
            h = 2 * hp + sub
            for r0 in range(0, rows, SOFTMAX_ROWS):
                r = slice(r0, r0 + SOFTMAX_ROWS)
                s = s_scr[hp, sub * rows + r0:sub * rows + r0 + SOFTMAX_ROWS, 0:width]
                s = s + jnp.concatenate([bias_ref[h, off + c, r, :] for c in range(n_slabs)], axis=1)
                e = jnp.exp(s - jnp.max(s, axis=-1, keepdims=True))
                p_scr[hp, r, sub * width:(sub + 1) * width] = e.astype(BF16)
    first = lax.broadcasted_iota(jnp.int32, (width, LANES), 1) < HEAD_DIM
    sums = jnp.concatenate([jnp.where(first, 1.0, 0.0), jnp.where(first, 0.0, 1.0)], axis=0).astype(BF16)
    outs = []
    for hp in range(n_pairs):
        o2 = jnp.dot(p_scr[hp, 0:rows, 0:2 * width], jnp.concatenate([get_v(hp), sums], axis=1),
                     preferred_element_type=F32)
        outs.append(o2[:, 0:LANES] * (1.0 / o2[:, LANES:2 * LANES]))
    return jnp.concatenate(outs, axis=1)


def _attn_prompt_kernel(q_ref, kt_ref, v_ref, bias_ref, g_ref, o_ref, s_scr, p_scr):
    pairs = q_ref.shape[2] // PAIR
    slots = s_scr.shape[0]
    t = pl.program_id(1)

    def run_pairs(first_step):
        for i in range(pairs):
            p = i if first_step else t * pairs + i
            n_slabs = min(i + 1, WIN_SLABS) if first_step else WIN_SLABS
            ws = max(p - (WIN_SLABS - 1), 0) if first_step else p - (WIN_SLABS - 1)
            off = ws + (WIN_SLABS - 1) - p
            k0 = ws * LANES if first_step else pl.multiple_of(ws * LANES, LANES)
            width = n_slabs * LANES

            def get_k(hp, ws=ws, n_slabs=n_slabs):
                return jnp.concatenate(
                    [kt_ref[0, ws + c, hp * LANES:(hp + 1) * LANES, :] for c in range(n_slabs)], axis=1)

            def get_v(hp, k0=k0, width=width):
                lanes = slice(hp * LANES, (hp + 1) * LANES)
                return jnp.concatenate([v_ref[0, 0, pl.ds(k0, width), lanes],
                                        v_ref[0, 1, pl.ds(k0, width), lanes]], axis=0)

            rows = slice(i * PAIR, (i + 1) * PAIR)

            def get_q(hp, rows=rows):
                lanes = slice(hp * LANES, (hp + 1) * LANES)
                return jnp.concatenate([q_ref[0, 0, rows, lanes], q_ref[0, 1, rows, lanes]], axis=0)

            k = i % slots
            a = _attend(PAIR, n_slabs, get_q, get_k, get_v, bias_ref, off, s_scr.at[k], p_scr.at[k])
            o_ref[0, rows, :] = _rms(a, g_ref[...]).astype(BF16)

    pl.when(t == 0)(functools.partial(run_pairs, True))
    pl.when(t > 0)(functools.partial(run_pairs, False))


def _attn_prompt(q, kt, v, bias, g_att, *, tq):
    nb, _, t_len, d_att = q.shape
    n_heads = d_att // HEAD_DIM
    assert t_len % tq == 0 and tq % PAIR == 0 and tq // PAIR >= WIN_SLABS - 1
    return pl.pallas_call(
        _attn_prompt_kernel,
        grid=(nb, t_len // tq),
        in_specs=[pl.BlockSpec((1, 2, tq, d_att), lambda b, t: (b, 0, t, 0)),
                  pl.BlockSpec((1, t_len // LANES, d_att, LANES), lambda b, t: (b, 0, 0, 0)),
                  pl.BlockSpec((1, 2, t_len, d_att), lambda b, t: (b, 0, 0, 0)),
                  _const_spec(bias.shape), _const_spec((1, d_att))],
        out_specs=pl.BlockSpec((1, tq, d_att), lambda b, t: (b, t, 0)),
        out_shape=jax.ShapeDtypeStruct((nb, t_len, d_att), BF16),
        scratch_shapes=_attn_scratch(n_heads, slots=2),
        compiler_params=pltpu.CompilerParams(
            dimension_semantics=("arbitrary", "arbitrary"), vmem_limit_bytes=VMEM_LIMIT),
        name="attn_prompt",
    )(q, kt, v, bias, g_att)


def _attn_sample_kernel(q_ref, ck_ref, cv_ref, kn_ref, vn_ref, bias_ref, g_ref, o_ref, kt_s, v_s,
                        s_scr, p_scr):
    d_att = q_ref.shape[-1]
    pad = jnp.zeros((WIN - ck_ref.shape[1] - kn_ref.shape[1], d_att), F32)
    k_all = jnp.concatenate([ck_ref[0].astype(F32), kn_ref[0], pad], axis=0)
    for c in range(WIN_SLABS):
        kt_s[c] = k_all[c * LANES:(c + 1) * LANES, :].T.astype(BF16)
    v_all = jnp.concatenate([cv_ref[0].astype(F32), vn_ref[0], pad], axis=0)
    even = (lax.broadcasted_iota(jnp.int32, v_all.shape, 1) & HEAD_DIM) == 0
    v_s[0] = jnp.where(even, v_all, 0.0).astype(BF16)
    v_s[1] = jnp.where(even, 0.0, v_all).astype(BF16)

    def get_k(hp):
        return jnp.concatenate(
            [kt_s[c, hp * LANES:(hp + 1) * LANES, :] for c in range(WIN_SLABS)], axis=1)

    def get_v(hp):
        lanes = slice(hp * LANES, (hp + 1) * LANES)
        return jnp.concatenate([v_s[0, :, lanes], v_s[1, :, lanes]], axis=0)

    def get_q(hp):
        lanes = slice(hp * LANES, (hp + 1) * LANES)
        return jnp.concatenate([q_ref[0, 0, :, lanes], q_ref[0, 1, :, lanes]], axis=0)

    a = _attend(q_ref.shape[2], WIN_SLABS, get_q, get_k, get_v, bias_ref, 0, s_scr.at[0], p_scr.at[0])
    o_ref[0] = _rms(a, g_ref[...]).astype(BF16)


def _attn_sample(q, cache_k, cache_v, k_new, v_new, bias, g_att):
    nb, s_len, d_att = k_new.shape
    n_heads = d_att // HEAD_DIM
    l_cache = cache_k.shape[1]
    assert s_len == CHUNK and l_cache == LEFT_CHUNKS * CHUNK
    per_b = lambda b: (b, 0, 0)
    return pl.pallas_call(
        _attn_sample_kernel,
        grid=(nb,),
        in_specs=[pl.BlockSpec((1, 2, s_len, d_att), lambda b: (0, 0, b, 0)),
                  pl.BlockSpec((1, l_cache, d_att), per_b), pl.BlockSpec((1, l_cache, d_att), per_b),
                  pl.BlockSpec((1, s_len, d_att), per_b), pl.BlockSpec((1, s_len, d_att), per_b),
                  _const_spec(bias.shape), _const_spec((1, d_att))],
        out_specs=pl.BlockSpec((1, s_len, d_att), per_b),
        out_shape=jax.ShapeDtypeStruct((nb, s_len, d_att), BF16),
        scratch_shapes=[pltpu.VMEM((WIN_SLABS, d_att, LANES), BF16), pltpu.VMEM((2, WIN, d_att), BF16)]
                       + _attn_scratch(n_heads, slots=1),
        compiler_params=pltpu.CompilerParams(
            dimension_semantics=("arbitrary",), vmem_limit_bytes=VMEM_LIMIT),
        name="attn_sample",
    )(q, cache_k, cache_v, k_new, v_new, bias, g_att)


def _dense_stage(x, cn, an, p, wout_ref, gffn_ref, w1_ref, w2_ref, gple_ref, wg_ref, wple_ref, ff_chunk):
    d_conv = cn.shape[-1]
    x1 = (x + jnp.dot(cn, wout_ref[0:d_conv, :], preferred_element_type=F32)
          + jnp.dot(an, wout_ref[d_conv:, :], preferred_element_type=F32))
    hb = _rms(x1, gffn_ref[...]).astype(BF16)
    ffn = None
    for lo in range(0, w1_ref.shape[1], ff_chunk):
        a = jnp.maximum(jnp.dot(hb, w1_ref[:, lo:lo + ff_chunk], preferred_element_type=F32), 0.0)
        part = jnp.dot((a * a).astype(BF16), w2_ref[lo:lo + ff_chunk, :], preferred_element_type=F32)
        ffn = part if ffn is None else ffn + part
    x2 = x1 + ffn
    gate = jax.nn.sigmoid(jnp.dot(_rms(x2, gple_ref[...]).astype(BF16), wg_ref[...],
                                  preferred_element_type=F32))
    ple = jnp.dot(p.astype(BF16), wple_ref[...], preferred_element_type=F32)
    return x2 + ple * gate


def _mix_kernel(x_ref, p_ref, an_ref, cn_ref, *refs, n_split, ff_chunk):
    dense_w, o_ref = refs[:-1], refs[-1]
    hm = x_ref.shape[1] // n_split
    for h in range(n_split):
        r = slice(h * hm, (h + 1) * hm)
        o_ref[0, r, :] = _dense_stage(x_ref[0, r, :], cn_ref[0, r, :], an_ref[0, r, :], p_ref[0, r, :],
                                      *dense_w, ff_chunk)


def _mix(x, p, an, cn, w_out, g_ffn, w1, w2, g_ple, w_gate, w_ple, *, tm):
    ng, rows, d_model = x.shape
    assert rows % tm == 0
    row = lambda g, t: (g, t, 0)
    consts = [w_out, g_ffn, w1, w2, g_ple, w_gate, w_ple]
    return pl.pallas_call(
        functools.partial(_mix_kernel, n_split=2, ff_chunk=1024),
        grid=(ng, rows // tm),
        in_specs=[pl.BlockSpec((1, tm, a.shape[-1]), row) for a in (x, p, an, cn)]
                 + [_const_spec(c.shape) for c in consts],
        out_specs=pl.BlockSpec((1, tm, d_model), row),
        out_shape=jax.ShapeDtypeStruct(x.shape, F32),
        compiler_params=pltpu.CompilerParams(
            dimension_semantics=("arbitrary", "arbitrary"), vmem_limit_bytes=VMEM_LIMIT),
        name="mix",
    )(x, p, an, cn, *consts)


def kernel(x_prompt, x_sample, p_prompt, p_sample, cache_att_k, cache_att_v, state_conv, g_mix, w_in, w_dw,
           b_dw, g_conv_ln, b_conv_ln, g_q, g_k, rel_bias, g_conv_out, g_att_out, w_out, g_ffn, w_ff1, w_ff2,
           g_ple, w_gate, w_ple):
    depth = w_in.shape[0]
    nb, t_len, d_model = x_prompt.shape
    sb, s_len, _ = x_sample.shape
    d_conv = w_dw.shape[-1]
    d_att = g_att_out.shape[-1]
    n_heads = d_att // HEAD_DIM
    keep = min(LEFT_CHUNKS * CHUNK, t_len)
    tile = 512

    xp = x_prompt
    xs = x_sample.reshape(1, sb * s_len, d_model)
    outs = [[] for _ in range(6)]
    for i in range(depth):
        vec = lambda a: a[i].reshape(1, -1)
        gq = jnp.tile(g_q[i], n_heads).reshape(1, d_att)
        gk = jnp.tile(g_k[i], n_heads).reshape(1, d_att)
        w_in_b = w_in[i].astype(BF16)
        conv_w = (w_dw[i], vec(b_dw), vec(g_conv_ln), vec(b_conv_ln), vec(g_conv_out))
        bias = _bias_tiles(rel_bias[i])

        (cn_p, u_tail, q_p, kt_p, vb_p, k_tail, v_tail, w_out_b, w_ff1_b, w_ff2_b, w_gate_b) = _in_proj(
            xp, vec(g_mix), w_in_b, gq, gk, conv_w, tm=tile, keep=keep,
            cast=(w_out[i], w_ff1[i], w_ff2[i], w_gate[i]))
        mix_w = (w_out_b, vec(g_ffn), w_ff1_b, w_ff2_b, vec(g_ple), w_gate_b, w_ple[i].astype(BF16))
        an_p = _attn_prompt(q_p, kt_p, vb_p, bias, vec(g_att_out), tq=tile)
        xp = _mix(xp, p_prompt[i], an_p, cn_p, *mix_w, tm=tile)
        outs[0].append(k_tail.reshape(nb, keep, n_heads, HEAD_DIM))
        outs[1].append(v_tail.reshape(nb, keep, n_heads, HEAD_DIM))
        outs[2].append(u_tail[:, HALO - (CONV_WIDTH - 1):])

        rows = sb * s_len
        cn_s, u_s, q_s, k_s, v_s = _in_proj(xs, vec(g_mix), w_in_b, gq, gk, conv_w, tm=rows, keep=rows,
                                            state=state_conv[i])
        per_b = lambda a: a.reshape(sb, s_len, a.shape[-1])
        ck = cache_att_k[i].astype(BF16).reshape(sb, -1, d_att)
        cv = cache_att_v[i].astype(BF16).reshape(sb, -1, d_att)
        an_s = _attn_sample(q_s, ck, cv, per_b(k_s), per_b(v_s), bias, vec(g_att_out))
        xs = _mix(xs, p_sample[i].reshape(1, rows, -1), an_s.reshape(1, rows, d_att), cn_s, *mix_w, tm=rows)
        outs[3].append(k_s.reshape(sb, s_len, n_heads, HEAD_DIM))
        outs[4].append(v_s.reshape(sb, s_len, n_heads, HEAD_DIM))
        conv_in_tail = jnp.concatenate([state_conv[i], per_b(u_s)], axis=1)[:, -(CONV_WIDTH - 1):]
        outs[5].append(conv_in_tail)

    return (xp, xs.reshape(sb, s_len, d_model)) + tuple(jnp.stack(o) for o in outs)
```

```python
import functools

import jax
import jax.numpy as jnp
from jax import lax
from jax.experimental import pallas as pl
from jax.experimental.pallas import tpu as pltpu

F32 = jnp.float32
BF16 = jnp.bfloat16

CHUNK = 64
LEFT_CHUNKS = 8
HEAD_DIM = 64
CONV_WIDTH = 31
MAX_REL = 128
EPS = 1e-6
NEG = -1e30

LANES = 128
SUBLANES = 8
MXU_TILE = 256
PAIR = 2 * CHUNK
WIN = (LEFT_CHUNKS + 2) * CHUNK
WIN_SLABS = WIN // LANES
HALO = 32
NORM_ROWS = 32
SOFTMAX_ROWS = 16
VMEM_LIMIT = 56 * 1024 * 1024


def _rms(x, g):
    return x * lax.rsqrt(jnp.mean(x * x, axis=-1, keepdims=True) + EPS) * g


def _const_spec(shape):
    zeros = (0,) * len(shape)
    return pl.BlockSpec(shape, lambda *_: zeros, pipeline_mode=pl.Buffered(1))


def _bias_kernel(rb_ref, o_ref):
    n_heads = rb_ref.shape[0]
    rb = rb_ref[...]
    lane = lax.broadcasted_iota(jnp.int32, (n_heads, LANES), 1)
    b0 = jnp.broadcast_to(rb[:, 0:1], (n_heads, LANES))
    t4 = jnp.where(lane < CHUNK, rb[:, LANES:2 * LANES], b0)
    t = jnp.concatenate([b0, b0, b0, rb[:, 0:LANES], t4, b0], axis=1)
    r_i = lax.broadcasted_iota(jnp.int32, (PAIR, LANES), 0)
    c_i = lax.broadcasted_iota(jnp.int32, (PAIR, LANES), 1)
    for h in range(n_heads):
        row = jnp.broadcast_to(t[h:h + 1, :], (PAIR, t.shape[1]))
        toe = pltpu.roll(row, 0, 1, stride=1, stride_axis=0)
        for c in range(WIN_SLABS):
            jj = c_i + c * LANES
            valid = ((r_i < CHUNK) & (jj < WIN - CHUNK)) | ((r_i >= CHUNK) & (jj >= CHUNK))
            o_ref[h, c] = jnp.where(valid, toe[:, c * LANES:(c + 1) * LANES], NEG)


def _bias_tiles(rel_bias):
    n_heads = rel_bias.shape[0]
    return pl.pallas_call(
        _bias_kernel,
        out_shape=jax.ShapeDtypeStruct((n_heads, WIN_SLABS, PAIR, LANES), F32),
        name="bias_tiles",
    )(rel_bias)


def _conv_stage(u, n_seg, halo_ref, ubuf, y_scr, cn_ref, wdw_ref, bdw_ref, gln_ref, bln_ref, gco_ref):
    d_conv = u.shape[-1]
    seg = u.shape[0] // n_seg
    off = HALO - (CONV_WIDTH - 1)
    n_a = HALO // SUBLANES + 1
    for s in range(n_seg):
        if halo_ref is not None:
            hr = halo_ref.shape[1]
            if hr < HALO:
                ubuf[s, 0:HALO - hr, :] = jnp.zeros((HALO - hr, d_conv), F32)
            ubuf[s, HALO - hr:HALO, :] = halo_ref[s]
        ubuf[s, HALO:HALO + seg, :] = u[s * seg:(s + 1) * seg, :]
        ubuf[s, HALO + seg:HALO + seg + SUBLANES, :] = jnp.zeros((SUBLANES, d_conv), F32)

    sub = lax.broadcasted_iota(jnp.int32, (SUBLANES, LANES), 0)
    from_lower = [sub < SUBLANES - b for b in range(SUBLANES)]
    for lt in range(d_conv // LANES):
        lanes = slice(lt * LANES, (lt + 1) * LANES)

        w = [jnp.broadcast_to(wdw_ref[j:j + 1, lanes], (SUBLANES, LANES)) for j in range(CONV_WIDTH)]

        def rotated_groups(s, blk):
            rows = [ubuf[s, SUBLANES * (blk + a):SUBLANES * (blk + a + 1), lanes] for a in range(n_a)]
            out = []
            for b in range(SUBLANES):
                z = None
                for a in range(n_a):
                    j = SUBLANES * a + b - off
                    if 0 <= j < CONV_WIDTH:
                        t = w[j] * rows[a]
                        z = t if z is None else z + t
                out.append(z if b == 0 else pltpu.roll(z, SUBLANES - b, 0))
            return out

        for s in range(n_seg):
            lower = rotated_groups(s, 0)
            for blk in range(seg // SUBLANES):
                upper = rotated_groups(s, blk + 1)
                y = lower[0]
                for b in range(1, SUBLANES):
                    y = y + jnp.where(from_lower[b], lower[b], upper[b])
                y_scr[s * seg + blk * SUBLANES:s * seg + (blk + 1) * SUBLANES, lanes] = y
                lower = upper

    for r0 in range(0, n_seg * seg, NORM_ROWS):
        y = y_scr[r0:r0 + NORM_ROWS, :] + bdw_ref[...]
        mu = jnp.mean(y, axis=-1, keepdims=True)
        yc = y - mu
        yn = yc * lax.rsqrt(jnp.mean(yc * yc, axis=-1, keepdims=True) + EPS) * gln_ref[...] + bln_ref[...]
        c = yn * jax.nn.sigmoid(yn)
        cn_ref[r0:r0 + NORM_ROWS, :] = _rms(c, gco_ref[...]).astype(BF16)


def _inproj_kernel(*refs, d_conv, d_att, tail0, prompt, n_cast):
    if prompt:
        x_ref, gmix_ref, win_ref, gq_ref, gk_ref = refs[:5]
        halo_ref, conv_w = None, refs[5:10]
        cast_in = refs[10:10 + n_cast]
        cn_ref, ut_ref, q_ref, kt_ref, vb_ref, k32_ref, v32_ref = refs[10 + n_cast:17 + n_cast]
        cast_out = refs[17 + n_cast:17 + 2 * n_cast]
        ubuf, y_scr = refs[17 + 2 * n_cast:]
        t = pl.program_id(1)
        tm = x_ref.shape[1]

        for src, dst in zip(cast_in, cast_out):
            dst[...] = src[...].astype(BF16)

        @pl.when(t == 0)
        def _():
            ubuf[0, 0:HALO, :] = jnp.zeros((HALO, d_conv), F32)

        @pl.when(t > 0)
        def _():
            ubuf[0, 0:HALO, :] = ubuf[0, tm:tm + HALO, :]
    else:
        x_ref, gmix_ref, win_ref, gq_ref, gk_ref, halo_ref = refs[:6]
        conv_w = refs[6:11]
        cn_ref, u_ref, q_ref, k32_ref, v32_ref, ubuf, y_scr = refs[11:]
        tm = x_ref.shape[1]
    h = _rms(x_ref[0], gmix_ref[...]).astype(BF16)

    def proj(lo, width):
        return jnp.dot(h, win_ref[:, lo:lo + width], preferred_element_type=F32)

    u = proj(0, d_conv) * jax.nn.sigmoid(proj(d_conv, d_conv))
    if prompt:
        @pl.when(t == pl.num_programs(1) - 1)
        def _():
            ut_ref[0] = u[tm - HALO:tm, :]
    else:
        u_ref[0] = u
    _conv_stage(u, ubuf.shape[0], halo_ref, ubuf, y_scr, cn_ref.at[0], *conv_w)

    r_i = lax.shift_right_logical(lax.broadcasted_iota(jnp.int32, (MXU_TILE, MXU_TILE), 0), 6)
    c_i = lax.shift_right_logical(lax.broadcasted_iota(jnp.int32, (MXU_TILE, MXU_TILE), 1), 6)
    avg = jnp.where(r_i == c_i, 1.0 / HEAD_DIM, 0.0).astype(BF16)

    def head_rms(z, g):
        sq = (z * z).astype(BF16)
        ms = jnp.concatenate([jnp.dot(sq[:, lo:lo + MXU_TILE], avg, preferred_element_type=F32)
                              for lo in range(0, d_att, MXU_TILE)], axis=1)
        return z * lax.rsqrt(ms + EPS) * g

    o = 2 * d_conv
    even = (lax.broadcasted_iota(jnp.int32, (tm, d_att), 1) & HEAD_DIM) == 0

    def split_heads(ref, z):
        ref[0, 0] = jnp.where(even, z, 0.0).astype(BF16)
        ref[0, 1] = jnp.where(even, 0.0, z).astype(BF16)

    qn = head_rms(proj(o, d_att), gq_ref[...])
    split_heads(q_ref, qn * (HEAD_DIM ** -0.5))
    kn = head_rms(proj(o + d_att, d_att), gk_ref[...])
    v = proj(o + 2 * d_att, d_att)
    if prompt:
        for s in range(tm // LANES):
            kt_ref[0, s] = kn[s * LANES:(s + 1) * LANES, :].T.astype(BF16)
        split_heads(vb_ref, v)

    @pl.when(pl.program_id(1) >= tail0)
    def _():
        k32_ref[0] = kn
        v32_ref[0] = v


def _in_proj(x, g_mix, w_in, g_q, g_k, conv_w, *, tm, keep, state=None, cast=()):
    nb, t_len, d_model = x.shape
    d_att = g_q.shape[-1]
    d_in = w_in.shape[-1]
    d_conv = (d_in - 3 * d_att) // 2
    n_t = t_len // tm
    prompt = state is None
    assert t_len % tm == 0 and (t_len - keep) % tm == 0 and tm % LANES == 0 and tm >= HALO
    tail0 = (t_len - keep) // tm
    row = lambda b, t: (b, t, 0)
    tail = lambda b, t: (b, jnp.maximum(t - tail0, 0), 0)
    halves = pl.BlockSpec((1, 2, tm, d_att), lambda b, t: (b, 0, t, 0))
    out_shape = [jax.ShapeDtypeStruct((nb, t_len, d_conv), BF16)]
    out_specs = [pl.BlockSpec((1, tm, d_conv), row)]
    if prompt:
        n_seg = 1
        out_shape += [jax.ShapeDtypeStruct((nb, HALO, d_conv), F32)]
        out_specs += [pl.BlockSpec((1, HALO, d_conv), lambda b, t: (b, 0, 0))]
    else:
        n_seg = state.shape[0]
        assert nb == 1 and n_t == 1 and tm % n_seg == 0 and state.shape[1] == CONV_WIDTH - 1
        out_shape += [jax.ShapeDtypeStruct((nb, t_len, d_conv), F32)]
        out_specs += [pl.BlockSpec((1, tm, d_conv), row)]
    out_shape += [jax.ShapeDtypeStruct((nb, 2, t_len, d_att), BF16)]
    out_specs += [halves]
    if prompt:
        out_shape += [jax.ShapeDtypeStruct((nb, t_len // LANES, d_att, LANES), BF16),
                      jax.ShapeDtypeStruct((nb, 2, t_len, d_att), BF16)]
        out_specs += [pl.BlockSpec((1, tm // LANES, d_att, LANES), lambda b, t: (b, t, 0, 0)), halves]
    out_shape += [jax.ShapeDtypeStruct((nb, keep, d_att), F32)] * 2
    out_specs += [pl.BlockSpec((1, tm, d_att), tail)] * 2
    cast_specs = []
    for w in cast:
        rb = w.shape[0] // (nb * n_t)
        assert prompt and w.ndim == 2 and rb * nb * n_t == w.shape[0] and rb % (2 * SUBLANES) == 0
        cast_specs.append(pl.BlockSpec((rb, w.shape[1]), lambda b, t: (b * n_t + t, 0)))
        out_shape.append(jax.ShapeDtypeStruct(w.shape, BF16))
    out_specs += cast_specs
    kern = functools.partial(_inproj_kernel, d_conv=d_conv, d_att=d_att, tail0=tail0, prompt=prompt,
                             n_cast=len(cast))
    extra = [] if prompt else [state]
    return pl.pallas_call(
        kern,
        grid=(nb, n_t),
        in_specs=[pl.BlockSpec((1, tm, d_model), row),
                  _const_spec((1, d_model)), _const_spec((d_model, d_in)),
                  _const_spec((1, d_att)), _const_spec((1, d_att))]
                 + [_const_spec(a.shape) for a in extra + list(conv_w)] + cast_specs,
        out_specs=out_specs,
        out_shape=out_shape,
        scratch_shapes=[pltpu.VMEM((n_seg, HALO + tm // n_seg + SUBLANES, d_conv), F32),
                        pltpu.VMEM((tm, d_conv), F32)],
        compiler_params=pltpu.CompilerParams(
            dimension_semantics=("arbitrary", "arbitrary"), vmem_limit_bytes=VMEM_LIMIT),
        name="in_proj",
    )(x, g_mix, w_in, g_q, g_k, *extra, *conv_w, *cast)


def _attn_scratch(n_heads, slots):
    return [pltpu.VMEM((slots, n_heads // 2, 2 * PAIR, WIN), F32),
            pltpu.VMEM((slots, n_heads // 2, PAIR, 2 * WIN), BF16)]


def _attend(rows, n_slabs, get_q, get_k, get_v, bias_ref, off, s_scr, p_scr):
    n_pairs = s_scr.shape[0]
    width = n_slabs * LANES
    for hp in range(n_pairs):
        s_scr[hp, 0:2 * rows, 0:width] = jnp.dot(get_q(hp), get_k(hp), preferred_element_type=F32)
    for hp in range(n_pairs):
        for sub in range(2):
            h = 2 * hp + sub
            for r0 in range(0, rows, SOFTMAX_ROWS):
                r = slice(r0, r0 + SOFTMAX_ROWS)
                s = s_scr[hp, sub * rows + r0:sub * rows + r0 + SOFTMAX_ROWS, 0:width]
                s = s + jnp.concatenate([bias_ref[h, off + c, r, :] for c in range(n_slabs)], axis=1)
                e = jnp.exp(s - jnp.max(s, axis=-1, keepdims=True))
                p_scr[hp, r, sub * width:(sub + 1) * width] = e.astype(BF16)
    first = lax.broadcasted_iota(jnp.int32, (width, LANES), 1) < HEAD_DIM
    sums = jnp.concatenate([jnp.where(first, 1.0, 0.0), jnp.where(first, 0.0, 1.0)], axis=0).astype(BF16)
    outs = []
    for hp in range(n_pairs):
        o2 = jnp.dot(p_scr[hp, 0:rows, 0:2 * width], jnp.concatenate([get_v(hp), sums], axis=1),
                     preferred_element_type=F32)
        outs.append(o2[:, 0:LANES] * (1.0 / o2[:, LANES:2 * LANES]))
    return jnp.concatenate(outs, axis=1)


def _attn_prompt_kernel(q_ref, kt_ref, v_ref, bias_ref, g_ref, o_ref, s_scr, p_scr):
    pairs = q_ref.shape[2] // PAIR
    slots = s_scr.shape[0]
    t = pl.program_id(1)

    def run_pairs(first_step):
        for i in range(pairs):
            p = i if first_step else t * pairs + i
            n_slabs = min(i + 1, WIN_SLABS) if first_step else WIN_SLABS
            ws = max(p - (WIN_SLABS - 1), 0) if first_step else p - (WIN_SLABS - 1)
            off = ws + (WIN_SLABS - 1) - p
            k0 = ws * LANES if first_step else pl.multiple_of(ws * LANES, LANES)
            width = n_slabs * LANES

            def get_k(hp, ws=ws, n_slabs=n_slabs):
                return jnp.concatenate(
                    [kt_ref[0, ws + c, hp * LANES:(hp + 1) * LANES, :] for c in range(n_slabs)], axis=1)

            def get_v(hp, k0=k0, width=width):
                lanes = slice(hp * LANES, (hp + 1) * LANES)
                return jnp.concatenate([v_ref[0, 0, pl.ds(k0, width), lanes],
                                        v_ref[0, 1, pl.ds(k0, width), lanes]], axis=0)

            rows = slice(i * PAIR, (i + 1) * PAIR)

            def get_q(hp, rows=rows):
                lanes = slice(hp * LANES, (hp + 1) * LANES)
                return jnp.concatenate([q_ref[0, 0, rows, lanes], q_ref[0, 1, rows, lanes]], axis=0)

            k = i % slots
            a = _attend(PAIR, n_slabs, get_q, get_k, get_v, bias_ref, off, s_scr.at[k], p_scr.at[k])
            o_ref[0, rows, :] = _rms(a, g_ref[...]).astype(BF16)

    pl.when(t == 0)(functools.partial(run_pairs, True))
    pl.when(t > 0)(functools.partial(run_pairs, False))


def _attn_prompt(q, kt, v, bias, g_att, *, tq):
    nb, _, t_len, d_att = q.shape
    n_heads = d_att // HEAD_DIM
    assert t_len % tq == 0 and tq % PAIR == 0 and tq // PAIR >= WIN_SLABS - 1
    return pl.pallas_call(
        _attn_prompt_kernel,
        grid=(nb, t_len // tq),
        in_specs=[pl.BlockSpec((1, 2, tq, d_att), lambda b, t: (b, 0, t, 0)),
                  pl.BlockSpec((1, t_len // LANES, d_att, LANES), lambda b, t: (b, 0, 0, 0)),
                  pl.BlockSpec((1, 2, t_len, d_att), lambda b, t: (b, 0, 0, 0)),
                  _const_spec(bias.shape), _const_spec((1, d_att))],
        out_specs=pl.BlockSpec((1, tq, d_att), lambda b, t: (b, t, 0)),
        out_shape=jax.ShapeDtypeStruct((nb, t_len, d_att), BF16),
        scratch_shapes=_attn_scratch(n_heads, slots=2),
        compiler_params=pltpu.CompilerParams(
            dimension_semantics=("arbitrary", "arbitrary"), vmem_limit_bytes=VMEM_LIMIT),
        name="attn_prompt",
    )(q, kt, v, bias, g_att)


def _attn_sample_kernel(q_ref, ck_ref, cv_ref, kn_ref, vn_ref, bias_ref, g_ref, o_ref, kt_s, v_s,
                        s_scr, p_scr):
    d_att = q_ref.shape[-1]
    pad = jnp.zeros((WIN - ck_ref.shape[1] - kn_ref.shape[1], d_att), F32)
    k_all = jnp.concatenate([ck_ref[0].astype(F32), kn_ref[0], pad], axis=0)
    for c in range(WIN_SLABS):
        kt_s[c] = k_all[c * LANES:(c + 1) * LANES, :].T.astype(BF16)
    v_all = jnp.concatenate([cv_ref[0].astype(F32), vn_ref[0], pad], axis=0)
    even = (lax.broadcasted_iota(jnp.int32, v_all.shape, 1) & HEAD_DIM) == 0
    v_s[0] = jnp.where(even, v_all, 0.0).astype(BF16)
    v_s[1] = jnp.where(even, 0.0, v_all).astype(BF16)

    def get_k(hp):
        return jnp.concatenate(
            [kt_s[c, hp * LANES:(hp + 1) * LANES, :] for c in range(WIN_SLABS)], axis=1)

    def get_v(hp):
        lanes = slice(hp * LANES, (hp + 1) * LANES)
        return jnp.concatenate([v_s[0, :, lanes], v_s[1, :, lanes]], axis=0)

    def get_q(hp):
        lanes = slice(hp * LANES, (hp + 1) * LANES)
        return jnp.concatenate([q_ref[0, 0, :, lanes], q_ref[0, 1, :, lanes]], axis=0)

    a = _attend(q_ref.shape[2], WIN_SLABS, get_q, get_k, get_v, bias_ref, 0, s_scr.at[0], p_scr.at[0])
    o_ref[0] = _rms(a, g_ref[...]).astype(BF16)


def _attn_sample(q, cache_k, cache_v, k_new, v_new, bias, g_att):
    nb, s_len, d_att = k_new.shape
    n_heads = d_att // HEAD_DIM
    l_cache = cache_k.shape[1]
    assert s_len == CHUNK and l_cache == LEFT_CHUNKS * CHUNK
    per_b = lambda b: (b, 0, 0)
    return pl.pallas_call(
        _attn_sample_kernel,
        grid=(nb,),
        in_specs=[pl.BlockSpec((1, 2, s_len, d_att), lambda b: (0, 0, b, 0)),
                  pl.BlockSpec((1, l_cache, d_att), per_b), pl.BlockSpec((1, l_cache, d_att), per_b),
                  pl.BlockSpec((1, s_len, d_att), per_b), pl.BlockSpec((1, s_len, d_att), per_b),
                  _const_spec(bias.shape), _const_spec((1, d_att))],
        out_specs=pl.BlockSpec((1, s_len, d_att), per_b),
        out_shape=jax.ShapeDtypeStruct((nb, s_len, d_att), BF16),
        scratch_shapes=[pltpu.VMEM((WIN_SLABS, d_att, LANES), BF16), pltpu.VMEM((2, WIN, d_att), BF16)]
                       + _attn_scratch(n_heads, slots=1),
        compiler_params=pltpu.CompilerParams(
            dimension_semantics=("arbitrary",), vmem_limit_bytes=VMEM_LIMIT),
        name="attn_sample",
    )(q, cache_k, cache_v, k_new, v_new, bias, g_att)


def _dense_stage(x, cn, an, p, wout_ref, gffn_ref, w1_ref, w2_ref, gple_ref, wg_ref, wple_ref, ff_chunk):
    d_conv = cn.shape[-1]
    x1 = (x + jnp.dot(cn, wout_ref[0:d_conv, :], preferred_element_type=F32)
          + jnp.dot(an, wout_ref[d_conv:, :], preferred_element_type=F32))
    hb = _rms(x1, gffn_ref[...]).astype(BF16)
    ffn = None
    for lo in range(0, w1_ref.shape[1], ff_chunk):
        a = jnp.maximum(jnp.dot(hb, w1_ref[:, lo:lo + ff_chunk], preferred_element_type=F32), 0.0)
        part = jnp.dot((a * a).astype(BF16), w2_ref[lo:lo + ff_chunk, :], preferred_element_type=F32)
        ffn = part if ffn is None else ffn + part
    x2 = x1 + ffn
    gate = jax.nn.sigmoid(jnp.dot(_rms(x2, gple_ref[...]).astype(BF16), wg_ref[...],
                                  preferred_element_type=F32))
    ple = jnp.dot(p.astype(BF16), wple_ref[...], preferred_element_type=F32)
    return x2 + ple * gate


def _mix_kernel(*refs, starts, ff_chunk):
    n_groups = len(starts) - 1
    ins, dense_w, outs = refs[:4 * n_groups], refs[4 * n_groups:-n_groups], refs[-n_groups:]
    i = pl.program_id(0)
    for g in range(n_groups):
        x_ref, p_ref, an_ref, cn_ref = ins[4 * g:4 * g + 4]

        @pl.when((i >= starts[g]) & (i < starts[g + 1]))
        def _():
            outs[g][0] = _dense_stage(x_ref[0], cn_ref[0], an_ref[0], p_ref[0], *dense_w, ff_chunk)


def _mix(groups, w_out, g_ffn, w1, w2, g_ple, w_gate, w_ple, *, tm):
    consts = [w_out, g_ffn, w1, w2, g_ple, w_gate, w_ple]
    starts, in_specs, out_specs, out_shape, operands = [0], [], [], [], []
    for x, p, an, cn in groups:
        ng, rows, d_model = x.shape
        assert rows % tm == 0
        n_t = rows // tm
        first, count = starts[-1], ng * n_t
        starts.append(first + count)

        def tile(i, first=first, count=count, n_t=n_t):
            j = jnp.clip(i - first, 0, count - 1)
            return (j // n_t, j % n_t, 0)

        in_specs += [pl.BlockSpec((1, tm, a.shape[-1]), tile) for a in (x, p, an, cn)]
        out_specs.append(pl.BlockSpec((1, tm, d_model), tile))
        out_shape.append(jax.ShapeDtypeStruct(x.shape, F32))
        operands += [x, p, an, cn]
    return pl.pallas_call(
        functools.partial(_mix_kernel, starts=tuple(starts), ff_chunk=1024),
        grid=(starts[-1],),
        in_specs=in_specs + [_const_spec(c.shape) for c in consts],
        out_specs=out_specs,
        out_shape=out_shape,
        compiler_params=pltpu.CompilerParams(
            dimension_semantics=("arbitrary",), vmem_limit_bytes=VMEM_LIMIT),
        name="mix",
    )(*operands, *consts)


def kernel(x_prompt, x_sample, p_prompt, p_sample, cache_att_k, cache_att_v, state_conv, g_mix, w_in, w_dw,
           b_dw, g_conv_ln, b_conv_ln, g_q, g_k, rel_bias, g_conv_out, g_att_out, w_out, g_ffn, w_ff1, w_ff2,
           g_ple, w_gate, w_ple):
    depth = w_in.shape[0]
    nb, t_len, d_model = x_prompt.shape
    sb, s_len, _ = x_sample.shape
    d_conv = w_dw.shape[-1]
    d_att = g_att_out.shape[-1]
    n_heads = d_att // HEAD_DIM
    keep = min(LEFT_CHUNKS * CHUNK, t_len)
    tile = 512

    xp = x_prompt
    xs = x_sample.reshape(1, sb * s_len, d_model)
    outs = [[] for _ in range(6)]
    for i in range(depth):
        vec = lambda a: a[i].reshape(1, -1)
        gq = jnp.tile(g_q[i], n_heads).reshape(1, d_att)
        gk = jnp.tile(g_k[i], n_heads).reshape(1, d_att)
        w_in_b = w_in[i].astype(BF16)
        conv_w = (w_dw[i], vec(b_dw), vec(g_conv_ln), vec(b_conv_ln), vec(g_conv_out))
        bias = _bias_tiles(rel_bias[i])

        (cn_p, u_tail, q_p, kt_p, vb_p, k_tail, v_tail, w_out_b, w_ff1_b, w_ff2_b, w_gate_b) = _in_proj(
            xp, vec(g_mix), w_in_b, gq, gk, conv_w, tm=tile, keep=keep,
            cast=(w_out[i], w_ff1[i], w_ff2[i], w_gate[i]))
        mix_w = (w_out_b, vec(g_ffn), w_ff1_b, w_ff2_b, vec(g_ple), w_gate_b, w_ple[i].astype(BF16))
        an_p = _attn_prompt(q_p, kt_p, vb_p, bias, vec(g_att_out), tq=tile)
        outs[0].append(k_tail.reshape(nb, keep, n_heads, HEAD_DIM))
        outs[1].append(v_tail.reshape(nb, keep, n_heads, HEAD_DIM))
        outs[2].append(u_tail[:, HALO - (CONV_WIDTH - 1):])

        rows = sb * s_len
        cn_s, u_s, q_s, k_s, v_s = _in_proj(xs, vec(g_mix), w_in_b, gq, gk, conv_w, tm=rows, keep=rows,
                                            state=state_conv[i])
        per_b = lambda a: a.reshape(sb, s_len, a.shape[-1])
        ck = cache_att_k[i].astype(BF16).reshape(sb, -1, d_att)
        cv = cache_att_v[i].astype(BF16).reshape(sb, -1, d_att)
        an_s = _attn_sample(q_s, ck, cv, per_b(k_s), per_b(v_s), bias, vec(g_att_out))
        outs[3].append(k_s.reshape(sb, s_len, n_heads, HEAD_DIM))
        outs[4].append(v_s.reshape(sb, s_len, n_heads, HEAD_DIM))
        conv_in_tail = jnp.concatenate([state_conv[i], per_b(u_s)], axis=1)[:, -(CONV_WIDTH - 1):]
        outs[5].append(conv_in_tail)

        xp, xs = _mix([(xp, p_prompt[i], an_p, cn_p),
                       (xs, p_sample[i].reshape(1, rows, -1), an_s.reshape(1, rows, d_att), cn_s)],
                      *mix_w, tm=tile)

    return (xp, xs.reshape(sb, s_len, d_model)) + tuple(jnp.stack(o) for o in outs)
```

```python
import functools

import jax
import jax.numpy as jnp
from jax import lax
from jax.experimental import pallas as pl
from jax.experimental.pallas import tpu as pltpu

F32 = jnp.float32
BF16 = jnp.bfloat16

CHUNK = 64
LEFT_CHUNKS = 8
HEAD_DIM = 64
CONV_WIDTH = 31
MAX_REL = 128
EPS = 1e-6
NEG = -1e30

LANES = 128
SUBLANES = 8
MXU_TILE = 256
PAIR = 2 * CHUNK
WIN = (LEFT_CHUNKS + 2) * CHUNK
WIN_SLABS = WIN // LANES
HALO = 32
NORM_ROWS = 32
SOFTMAX_ROWS = 16
VMEM_LIMIT = 56 * 1024 * 1024


def _rms(x, g):
    return x * lax.rsqrt(jnp.mean(x * x, axis=-1, keepdims=True) + EPS) * g


def _const_spec(shape):
    zeros = (0,) * len(shape)
    return pl.BlockSpec(shape, lambda *_: zeros, pipeline_mode=pl.Buffered(1))


def _bias_kernel(rb_ref, o_ref):
    n_heads = rb_ref.shape[0]
    rb = rb_ref[...]
    lane = lax.broadcasted_iota(jnp.int32, (n_heads, LANES), 1)
    b0 = jnp.broadcast_to(rb[:, 0:1], (n_heads, LANES))
    t4 = jnp.where(lane < CHUNK, rb[:, LANES:2 * LANES], b0)
    t = jnp.concatenate([b0, b0, b0, rb[:, 0:LANES], t4, b0], axis=1)
    r_i = lax.broadcasted_iota(jnp.int32, (PAIR, LANES), 0)
    c_i = lax.broadcasted_iota(jnp.int32, (PAIR, LANES), 1)
    for h in range(n_heads):
        row = jnp.broadcast_to(t[h:h + 1, :], (PAIR, t.shape[1]))
        toe = pltpu.roll(row, 0, 1, stride=1, stride_axis=0)
        for c in range(WIN_SLABS):
            jj = c_i + c * LANES
            valid = ((r_i < CHUNK) & (jj < WIN - CHUNK)) | ((r_i >= CHUNK) & (jj >= CHUNK))
            o_ref[h, c] = jnp.where(valid, toe[:, c * LANES:(c + 1) * LANES], NEG)


def _bias_tiles(rel_bias):
    n_heads = rel_bias.shape[0]
    return pl.pallas_call(
        _bias_kernel,
        out_shape=jax.ShapeDtypeStruct((n_heads, WIN_SLABS, PAIR, LANES), F32),
        name="bias_tiles",
    )(rel_bias)


def _conv_stage(u, n_seg, halo_ref, ubuf, y_scr, cn_ref, wdw_ref, bdw_ref, gln_ref, bln_ref, gco_ref):
    d_conv = u.shape[-1]
    seg = u.shape[0] // n_seg
    off = HALO - (CONV_WIDTH - 1)
    n_a = HALO // SUBLANES + 1
    for s in range(n_seg):
        if halo_ref is not None:
            hr = halo_ref.shape[1]
            if hr < HALO:
                ubuf[s, 0:HALO - hr, :] = jnp.zeros((HALO - hr, d_conv), F32)
            ubuf[s, HALO - hr:HALO, :] = halo_ref[s]
        ubuf[s, HALO:HALO + seg, :] = u[s * seg:(s + 1) * seg, :]
        ubuf[s, HALO + seg:HALO + seg + SUBLANES, :] = jnp.zeros((SUBLANES, d_conv), F32)

    sub = lax.broadcasted_iota(jnp.int32, (SUBLANES, LANES), 0)
    from_lower = [sub < SUBLANES - b for b in range(SUBLANES)]
    for lt in range(d_conv // LANES):
        lanes = slice(lt * LANES, (lt + 1) * LANES)

        w = [jnp.broadcast_to(wdw_ref[j:j + 1, lanes], (SUBLANES, LANES)) for j in range(CONV_WIDTH)]

        def rotated_groups(s, blk):
            rows = [ubuf[s, SUBLANES * (blk + a):SUBLANES * (blk + a + 1), lanes] for a in range(n_a)]
            out = []
            for b in range(SUBLANES):
                z = None
                for a in range(n_a):
                    j = SUBLANES * a + b - off
                    if 0 <= j < CONV_WIDTH:
                        t = w[j] * rows[a]
                        z = t if z is None else z + t
                out.append(z if b == 0 else pltpu.roll(z, SUBLANES - b, 0))
            return out

        for s in range(n_seg):
            lower = rotated_groups(s, 0)
            for blk in range(seg // SUBLANES):
                upper = rotated_groups(s, blk + 1)
                y = lower[0]
                for b in range(1, SUBLANES):
                    y = y + jnp.where(from_lower[b], lower[b], upper[b])
                y_scr[s * seg + blk * SUBLANES:s * seg + (blk + 1) * SUBLANES, lanes] = y
                lower = upper

    for r0 in range(0, n_seg * seg, NORM_ROWS):
        y = y_scr[r0:r0 + NORM_ROWS, :] + bdw_ref[...]
        mu = jnp.mean(y, axis=-1, keepdims=True)
        yc = y - mu
        yn = yc * lax.rsqrt(jnp.mean(yc * yc, axis=-1, keepdims=True) + EPS) * gln_ref[...] + bln_ref[...]
        c = yn * jax.nn.sigmoid(yn)
        cn_ref[r0:r0 + NORM_ROWS, :] = _rms(c, gco_ref[...]).astype(BF16)


def _inproj_kernel(*refs, d_conv, d_att, tail0, prompt, n_cast):
    if prompt:
        x_ref, gmix_ref, win_ref, gq_ref, gk_ref = refs[:5]
        halo_ref, conv_w = None, refs[5:10]
        cast_in = refs[10:10 + n_cast]
        cn_ref, ut_ref, q_ref, kt_ref, vb_ref, k32_ref, v32_ref = refs[10 + n_cast:17 + n_cast]
        cast_out = refs[17 + n_cast:17 + 2 * n_cast]
        ubuf, y_scr = refs[17 + 2 * n_cast:]
        t = pl.program_id(1)
        tm = x_ref.shape[1]

        for src, dst in zip(cast_in, cast_out):
            dst[...] = src[...].astype(BF16)

        @pl.when(t == 0)
        def _():
            ubuf[0, 0:HALO, :] = jnp.zeros((HALO, d_conv), F32)

        @pl.when(t > 0)
        def _():
            ubuf[0, 0:HALO, :] = ubuf[0, tm:tm + HALO, :]
    else:
        x_ref, gmix_ref, win_ref, gq_ref, gk_ref, halo_ref = refs[:6]
        conv_w = refs[6:11]
        cn_ref, u_ref, q_ref, k32_ref, v32_ref, ubuf, y_scr = refs[11:]
        tm = x_ref.shape[1]
    h = _rms(x_ref[0], gmix_ref[...]).astype(BF16)

    def proj(lo, width):
        return jnp.dot(h, win_ref[:, lo:lo + width], preferred_element_type=F32)

    u = proj(0, d_conv) * jax.nn.sigmoid(proj(d_conv, d_conv))
    if prompt:
        @pl.when(t == pl.num_programs(1) - 1)
        def _():
            ut_ref[0] = u[tm - HALO:tm, :]
    else:
        u_ref[0] = u
    _conv_stage(u, ubuf.shape[0], halo_ref, ubuf, y_scr, cn_ref.at[0], *conv_w)

    r_i = lax.shift_right_logical(lax.broadcasted_iota(jnp.int32, (MXU_TILE, MXU_TILE), 0), 6)
    c_i = lax.shift_right_logical(lax.broadcasted_iota(jnp.int32, (MXU_TILE, MXU_TILE), 1), 6)
    avg = jnp.where(r_i == c_i, 1.0 / HEAD_DIM, 0.0).astype(BF16)

    def head_rms(z, g):
        sq = (z * z).astype(BF16)
        ms = jnp.concatenate([jnp.dot(sq[:, lo:lo + MXU_TILE], avg, preferred_element_type=F32)
                              for lo in range(0, d_att, MXU_TILE)], axis=1)
        return z * lax.rsqrt(ms + EPS) * g

    o = 2 * d_conv
    qn = head_rms(proj(o, d_att), gq_ref[...])
    q_ref[0] = (qn * (HEAD_DIM ** -0.5)).astype(BF16)
    kn = head_rms(proj(o + d_att, d_att), gk_ref[...])
    v = proj(o + 2 * d_att, d_att)
    if prompt:
        even_row = (lax.broadcasted_iota(jnp.int32, (d_att, LANES), 0) & HEAD_DIM) == 0
        for s in range(tm // LANES):
            kt = kn[s * LANES:(s + 1) * LANES, :].T
            kt_ref[0, 0, s] = jnp.where(even_row, kt, 0.0).astype(BF16)
            kt_ref[0, 1, s] = jnp.where(even_row, 0.0, kt).astype(BF16)
        even = (lax.broadcasted_iota(jnp.int32, (tm, d_att), 1) & HEAD_DIM) == 0
        vb_ref[0, 0] = jnp.where(even, v, 0.0).astype(BF16)
        vb_ref[0, 1] = jnp.where(even, 0.0, v).astype(BF16)

    @pl.when(pl.program_id(1) >= tail0)
    def _():
        k32_ref[0] = kn
        v32_ref[0] = v


def _in_proj(x, g_mix, w_in, g_q, g_k, conv_w, *, tm, keep, state=None, cast=()):
    nb, t_len, d_model = x.shape
    d_att = g_q.shape[-1]
    d_in = w_in.shape[-1]
    d_conv = (d_in - 3 * d_att) // 2
    n_t = t_len // tm
    prompt = state is None
    assert t_len % tm == 0 and (t_len - keep) % tm == 0 and tm % LANES == 0 and tm >= HALO
    tail0 = (t_len - keep) // tm
    row = lambda b, t: (b, t, 0)
    tail = lambda b, t: (b, jnp.maximum(t - tail0, 0), 0)
    halves = pl.BlockSpec((1, 2, tm, d_att), lambda b, t: (b, 0, t, 0))
    out_shape = [jax.ShapeDtypeStruct((nb, t_len, d_conv), BF16)]
    out_specs = [pl.BlockSpec((1, tm, d_conv), row)]
    if prompt:
        n_seg = 1
        out_shape += [jax.ShapeDtypeStruct((nb, HALO, d_conv), F32)]
        out_specs += [pl.BlockSpec((1, HALO, d_conv), lambda b, t: (b, 0, 0))]
    else:
        n_seg = state.shape[0]
        assert nb == 1 and n_t == 1 and tm % n_seg == 0 and state.shape[1] == CONV_WIDTH - 1
        out_shape += [jax.ShapeDtypeStruct((nb, t_len, d_conv), F32)]
        out_specs += [pl.BlockSpec((1, tm, d_conv), row)]
    out_shape += [jax.ShapeDtypeStruct((nb, t_len, d_att), BF16)]
    out_specs += [pl.BlockSpec((1, tm, d_att), row)]
    if prompt:
        out_shape += [jax.ShapeDtypeStruct((nb, 2, t_len // LANES, d_att, LANES), BF16),
                      jax.ShapeDtypeStruct((nb, 2, t_len, d_att), BF16)]
        out_specs += [pl.BlockSpec((1, 2, tm // LANES, d_att, LANES), lambda b, t: (b, 0, t, 0, 0)), halves]
    out_shape += [jax.ShapeDtypeStruct((nb, keep, d_att), F32)] * 2
    out_specs += [pl.BlockSpec((1, tm, d_att), tail)] * 2
    cast_specs = []
    for w in cast:
        rb = w.shape[0] // (nb * n_t)
        assert prompt and w.ndim == 2 and rb * nb * n_t == w.shape[0] and rb % (2 * SUBLANES) == 0
        cast_specs.append(pl.BlockSpec((rb, w.shape[1]), lambda b, t: (b * n_t + t, 0)))
        out_shape.append(jax.ShapeDtypeStruct(w.shape, BF16))
    out_specs += cast_specs
    kern = functools.partial(_inproj_kernel, d_conv=d_conv, d_att=d_att, tail0=tail0, prompt=prompt,
                             n_cast=len(cast))
    extra = [] if prompt else [state]
    return pl.pallas_call(
        kern,
        grid=(nb, n_t),
        in_specs=[pl.BlockSpec((1, tm, d_model), row),
                  _const_spec((1, d_model)), _const_spec((d_model, d_in)),
                  _const_spec((1, d_att)), _const_spec((1, d_att))]
                 + [_const_spec(a.shape) for a in extra + list(conv_w)] + cast_specs,
        out_specs=out_specs,
        out_shape=out_shape,
        scratch_shapes=[pltpu.VMEM((n_seg, HALO + tm // n_seg + SUBLANES, d_conv), F32),
                        pltpu.VMEM((tm, d_conv), F32)],
        compiler_params=pltpu.CompilerParams(
            dimension_semantics=("arbitrary", "arbitrary"), vmem_limit_bytes=VMEM_LIMIT),
        name="in_proj",
    )(x, g_mix, w_in, g_q, g_k, *extra, *conv_w, *cast)


def _attn_scratch(n_heads, slots):
    return [pltpu.VMEM((slots, n_heads // 2, PAIR, 2 * WIN), F32),
            pltpu.VMEM((slots, n_heads // 2, PAIR, 2 * WIN), BF16)]


def _attend(rows, n_slabs, get_q, get_k, get_v, bias_ref, off, s_scr, p_scr):
    n_pairs = s_scr.shape[0]
    width = n_slabs * LANES
    for hp in range(n_pairs):
        s_scr[hp, 0:rows, 0:2 * width] = jnp.dot(get_q(hp), get_k(hp), preferred_element_type=F32)
    for hp in range(n_pairs):
        for sub in range(2):
            h = 2 * hp + sub
            for r0 in range(0, rows, SOFTMAX_ROWS):
                r = slice(r0, r0 + SOFTMAX_ROWS)
                s = s_scr[hp, r, sub * width:(sub + 1) * width]
                s = s + jnp.concatenate([bias_ref[h, off + c, r, :] for c in range(n_slabs)], axis=1)
                e = jnp.exp(s - jnp.max(s, axis=-1, keepdims=True))
                p_scr[hp, r, sub * width:(sub + 1) * width] = e.astype(BF16)
    first = lax.broadcasted_iota(jnp.int32, (width, LANES), 1) < HEAD_DIM
    sums = jnp.concatenate([jnp.where(first, 1.0, 0.0), jnp.where(first, 0.0, 1.0)], axis=0).astype(BF16)
    outs = []
    for hp in range(n_pairs):
        o2 = jnp.dot(p_scr[hp, 0:rows, 0:2 * width], jnp.concatenate([get_v(hp), sums], axis=1),
                     preferred_element_type=F32)
        outs.append(o2[:, 0:LANES] * (1.0 / o2[:, LANES:2 * LANES]))
    return jnp.concatenate(outs, axis=1)


def _attn_prompt_kernel(q_ref, kt_ref, v_ref, bias_ref, g_ref, o_ref, s_scr, p_scr):
    pairs = q_ref.shape[1] // PAIR
    slots = s_scr.shape[0]
    t = pl.program_id(1)

    def run_pairs(first_step):
        for i in range(pairs):
            p = i if first_step else t * pairs + i
            n_slabs = min(i + 1, WIN_SLABS) if first_step else WIN_SLABS
            ws = max(p - (WIN_SLABS - 1), 0) if first_step else p - (WIN_SLABS - 1)
            off = ws + (WIN_SLABS - 1) - p
            k0 = ws * LANES if first_step else pl.multiple_of(ws * LANES, LANES)
            width = n_slabs * LANES

            def get_k(hp, ws=ws, n_slabs=n_slabs):
                return jnp.concatenate(
                    [kt_ref[0, e, ws + c, hp * LANES:(hp + 1) * LANES, :]
                     for e in range(2) for c in range(n_slabs)], axis=1)

            def get_v(hp, k0=k0, width=width):
                lanes = slice(hp * LANES, (hp + 1) * LANES)
                return jnp.concatenate([v_ref[0, 0, pl.ds(k0, width), lanes],
                                        v_ref[0, 1, pl.ds(k0, width), lanes]], axis=0)

            rows = slice(i * PAIR, (i + 1) * PAIR)

            def get_q(hp, rows=rows):
                return q_ref[0, rows, hp * LANES:(hp + 1) * LANES]

            k = i % slots
            a = _attend(PAIR, n_slabs, get_q, get_k, get_v, bias_ref, off, s_scr.at[k], p_scr.at[k])
            o_ref[0, rows, :] = _rms(a, g_ref[...]).astype(BF16)

    pl.when(t == 0)(functools.partial(run_pairs, True))
    pl.when(t > 0)(functools.partial(run_pairs, False))


def _attn_prompt(q, kt, v, bias, g_att, *, tq):
    nb, t_len, d_att = q.shape
    n_heads = d_att // HEAD_DIM
    assert t_len % tq == 0 and tq % PAIR == 0 and tq // PAIR >= WIN_SLABS - 1
    return pl.pallas_call(
        _attn_prompt_kernel,
        grid=(nb, t_len // tq),
        in_specs=[pl.BlockSpec((1, tq, d_att), lambda b, t: (b, t, 0)),
                  pl.BlockSpec((1, 2, t_len // LANES, d_att, LANES), lambda b, t: (b, 0, 0, 0, 0)),
                  pl.BlockSpec((1, 2, t_len, d_att), lambda b, t: (b, 0, 0, 0)),
                  _const_spec(bias.shape), _const_spec((1, d_att))],
        out_specs=pl.BlockSpec((1, tq, d_att), lambda b, t: (b, t, 0)),
        out_shape=jax.ShapeDtypeStruct((nb, t_len, d_att), BF16),
        scratch_shapes=_attn_scratch(n_heads, slots=2),
        compiler_params=pltpu.CompilerParams(
            dimension_semantics=("arbitrary", "arbitrary"), vmem_limit_bytes=VMEM_LIMIT),
        name="attn_prompt",
    )(q, kt, v, bias, g_att)


def _attn_sample_kernel(q_ref, ck_ref, cv_ref, kn_ref, vn_ref, bias_ref, g_ref, o_ref, kt_s, v_s,
                        s_scr, p_scr):
    d_att = q_ref.shape[-1]
    pad = jnp.zeros((WIN - ck_ref.shape[1] - kn_ref.shape[1], d_att), F32)
    k_all = jnp.concatenate([ck_ref[0].astype(F32), kn_ref[0], pad], axis=0)
    even_row = (lax.broadcasted_iota(jnp.int32, (d_att, LANES), 0) & HEAD_DIM) == 0
    for c in range(WIN_SLABS):
        kt = k_all[c * LANES:(c + 1) * LANES, :].T
        kt_s[0, c] = jnp.where(even_row, kt, 0.0).astype(BF16)
        kt_s[1, c] = jnp.where(even_row, 0.0, kt).astype(BF16)
    v_all = jnp.concatenate([cv_ref[0].astype(F32), vn_ref[0], pad], axis=0)
    even = (lax.broadcasted_iota(jnp.int32, v_all.shape, 1) & HEAD_DIM) == 0
    v_s[0] = jnp.where(even, v_all, 0.0).astype(BF16)
    v_s[1] = jnp.where(even, 0.0, v_all).astype(BF16)

    def get_k(hp):
        return jnp.concatenate(
            [kt_s[e, c, hp * LANES:(hp + 1) * LANES, :] for e in range(2) for c in range(WIN_SLABS)], axis=1)

    def get_v(hp):
        lanes = slice(hp * LANES, (hp + 1) * LANES)
        return jnp.concatenate([v_s[0, :, lanes], v_s[1, :, lanes]], axis=0)

    def get_q(hp):
        return q_ref[0, :, hp * LANES:(hp + 1) * LANES]

    a = _attend(q_ref.shape[1], WIN_SLABS, get_q, get_k, get_v, bias_ref, 0, s_scr.at[0], p_scr.at[0])
    o_ref[0] = _rms(a, g_ref[...]).astype(BF16)


def _attn_sample(q, cache_k, cache_v, k_new, v_new, bias, g_att):
    nb, s_len, d_att = k_new.shape
    n_heads = d_att // HEAD_DIM
    l_cache = cache_k.shape[1]
    assert s_len == CHUNK and l_cache == LEFT_CHUNKS * CHUNK
    per_b = lambda b: (b, 0, 0)
    return pl.pallas_call(
        _attn_sample_kernel,
        grid=(nb,),
        in_specs=[pl.BlockSpec((1, s_len, d_att), per_b),
                  pl.BlockSpec((1, l_cache, d_att), per_b), pl.BlockSpec((1, l_cache, d_att), per_b),
                  pl.BlockSpec((1, s_len, d_att), per_b), pl.BlockSpec((1, s_len, d_att), per_b),
                  _const_spec(bias.shape), _const_spec((1, d_att))],
        out_specs=pl.BlockSpec((1, s_len, d_att), per_b),
        out_shape=jax.ShapeDtypeStruct((nb, s_len, d_att), BF16),
        scratch_shapes=[pltpu.VMEM((2, WIN_SLABS, d_att, LANES), BF16), pltpu.VMEM((2, WIN, d_att), BF16)]
                       + _attn_scratch(n_heads, slots=1),
        compiler_params=pltpu.CompilerParams(
            dimension_semantics=("arbitrary",), vmem_limit_bytes=VMEM_LIMIT),
        name="attn_sample",
    )(q, cache_k, cache_v, k_new, v_new, bias, g_att)


def _dense_stage(x, cn, an, p, wout_ref, gffn_ref, w1_ref, w2_ref, gple_ref, wg_ref, wple_ref, ff_chunk):
    d_conv = cn.shape[-1]
    x1 = (x + jnp.dot(cn, wout_ref[0:d_conv, :], preferred_element_type=F32)
          + jnp.dot(an, wout_ref[d_conv:, :], preferred_element_type=F32))
    hb = _rms(x1, gffn_ref[...]).astype(BF16)
    ffn = None
    for lo in range(0, w1_ref.shape[1], ff_chunk):
        a = jnp.maximum(jnp.dot(hb, w1_ref[:, lo:lo + ff_chunk], preferred_element_type=F32), 0.0)
        part = jnp.dot((a * a).astype(BF16), w2_ref[lo:lo + ff_chunk, :], preferred_element_type=F32)
        ffn = part if ffn is None else ffn + part
    x2 = x1 + ffn
    gate = jax.nn.sigmoid(jnp.dot(_rms(x2, gple_ref[...]).astype(BF16), wg_ref[...],
                                  preferred_element_type=F32))
    ple = jnp.dot(p.astype(BF16), wple_ref[...], preferred_element_type=F32)
    return x2 + ple * gate


def _mix_kernel(*refs, starts, ff_chunk):
    n_groups = len(starts) - 1
    ins, dense_w, outs = refs[:4 * n_groups], refs[4 * n_groups:-n_groups], refs[-n_groups:]
    i = pl.program_id(0)
    for g in range(n_groups):
        x_ref, p_ref, an_ref, cn_ref = ins[4 * g:4 * g + 4]

        @pl.when((i >= starts[g]) & (i < starts[g + 1]))
        def _():
            outs[g][0] = _dense_stage(x_ref[0], cn_ref[0], an_ref[0], p_ref[0], *dense_w, ff_chunk)


def _mix(groups, w_out, g_ffn, w1, w2, g_ple, w_gate, w_ple, *, tm):
    consts = [w_out, g_ffn, w1, w2, g_ple, w_gate, w_ple]
    starts, in_specs, out_specs, out_shape, operands = [0], [], [], [], []
    for x, p, an, cn in groups:
        ng, rows, d_model = x.shape
        assert rows % tm == 0
        n_t = rows // tm
        first, count = starts[-1], ng * n_t
        starts.append(first + count)

        def tile(i, first=first, count=count, n_t=n_t):
            j = jnp.clip(i - first, 0, count - 1)
            return (j // n_t, j % n_t, 0)

        in_specs += [pl.BlockSpec((1, tm, a.shape[-1]), tile) for a in (x, p, an, cn)]
        out_specs.append(pl.BlockSpec((1, tm, d_model), tile))
        out_shape.append(jax.ShapeDtypeStruct(x.shape, F32))
        operands += [x, p, an, cn]
    return pl.pallas_call(
        functools.partial(_mix_kernel, starts=tuple(starts), ff_chunk=1024),
        grid=(starts[-1],),
        in_specs=in_specs + [_const_spec(c.shape) for c in consts],
        out_specs=out_specs,
        out_shape=out_shape,
        compiler_params=pltpu.CompilerParams(
            dimension_semantics=("arbitrary",), vmem_limit_bytes=VMEM_LIMIT),
        name="mix",
    )(*operands, *consts)


def kernel(x_prompt, x_sample, p_prompt, p_sample, cache_att_k, cache_att_v, state_conv, g_mix, w_in, w_dw,
           b_dw, g_conv_ln, b_conv_ln, g_q, g_k, rel_bias, g_conv_out, g_att_out, w_out, g_ffn, w_ff1, w_ff2,
           g_ple, w_gate, w_ple):
    depth = w_in.shape[0]
    nb, t_len, d_model = x_prompt.shape
    sb, s_len, _ = x_sample.shape
    d_conv = w_dw.shape[-1]
    d_att = g_att_out.shape[-1]
    n_heads = d_att // HEAD_DIM
    keep = min(LEFT_CHUNKS * CHUNK, t_len)
    tile = 512

    xp = x_prompt
    xs = x_sample.reshape(1, sb * s_len, d_model)
    outs = [[] for _ in range(6)]
    for i in range(depth):
        vec = lambda a: a[i].reshape(1, -1)
        gq = jnp.tile(g_q[i], n_heads).reshape(1, d_att)
        gk = jnp.tile(g_k[i], n_heads).reshape(1, d_att)
        w_in_b = w_in[i].astype(BF16)
        conv_w = (w_dw[i], vec(b_dw), vec(g_conv_ln), vec(b_conv_ln), vec(g_conv_out))
        bias = _bias_tiles(rel_bias[i])

        (cn_p, u_tail, q_p, kt_p, vb_p, k_tail, v_tail, w_out_b, w_ff1_b, w_ff2_b, w_gate_b) = _in_proj(
            xp, vec(g_mix), w_in_b, gq, gk, conv_w, tm=tile, keep=keep,
            cast=(w_out[i], w_ff1[i], w_ff2[i], w_gate[i]))
        mix_w = (w_out_b, vec(g_ffn), w_ff1_b, w_ff2_b, vec(g_ple), w_gate_b, w_ple[i].astype(BF16))
        an_p = _attn_prompt(q_p, kt_p, vb_p, bias, vec(g_att_out), tq=tile)
        outs[0].append(k_tail.reshape(nb, keep, n_heads, HEAD_DIM))
        outs[1].append(v_tail.reshape(nb, keep, n_heads, HEAD_DIM))
        outs[2].append(u_tail[:, HALO - (CONV_WIDTH - 1):])

        rows = sb * s_len
        cn_s, u_s, q_s, k_s, v_s = _in_proj(xs, vec(g_mix), w_in_b, gq, gk, conv_w, tm=rows, keep=rows,
                                            state=state_conv[i])
        per_b = lambda a: a.reshape(sb, s_len, a.shape[-1])
        ck = cache_att_k[i].astype(BF16).reshape(sb, -1, d_att)
        cv = cache_att_v[i].astype(BF16).reshape(sb, -1, d_att)
        an_s = _attn_sample(per_b(q_s), ck, cv, per_b(k_s), per_b(v_s), bias, vec(g_att_out))
        outs[3].append(k_s.reshape(sb, s_len, n_heads, HEAD_DIM))
        outs[4].append(v_s.reshape(sb, s_len, n_heads, HEAD_DIM))
        conv_in_tail = jnp.concatenate([state_conv[i], per_b(u_s)], axis=1)[:, -(CONV_WIDTH - 1):]
        outs[5].append(conv_in_tail)

        xp, xs = _mix([(xp, p_prompt[i], an_p, cn_p),
                       (xs, p_sample[i].reshape(1, rows, -1), an_s.reshape(1, rows, d_att), cn_s)],
                      *mix_w, tm=tile)

    return (xp, xs.reshape(sb, s_len, d_model)) + tuple(jnp.stack(o) for o in outs)
```

```python
import functools

import jax
import jax.numpy as jnp
from jax import lax
from jax.experimental import pallas as pl
from jax.experimental.pallas import tpu as pltpu

F32 = jnp.float32
BF16 = jnp.bfloat16

CHUNK = 64
LEFT_CHUNKS = 8
HEAD_DIM = 64
CONV_WIDTH = 31
MAX_REL = 128
EPS = 1e-6
NEG = -1e30

LANES = 128
SUBLANES = 8
MXU_TILE = 256
PAIR = 2 * CHUNK
WIN = (LEFT_CHUNKS + 2) * CHUNK
WIN_SLABS = WIN // LANES
HALO = 32
NORM_ROWS = 32
SOFTMAX_ROWS = 16
VMEM_LIMIT = 56 * 1024 * 1024


def _rms(x, g):
    return x * lax.rsqrt(jnp.mean(x * x, axis=-1, keepdims=True) + EPS) * g


def _const_spec(shape):
    zeros = (0,) * len(shape)
    return pl.BlockSpec(shape, lambda *_: zeros, pipeline_mode=pl.Buffered(1))


def _bias_kernel(rb_ref, o_ref):
    n_heads = rb_ref.shape[0]
    rb = rb_ref[...]
    lane = lax.broadcasted_iota(jnp.int32, (n_heads, LANES), 1)
    b0 = jnp.broadcast_to(rb[:, 0:1], (n_heads, LANES))
    t4 = jnp.where(lane < CHUNK, rb[:, LANES:2 * LANES], b0)
    t = jnp.concatenate([b0, b0, b0, rb[:, 0:LANES], t4, b0], axis=1)
    r_i = lax.broadcasted_iota(jnp.int32, (PAIR, LANES), 0)
    c_i = lax.broadcasted_iota(jnp.int32, (PAIR, LANES), 1)
    for h in range(n_heads):
        row = jnp.broadcast_to(t[h:h + 1, :], (PAIR, t.shape[1]))
        toe = pltpu.roll(row, 0, 1, stride=1, stride_axis=0)
        for c in range(WIN_SLABS):
            jj = c_i + c * LANES
            valid = ((r_i < CHUNK) & (jj < WIN - CHUNK)) | ((r_i >= CHUNK) & (jj >= CHUNK))
            o_ref[h, c] = jnp.where(valid, toe[:, c * LANES:(c + 1) * LANES], NEG)


def _conv_stage(u, n_seg, halo_ref, ubuf, y_scr, cn_ref, wdw_ref, bdw_ref, gln_ref, bln_ref, gco_ref):
    d_conv = u.shape[-1]
    seg = u.shape[0] // n_seg
    off = HALO - (CONV_WIDTH - 1)
    n_a = HALO // SUBLANES + 1
    for s in range(n_seg):
        if halo_ref is not None:
            hr = halo_ref.shape[1]
            if hr < HALO:
                ubuf[s, 0:HALO - hr, :] = jnp.zeros((HALO - hr, d_conv), F32)
            ubuf[s, HALO - hr:HALO, :] = halo_ref[s]
        ubuf[s, HALO:HALO + seg, :] = u[s * seg:(s + 1) * seg, :]
        ubuf[s, HALO + seg:HALO + seg + SUBLANES, :] = jnp.zeros((SUBLANES, d_conv), F32)

    sub = lax.broadcasted_iota(jnp.int32, (SUBLANES, LANES), 0)
    from_lower = [sub < SUBLANES - b for b in range(SUBLANES)]
    for lt in range(d_conv // LANES):
        lanes = slice(lt * LANES, (lt + 1) * LANES)

        w = [jnp.broadcast_to(wdw_ref[j:j + 1, lanes], (SUBLANES, LANES)) for j in range(CONV_WIDTH)]

        def rotated_groups(s, blk):
            rows = [ubuf[s, SUBLANES * (blk + a):SUBLANES * (blk + a + 1), lanes] for a in range(n_a)]
            out = []
            for b in range(SUBLANES):
                z = None
                for a in range(n_a):
                    j = SUBLANES * a + b - off
                    if 0 <= j < CONV_WIDTH:
                        t = w[j] * rows[a]
                        z = t if z is None else z + t
                out.append(z if b == 0 else pltpu.roll(z, SUBLANES - b, 0))
            return out

        for s in range(n_seg):
            lower = rotated_groups(s, 0)
            for blk in range(seg // SUBLANES):
                upper = rotated_groups(s, blk + 1)
                y = lower[0]
                for b in range(1, SUBLANES):
                    y = y + jnp.where(from_lower[b], lower[b], upper[b])
                y_scr[s * seg + blk * SUBLANES:s * seg + (blk + 1) * SUBLANES, lanes] = y
                lower = upper

    for r0 in range(0, n_seg * seg, NORM_ROWS):
        y = y_scr[r0:r0 + NORM_ROWS, :] + bdw_ref[...]
        mu = jnp.mean(y, axis=-1, keepdims=True)
        yc = y - mu
        yn = yc * lax.rsqrt(jnp.mean(yc * yc, axis=-1, keepdims=True) + EPS) * gln_ref[...] + bln_ref[...]
        c = yn * jax.nn.sigmoid(yn)
        cn_ref[r0:r0 + NORM_ROWS, :] = _rms(c, gco_ref[...]).astype(BF16)


def _inproj_kernel(*refs, d_conv, d_att, tail0, prompt, n_cast):
    if prompt:
        x_ref, gmix_ref, win_ref, gq_ref, gk_ref = refs[:5]
        halo_ref, conv_w = None, refs[5:10]
        cast_in, rb_ref = refs[10:10 + n_cast], refs[10 + n_cast]
        cn_ref, ut_ref, q_ref, kt_ref, vb_ref, k32_ref, v32_ref = refs[11 + n_cast:18 + n_cast]
        cast_out, bias_ref = refs[18 + n_cast:18 + 2 * n_cast], refs[18 + 2 * n_cast]
        ubuf, y_scr = refs[19 + 2 * n_cast:]
        t = pl.program_id(1)
        tm = x_ref.shape[1]

        for src, dst in zip(cast_in, cast_out):
            dst[...] = src[...].astype(BF16)

        @pl.when((pl.program_id(0) == 0) & (t == 0))
        def _():
            _bias_kernel(rb_ref, bias_ref)

        @pl.when(t == 0)
        def _():
            ubuf[0, 0:HALO, :] = jnp.zeros((HALO, d_conv), F32)

        @pl.when(t > 0)
        def _():
            ubuf[0, 0:HALO, :] = ubuf[0, tm:tm + HALO, :]
    else:
        x_ref, gmix_ref, win_ref, gq_ref, gk_ref, halo_ref = refs[:6]
        conv_w = refs[6:11]
        cn_ref, u_ref, q_ref, k32_ref, v32_ref, ubuf, y_scr = refs[11:]
        tm = x_ref.shape[1]
    h = _rms(x_ref[0], gmix_ref[...]).astype(BF16)

    def proj(lo, width):
        return jnp.dot(h, win_ref[:, lo:lo + width], preferred_element_type=F32)

    u = proj(0, d_conv) * jax.nn.sigmoid(proj(d_conv, d_conv))
    if prompt:
        @pl.when(t == pl.num_programs(1) - 1)
        def _():
            ut_ref[0] = u[tm - HALO:tm, :]
    else:
        u_ref[0] = u
    _conv_stage(u, ubuf.shape[0], halo_ref, ubuf, y_scr, cn_ref.at[0], *conv_w)

    r_i = lax.shift_right_logical(lax.broadcasted_iota(jnp.int32, (MXU_TILE, MXU_TILE), 0), 6)
    c_i = lax.shift_right_logical(lax.broadcasted_iota(jnp.int32, (MXU_TILE, MXU_TILE), 1), 6)
    avg = jnp.where(r_i == c_i, 1.0 / HEAD_DIM, 0.0).astype(BF16)

    def head_rms(z, g):
        sq = (z * z).astype(BF16)
        ms = jnp.concatenate([jnp.dot(sq[:, lo:lo + MXU_TILE], avg, preferred_element_type=F32)
                              for lo in range(0, d_att, MXU_TILE)], axis=1)
        return z * lax.rsqrt(ms + EPS) * g

    o = 2 * d_conv
    qn = head_rms(proj(o, d_att), gq_ref[...])
    q_ref[0] = (qn * (HEAD_DIM ** -0.5)).astype(BF16)
    kn = head_rms(proj(o + d_att, d_att), gk_ref[...])
    v = proj(o + 2 * d_att, d_att)
    if prompt:
        even_row = (lax.broadcasted_iota(jnp.int32, (d_att, LANES), 0) & HEAD_DIM) == 0
        for s in range(tm // LANES):
            kt = kn[s * LANES:(s + 1) * LANES, :].T
            kt_ref[0, 0, s] = jnp.where(even_row, kt, 0.0).astype(BF16)
            kt_ref[0, 1, s] = jnp.where(even_row, 0.0, kt).astype(BF16)
        even = (lax.broadcasted_iota(jnp.int32, (tm, d_att), 1) & HEAD_DIM) == 0
        vb_ref[0, 0] = jnp.where(even, v, 0.0).astype(BF16)
        vb_ref[0, 1] = jnp.where(even, 0.0, v).astype(BF16)

    @pl.when(pl.program_id(1) >= tail0)
    def _():
        k32_ref[0] = kn
        v32_ref[0] = v


def _in_proj(x, g_mix, w_in, g_q, g_k, conv_w, *, tm, keep, state=None, cast=(), rel_bias=None):
    nb, t_len, d_model = x.shape
    d_att = g_q.shape[-1]
    d_in = w_in.shape[-1]
    d_conv = (d_in - 3 * d_att) // 2
    n_t = t_len // tm
    prompt = state is None
    assert t_len % tm == 0 and (t_len - keep) % tm == 0 and tm % LANES == 0 and tm >= HALO
    tail0 = (t_len - keep) // tm
    row = lambda b, t: (b, t, 0)
    tail = lambda b, t: (b, jnp.maximum(t - tail0, 0), 0)
    halves = pl.BlockSpec((1, 2, tm, d_att), lambda b, t: (b, 0, t, 0))
    out_shape = [jax.ShapeDtypeStruct((nb, t_len, d_conv), BF16)]
    out_specs = [pl.BlockSpec((1, tm, d_conv), row)]
    if prompt:
        n_seg = 1
        out_shape += [jax.ShapeDtypeStruct((nb, HALO, d_conv), F32)]
        out_specs += [pl.BlockSpec((1, HALO, d_conv), lambda b, t: (b, 0, 0))]
    else:
        n_seg = state.shape[0]
        assert nb == 1 and n_t == 1 and tm % n_seg == 0 and state.shape[1] == CONV_WIDTH - 1
        out_shape += [jax.ShapeDtypeStruct((nb, t_len, d_conv), F32)]
        out_specs += [pl.BlockSpec((1, tm, d_conv), row)]
    out_shape += [jax.ShapeDtypeStruct((nb, t_len, d_att), BF16)]
    out_specs += [pl.BlockSpec((1, tm, d_att), row)]
    if prompt:
        out_shape += [jax.ShapeDtypeStruct((nb, 2, t_len // LANES, d_att, LANES), BF16),
                      jax.ShapeDtypeStruct((nb, 2, t_len, d_att), BF16)]
        out_specs += [pl.BlockSpec((1, 2, tm // LANES, d_att, LANES), lambda b, t: (b, 0, t, 0, 0)), halves]
    out_shape += [jax.ShapeDtypeStruct((nb, keep, d_att), F32)] * 2
    out_specs += [pl.BlockSpec((1, tm, d_att), tail)] * 2
    cast_specs = []
    for w in cast:
        rb = w.shape[0] // (nb * n_t)
        assert prompt and w.ndim == 2 and rb * nb * n_t == w.shape[0] and rb % (2 * SUBLANES) == 0
        cast_specs.append(pl.BlockSpec((rb, w.shape[1]), lambda b, t: (b * n_t + t, 0)))
        out_shape.append(jax.ShapeDtypeStruct(w.shape, BF16))
    out_specs += cast_specs
    side_specs, side_in = list(cast_specs), list(cast)
    if prompt:
        bias_shape = (rel_bias.shape[0], WIN_SLABS, PAIR, LANES)
        side_specs.append(_const_spec(rel_bias.shape))
        side_in.append(rel_bias)
        out_shape.append(jax.ShapeDtypeStruct(bias_shape, F32))
        out_specs.append(pl.BlockSpec(bias_shape, lambda b, t: (0, 0, 0, 0)))
    kern = functools.partial(_inproj_kernel, d_conv=d_conv, d_att=d_att, tail0=tail0, prompt=prompt,
                             n_cast=len(cast))
    extra = [] if prompt else [state]
    return pl.pallas_call(
        kern,
        grid=(nb, n_t),
        in_specs=[pl.BlockSpec((1, tm, d_model), row),
                  _const_spec((1, d_model)), _const_spec((d_model, d_in)),
                  _const_spec((1, d_att)), _const_spec((1, d_att))]
                 + [_const_spec(a.shape) for a in extra + list(conv_w)] + side_specs,
        out_specs=out_specs,
        out_shape=out_shape,
        scratch_shapes=[pltpu.VMEM((n_seg, HALO + tm // n_seg + SUBLANES, d_conv), F32),
                        pltpu.VMEM((tm, d_conv), F32)],
        compiler_params=pltpu.CompilerParams(
            dimension_semantics=("arbitrary", "arbitrary"), vmem_limit_bytes=VMEM_LIMIT),
        name="in_proj",
    )(x, g_mix, w_in, g_q, g_k, *extra, *conv_w, *side_in)


def _attn_scratch(n_heads, slots):
    return [pltpu.VMEM((slots, n_heads // 2, PAIR, 2 * WIN), F32),
            pltpu.VMEM((slots, n_heads // 2, PAIR, 2 * WIN), BF16)]


def _attend(rows, n_slabs, get_q, get_k, get_v, bias_ref, off, s_scr, p_scr):
    n_pairs = s_scr.shape[0]
    width = n_slabs * LANES
    for hp in range(n_pairs):
        s_scr[hp, 0:rows, 0:2 * width] = jnp.dot(get_q(hp), get_k(hp), preferred_element_type=F32)
    for hp in range(n_pairs):
        for sub in range(2):
            h = 2 * hp + sub
            for r0 in range(0, rows, SOFTMAX_ROWS):
                r = slice(r0, r0 + SOFTMAX_ROWS)
                s = s_scr[hp, r, sub * width:(sub + 1) * width]
                s = s + jnp.concatenate([bias_ref[h, off + c, r, :] for c in range(n_slabs)], axis=1)
                e = jnp.exp(s - jnp.max(s, axis=-1, keepdims=True))
                p_scr[hp, r, sub * width:(sub + 1) * width] = e.astype(BF16)
    first = lax.broadcasted_iota(jnp.int32, (width, LANES), 1) < HEAD_DIM
    sums = jnp.concatenate([jnp.where(first, 1.0, 0.0), jnp.where(first, 0.0, 1.0)], axis=0).astype(BF16)
    outs = []
    for hp in range(n_pairs):
        o2 = jnp.dot(p_scr[hp, 0:rows, 0:2 * width], jnp.concatenate([get_v(hp), sums], axis=1),
                     preferred_element_type=F32)
        outs.append(o2[:, 0:LANES] * (1.0 / o2[:, LANES:2 * LANES]))
    return jnp.concatenate(outs, axis=1)


def _attn_prompt_kernel(q_ref, kt_ref, v_ref, bias_ref, g_ref, o_ref, s_scr, p_scr):
    pairs = q_ref.shape[1] // PAIR
    slots = s_scr.shape[0]
    t = pl.program_id(1)

    def run_pairs(first_step):
        for i in range(pairs):
            p = i if first_step else t * pairs + i
            n_slabs = min(i + 1, WIN_SLABS) if first_step else WIN_SLABS
            ws = max(p - (WIN_SLABS - 1), 0) if first_step else p - (WIN_SLABS - 1)
            off = ws + (WIN_SLABS - 1) - p
            k0 = ws * LANES if first_step else pl.multiple_of(ws * LANES, LANES)
            width = n_slabs * LANES

            def get_k(hp, ws=ws, n_slabs=n_slabs):
                return jnp.concatenate(
                    [kt_ref[0, e, ws + c, hp * LANES:(hp + 1) * LANES, :]
                     for e in range(2) for c in range(n_slabs)], axis=1)

            def get_v(hp, k0=k0, width=width):
                lanes = slice(hp * LANES, (hp + 1) * LANES)
                return jnp.concatenate([v_ref[0, 0, pl.ds(k0, width), lanes],
                                        v_ref[0, 1, pl.ds(k0, width), lanes]], axis=0)

            rows = slice(i * PAIR, (i + 1) * PAIR)

            def get_q(hp, rows=rows):
                return q_ref[0, rows, hp * LANES:(hp + 1) * LANES]

            k = i % slots
            a = _attend(PAIR, n_slabs, get_q, get_k, get_v, bias_ref, off, s_scr.at[k], p_scr.at[k])
            o_ref[0, rows, :] = _rms(a, g_ref[...]).astype(BF16)

    pl.when(t == 0)(functools.partial(run_pairs, True))
    pl.when(t > 0)(functools.partial(run_pairs, False))


def _attn_prompt(q, kt, v, bias, g_att, *, tq):
    nb, t_len, d_att = q.shape
    n_heads = d_att // HEAD_DIM
    assert t_len % tq == 0 and tq % PAIR == 0 and tq // PAIR >= WIN_SLABS - 1
    return pl.pallas_call(
        _attn_prompt_kernel,
        grid=(nb, t_len // tq),
        in_specs=[pl.BlockSpec((1, tq, d_att), lambda b, t: (b, t, 0)),
                  pl.BlockSpec((1, 2, t_len // LANES, d_att, LANES), lambda b, t: (b, 0, 0, 0, 0)),
                  pl.BlockSpec((1, 2, t_len, d_att), lambda b, t: (b, 0, 0, 0)),
                  _const_spec(bias.shape), _const_spec((1, d_att))],
        out_specs=pl.BlockSpec((1, tq, d_att), lambda b, t: (b, t, 0)),
        out_shape=jax.ShapeDtypeStruct((nb, t_len, d_att), BF16),
        scratch_shapes=_attn_scratch(n_heads, slots=2),
        compiler_params=pltpu.CompilerParams(
            dimension_semantics=("arbitrary", "arbitrary"), vmem_limit_bytes=VMEM_LIMIT),
        name="attn_prompt",
    )(q, kt, v, bias, g_att)


def _attn_sample_kernel(q_ref, ck_ref, cv_ref, kn_ref, vn_ref, bias_ref, g_ref, o_ref, kt_s, v_s,
                        s_scr, p_scr):
    d_att = q_ref.shape[-1]
    pad = jnp.zeros((WIN - ck_ref.shape[1] - kn_ref.shape[1], d_att), F32)
    k_all = jnp.concatenate([ck_ref[0].astype(F32), kn_ref[0], pad], axis=0)
    even_row = (lax.broadcasted_iota(jnp.int32, (d_att, LANES), 0) & HEAD_DIM) == 0
    for c in range(WIN_SLABS):
        kt = k_all[c * LANES:(c + 1) * LANES, :].T
        kt_s[0, c] = jnp.where(even_row, kt, 0.0).astype(BF16)
        kt_s[1, c] = jnp.where(even_row, 0.0, kt).astype(BF16)
    v_all = jnp.concatenate([cv_ref[0].astype(F32), vn_ref[0], pad], axis=0)
    even = (lax.broadcasted_iota(jnp.int32, v_all.shape, 1) & HEAD_DIM) == 0
    v_s[0] = jnp.where(even, v_all, 0.0).astype(BF16)
    v_s[1] = jnp.where(even, 0.0, v_all).astype(BF16)

    def get_k(hp):
        return jnp.concatenate(
            [kt_s[e, c, hp * LANES:(hp + 1) * LANES, :] for e in range(2) for c in range(WIN_SLABS)], axis=1)

    def get_v(hp):
        lanes = slice(hp * LANES, (hp + 1) * LANES)
        return jnp.concatenate([v_s[0, :, lanes], v_s[1, :, lanes]], axis=0)

    def get_q(hp):
        return q_ref[0, :, hp * LANES:(hp + 1) * LANES]

    a = _attend(q_ref.shape[1], WIN_SLABS, get_q, get_k, get_v, bias_ref, 0, s_scr.at[0], p_scr.at[0])
    o_ref[0] = _rms(a, g_ref[...]).astype(BF16)


def _attn_sample(q, cache_k, cache_v, k_new, v_new, bias, g_att):
    nb, s_len, d_att = k_new.shape
    n_heads = d_att // HEAD_DIM
    l_cache = cache_k.shape[1]
    assert s_len == CHUNK and l_cache == LEFT_CHUNKS * CHUNK
    per_b = lambda b: (b, 0, 0)
    return pl.pallas_call(
        _attn_sample_kernel,
        grid=(nb,),
        in_specs=[pl.BlockSpec((1, s_len, d_att), per_b),
                  pl.BlockSpec((1, l_cache, d_att), per_b), pl.BlockSpec((1, l_cache, d_att), per_b),
                  pl.BlockSpec((1, s_len, d_att), per_b), pl.BlockSpec((1, s_len, d_att), per_b),
                  _const_spec(bias.shape), _const_spec((1, d_att))],
        out_specs=pl.BlockSpec((1, s_len, d_att), per_b),
        out_shape=jax.ShapeDtypeStruct((nb, s_len, d_att), BF16),
        scratch_shapes=[pltpu.VMEM((2, WIN_SLABS, d_att, LANES), BF16), pltpu.VMEM((2, WIN, d_att), BF16)]
                       + _attn_scratch(n_heads, slots=1),
        compiler_params=pltpu.CompilerParams(
            dimension_semantics=("arbitrary",), vmem_limit_bytes=VMEM_LIMIT),
        name="attn_sample",
    )(q, cache_k, cache_v, k_new, v_new, bias, g_att)


def _dense_stage(x, cn, an, p, wout_ref, gffn_ref, w1_ref, w2_ref, gple_ref, wg_ref, wple_ref, ff_chunk):
    d_conv = cn.shape[-1]
    x1 = (x + jnp.dot(cn, wout_ref[0:d_conv, :], preferred_element_type=F32)
          + jnp.dot(an, wout_ref[d_conv:, :], preferred_element_type=F32))
    hb = _rms(x1, gffn_ref[...]).astype(BF16)
    ffn = None
    for lo in range(0, w1_ref.shape[1], ff_chunk):
        a = jnp.maximum(jnp.dot(hb, w1_ref[:, lo:lo + ff_chunk], preferred_element_type=F32), 0.0)
        part = jnp.dot((a * a).astype(BF16), w2_ref[lo:lo + ff_chunk, :], preferred_element_type=F32)
        ffn = part if ffn is None else ffn + part
    x2 = x1 + ffn
    gate = jax.nn.sigmoid(jnp.dot(_rms(x2, gple_ref[...]).astype(BF16), wg_ref[...],
                                  preferred_element_type=F32))
    ple = jnp.dot(p.astype(BF16), wple_ref[...], preferred_element_type=F32)
    return x2 + ple * gate


def _mix_kernel(*refs, starts, ff_chunk):
    n_groups = len(starts) - 1
    ins, dense_w, outs = refs[:4 * n_groups], refs[4 * n_groups:-n_groups], refs[-n_groups:]
    i = pl.program_id(0)
    for g in range(n_groups):
        x_ref, p_ref, an_ref, cn_ref = ins[4 * g:4 * g + 4]

        @pl.when((i >= starts[g]) & (i < starts[g + 1]))
        def _():
            outs[g][0] = _dense_stage(x_ref[0], cn_ref[0], an_ref[0], p_ref[0], *dense_w, ff_chunk)


def _mix(groups, w_out, g_ffn, w1, w2, g_ple, w_gate, w_ple, *, tm):
    consts = [w_out, g_ffn, w1, w2, g_ple, w_gate, w_ple]
    starts, in_specs, out_specs, out_shape, operands = [0], [], [], [], []
    for x, p, an, cn in groups:
        ng, rows, d_model = x.shape
        assert rows % tm == 0
        n_t = rows // tm
        first, count = starts[-1], ng * n_t
        starts.append(first + count)

        def tile(i, first=first, count=count, n_t=n_t):
            j = jnp.clip(i - first, 0, count - 1)
            return (j // n_t, j % n_t, 0)

        in_specs += [pl.BlockSpec((1, tm, a.shape[-1]), tile) for a in (x, p, an, cn)]
        out_specs.append(pl.BlockSpec((1, tm, d_model), tile))
        out_shape.append(jax.ShapeDtypeStruct(x.shape, F32))
        operands += [x, p, an, cn]
    return pl.pallas_call(
        functools.partial(_mix_kernel, starts=tuple(starts), ff_chunk=1024),
        grid=(starts[-1],),
        in_specs=in_specs + [_const_spec(c.shape) for c in consts],
        out_specs=out_specs,
        out_shape=out_shape,
        compiler_params=pltpu.CompilerParams(
            dimension_semantics=("arbitrary",), vmem_limit_bytes=VMEM_LIMIT),
        name="mix",
    )(*operands, *consts)


def kernel(x_prompt, x_sample, p_prompt, p_sample, cache_att_k, cache_att_v, state_conv, g_mix, w_in, w_dw,
           b_dw, g_conv_ln, b_conv_ln, g_q, g_k, rel_bias, g_conv_out, g_att_out, w_out, g_ffn, w_ff1, w_ff2,
           g_ple, w_gate, w_ple):
    depth = w_in.shape[0]
    nb, t_len, d_model = x_prompt.shape
    sb, s_len, _ = x_sample.shape
    d_conv = w_dw.shape[-1]
    d_att = g_att_out.shape[-1]
    n_heads = d_att // HEAD_DIM
    keep = min(LEFT_CHUNKS * CHUNK, t_len)
    tile = 512

    xp = x_prompt
    xs = x_sample.reshape(1, sb * s_len, d_model)
    outs = [[] for _ in range(6)]
    for i in range(depth):
        vec = lambda a: a[i].reshape(1, -1)
        gq = jnp.tile(g_q[i], n_heads).reshape(1, d_att)
        gk = jnp.tile(g_k[i], n_heads).reshape(1, d_att)
        w_in_b = w_in[i].astype(BF16)
        conv_w = (w_dw[i], vec(b_dw), vec(g_conv_ln), vec(b_conv_ln), vec(g_conv_out))

        (cn_p, u_tail, q_p, kt_p, vb_p, k_tail, v_tail, w_out_b, w_ff1_b, w_ff2_b, w_gate_b, bias) = _in_proj(
            xp, vec(g_mix), w_in_b, gq, gk, conv_w, tm=tile, keep=keep,
            cast=(w_out[i], w_ff1[i], w_ff2[i], w_gate[i]), rel_bias=rel_bias[i])
        mix_w = (w_out_b, vec(g_ffn), w_ff1_b, w_ff2_b, vec(g_ple), w_gate_b, w_ple[i].astype(BF16))
        an_p = _attn_prompt(q_p, kt_p, vb_p, bias, vec(g_att_out), tq=tile)
        outs[0].append(k_tail.reshape(nb, keep, n_heads, HEAD_DIM))
        outs[1].append(v_tail.reshape(nb, keep, n_heads, HEAD_DIM))
        outs[2].append(u_tail[:, HALO - (CONV_WIDTH - 1):])

        rows = sb * s_len
        cn_s, u_s, q_s, k_s, v_s = _in_proj(xs, vec(g_mix), w_in_b, gq, gk, conv_w, tm=rows, keep=rows,
                                            state=state_conv[i])
        per_b = lambda a: a.reshape(sb, s_len, a.shape[-1])
        ck = cache_att_k[i].astype(BF16).reshape(sb, -1, d_att)
        cv = cache_att_v[i].astype(BF16).reshape(sb, -1, d_att)
        an_s = _attn_sample(per_b(q_s), ck, cv, per_b(k_s), per_b(v_s), bias, vec(g_att_out))
        outs[3].append(k_s.reshape(sb, s_len, n_heads, HEAD_DIM))
        outs[4].append(v_s.reshape(sb, s_len, n_heads, HEAD_DIM))
        conv_in_tail = jnp.concatenate([state_conv[i], per_b(u_s)], axis=1)[:, -(CONV_WIDTH - 1):]
        outs[5].append(conv_in_tail)

        xp, xs = _mix([(xp, p_prompt[i], an_p, cn_p),
                       (xs, p_sample[i].reshape(1, rows, -1), an_s.reshape(1, rows, d_att), cn_s)],
                      *mix_w, tm=tile)

    return (xp, xs.reshape(sb, s_len, d_model)) + tuple(jnp.stack(o) for o in outs)
```

```python
import functools

import jax
import jax.numpy as jnp
import numpy as np
from jax import lax
from jax.experimental import pallas as pl
from jax.experimental.pallas import tpu as pltpu

F32 = jnp.float32
BF16 = jnp.bfloat16

CHUNK = 64
LEFT_CHUNKS = 8
HEAD_DIM = 64
CONV_WIDTH = 31
MAX_REL = 128
EPS = 1e-6
NEG = -1e30

LANES = 128
SUBLANES = 8
MXU_TILE = 256
PAIR = 2 * CHUNK
WIN = (LEFT_CHUNKS + 2) * CHUNK
WIN_SLABS = WIN // LANES
HALO = 32
CONV_HOP = 128
NORM_ROWS = 32
SOFTMAX_ROWS = 16
VMEM_LIMIT = 56 * 1024 * 1024


def _rms(x, g):
    return x * lax.rsqrt(jnp.mean(x * x, axis=-1, keepdims=True) + EPS) * g


def _const_spec(shape):
    zeros = (0,) * len(shape)
    return pl.BlockSpec(shape, lambda *_: zeros, pipeline_mode=pl.Buffered(1))


def _bias_kernel(rb_ref, o_ref):
    n_heads = rb_ref.shape[0]
    rb = rb_ref[...]
    lane = lax.broadcasted_iota(jnp.int32, (n_heads, LANES), 1)
    b0 = jnp.broadcast_to(rb[:, 0:1], (n_heads, LANES))
    t4 = jnp.where(lane < CHUNK, rb[:, LANES:2 * LANES], b0)
    t = jnp.concatenate([b0, b0, b0, rb[:, 0:LANES], t4, b0], axis=1)
    r_i = lax.broadcasted_iota(jnp.int32, (PAIR, LANES), 0)
    c_i = lax.broadcasted_iota(jnp.int32, (PAIR, LANES), 1)
    for h in range(n_heads):
        row = jnp.broadcast_to(t[h:h + 1, :], (PAIR, t.shape[1]))
        toe = pltpu.roll(row, 0, 1, stride=1, stride_axis=0)
        for c in range(WIN_SLABS):
            jj = c_i + c * LANES
            valid = ((r_i < CHUNK) & (jj < WIN - CHUNK)) | ((r_i >= CHUNK) & (jj >= CHUNK))
            o_ref[h, c] = jnp.where(valid, toe[:, c * LANES:(c + 1) * LANES], NEG)


def _dft_tables(n, hop):
    half = n // 2
    t = np.arange(n)[None, :]
    f = np.arange(half)[:, None]
    ang = 2.0 * np.pi * f * t / n
    fwd = np.concatenate([np.cos(ang), np.cos(np.pi * t), np.sin(ang[1:])], axis=0)
    rows = np.arange(n - hop, n)[:, None]
    ang_o = 2.0 * np.pi * rows * f.T / n
    scale = np.where(f.T == 0, 1.0, 2.0) / n
    inv = np.concatenate([scale * np.cos(ang_o), np.cos(np.pi * rows) / n, (2.0 / n) * np.sin(ang_o[:, 1:])], axis=1)
    delay = (CONV_WIDTH - 1) - np.arange(CONV_WIDTH)[None, :]
    ang_h = 2.0 * np.pi * f * delay / n
    p, q = np.cos(ang_h), np.sin(ang_h)
    r = p.copy()
    r[0] = np.cos(np.pi * delay[0])
    return (jnp.asarray(fwd, F32), jnp.asarray(inv, F32), jnp.asarray(np.stack([p, q, r]), F32))


def _filter_response(wdw_ref, resp_ref, h_scr):
    for k in range(3):
        acc = None
        for j in range(CONV_WIDTH):
            term = resp_ref[k, :, j:j + 1] * wdw_ref[j:j + 1, :]
            acc = term if acc is None else acc + term
        h_scr[k] = acc


def _conv_stage(u, n_seg, halo_ref, ubuf, y_scr, cn_ref, h_scr, fwd_ref, inv_ref,
                bdw_ref, gln_ref, bln_ref, gco_ref):
    d_conv = u.shape[-1]
    seg = u.shape[0] // n_seg
    n, hop = fwd_ref.shape[0], inv_ref.shape[0]
    half = n // 2
    assert n - hop == HALO and HALO >= CONV_WIDTH - 1 and seg % hop == 0
    for s in range(n_seg):
        if halo_ref is not None:
            hr = halo_ref.shape[1]
            if hr < HALO:
                ubuf[s, 0:HALO - hr, :] = jnp.zeros((HALO - hr, d_conv), F32)
            ubuf[s, HALO - hr:HALO, :] = halo_ref[s]
        ubuf[s, HALO:HALO + seg, :] = u[s * seg:(s + 1) * seg, :]

    blocks = [(s, i) for s in range(n_seg) for i in range(seg // hop)]
    windows = jnp.concatenate([ubuf[s, hop * i:hop * i + n, :].astype(BF16) for s, i in blocks], axis=1)
    spec = jnp.dot(fwd_ref[...].astype(BF16), windows, preferred_element_type=F32)
    p, q, r = h_scr[0], h_scr[1], h_scr[2]
    prods = []
    for k in range(len(blocks)):
        top = spec[0:half, k * d_conv:(k + 1) * d_conv]
        bot = spec[half:n, k * d_conv:(k + 1) * d_conv]
        prods.append(jnp.concatenate([p * top - q * bot, r * bot + q * top], axis=0).astype(BF16))
    y = jnp.dot(inv_ref[...].astype(BF16), jnp.concatenate(prods, axis=1),
                preferred_element_type=F32)
    for k, (s, i) in enumerate(blocks):
        y_scr[s * seg + i * hop:s * seg + (i + 1) * hop, :] = y[:, k * d_conv:(k + 1) * d_conv]

    for r0 in range(0, n_seg * seg, NORM_ROWS):
        y = y_scr[r0:r0 + NORM_ROWS, :] + bdw_ref[...]
        mu = jnp.mean(y, axis=-1, keepdims=True)
        yc = y - mu
        yn = yc * lax.rsqrt(jnp.mean(yc * yc, axis=-1, keepdims=True) + EPS) * gln_ref[...] + bln_ref[...]
        c = yn * jax.nn.sigmoid(yn)
        cn_ref[r0:r0 + NORM_ROWS, :] = _rms(c, gco_ref[...]).astype(BF16)


def _inproj_kernel(*refs, d_conv, d_att, tail0, prompt, n_cast):
    if prompt:
        x_ref, gmix_ref, win_ref, gq_ref, gk_ref = refs[:5]
        halo_ref, conv_w = None, refs[5:13]
        cast_in, rb_ref = refs[13:13 + n_cast], refs[13 + n_cast]
        cn_ref, ut_ref, q_ref, kt_ref, vb_ref, k32_ref, v32_ref = refs[14 + n_cast:21 + n_cast]
        cast_out, bias_ref = refs[21 + n_cast:21 + 2 * n_cast], refs[21 + 2 * n_cast]
        ubuf, y_scr, h_scr = refs[22 + 2 * n_cast:]
        t = pl.program_id(1)
        tm = x_ref.shape[1]

        for src, dst in zip(cast_in, cast_out):
            dst[...] = src[...].astype(BF16)

        @pl.when((pl.program_id(0) == 0) & (t == 0))
        def _():
            _bias_kernel(rb_ref, bias_ref)
            _filter_response(conv_w[0], conv_w[7], h_scr)

        @pl.when(t == 0)
        def _():
            ubuf[0, 0:HALO, :] = jnp.zeros((HALO, d_conv), F32)

        @pl.when(t > 0)
        def _():
            ubuf[0, 0:HALO, :] = ubuf[0, tm:tm + HALO, :]
    else:
        x_ref, gmix_ref, win_ref, gq_ref, gk_ref, halo_ref = refs[:6]
        conv_w = refs[6:14]
        cn_ref, u_ref, q_ref, k32_ref, v32_ref, ubuf, y_scr, h_scr = refs[14:]
        tm = x_ref.shape[1]
        _filter_response(conv_w[0], conv_w[7], h_scr)
    h = _rms(x_ref[0], gmix_ref[...]).astype(BF16)

    def proj(lo, width):
        return jnp.dot(h, win_ref[:, lo:lo + width], preferred_element_type=F32)

    u = proj(0, d_conv) * jax.nn.sigmoid(proj(d_conv, d_conv))
    if prompt:
        @pl.when(t == pl.num_programs(1) - 1)
        def _():
            ut_ref[0] = u[tm - HALO:tm, :]
    else:
        u_ref[0] = u
    wdw_ref, bdw_ref, gln_ref, bln_ref, gco_ref, fwd_ref, inv_ref, _ = conv_w
    _conv_stage(u, ubuf.shape[0], halo_ref, ubuf, y_scr, cn_ref.at[0], h_scr, fwd_ref, inv_ref,
                bdw_ref, gln_ref, bln_ref, gco_ref)

    r_i = lax.shift_right_logical(lax.broadcasted_iota(jnp.int32, (MXU_TILE, MXU_TILE), 0), 6)
    c_i = lax.shift_right_logical(lax.broadcasted_iota(jnp.int32, (MXU_TILE, MXU_TILE), 1), 6)
    avg = jnp.where(r_i == c_i, 1.0 / HEAD_DIM, 0.0).astype(BF16)

    def head_rms(z, g):
        sq = (z * z).astype(BF16)
        ms = jnp.concatenate([jnp.dot(sq[:, lo:lo + MXU_TILE], avg, preferred_element_type=F32)
                              for lo in range(0, d_att, MXU_TILE)], axis=1)
        return z * lax.rsqrt(ms + EPS) * g

    o = 2 * d_conv
    qn = head_rms(proj(o, d_att), gq_ref[...])
    q_ref[0] = (qn * (HEAD_DIM ** -0.5)).astype(BF16)
    kn = head_rms(proj(o + d_att, d_att), gk_ref[...])
    v = proj(o + 2 * d_att, d_att)
    if prompt:
        even_row = (lax.broadcasted_iota(jnp.int32, (d_att, LANES), 0) & HEAD_DIM) == 0
        for s in range(tm // LANES):
            kt = kn[s * LANES:(s + 1) * LANES, :].T
            kt_ref[0, 0, s] = jnp.where(even_row, kt, 0.0).astype(BF16)
            kt_ref[0, 1, s] = jnp.where(even_row, 0.0, kt).astype(BF16)
        even = (lax.broadcasted_iota(jnp.int32, (tm, d_att), 1) & HEAD_DIM) == 0
        vb_ref[0, 0] = jnp.where(even, v, 0.0).astype(BF16)
        vb_ref[0, 1] = jnp.where(even, 0.0, v).astype(BF16)

    @pl.when(pl.program_id(1) >= tail0)
    def _():
        k32_ref[0] = kn
        v32_ref[0] = v


def _in_proj(x, g_mix, w_in, g_q, g_k, conv_w, *, tm, keep, state=None, cast=(), rel_bias=None):
    nb, t_len, d_model = x.shape
    d_att = g_q.shape[-1]
    d_in = w_in.shape[-1]
    d_conv = (d_in - 3 * d_att) // 2
    n_t = t_len // tm
    prompt = state is None
    assert t_len % tm == 0 and (t_len - keep) % tm == 0 and tm % LANES == 0 and tm >= HALO
    tail0 = (t_len - keep) // tm
    row = lambda b, t: (b, t, 0)
    tail = lambda b, t: (b, jnp.maximum(t - tail0, 0), 0)
    halves = pl.BlockSpec((1, 2, tm, d_att), lambda b, t: (b, 0, t, 0))
    out_shape = [jax.ShapeDtypeStruct((nb, t_len, d_conv), BF16)]
    out_specs = [pl.BlockSpec((1, tm, d_conv), row)]
    if prompt:
        n_seg = 1
        out_shape += [jax.ShapeDtypeStruct((nb, HALO, d_conv), F32)]
        out_specs += [pl.BlockSpec((1, HALO, d_conv), lambda b, t: (b, 0, 0))]
    else:
        n_seg = state.shape[0]
        assert nb == 1 and n_t == 1 and tm % n_seg == 0 and state.shape[1] == CONV_WIDTH - 1
        out_shape += [jax.ShapeDtypeStruct((nb, t_len, d_conv), F32)]
        out_specs += [pl.BlockSpec((1, tm, d_conv), row)]
    out_shape += [jax.ShapeDtypeStruct((nb, t_len, d_att), BF16)]
    out_specs += [pl.BlockSpec((1, tm, d_att), row)]
    if prompt:
        out_shape += [jax.ShapeDtypeStruct((nb, 2, t_len // LANES, d_att, LANES), BF16),
                      jax.ShapeDtypeStruct((nb, 2, t_len, d_att), BF16)]
        out_specs += [pl.BlockSpec((1, 2, tm // LANES, d_att, LANES), lambda b, t: (b, 0, t, 0, 0)), halves]
    out_shape += [jax.ShapeDtypeStruct((nb, keep, d_att), F32)] * 2
    out_specs += [pl.BlockSpec((1, tm, d_att), tail)] * 2
    cast_specs = []
    for w in cast:
        rb = w.shape[0] // (nb * n_t)
        assert prompt and w.ndim == 2 and rb * nb * n_t == w.shape[0] and rb % (2 * SUBLANES) == 0
        cast_specs.append(pl.BlockSpec((rb, w.shape[1]), lambda b, t: (b * n_t + t, 0)))
        out_shape.append(jax.ShapeDtypeStruct(w.shape, BF16))
    out_specs += cast_specs
    side_specs, side_in = list(cast_specs), list(cast)
    if prompt:
        bias_shape = (rel_bias.shape[0], WIN_SLABS, PAIR, LANES)
        side_specs.append(_const_spec(rel_bias.shape))
        side_in.append(rel_bias)
        out_shape.append(jax.ShapeDtypeStruct(bias_shape, F32))
        out_specs.append(pl.BlockSpec(bias_shape, lambda b, t: (0, 0, 0, 0)))
    kern = functools.partial(_inproj_kernel, d_conv=d_conv, d_att=d_att, tail0=tail0, prompt=prompt,
                             n_cast=len(cast))
    extra = [] if prompt else [state]
    hop = min(CONV_HOP, tm // n_seg)
    conv_w = tuple(conv_w) + _dft_tables(HALO + hop, hop)
    return pl.pallas_call(
        kern,
        grid=(nb, n_t),
        in_specs=[pl.BlockSpec((1, tm, d_model), row),
                  _const_spec((1, d_model)), _const_spec((d_model, d_in)),
                  _const_spec((1, d_att)), _const_spec((1, d_att))]
                 + [_const_spec(a.shape) for a in extra + list(conv_w)] + side_specs,
        out_specs=out_specs,
        out_shape=out_shape,
        scratch_shapes=[pltpu.VMEM((n_seg, HALO + tm // n_seg, d_conv), F32),
                        pltpu.VMEM((tm, d_conv), F32), pltpu.VMEM((3, (HALO + hop) // 2, d_conv), F32)],
        compiler_params=pltpu.CompilerParams(
            dimension_semantics=("arbitrary", "arbitrary"), vmem_limit_bytes=VMEM_LIMIT),
        name="in_proj",
    )(x, g_mix, w_in, g_q, g_k, *extra, *conv_w, *side_in)


def _attn_scratch(n_heads, slots):
    return [pltpu.VMEM((slots, n_heads // 2, PAIR, 2 * WIN), F32),
            pltpu.VMEM((slots, n_heads // 2, PAIR, 2 * WIN), BF16)]


def _attend(rows, n_slabs, get_q, get_k, get_v, bias_ref, off, s_scr, p_scr):
    n_pairs = s_scr.shape[0]
    width = n_slabs * LANES
    for hp in range(n_pairs):
        s_scr[hp, 0:rows, 0:2 * width] = jnp.dot(get_q(hp), get_k(hp), preferred_element_type=F32)
    for hp in range(n_pairs):
        for sub in range(2):
            h = 2 * hp + sub
            for r0 in range(0, rows, SOFTMAX_ROWS):
                r = slice(r0, r0 + SOFTMAX_ROWS)
                s = s_scr[hp, r, sub * width:(sub + 1) * width]
                s = s + jnp.concatenate([bias_ref[h, off + c, r, :] for c in range(n_slabs)], axis=1)
                e = jnp.exp(s - jnp.max(s, axis=-1, keepdims=True))
                p_scr[hp, r, sub * width:(sub + 1) * width] = e.astype(BF16)
    first = lax.broadcasted_iota(jnp.int32, (width, LANES), 1) < HEAD_DIM
    sums = jnp.concatenate([jnp.where(first, 1.0, 0.0), jnp.where(first, 0.0, 1.0)], axis=0).astype(BF16)
    outs = []
    for hp in range(n_pairs):
        o2 = jnp.dot(p_scr[hp, 0:rows, 0:2 * width], jnp.concatenate([get_v(hp), sums], axis=1),
                     preferred_element_type=F32)
        outs.append(o2[:, 0:LANES] * (1.0 / o2[:, LANES:2 * LANES]))
    return jnp.concatenate(outs, axis=1)


def _attn_prompt_kernel(q_ref, kt_ref, v_ref, bias_ref, g_ref, o_ref, s_scr, p_scr):
    pairs = q_ref.shape[1] // PAIR
    slots = s_scr.shape[0]
    t = pl.program_id(1)

    def run_pairs(first_step):
        for i in range(pairs):
            p = i if first_step else t * pairs + i
            n_slabs = min(i + 1, WIN_SLABS) if first_step else WIN_SLABS
            ws = max(p - (WIN_SLABS - 1), 0) if first_step else p - (WIN_SLABS - 1)
            off = ws + (WIN_SLABS - 1) - p
            k0 = ws * LANES if first_step else pl.multiple_of(ws * LANES, LANES)
            width = n_slabs * LANES

            def get_k(hp, ws=ws, n_slabs=n_slabs):
                return jnp.concatenate(
                    [kt_ref[0, e, ws + c, hp * LANES:(hp + 1) * LANES, :]
                     for e in range(2) for c in range(n_slabs)], axis=1)

            def get_v(hp, k0=k0, width=width):
                lanes = slice(hp * LANES, (hp + 1) * LANES)
                return jnp.concatenate([v_ref[0, 0, pl.ds(k0, width), lanes],
                                        v_ref[0, 1, pl.ds(k0, width), lanes]], axis=0)

            rows = slice(i * PAIR, (i + 1) * PAIR)

            def get_q(hp, rows=rows):
                return q_ref[0, rows, hp * LANES:(hp + 1) * LANES]

            k = i % slots
            a = _attend(PAIR, n_slabs, get_q, get_k, get_v, bias_ref, off, s_scr.at[k], p_scr.at[k])
            o_ref[0, rows, :] = _rms(a, g_ref[...]).astype(BF16)

    pl.when(t == 0)(functools.partial(run_pairs, True))
    pl.when(t > 0)(functools.partial(run_pairs, False))


def _attn_prompt(q, kt, v, bias, g_att, *, tq):
    nb, t_len, d_att = q.shape
    n_heads = d_att // HEAD_DIM
    assert t_len % tq == 0 and tq % PAIR == 0 and tq // PAIR >= WIN_SLABS - 1
    return pl.pallas_call(
        _attn_prompt_kernel,
        grid=(nb, t_len // tq),
        in_specs=[pl.BlockSpec((1, tq, d_att), lambda b, t: (b, t, 0)),
                  pl.BlockSpec((1, 2, t_len // LANES, d_att, LANES), lambda b, t: (b, 0, 0, 0, 0)),
                  pl.BlockSpec((1, 2, t_len, d_att), lambda b, t: (b, 0, 0, 0)),
                  _const_spec(bias.shape), _const_spec((1, d_att))],
        out_specs=pl.BlockSpec((1, tq, d_att), lambda b, t: (b, t, 0)),
        out_shape=jax.ShapeDtypeStruct((nb, t_len, d_att), BF16),
        scratch_shapes=_attn_scratch(n_heads, slots=2),
        compiler_params=pltpu.CompilerParams(
            dimension_semantics=("arbitrary", "arbitrary"), vmem_limit_bytes=VMEM_LIMIT),
        name="attn_prompt",
    )(q, kt, v, bias, g_att)


def _attn_sample_kernel(q_ref, ck_ref, cv_ref, kn_ref, vn_ref, bias_ref, g_ref, o_ref, kt_s, v_s,
                        s_scr, p_scr):
    d_att = q_ref.shape[-1]
    pad = jnp.zeros((WIN - ck_ref.shape[1] - kn_ref.shape[1], d_att), F32)
    k_all = jnp.concatenate([ck_ref[0].astype(F32), kn_ref[0], pad], axis=0)
    even_row = (lax.broadcasted_iota(jnp.int32, (d_att, LANES), 0) & HEAD_DIM) == 0
    for c in range(WIN_SLABS):
        kt = k_all[c * LANES:(c + 1) * LANES, :].T
        kt_s[0, c] = jnp.where(even_row, kt, 0.0).astype(BF16)
        kt_s[1, c] = jnp.where(even_row, 0.0, kt).astype(BF16)
    v_all = jnp.concatenate([cv_ref[0].astype(F32), vn_ref[0], pad], axis=0)
    even = (lax.broadcasted_iota(jnp.int32, v_all.shape, 1) & HEAD_DIM) == 0
    v_s[0] = jnp.where(even, v_all, 0.0).astype(BF16)
    v_s[1] = jnp.where(even, 0.0, v_all).astype(BF16)

    def get_k(hp):
        return jnp.concatenate(
            [kt_s[e, c, hp * LANES:(hp + 1) * LANES, :] for e in range(2) for c in range(WIN_SLABS)], axis=1)

    def get_v(hp):
        lanes = slice(hp * LANES, (hp + 1) * LANES)
        return jnp.concatenate([v_s[0, :, lanes], v_s[1, :, lanes]], axis=0)

    def get_q(hp):
        return q_ref[0, :, hp * LANES:(hp + 1) * LANES]

    a = _attend(q_ref.shape[1], WIN_SLABS, get_q, get_k, get_v, bias_ref, 0, s_scr.at[0], p_scr.at[0])
    o_ref[0] = _rms(a, g_ref[...]).astype(BF16)


def _attn_sample(q, cache_k, cache_v, k_new, v_new, bias, g_att):
    nb, s_len, d_att = k_new.shape
    n_heads = d_att // HEAD_DIM
    l_cache = cache_k.shape[1]
    assert s_len == CHUNK and l_cache == LEFT_CHUNKS * CHUNK
    per_b = lambda b: (b, 0, 0)
    return pl.pallas_call(
        _attn_sample_kernel,
        grid=(nb,),
        in_specs=[pl.BlockSpec((1, s_len, d_att), per_b),
                  pl.BlockSpec((1, l_cache, d_att), per_b), pl.BlockSpec((1, l_cache, d_att), per_b),
                  pl.BlockSpec((1, s_len, d_att), per_b), pl.BlockSpec((1, s_len, d_att), per_b),
                  _const_spec(bias.shape), _const_spec((1, d_att))],
        out_specs=pl.BlockSpec((1, s_len, d_att), per_b),
        out_shape=jax.ShapeDtypeStruct((nb, s_len, d_att), BF16),
        scratch_shapes=[pltpu.VMEM((2, WIN_SLABS, d_att, LANES), BF16), pltpu.VMEM((2, WIN, d_att), BF16)]
                       + _attn_scratch(n_heads, slots=1),
        compiler_params=pltpu.CompilerParams(
            dimension_semantics=("arbitrary",), vmem_limit_bytes=VMEM_LIMIT),
        name="attn_sample",
    )(q, cache_k, cache_v, k_new, v_new, bias, g_att)


def _dense_stage(x, cn, an, p, wout_ref, gffn_ref, w1_ref, w2_ref, gple_ref, wg_ref, wple_ref, ff_chunk):
    d_conv = cn.shape[-1]
    x1 = (x + jnp.dot(cn, wout_ref[0:d_conv, :], preferred_element_type=F32)
          + jnp.dot(an, wout_ref[d_conv:, :], preferred_element_type=F32))
    hb = _rms(x1, gffn_ref[...]).astype(BF16)
    ffn = None
    for lo in range(0, w1_ref.shape[1], ff_chunk):
        a = jnp.maximum(jnp.dot(hb, w1_ref[:, lo:lo + ff_chunk], preferred_element_type=F32), 0.0)
        part = jnp.dot((a * a).astype(BF16), w2_ref[lo:lo + ff_chunk, :], preferred_element_type=F32)
        ffn = part if ffn is None else ffn + part
    x2 = x1 + ffn
    gate = jax.nn.sigmoid(jnp.dot(_rms(x2, gple_ref[...]).astype(BF16), wg_ref[...],
                                  preferred_element_type=F32))
    ple = jnp.dot(p.astype(BF16), wple_ref[...], preferred_element_type=F32)
    return x2 + ple * gate


def _mix_kernel(*refs, starts, ff_chunk):
    n_groups = len(starts) - 1
    ins, dense_w, outs = refs[:4 * n_groups], refs[4 * n_groups:-n_groups], refs[-n_groups:]
    i = pl.program_id(0)
    for g in range(n_groups):
        x_ref, p_ref, an_ref, cn_ref = ins[4 * g:4 * g + 4]

        @pl.when((i >= starts[g]) & (i < starts[g + 1]))
        def _():
            outs[g][0] = _dense_stage(x_ref[0], cn_ref[0], an_ref[0], p_ref[0], *dense_w, ff_chunk)


def _mix(groups, w_out, g_ffn, w1, w2, g_ple, w_gate, w_ple, *, tm):
    consts = [w_out, g_ffn, w1, w2, g_ple, w_gate, w_ple]
    starts, in_specs, out_specs, out_shape, operands = [0], [], [], [], []
    for x, p, an, cn in groups:
        ng, rows, d_model = x.shape
        assert rows % tm == 0
        n_t = rows // tm
        first, count = starts[-1], ng * n_t
        starts.append(first + count)

        def tile(i, first=first, count=count, n_t=n_t):
            j = jnp.clip(i - first, 0, count - 1)
            return (j // n_t, j % n_t, 0)

        in_specs += [pl.BlockSpec((1, tm, a.shape[-1]), tile) for a in (x, p, an, cn)]
        out_specs.append(pl.BlockSpec((1, tm, d_model), tile))
        out_shape.append(jax.ShapeDtypeStruct(x.shape, F32))
        operands += [x, p, an, cn]
    return pl.pallas_call(
        functools.partial(_mix_kernel, starts=tuple(starts), ff_chunk=1024),
        grid=(starts[-1],),
        in_specs=in_specs + [_const_spec(c.shape) for c in consts],
        out_specs=out_specs,
        out_shape=out_shape,
        compiler_params=pltpu.CompilerParams(
            dimension_semantics=("arbitrary",), vmem_limit_bytes=VMEM_LIMIT),
        name="mix",
    )(*operands, *consts)


def kernel(x_prompt, x_sample, p_prompt, p_sample, cache_att_k, cache_att_v, state_conv, g_mix, w_in, w_dw,
           b_dw, g_conv_ln, b_conv_ln, g_q, g_k, rel_bias, g_conv_out, g_att_out, w_out, g_ffn, w_ff1, w_ff2,
           g_ple, w_gate, w_ple):
    depth = w_in.shape[0]
    nb, t_len, d_model = x_prompt.shape
    sb, s_len, _ = x_sample.shape
    d_conv = w_dw.shape[-1]
    d_att = g_att_out.shape[-1]
    n_heads = d_att // HEAD_DIM
    keep = min(LEFT_CHUNKS * CHUNK, t_len)
    tile = 512

    xp = x_prompt
    xs = x_sample.reshape(1, sb * s_len, d_model)
    outs = [[] for _ in range(6)]
    for i in range(depth):
        vec = lambda a: a[i].reshape(1, -1)
        gq = jnp.tile(g_q[i], n_heads).reshape(1, d_att)
        gk = jnp.tile(g_k[i], n_heads).reshape(1, d_att)
        w_in_b = w_in[i].astype(BF16)
        conv_w = (w_dw[i], vec(b_dw), vec(g_conv_ln), vec(b_conv_ln), vec(g_conv_out))

        (cn_p, u_tail, q_p, kt_p, vb_p, k_tail, v_tail, w_out_b, w_ff1_b, w_ff2_b, w_gate_b, bias) = _in_proj(
            xp, vec(g_mix), w_in_b, gq, gk, conv_w, tm=tile, keep=keep,
            cast=(w_out[i], w_ff1[i], w_ff2[i], w_gate[i]), rel_bias=rel_bias[i])
        mix_w = (w_out_b, vec(g_ffn), w_ff1_b, w_ff2_b, vec(g_ple), w_gate_b, w_ple[i].astype(BF16))
        an_p = _attn_prompt(q_p, kt_p, vb_p, bias, vec(g_att_out), tq=tile)
        outs[0].append(k_tail.reshape(nb, keep, n_heads, HEAD_DIM))
        outs[1].append(v_tail.reshape(nb, keep, n_heads, HEAD_DIM))
        outs[2].append(u_tail[:, HALO - (CONV_WIDTH - 1):])

        rows = sb * s_len
        cn_s, u_s, q_s, k_s, v_s = _in_proj(xs, vec(g_mix), w_in_b, gq, gk, conv_w, tm=rows, keep=rows,
                                            state=state_conv[i])
        per_b = lambda a: a.reshape(sb, s_len, a.shape[-1])
        ck = cache_att_k[i].astype(BF16).reshape(sb, -1, d_att)
        cv = cache_att_v[i].astype(BF16).reshape(sb, -1, d_att)
        an_s = _attn_sample(per_b(q_s), ck, cv, per_b(k_s), per_b(v_s), bias, vec(g_att_out))
        outs[3].append(k_s.reshape(sb, s_len, n_heads, HEAD_DIM))
        outs[4].append(v_s.reshape(sb, s_len, n_heads, HEAD_DIM))
        conv_in_tail = jnp.concatenate([state_conv[i], per_b(u_s)], axis=1)[:, -(CONV_WIDTH - 1):]
        outs[5].append(conv_in_tail)

        xp, xs = _mix([(xp, p_prompt[i], an_p, cn_p),
                       (xs, p_sample[i].reshape(1, rows, -1), an_s.reshape(1, rows, d_att), cn_s)],
                      *mix_w, tm=tile)

    return (xp, xs.reshape(sb, s_len, d_model)) + tuple(jnp.stack(o) for o in outs)
```

```python
import functools

import jax
import jax.numpy as jnp
import numpy as np
from jax import lax
from jax.experimental import pallas as pl
from jax.experimental.pallas import tpu as pltpu

F32 = jnp.float32
BF16 = jnp.bfloat16

CHUNK = 64
LEFT_CHUNKS = 8
HEAD_DIM = 64
CONV_WIDTH = 31
MAX_REL = 128
EPS = 1e-6
NEG = -1e30

LANES = 128
SUBLANES = 8
MXU_TILE = 256
PAIR = 2 * CHUNK
WIN = (LEFT_CHUNKS + 2) * CHUNK
WIN_SLABS = WIN // LANES
HALO = 32
CONV_HOP = 128
SOFTMAX_ROWS = 16
VMEM_LIMIT = 56 * 1024 * 1024


def _rms(x, g):
    return x * lax.rsqrt(jnp.mean(x * x, axis=-1, keepdims=True) + EPS) * g


def _const_spec(shape):
    zeros = (0,) * len(shape)
    return pl.BlockSpec(shape, lambda *_: zeros, pipeline_mode=pl.Buffered(1))


def _bias_kernel(rb_ref, o_ref):
    assert rb_ref.shape[1] == 2 * MAX_REL + 1 and MAX_REL == LANES and CHUNK == LANES // 2
    assert LEFT_CHUNKS * CHUNK - MAX_REL == 3 * LANES and WIN_SLABS == 5
    n_heads = rb_ref.shape[0]
    rb = rb_ref[...]
    lane = lax.broadcasted_iota(jnp.int32, (n_heads, LANES), 1)
    b0 = jnp.broadcast_to(rb[:, 0:1], (n_heads, LANES))
    t4 = jnp.where(lane < CHUNK, rb[:, LANES:2 * LANES], b0)
    t = jnp.concatenate([b0, b0, b0, rb[:, 0:LANES], t4, b0], axis=1)
    r_i = lax.broadcasted_iota(jnp.int32, (PAIR, LANES), 0)
    c_i = lax.broadcasted_iota(jnp.int32, (PAIR, LANES), 1)
    for h in range(n_heads):
        row = jnp.broadcast_to(t[h:h + 1, :], (PAIR, t.shape[1]))
        toe = pltpu.roll(row, 0, 1, stride=1, stride_axis=0)
        for c in range(WIN_SLABS):
            jj = c_i + c * LANES
            valid = ((r_i < CHUNK) & (jj < WIN - CHUNK)) | ((r_i >= CHUNK) & (jj >= CHUNK))
            o_ref[h, c] = jnp.where(valid, toe[:, c * LANES:(c + 1) * LANES], NEG)


def _dft_tables(n, hop):
    half = n // 2
    t = np.arange(n)[None, :]
    f = np.arange(half)[:, None]
    ang = 2.0 * np.pi * f * t / n
    fwd = np.concatenate([np.cos(ang), np.cos(np.pi * t), np.sin(ang[1:])], axis=0)
    rows = np.arange(n - hop, n)[:, None]
    ang_o = 2.0 * np.pi * rows * f.T / n
    scale = np.where(f.T == 0, 1.0, 2.0) / n
    inv = np.concatenate([scale * np.cos(ang_o), np.cos(np.pi * rows) / n, (2.0 / n) * np.sin(ang_o[:, 1:])], axis=1)
    delay = (CONV_WIDTH - 1) - np.arange(CONV_WIDTH)[None, :]
    ang_h = 2.0 * np.pi * f * delay / n
    p, q = np.cos(ang_h), np.sin(ang_h)
    r = p.copy()
    r[0] = np.cos(np.pi * delay[0])
    return (jnp.asarray(fwd, F32), jnp.asarray(inv, F32), jnp.asarray(np.stack([p, q, r]), F32))


def _filter_response(wdw_ref, resp_ref, h_scr):
    for k in range(3):
        acc = None
        for j in range(CONV_WIDTH):
            term = resp_ref[k, :, j:j + 1] * wdw_ref[j:j + 1, :]
            acc = term if acc is None else acc + term
        h_scr[k] = acc


def _conv_stage(u, n_seg, halo_ref, ubuf, cn_ref, h_scr, fwd_ref, inv_ref,
                bdw_ref, gln_ref, bln_ref, gco_ref):
    d_conv = u.shape[-1]
    seg = u.shape[0] // n_seg
    n, hop = fwd_ref.shape[0], inv_ref.shape[0]
    half = n // 2
    assert n - hop == HALO and HALO >= CONV_WIDTH - 1 and seg % hop == 0
    for s in range(n_seg):
        if halo_ref is not None:
            hr = halo_ref.shape[1]
            if hr < HALO:
                ubuf[s, 0:HALO - hr, :] = jnp.zeros((HALO - hr, d_conv), F32)
            ubuf[s, HALO - hr:HALO, :] = halo_ref[s]
        ubuf[s, HALO:HALO + seg, :] = u[s * seg:(s + 1) * seg, :]

    blocks = [(s, i) for s in range(n_seg) for i in range(seg // hop)]
    windows = jnp.concatenate([ubuf[s, hop * i:hop * i + n, :].astype(BF16) for s, i in blocks], axis=1)
    spec = jnp.dot(fwd_ref[...].astype(BF16), windows, preferred_element_type=F32)
    p, q, r = h_scr[0], h_scr[1], h_scr[2]
    prods = []
    for k in range(len(blocks)):
        top = spec[0:half, k * d_conv:(k + 1) * d_conv]
        bot = spec[half:n, k * d_conv:(k + 1) * d_conv]
        prods.append(jnp.concatenate([p * top - q * bot, r * bot + q * top], axis=0).astype(BF16))
    y = jnp.dot(inv_ref[...].astype(BF16), jnp.concatenate(prods, axis=1),
                preferred_element_type=F32)
    for k, (s, i) in enumerate(blocks):
        yk = y[:, k * d_conv:(k + 1) * d_conv] + bdw_ref[...]
        mu = jnp.mean(yk, axis=-1, keepdims=True)
        yc = yk - mu
        yn = yc * lax.rsqrt(jnp.mean(yc * yc, axis=-1, keepdims=True) + EPS) * gln_ref[...] + bln_ref[...]
        c = yn * jax.nn.sigmoid(yn)
        cn_ref[s * seg + i * hop:s * seg + (i + 1) * hop, :] = _rms(c, gco_ref[...]).astype(BF16)


def _inproj_kernel(*refs, d_conv, d_att, tail0, prompt, n_cast):
    if prompt:
        x_ref, gmix_ref, win_ref, gq_ref, gk_ref = refs[:5]
        halo_ref, conv_w = None, refs[5:13]
        cast_in, rb_ref = refs[13:13 + n_cast], refs[13 + n_cast]
        cn_ref, ut_ref, q_ref, kt_ref, vb_ref, k32_ref, v32_ref = refs[14 + n_cast:21 + n_cast]
        cast_out, bias_ref = refs[21 + n_cast:21 + 2 * n_cast], refs[21 + 2 * n_cast]
        ubuf, h_scr = refs[22 + 2 * n_cast:]
        t = pl.program_id(1)
        tm = x_ref.shape[1]

        for src, dst in zip(cast_in, cast_out):
            dst[...] = src[...].astype(BF16)

        @pl.when((pl.program_id(0) == 0) & (t == 0))
        def _():
            _bias_kernel(rb_ref, bias_ref)
            _filter_response(conv_w[0], conv_w[7], h_scr)

        @pl.when(t == 0)
        def _():
            ubuf[0, 0:HALO, :] = jnp.zeros((HALO, d_conv), F32)

        @pl.when(t > 0)
        def _():
            ubuf[0, 0:HALO, :] = ubuf[0, tm:tm + HALO, :]
    else:
        x_ref, gmix_ref, win_ref, gq_ref, gk_ref, halo_ref = refs[:6]
        conv_w = refs[6:14]
        cn_ref, u_ref, q_ref, k32_ref, v32_ref, ubuf, h_scr = refs[14:]
        tm = x_ref.shape[1]
        _filter_response(conv_w[0], conv_w[7], h_scr)
    h = _rms(x_ref[0], gmix_ref[...]).astype(BF16)

    def proj(lo, width):
        return jnp.dot(h, win_ref[:, lo:lo + width], preferred_element_type=F32)

    u = proj(0, d_conv) * jax.nn.sigmoid(proj(d_conv, d_conv))
    if prompt:
        @pl.when(t == pl.num_programs(1) - 1)
        def _():
            ut_ref[0] = u[tm - HALO:tm, :]
    else:
        u_ref[0] = u
    wdw_ref, bdw_ref, gln_ref, bln_ref, gco_ref, fwd_ref, inv_ref, _ = conv_w
    _conv_stage(u, ubuf.shape[0], halo_ref, ubuf, cn_ref.at[0], h_scr, fwd_ref, inv_ref,
                bdw_ref, gln_ref, bln_ref, gco_ref)

    head_bits = HEAD_DIM.bit_length() - 1
    r_i = lax.shift_right_logical(lax.broadcasted_iota(jnp.int32, (MXU_TILE, MXU_TILE), 0), head_bits)
    c_i = lax.shift_right_logical(lax.broadcasted_iota(jnp.int32, (MXU_TILE, MXU_TILE), 1), head_bits)
    avg = jnp.where(r_i == c_i, 1.0 / HEAD_DIM, 0.0).astype(BF16)

    def head_rms(z, g):
        sq = (z * z).astype(BF16)
        ms = jnp.concatenate([jnp.dot(sq[:, lo:lo + MXU_TILE], avg, preferred_element_type=F32)
                              for lo in range(0, d_att, MXU_TILE)], axis=1)
        return z * lax.rsqrt(ms + EPS) * g

    o = 2 * d_conv
    qn = head_rms(proj(o, d_att), gq_ref[...])
    q_ref[0] = (qn * (HEAD_DIM ** -0.5)).astype(BF16)
    kn = head_rms(proj(o + d_att, d_att), gk_ref[...])
    v = proj(o + 2 * d_att, d_att)
    if prompt:
        even_row = (lax.broadcasted_iota(jnp.int32, (d_att, LANES), 0) & HEAD_DIM) == 0
        for s in range(tm // LANES):
            kt = kn[s * LANES:(s + 1) * LANES, :].T
            kt_ref[0, 0, s] = jnp.where(even_row, kt, 0.0).astype(BF16)
            kt_ref[0, 1, s] = jnp.where(even_row, 0.0, kt).astype(BF16)
        even = (lax.broadcasted_iota(jnp.int32, (tm, d_att), 1) & HEAD_DIM) == 0
        vb_ref[0, 0] = jnp.where(even, v, 0.0).astype(BF16)
        vb_ref[0, 1] = jnp.where(even, 0.0, v).astype(BF16)

    @pl.when(pl.program_id(1) >= tail0)
    def _():
        k32_ref[0] = kn
        v32_ref[0] = v


def _in_proj(x, g_mix, w_in, g_q, g_k, conv_w, *, tm, keep, state=None, cast=(), rel_bias=None):
    nb, t_len, d_model = x.shape
    d_att = g_q.shape[-1]
    d_in = w_in.shape[-1]
    d_conv = (d_in - 3 * d_att) // 2
    n_t = t_len // tm
    prompt = state is None
    assert t_len % tm == 0 and (t_len - keep) % tm == 0 and tm % LANES == 0 and tm >= HALO
    tail0 = (t_len - keep) // tm
    row = lambda b, t: (b, t, 0)
    tail = lambda b, t: (b, jnp.maximum(t - tail0, 0), 0)
    halves = pl.BlockSpec((1, 2, tm, d_att), lambda b, t: (b, 0, t, 0))
    out_shape = [jax.ShapeDtypeStruct((nb, t_len, d_conv), BF16)]
    out_specs = [pl.BlockSpec((1, tm, d_conv), row)]
    if prompt:
        n_seg = 1
        out_shape += [jax.ShapeDtypeStruct((nb, HALO, d_conv), F32)]
        out_specs += [pl.BlockSpec((1, HALO, d_conv), lambda b, t: (b, 0, 0))]
    else:
        n_seg = state.shape[0]
        assert nb == 1 and n_t == 1 and tm % n_seg == 0 and state.shape[1] == CONV_WIDTH - 1
        out_shape += [jax.ShapeDtypeStruct((nb, t_len, d_conv), F32)]
        out_specs += [pl.BlockSpec((1, tm, d_conv), row)]
    out_shape += [jax.ShapeDtypeStruct((nb, t_len, d_att), BF16)]
    out_specs += [pl.BlockSpec((1, tm, d_att), row)]
    if prompt:
        out_shape += [jax.ShapeDtypeStruct((nb, 2, t_len // LANES, d_att, LANES), BF16),
                      jax.ShapeDtypeStruct((nb, 2, t_len, d_att), BF16)]
        out_specs += [pl.BlockSpec((1, 2, tm // LANES, d_att, LANES), lambda b, t: (b, 0, t, 0, 0)), halves]
    out_shape += [jax.ShapeDtypeStruct((nb, keep, d_att), F32)] * 2
    out_specs += [pl.BlockSpec((1, tm, d_att), tail)] * 2
    cast_specs = []
    for w in cast:
        rb = w.shape[0] // (nb * n_t)
        assert prompt and w.ndim == 2 and rb * nb * n_t == w.shape[0] and rb % (2 * SUBLANES) == 0
        cast_specs.append(pl.BlockSpec((rb, w.shape[1]), lambda b, t: (b * n_t + t, 0)))
        out_shape.append(jax.ShapeDtypeStruct(w.shape, BF16))
    out_specs += cast_specs
    side_specs, side_in = list(cast_specs), list(cast)
    if prompt:
        bias_shape = (rel_bias.shape[0], WIN_SLABS, PAIR, LANES)
        side_specs.append(_const_spec(rel_bias.shape))
        side_in.append(rel_bias)
        out_shape.append(jax.ShapeDtypeStruct(bias_shape, F32))
        out_specs.append(pl.BlockSpec(bias_shape, lambda b, t: (0, 0, 0, 0)))
    kern = functools.partial(_inproj_kernel, d_conv=d_conv, d_att=d_att, tail0=tail0, prompt=prompt,
                             n_cast=len(cast))
    extra = [] if prompt else [state]
    hop = min(CONV_HOP, tm // n_seg)
    conv_w = tuple(conv_w) + _dft_tables(HALO + hop, hop)
    return pl.pallas_call(
        kern,
        grid=(nb, n_t),
        in_specs=[pl.BlockSpec((1, tm, d_model), row),
                  _const_spec((1, d_model)), _const_spec((d_model, d_in)),
                  _const_spec((1, d_att)), _const_spec((1, d_att))]
                 + [_const_spec(a.shape) for a in extra + list(conv_w)] + side_specs,
        out_specs=out_specs,
        out_shape=out_shape,
        scratch_shapes=[pltpu.VMEM((n_seg, HALO + tm // n_seg, d_conv), F32),
                        pltpu.VMEM((3, (HALO + hop) // 2, d_conv), F32)],
        compiler_params=pltpu.CompilerParams(
            dimension_semantics=("arbitrary", "arbitrary"), vmem_limit_bytes=VMEM_LIMIT),
        name="in_proj",
    )(x, g_mix, w_in, g_q, g_k, *extra, *conv_w, *side_in)


def _attn_scratch(n_heads, slots):
    return [pltpu.VMEM((slots, n_heads // 2, PAIR, 2 * WIN), F32),
            pltpu.VMEM((slots, n_heads // 2, PAIR, 2 * WIN), BF16)]


def _attend(rows, n_slabs, get_q, get_k, get_v, bias_ref, off, s_scr, p_scr):
    n_pairs = s_scr.shape[0]
    width = n_slabs * LANES
    for hp in range(n_pairs):
        s_scr[hp, 0:rows, 0:2 * width] = jnp.dot(get_q(hp), get_k(hp), preferred_element_type=F32)
    for hp in range(n_pairs):
        for sub in range(2):
            h = 2 * hp + sub
            for r0 in range(0, rows, SOFTMAX_ROWS):
                r = slice(r0, r0 + SOFTMAX_ROWS)
                s = s_scr[hp, r, sub * width:(sub + 1) * width]
                s = s + jnp.concatenate([bias_ref[h, off + c, r, :] for c in range(n_slabs)], axis=1)
                e = jnp.exp(s - jnp.max(s, axis=-1, keepdims=True))
                p_scr[hp, r, sub * width:(sub + 1) * width] = e.astype(BF16)
    first = lax.broadcasted_iota(jnp.int32, (width, LANES), 1) < HEAD_DIM
    sums = jnp.concatenate([jnp.where(first, 1.0, 0.0), jnp.where(first, 0.0, 1.0)], axis=0).astype(BF16)
    outs = []
    for hp in range(n_pairs):
        o2 = jnp.dot(p_scr[hp, 0:rows, 0:2 * width], jnp.concatenate([get_v(hp), sums], axis=1),
                     preferred_element_type=F32)
        outs.append(o2[:, 0:LANES] * (1.0 / o2[:, LANES:2 * LANES]))
    return jnp.concatenate(outs, axis=1)


def _attn_prompt_kernel(q_ref, kt_ref, v_ref, bias_ref, g_ref, o_ref, s_scr, p_scr):
    pairs = q_ref.shape[1] // PAIR
    slots = s_scr.shape[0]
    t = pl.program_id(1)

    def run_pairs(first_step):
        for i in range(pairs):
            p = i if first_step else t * pairs + i
            n_slabs = min(i + 1, WIN_SLABS) if first_step else WIN_SLABS
            ws = max(p - (WIN_SLABS - 1), 0) if first_step else p - (WIN_SLABS - 1)
            off = ws + (WIN_SLABS - 1) - p
            k0 = ws * LANES if first_step else pl.multiple_of(ws * LANES, LANES)
            width = n_slabs * LANES

            def get_k(hp, ws=ws, n_slabs=n_slabs):
                return jnp.concatenate(
                    [kt_ref[0, e, ws + c, hp * LANES:(hp + 1) * LANES, :]
                     for e in range(2) for c in range(n_slabs)], axis=1)

            def get_v(hp, k0=k0, width=width):
                lanes = slice(hp * LANES, (hp + 1) * LANES)
                return jnp.concatenate([v_ref[0, 0, pl.ds(k0, width), lanes],
                                        v_ref[0, 1, pl.ds(k0, width), lanes]], axis=0)

            rows = slice(i * PAIR, (i + 1) * PAIR)

            def get_q(hp, rows=rows):
                return q_ref[0, rows, hp * LANES:(hp + 1) * LANES]

            k = i % slots
            a = _attend(PAIR, n_slabs, get_q, get_k, get_v, bias_ref, off, s_scr.at[k], p_scr.at[k])
            o_ref[0, rows, :] = _rms(a, g_ref[...]).astype(BF16)

    pl.when(t == 0)(functools.partial(run_pairs, True))
    pl.when(t > 0)(functools.partial(run_pairs, False))


def _attn_prompt(q, kt, v, bias, g_att, *, tq):
    nb, t_len, d_att = q.shape
    n_heads = d_att // HEAD_DIM
    assert t_len % tq == 0 and tq % PAIR == 0 and tq // PAIR >= WIN_SLABS - 1
    return pl.pallas_call(
        _attn_prompt_kernel,
        grid=(nb, t_len // tq),
        in_specs=[pl.BlockSpec((1, tq, d_att), lambda b, t: (b, t, 0)),
                  pl.BlockSpec((1, 2, t_len // LANES, d_att, LANES), lambda b, t: (b, 0, 0, 0, 0)),
                  pl.BlockSpec((1, 2, t_len, d_att), lambda b, t: (b, 0, 0, 0)),
                  _const_spec(bias.shape), _const_spec((1, d_att))],
        out_specs=pl.BlockSpec((1, tq, d_att), lambda b, t: (b, t, 0)),
        out_shape=jax.ShapeDtypeStruct((nb, t_len, d_att), BF16),
        scratch_shapes=_attn_scratch(n_heads, slots=2),
        compiler_params=pltpu.CompilerParams(
            dimension_semantics=("arbitrary", "arbitrary"), vmem_limit_bytes=VMEM_LIMIT),
        name="attn_prompt",
    )(q, kt, v, bias, g_att)


def _attn_sample_kernel(q_ref, ck_ref, cv_ref, kn_ref, vn_ref, bias_ref, g_ref, o_ref, kt_s, v_s,
                        s_scr, p_scr):
    d_att = q_ref.shape[-1]
    pad = jnp.zeros((WIN - ck_ref.shape[1] - kn_ref.shape[1], d_att), F32)
    k_all = jnp.concatenate([ck_ref[0].astype(F32), kn_ref[0], pad], axis=0)
    even_row = (lax.broadcasted_iota(jnp.int32, (d_att, LANES), 0) & HEAD_DIM) == 0
    for c in range(WIN_SLABS):
        kt = k_all[c * LANES:(c + 1) * LANES, :].T
        kt_s[0, c] = jnp.where(even_row, kt, 0.0).astype(BF16)
        kt_s[1, c] = jnp.where(even_row, 0.0, kt).astype(BF16)
    v_all = jnp.concatenate([cv_ref[0].astype(F32), vn_ref[0], pad], axis=0)
    even = (lax.broadcasted_iota(jnp.int32, v_all.shape, 1) & HEAD_DIM) == 0
    v_s[0] = jnp.where(even, v_all, 0.0).astype(BF16)
    v_s[1] = jnp.where(even, 0.0, v_all).astype(BF16)

    def get_k(hp):
        return jnp.concatenate(
            [kt_s[e, c, hp * LANES:(hp + 1) * LANES, :] for e in range(2) for c in range(WIN_SLABS)], axis=1)

    def get_v(hp):
        lanes = slice(hp * LANES, (hp + 1) * LANES)
        return jnp.concatenate([v_s[0, :, lanes], v_s[1, :, lanes]], axis=0)

    def get_q(hp):
        return q_ref[0, :, hp * LANES:(hp + 1) * LANES]

    a = _attend(q_ref.shape[1], WIN_SLABS, get_q, get_k, get_v, bias_ref, 0, s_scr.at[0], p_scr.at[0])
    o_ref[0] = _rms(a, g_ref[...]).astype(BF16)


def _attn_sample(q, cache_k, cache_v, k_new, v_new, bias, g_att):
    nb, s_len, d_att = k_new.shape
    n_heads = d_att // HEAD_DIM
    l_cache = cache_k.shape[1]
    assert s_len == CHUNK and l_cache == LEFT_CHUNKS * CHUNK
    per_b = lambda b: (b, 0, 0)
    return pl.pallas_call(
        _attn_sample_kernel,
        grid=(nb,),
        in_specs=[pl.BlockSpec((1, s_len, d_att), per_b),
                  pl.BlockSpec((1, l_cache, d_att), per_b), pl.BlockSpec((1, l_cache, d_att), per_b),
                  pl.BlockSpec((1, s_len, d_att), per_b), pl.BlockSpec((1, s_len, d_att), per_b),
                  _const_spec(bias.shape), _const_spec((1, d_att))],
        out_specs=pl.BlockSpec((1, s_len, d_att), per_b),
        out_shape=jax.ShapeDtypeStruct((nb, s_len, d_att), BF16),
        scratch_shapes=[pltpu.VMEM((2, WIN_SLABS, d_att, LANES), BF16), pltpu.VMEM((2, WIN, d_att), BF16)]
                       + _attn_scratch(n_heads, slots=1),
        compiler_params=pltpu.CompilerParams(
            dimension_semantics=("arbitrary",), vmem_limit_bytes=VMEM_LIMIT),
        name="attn_sample",
    )(q, cache_k, cache_v, k_new, v_new, bias, g_att)


def _dense_stage(x, cn, an, p, wout_ref, gffn_ref, w1_ref, w2_ref, gple_ref, wg_ref, wple_ref, ff_chunk):
    d_conv = cn.shape[-1]
    x1 = (x + jnp.dot(cn, wout_ref[0:d_conv, :], preferred_element_type=F32)
          + jnp.dot(an, wout_ref[d_conv:, :], preferred_element_type=F32))
    hb = _rms(x1, gffn_ref[...]).astype(BF16)
    ffn = None
    for lo in range(0, w1_ref.shape[1], ff_chunk):
        a = jnp.maximum(jnp.dot(hb, w1_ref[:, lo:lo + ff_chunk], preferred_element_type=F32), 0.0)
        part = jnp.dot((a * a).astype(BF16), w2_ref[lo:lo + ff_chunk, :], preferred_element_type=F32)
        ffn = part if ffn is None else ffn + part
    x2 = x1 + ffn
    gate = jax.nn.sigmoid(jnp.dot(_rms(x2, gple_ref[...]).astype(BF16), wg_ref[...],
                                  preferred_element_type=F32))
    ple = jnp.dot(p.astype(BF16), wple_ref[...], preferred_element_type=F32)
    return x2 + ple * gate


def _mix_kernel(*refs, starts, ff_chunk):
    n_groups = len(starts) - 1
    ins, dense_w, outs = refs[:4 * n_groups], refs[4 * n_groups:-n_groups], refs[-n_groups:]
    i = pl.program_id(0)
    for g in range(n_groups):
        x_ref, p_ref, an_ref, cn_ref = ins[4 * g:4 * g + 4]

        @pl.when((i >= starts[g]) & (i < starts[g + 1]))
        def _():
            outs[g][0] = _dense_stage(x_ref[0], cn_ref[0], an_ref[0], p_ref[0], *dense_w, ff_chunk)


def _mix(groups, w_out, g_ffn, w1, w2, g_ple, w_gate, w_ple, *, tm):
    consts = [w_out, g_ffn, w1, w2, g_ple, w_gate, w_ple]
    starts, in_specs, out_specs, out_shape, operands = [0], [], [], [], []
    for x, p, an, cn in groups:
        ng, rows, d_model = x.shape
        assert rows % tm == 0
        n_t = rows // tm
        first, count = starts[-1], ng * n_t
        starts.append(first + count)

        def tile(i, first=first, count=count, n_t=n_t):
            j = jnp.clip(i - first, 0, count - 1)
            return (j // n_t, j % n_t, 0)

        in_specs += [pl.BlockSpec((1, tm, a.shape[-1]), tile) for a in (x, p, an, cn)]
        out_specs.append(pl.BlockSpec((1, tm, d_model), tile))
        out_shape.append(jax.ShapeDtypeStruct(x.shape, F32))
        operands += [x, p, an, cn]
    return pl.pallas_call(
        functools.partial(_mix_kernel, starts=tuple(starts), ff_chunk=1024),
        grid=(starts[-1],),
        in_specs=in_specs + [_const_spec(c.shape) for c in consts],
        out_specs=out_specs,
        out_shape=out_shape,
        compiler_params=pltpu.CompilerParams(
            dimension_semantics=("arbitrary",), vmem_limit_bytes=VMEM_LIMIT),
        name="mix",
    )(*operands, *consts)


def kernel(x_prompt, x_sample, p_prompt, p_sample, cache_att_k, cache_att_v, state_conv, g_mix, w_in, w_dw,
           b_dw, g_conv_ln, b_conv_ln, g_q, g_k, rel_bias, g_conv_out, g_att_out, w_out, g_ffn, w_ff1, w_ff2,
           g_ple, w_gate, w_ple):
    depth = w_in.shape[0]
    nb, t_len, d_model = x_prompt.shape
    sb, s_len, _ = x_sample.shape
    d_conv = w_dw.shape[-1]
    d_att = g_att_out.shape[-1]
    n_heads = d_att // HEAD_DIM
    keep = min(LEFT_CHUNKS * CHUNK, t_len)
    tile = 512

    xp = x_prompt
    xs = x_sample.reshape(1, sb * s_len, d_model)
    outs = [[] for _ in range(6)]
    for i in range(depth):
        vec = lambda a: a[i].reshape(1, -1)
        gq = jnp.tile(g_q[i], n_heads).reshape(1, d_att)
        gk = jnp.tile(g_k[i], n_heads).reshape(1, d_att)
        w_in_b = w_in[i].astype(BF16)
        conv_w = (w_dw[i], vec(b_dw), vec(g_conv_ln), vec(b_conv_ln), vec(g_conv_out))

        (cn_p, u_tail, q_p, kt_p, vb_p, k_tail, v_tail, w_out_b, w_ff1_b, w_ff2_b, w_gate_b, bias) = _in_proj(
            xp, vec(g_mix), w_in_b, gq, gk, conv_w, tm=tile, keep=keep,
            cast=(w_out[i], w_ff1[i], w_ff2[i], w_gate[i]), rel_bias=rel_bias[i])
        mix_w = (w_out_b, vec(g_ffn), w_ff1_b, w_ff2_b, vec(g_ple), w_gate_b, w_ple[i].astype(BF16))
        an_p = _attn_prompt(q_p, kt_p, vb_p, bias, vec(g_att_out), tq=tile)
        outs[0].append(k_tail.reshape(nb, keep, n_heads, HEAD_DIM))
        outs[1].append(v_tail.reshape(nb, keep, n_heads, HEAD_DIM))
        outs[2].append(u_tail[:, HALO - (CONV_WIDTH - 1):])

        rows = sb * s_len
        cn_s, u_s, q_s, k_s, v_s = _in_proj(xs, vec(g_mix), w_in_b, gq, gk, conv_w, tm=rows, keep=rows,
                                            state=state_conv[i])
        per_b = lambda a: a.reshape(sb, s_len, a.shape[-1])
        ck = cache_att_k[i].astype(BF16).reshape(sb, -1, d_att)
        cv = cache_att_v[i].astype(BF16).reshape(sb, -1, d_att)
        an_s = _attn_sample(per_b(q_s), ck, cv, per_b(k_s), per_b(v_s), bias, vec(g_att_out))
        outs[3].append(k_s.reshape(sb, s_len, n_heads, HEAD_DIM))
        outs[4].append(v_s.reshape(sb, s_len, n_heads, HEAD_DIM))
        conv_in_tail = jnp.concatenate([state_conv[i], per_b(u_s)], axis=1)[:, -(CONV_WIDTH - 1):]
        outs[5].append(conv_in_tail)

        xp, xs = _mix([(xp, p_prompt[i], an_p, cn_p),
                       (xs, p_sample[i].reshape(1, rows, -1), an_s.reshape(1, rows, d_att), cn_s)],
                      *mix_w, tm=tile)

    return (xp, xs.reshape(sb, s_len, d_model)) + tuple(jnp.stack(o) for o in outs)
```

```python
import functools

import jax
import jax.numpy as jnp
import numpy as np
from jax import lax
from jax.experimental import pallas as pl
from jax.experimental.pallas import tpu as pltpu

F32 = jnp.float32
BF16 = jnp.bfloat16

CHUNK = 64
LEFT_CHUNKS = 8
HEAD_DIM = 64
CONV_WIDTH = 31
MAX_REL = 128
EPS = 1e-6
NEG = -1e30
LOG2_E = 1.4426950408889634

LANES = 128
SUBLANES = 8
MXU_TILE = 256
PAIR = 2 * CHUNK
WIN = (LEFT_CHUNKS + 2) * CHUNK
WIN_SLABS = WIN // LANES
HALO = 32
CONV_HOP = 128
SOFTMAX_ROWS = 16
VMEM_LIMIT = 56 * 1024 * 1024


def _rms(x, g):
    return x * lax.rsqrt(jnp.mean(x * x, axis=-1, keepdims=True) + EPS) * g


def _const_spec(shape):
    zeros = (0,) * len(shape)
    return pl.BlockSpec(shape, lambda *_: zeros, pipeline_mode=pl.Buffered(1))


def _bias_kernel(rb_ref, o_ref):
    assert rb_ref.shape[1] == 2 * MAX_REL + 1 and MAX_REL == LANES and CHUNK == LANES // 2
    assert LEFT_CHUNKS * CHUNK - MAX_REL == 3 * LANES and WIN_SLABS == 5
    n_heads = rb_ref.shape[0]
    rb = rb_ref[...]
    lane = lax.broadcasted_iota(jnp.int32, (n_heads, LANES), 1)
    b0 = jnp.broadcast_to(rb[:, 0:1], (n_heads, LANES))
    t4 = jnp.where(lane < CHUNK, rb[:, LANES:2 * LANES], b0)
    t = jnp.concatenate([b0, b0, b0, rb[:, 0:LANES], t4, b0], axis=1)
    r_i = lax.broadcasted_iota(jnp.int32, (PAIR, LANES), 0)
    c_i = lax.broadcasted_iota(jnp.int32, (PAIR, LANES), 1)
    for h in range(n_heads):
        row = jnp.broadcast_to(t[h:h + 1, :], (PAIR, t.shape[1]))
        toe = pltpu.roll(row, 0, 1, stride=1, stride_axis=0)
        for c in range(WIN_SLABS):
            jj = c_i + c * LANES
            valid = ((r_i < CHUNK) & (jj < WIN - CHUNK)) | ((r_i >= CHUNK) & (jj >= CHUNK))
            o_ref[h, c] = jnp.where(valid, toe[:, c * LANES:(c + 1) * LANES] * LOG2_E, NEG)


def _dft_tables(n, hop):
    half = n // 2
    t = np.arange(n)[None, :]
    f = np.arange(half)[:, None]
    ang = 2.0 * np.pi * f * t / n
    fwd = np.concatenate([np.cos(ang), np.cos(np.pi * t), np.sin(ang[1:])], axis=0)
    rows = np.arange(n - hop, n)[:, None]
    ang_o = 2.0 * np.pi * rows * f.T / n
    scale = np.where(f.T == 0, 1.0, 2.0) / n
    inv = np.concatenate([scale * np.cos(ang_o), np.cos(np.pi * rows) / n, (2.0 / n) * np.sin(ang_o[:, 1:])], axis=1)
    delay = (CONV_WIDTH - 1) - np.arange(CONV_WIDTH)[None, :]
    ang_h = 2.0 * np.pi * f * delay / n
    p, q = np.cos(ang_h), np.sin(ang_h)
    r = p.copy()
    r[0] = np.cos(np.pi * delay[0])
    return (jnp.asarray(fwd, F32), jnp.asarray(inv, F32), jnp.asarray(np.stack([p, q, r]), F32))


def _filter_response(wdw_ref, resp_ref, h_scr):
    for k in range(3):
        acc = None
        for j in range(CONV_WIDTH):
            term = resp_ref[k, :, j:j + 1] * wdw_ref[j:j + 1, :]
            acc = term if acc is None else acc + term
        h_scr[k] = acc


def _conv_stage(u, n_seg, halo_ref, ubuf, cn_ref, h_scr, fwd_ref, inv_ref,
                bdw_ref, gln_ref, bln_ref, gco_ref):
    d_conv = u.shape[-1]
    seg = u.shape[0] // n_seg
    n, hop = fwd_ref.shape[0], inv_ref.shape[0]
    half = n // 2
    assert n - hop == HALO and HALO >= CONV_WIDTH - 1 and seg % hop == 0
    for s in range(n_seg):
        if halo_ref is not None:
            hr = halo_ref.shape[1]
            if hr < HALO:
                ubuf[s, 0:HALO - hr, :] = jnp.zeros((HALO - hr, d_conv), F32)
            ubuf[s, HALO - hr:HALO, :] = halo_ref[s]
        ubuf[s, HALO:HALO + seg, :] = u[s * seg:(s + 1) * seg, :]

    blocks = [(s, i) for s in range(n_seg) for i in range(seg // hop)]
    windows = jnp.concatenate([ubuf[s, hop * i:hop * i + n, :].astype(BF16) for s, i in blocks], axis=1)
    spec = jnp.dot(fwd_ref[...].astype(BF16), windows, preferred_element_type=F32)
    p, q, r = h_scr[0], h_scr[1], h_scr[2]
    prods = []
    for k in range(len(blocks)):
        top = spec[0:half, k * d_conv:(k + 1) * d_conv]
        bot = spec[half:n, k * d_conv:(k + 1) * d_conv]
        prods.append(jnp.concatenate([p * top - q * bot, r * bot + q * top], axis=0).astype(BF16))
    y = jnp.dot(inv_ref[...].astype(BF16), jnp.concatenate(prods, axis=1),
                preferred_element_type=F32)
    for k, (s, i) in enumerate(blocks):
        yk = y[:, k * d_conv:(k + 1) * d_conv] + bdw_ref[...]
        mu = jnp.mean(yk, axis=-1, keepdims=True)
        yc = yk - mu
        yn = yc * lax.rsqrt(jnp.mean(yc * yc, axis=-1, keepdims=True) + EPS) * gln_ref[...] + bln_ref[...]
        c = yn * jax.nn.sigmoid(yn)
        cn_ref[s * seg + i * hop:s * seg + (i + 1) * hop, :] = _rms(c, gco_ref[...]).astype(BF16)


def _inproj_kernel(*refs, d_conv, d_att, tail0, prompt, n_cast):
    if prompt:
        x_ref, gmix_ref, win_ref, gq_ref, gk_ref = refs[:5]
        halo_ref, conv_w = None, refs[5:13]
        cast_in, rb_ref = refs[13:13 + n_cast], refs[13 + n_cast]
        cn_ref, ut_ref, q_ref, kt_ref, vb_ref, k32_ref, v32_ref = refs[14 + n_cast:21 + n_cast]
        cast_out, bias_ref = refs[21 + n_cast:21 + 2 * n_cast], refs[21 + 2 * n_cast]
        ubuf, h_scr = refs[22 + 2 * n_cast:]
        t = pl.program_id(1)
        tm = x_ref.shape[1]

        for src, dst in zip(cast_in, cast_out):
            dst[...] = src[...].astype(BF16)

        @pl.when((pl.program_id(0) == 0) & (t == 0))
        def _():
            _bias_kernel(rb_ref, bias_ref)
            _filter_response(conv_w[0], conv_w[7], h_scr)

        @pl.when(t == 0)
        def _():
            ubuf[0, 0:HALO, :] = jnp.zeros((HALO, d_conv), F32)

        @pl.when(t > 0)
        def _():
            ubuf[0, 0:HALO, :] = ubuf[0, tm:tm + HALO, :]
    else:
        x_ref, gmix_ref, win_ref, gq_ref, gk_ref, halo_ref = refs[:6]
        conv_w = refs[6:14]
        cn_ref, u_ref, q_ref, k32_ref, v32_ref, ubuf, h_scr = refs[14:]
        tm = x_ref.shape[1]
        _filter_response(conv_w[0], conv_w[7], h_scr)
    h = _rms(x_ref[0], gmix_ref[...]).astype(BF16)

    def proj(lo, width):
        return jnp.dot(h, win_ref[:, lo:lo + width], preferred_element_type=F32)

    u = proj(0, d_conv) * jax.nn.sigmoid(proj(d_conv, d_conv))
    if prompt:
        @pl.when(t == pl.num_programs(1) - 1)
        def _():
            ut_ref[0] = u[tm - HALO:tm, :]
    else:
        u_ref[0] = u
    wdw_ref, bdw_ref, gln_ref, bln_ref, gco_ref, fwd_ref, inv_ref, _ = conv_w
    _conv_stage(u, ubuf.shape[0], halo_ref, ubuf, cn_ref.at[0], h_scr, fwd_ref, inv_ref,
                bdw_ref, gln_ref, bln_ref, gco_ref)

    head_bits = HEAD_DIM.bit_length() - 1
    r_i = lax.shift_right_logical(lax.broadcasted_iota(jnp.int32, (MXU_TILE, MXU_TILE), 0), head_bits)
    c_i = lax.shift_right_logical(lax.broadcasted_iota(jnp.int32, (MXU_TILE, MXU_TILE), 1), head_bits)
    avg = jnp.where(r_i == c_i, 1.0 / HEAD_DIM, 0.0).astype(BF16)

    def head_rms(z, g):
        sq = (z * z).astype(BF16)
        ms = jnp.concatenate([jnp.dot(sq[:, lo:lo + MXU_TILE], avg, preferred_element_type=F32)
                              for lo in range(0, d_att, MXU_TILE)], axis=1)
        return z * lax.rsqrt(ms + EPS) * g

    o = 2 * d_conv
    qn = head_rms(proj(o, d_att), gq_ref[...])
    q_ref[0] = (qn * (HEAD_DIM ** -0.5 * LOG2_E)).astype(BF16)
    kn = head_rms(proj(o + d_att, d_att), gk_ref[...])
    v = proj(o + 2 * d_att, d_att)
    if prompt:
        even_row = (lax.broadcasted_iota(jnp.int32, (d_att, LANES), 0) & HEAD_DIM) == 0
        for s in range(tm // LANES):
            kt = kn[s * LANES:(s + 1) * LANES, :].T
            kt_ref[0, 0, s] = jnp.where(even_row, kt, 0.0).astype(BF16)
            kt_ref[0, 1, s] = jnp.where(even_row, 0.0, kt).astype(BF16)
        even = (lax.broadcasted_iota(jnp.int32, (tm, d_att), 1) & HEAD_DIM) == 0
        vb_ref[0, 0] = jnp.where(even, v, 0.0).astype(BF16)
        vb_ref[0, 1] = jnp.where(even, 0.0, v).astype(BF16)

    @pl.when(pl.program_id(1) >= tail0)
    def _():
        k32_ref[0] = kn
        v32_ref[0] = v


def _in_proj(x, g_mix, w_in, g_q, g_k, conv_w, *, tm, keep, state=None, cast=(), rel_bias=None):
    nb, t_len, d_model = x.shape
    d_att = g_q.shape[-1]
    d_in = w_in.shape[-1]
    d_conv = (d_in - 3 * d_att) // 2
    n_t = t_len // tm
    prompt = state is None
    assert t_len % tm == 0 and (t_len - keep) % tm == 0 and tm % LANES == 0 and tm >= HALO
    tail0 = (t_len - keep) // tm
    row = lambda b, t: (b, t, 0)
    tail = lambda b, t: (b, jnp.maximum(t - tail0, 0), 0)
    halves = pl.BlockSpec((1, 2, tm, d_att), lambda b, t: (b, 0, t, 0))
    out_shape = [jax.ShapeDtypeStruct((nb, t_len, d_conv), BF16)]
    out_specs = [pl.BlockSpec((1, tm, d_conv), row)]
    if prompt:
        n_seg = 1
        out_shape += [jax.ShapeDtypeStruct((nb, HALO, d_conv), F32)]
        out_specs += [pl.BlockSpec((1, HALO, d_conv), lambda b, t: (b, 0, 0))]
    else:
        n_seg = state.shape[0]
        assert nb == 1 and n_t == 1 and tm % n_seg == 0 and state.shape[1] == CONV_WIDTH - 1
        out_shape += [jax.ShapeDtypeStruct((nb, t_len, d_conv), F32)]
        out_specs += [pl.BlockSpec((1, tm, d_conv), row)]
    out_shape += [jax.ShapeDtypeStruct((nb, t_len, d_att), BF16)]
    out_specs += [pl.BlockSpec((1, tm, d_att), row)]
    if prompt:
        out_shape += [jax.ShapeDtypeStruct((nb, 2, t_len // LANES, d_att, LANES), BF16),
                      jax.ShapeDtypeStruct((nb, 2, t_len, d_att), BF16)]
        out_specs += [pl.BlockSpec((1, 2, tm // LANES, d_att, LANES), lambda b, t: (b, 0, t, 0, 0)), halves]
    out_shape += [jax.ShapeDtypeStruct((nb, keep, d_att), F32)] * 2
    out_specs += [pl.BlockSpec((1, tm, d_att), tail)] * 2
    cast_specs = []
    for w in cast:
        rb = w.shape[0] // (nb * n_t)
        assert prompt and w.ndim == 2 and rb * nb * n_t == w.shape[0] and rb % (2 * SUBLANES) == 0
        cast_specs.append(pl.BlockSpec((rb, w.shape[1]), lambda b, t: (b * n_t + t, 0)))
        out_shape.append(jax.ShapeDtypeStruct(w.shape, BF16))
    out_specs += cast_specs
    side_specs, side_in = list(cast_specs), list(cast)
    if prompt:
        bias_shape = (rel_bias.shape[0], WIN_SLABS, PAIR, LANES)
        side_specs.append(_const_spec(rel_bias.shape))
        side_in.append(rel_bias)
        out_shape.append(jax.ShapeDtypeStruct(bias_shape, F32))
        out_specs.append(pl.BlockSpec(bias_shape, lambda b, t: (0, 0, 0, 0)))
    kern = functools.partial(_inproj_kernel, d_conv=d_conv, d_att=d_att, tail0=tail0, prompt=prompt,
                             n_cast=len(cast))
    extra = [] if prompt else [state]
    hop = min(CONV_HOP, tm // n_seg)
    conv_w = tuple(conv_w) + _dft_tables(HALO + hop, hop)
    return pl.pallas_call(
        kern,
        grid=(nb, n_t),
        in_specs=[pl.BlockSpec((1, tm, d_model), row),
                  _const_spec((1, d_model)), _const_spec((d_model, d_in)),
                  _const_spec((1, d_att)), _const_spec((1, d_att))]
                 + [_const_spec(a.shape) for a in extra + list(conv_w)] + side_specs,
        out_specs=out_specs,
        out_shape=out_shape,
        scratch_shapes=[pltpu.VMEM((n_seg, HALO + tm // n_seg, d_conv), F32),
                        pltpu.VMEM((3, (HALO + hop) // 2, d_conv), F32)],
        compiler_params=pltpu.CompilerParams(
            dimension_semantics=("arbitrary", "arbitrary"), vmem_limit_bytes=VMEM_LIMIT),
        name="in_proj",
    )(x, g_mix, w_in, g_q, g_k, *extra, *conv_w, *side_in)


def _attn_scratch(n_heads, slots):
    return [pltpu.VMEM((slots, n_heads // 2, PAIR, 2 * WIN), F32),
            pltpu.VMEM((slots, n_heads // 2, PAIR, 2 * WIN), BF16)]


def _attend(rows, n_slabs, get_q, get_k, get_v, bias_ref, off, s_scr, p_scr):
    n_pairs = s_scr.shape[0]
    width = n_slabs * LANES
    for hp in range(n_pairs):
        s_scr[hp, 0:rows, 0:2 * width] = jnp.dot(get_q(hp), get_k(hp), preferred_element_type=F32)
    for hp in range(n_pairs):
        for sub in range(2):
            h = 2 * hp + sub
            for r0 in range(0, rows, SOFTMAX_ROWS):
                r = slice(r0, r0 + SOFTMAX_ROWS)
                s = s_scr[hp, r, sub * width:(sub + 1) * width]
                s = s + jnp.concatenate([bias_ref[h, off + c, r, :] for c in range(n_slabs)], axis=1)
                e = jnp.exp2(s - jnp.max(s, axis=-1, keepdims=True))
                p_scr[hp, r, sub * width:(sub + 1) * width] = e.astype(BF16)
    first = lax.broadcasted_iota(jnp.int32, (width, LANES), 1) < HEAD_DIM
    sums = jnp.concatenate([jnp.where(first, 1.0, 0.0), jnp.where(first, 0.0, 1.0)], axis=0).astype(BF16)
    outs = []
    for hp in range(n_pairs):
        o2 = jnp.dot(p_scr[hp, 0:rows, 0:2 * width], jnp.concatenate([get_v(hp), sums], axis=1),
                     preferred_element_type=F32)
        outs.append(o2[:, 0:LANES] * (1.0 / o2[:, LANES:2 * LANES]))
    return jnp.concatenate(outs, axis=1)


def _attn_prompt_kernel(q_ref, kt_ref, v_ref, bias_ref, g_ref, o_ref, s_scr, p_scr):
    pairs = q_ref.shape[1] // PAIR
    slots = s_scr.shape[0]
    t = pl.program_id(1)

    def run_pairs(first_step):
        for i in range(pairs):
            p = i if first_step else t * pairs + i
            n_slabs = min(i + 1, WIN_SLABS) if first_step else WIN_SLABS
            ws = max(p - (WIN_SLABS - 1), 0) if first_step else p - (WIN_SLABS - 1)
            off = ws + (WIN_SLABS - 1) - p
            k0 = ws * LANES if first_step else pl.multiple_of(ws * LANES, LANES)
            width = n_slabs * LANES

            def get_k(hp, ws=ws, n_slabs=n_slabs):
                return jnp.concatenate(
                    [kt_ref[0, e, ws + c, hp * LANES:(hp + 1) * LANES, :]
                     for e in range(2) for c in range(n_slabs)], axis=1)

            def get_v(hp, k0=k0, width=width):
                lanes = slice(hp * LANES, (hp + 1) * LANES)
                return jnp.concatenate([v_ref[0, 0, pl.ds(k0, width), lanes],
                                        v_ref[0, 1, pl.ds(k0, width), lanes]], axis=0)

            rows = slice(i * PAIR, (i + 1) * PAIR)

            def get_q(hp, rows=rows):
                return q_ref[0, rows, hp * LANES:(hp + 1) * LANES]

            k = i % slots
            a = _attend(PAIR, n_slabs, get_q, get_k, get_v, bias_ref, off, s_scr.at[k], p_scr.at[k])
            o_ref[0, rows, :] = _rms(a, g_ref[...]).astype(BF16)

    pl.when(t == 0)(functools.partial(run_pairs, True))
    pl.when(t > 0)(functools.partial(run_pairs, False))


def _attn_prompt(q, kt, v, bias, g_att, *, tq):
    nb, t_len, d_att = q.shape
    n_heads = d_att // HEAD_DIM
    assert t_len % tq == 0 and tq % PAIR == 0 and tq // PAIR >= WIN_SLABS - 1
    return pl.pallas_call(
        _attn_prompt_kernel,
        grid=(nb, t_len // tq),
        in_specs=[pl.BlockSpec((1, tq, d_att), lambda b, t: (b, t, 0)),
                  pl.BlockSpec((1, 2, t_len // LANES, d_att, LANES), lambda b, t: (b, 0, 0, 0, 0)),
                  pl.BlockSpec((1, 2, t_len, d_att), lambda b, t: (b, 0, 0, 0)),
                  _const_spec(bias.shape), _const_spec((1, d_att))],
        out_specs=pl.BlockSpec((1, tq, d_att), lambda b, t: (b, t, 0)),
        out_shape=jax.ShapeDtypeStruct((nb, t_len, d_att), BF16),
        scratch_shapes=_attn_scratch(n_heads, slots=2),
        compiler_params=pltpu.CompilerParams(
            dimension_semantics=("arbitrary", "arbitrary"), vmem_limit_bytes=VMEM_LIMIT),
        name="attn_prompt",
    )(q, kt, v, bias, g_att)


def _attn_sample_kernel(q_ref, ck_ref, cv_ref, kn_ref, vn_ref, bias_ref, g_ref, o_ref, kt_s, v_s,
                        s_scr, p_scr):
    d_att = q_ref.shape[-1]
    pad = jnp.zeros((WIN - ck_ref.shape[1] - kn_ref.shape[1], d_att), F32)
    k_all = jnp.concatenate([ck_ref[0].astype(F32), kn_ref[0], pad], axis=0)
    even_row = (lax.broadcasted_iota(jnp.int32, (d_att, LANES), 0) & HEAD_DIM) == 0
    for c in range(WIN_SLABS):
        kt = k_all[c * LANES:(c + 1) * LANES, :].T
        kt_s[0, c] = jnp.where(even_row, kt, 0.0).astype(BF16)
        kt_s[1, c] = jnp.where(even_row, 0.0, kt).astype(BF16)
    v_all = jnp.concatenate([cv_ref[0].astype(F32), vn_ref[0], pad], axis=0)
    even = (lax.broadcasted_iota(jnp.int32, v_all.shape, 1) & HEAD_DIM) == 0
    v_s[0] = jnp.where(even, v_all, 0.0).astype(BF16)
    v_s[1] = jnp.where(even, 0.0, v_all).astype(BF16)

    def get_k(hp):
        return jnp.concatenate(
            [kt_s[e, c, hp * LANES:(hp + 1) * LANES, :] for e in range(2) for c in range(WIN_SLABS)], axis=1)

    def get_v(hp):
        lanes = slice(hp * LANES, (hp + 1) * LANES)
        return jnp.concatenate([v_s[0, :, lanes], v_s[1, :, lanes]], axis=0)

    def get_q(hp):
        return q_ref[0, :, hp * LANES:(hp + 1) * LANES]

    a = _attend(q_ref.shape[1], WIN_SLABS, get_q, get_k, get_v, bias_ref, 0, s_scr.at[0], p_scr.at[0])
    o_ref[0] = _rms(a, g_ref[...]).astype(BF16)


def _attn_sample(q, cache_k, cache_v, k_new, v_new, bias, g_att):
    nb, s_len, d_att = k_new.shape
    n_heads = d_att // HEAD_DIM
    l_cache = cache_k.shape[1]
    assert s_len == CHUNK and l_cache == LEFT_CHUNKS * CHUNK
    per_b = lambda b: (b, 0, 0)
    return pl.pallas_call(
        _attn_sample_kernel,
        grid=(nb,),
        in_specs=[pl.BlockSpec((1, s_len, d_att), per_b),
                  pl.BlockSpec((1, l_cache, d_att), per_b), pl.BlockSpec((1, l_cache, d_att), per_b),
                  pl.BlockSpec((1, s_len, d_att), per_b), pl.BlockSpec((1, s_len, d_att), per_b),
                  _const_spec(bias.shape), _const_spec((1, d_att))],
        out_specs=pl.BlockSpec((1, s_len, d_att), per_b),
        out_shape=jax.ShapeDtypeStruct((nb, s_len, d_att), BF16),
        scratch_shapes=[pltpu.VMEM((2, WIN_SLABS, d_att, LANES), BF16), pltpu.VMEM((2, WIN, d_att), BF16)]
                       + _attn_scratch(n_heads, slots=1),
        compiler_params=pltpu.CompilerParams(
            dimension_semantics=("arbitrary",), vmem_limit_bytes=VMEM_LIMIT),
        name="attn_sample",
    )(q, cache_k, cache_v, k_new, v_new, bias, g_att)


def _dense_stage(x, cn, an, p, wout_ref, gffn_ref, w1_ref, w2_ref, gple_ref, wg_ref, wple_ref, ff_chunk):
    d_conv = cn.shape[-1]
    x1 = (x + jnp.dot(cn, wout_ref[0:d_conv, :], preferred_element_type=F32)
          + jnp.dot(an, wout_ref[d_conv:, :], preferred_element_type=F32))
    hb = _rms(x1, gffn_ref[...]).astype(BF16)
    ffn = None
    for lo in range(0, w1_ref.shape[1], ff_chunk):
        a = jnp.maximum(jnp.dot(hb, w1_ref[:, lo:lo + ff_chunk], preferred_element_type=F32), 0.0)
        part = jnp.dot((a * a).astype(BF16), w2_ref[lo:lo + ff_chunk, :], preferred_element_type=F32)
        ffn = part if ffn is None else ffn + part
    x2 = x1 + ffn
    gate = jax.nn.sigmoid(jnp.dot(_rms(x2, gple_ref[...]).astype(BF16), wg_ref[...],
                                  preferred_element_type=F32))
    ple = jnp.dot(p.astype(BF16), wple_ref[...], preferred_element_type=F32)
    return x2 + ple * gate


def _mix_kernel(*refs, starts, ff_chunk):
    n_groups = len(starts) - 1
    ins, dense_w, outs = refs[:4 * n_groups], refs[4 * n_groups:-n_groups], refs[-n_groups:]
    i = pl.program_id(0)
    for g in range(n_groups):
        x_ref, p_ref, an_ref, cn_ref = ins[4 * g:4 * g + 4]

        @pl.when((i >= starts[g]) & (i < starts[g + 1]))
        def _():
            outs[g][0] = _dense_stage(x_ref[0], cn_ref[0], an_ref[0], p_ref[0], *dense_w, ff_chunk)


def _mix(groups, w_out, g_ffn, w1, w2, g_ple, w_gate, w_ple, *, tm):
    consts = [w_out, g_ffn, w1, w2, g_ple, w_gate, w_ple]
    starts, in_specs, out_specs, out_shape, operands = [0], [], [], [], []
    for x, p, an, cn in groups:
        ng, rows, d_model = x.shape
        assert rows % tm == 0
        n_t = rows // tm
        first, count = starts[-1], ng * n_t
        starts.append(first + count)

        def tile(i, first=first, count=count, n_t=n_t):
            j = jnp.clip(i - first, 0, count - 1)
            return (j // n_t, j % n_t, 0)

        in_specs += [pl.BlockSpec((1, tm, a.shape[-1]), tile) for a in (x, p, an, cn)]
        out_specs.append(pl.BlockSpec((1, tm, d_model), tile))
        out_shape.append(jax.ShapeDtypeStruct(x.shape, F32))
        operands += [x, p, an, cn]
    return pl.pallas_call(
        functools.partial(_mix_kernel, starts=tuple(starts), ff_chunk=1024),
        grid=(starts[-1],),
        in_specs=in_specs + [_const_spec(c.shape) for c in consts],
        out_specs=out_specs,
        out_shape=out_shape,
        compiler_params=pltpu.CompilerParams(
            dimension_semantics=("arbitrary",), vmem_limit_bytes=VMEM_LIMIT),
        name="mix",
    )(*operands, *consts)


def kernel(x_prompt, x_sample, p_prompt, p_sample, cache_att_k, cache_att_v, state_conv, g_mix, w_in, w_dw,
           b_dw, g_conv_ln, b_conv_ln, g_q, g_k, rel_bias, g_conv_out, g_att_out, w_out, g_ffn, w_ff1, w_ff2,
           g_ple, w_gate, w_ple):
    depth = w_in.shape[0]
    nb, t_len, d_model = x_prompt.shape
    sb, s_len, _ = x_sample.shape
    d_conv = w_dw.shape[-1]
    d_att = g_att_out.shape[-1]
    n_heads = d_att // HEAD_DIM
    keep = min(LEFT_CHUNKS * CHUNK, t_len)
    tile = 512

    xp = x_prompt
    xs = x_sample.reshape(1, sb * s_len, d_model)
    outs = [[] for _ in range(6)]
    for i in range(depth):
        vec = lambda a: a[i].reshape(1, -1)
        gq = jnp.tile(g_q[i], n_heads).reshape(1, d_att)
        gk = jnp.tile(g_k[i], n_heads).reshape(1, d_att)
        w_in_b = w_in[i].astype(BF16)
        conv_w = (w_dw[i], vec(b_dw), vec(g_conv_ln), vec(b_conv_ln), vec(g_conv_out))

        (cn_p, u_tail, q_p, kt_p, vb_p, k_tail, v_tail, w_out_b, w_ff1_b, w_ff2_b, w_gate_b, bias) = _in_proj(
            xp, vec(g_mix), w_in_b, gq, gk, conv_w, tm=tile, keep=keep,
            cast=(w_out[i], w_ff1[i], w_ff2[i], w_gate[i]), rel_bias=rel_bias[i])
        mix_w = (w_out_b, vec(g_ffn), w_ff1_b, w_ff2_b, vec(g_ple), w_gate_b, w_ple[i].astype(BF16))
        an_p = _attn_prompt(q_p, kt_p, vb_p, bias, vec(g_att_out), tq=tile)
        outs[0].append(k_tail.reshape(nb, keep, n_heads, HEAD_DIM))
        outs[1].append(v_tail.reshape(nb, keep, n_heads, HEAD_DIM))
        outs[2].append(u_tail[:, HALO - (CONV_WIDTH - 1):])

        rows = sb * s_len
        cn_s, u_s, q_s, k_s, v_s = _in_proj(xs, vec(g_mix), w_in_b, gq, gk, conv_w, tm=rows, keep=rows,
                                            state=state_conv[i])
        per_b = lambda a: a.reshape(sb, s_len, a.shape[-1])
        ck = cache_att_k[i].astype(BF16).reshape(sb, -1, d_att)
        cv = cache_att_v[i].astype(BF16).reshape(sb, -1, d_att)
        an_s = _attn_sample(per_b(q_s), ck, cv, per_b(k_s), per_b(v_s), bias, vec(g_att_out))
        outs[3].append(k_s.reshape(sb, s_len, n_heads, HEAD_DIM))
        outs[4].append(v_s.reshape(sb, s_len, n_heads, HEAD_DIM))
        conv_in_tail = jnp.concatenate([state_conv[i], per_b(u_s)], axis=1)[:, -(CONV_WIDTH - 1):]
        outs[5].append(conv_in_tail)

        xp, xs = _mix([(xp, p_prompt[i], an_p, cn_p),
                       (xs, p_sample[i].reshape(1, rows, -1), an_s.reshape(1, rows, d_att), cn_s)],
                      *mix_w, tm=tile)

    return (xp, xs.reshape(sb, s_len, d_model)) + tuple(jnp.stack(o) for o in outs)
```

```python
import functools

import jax
import jax.numpy as jnp
import numpy as np
from jax import lax
from jax.experimental import pallas as pl
from jax.experimental.pallas import tpu as pltpu

F32 = jnp.float32
BF16 = jnp.bfloat16

CHUNK = 64
LEFT_CHUNKS = 8
HEAD_DIM = 64
CONV_WIDTH = 31
MAX_REL = 128
EPS = 1e-6
NEG = -1e30
LOG2_E = 1.4426950408889634

LANES = 128
SUBLANES = 8
MXU_TILE = 256
PAIR = 2 * CHUNK
WIN = (LEFT_CHUNKS + 2) * CHUNK
WIN_SLABS = WIN // LANES
HALO = 32
CONV_HOP = 128
SOFTMAX_ROWS = 16
VMEM_LIMIT = 56 * 1024 * 1024


def _rms(x, g):
    return x * lax.rsqrt(jnp.mean(x * x, axis=-1, keepdims=True) + EPS) * g


def _const_spec(shape):
    zeros = (0,) * len(shape)
    return pl.BlockSpec(shape, lambda *_: zeros, pipeline_mode=pl.Buffered(1))


def _bias_kernel(rb_ref, o_ref):
    assert rb_ref.shape[1] == 2 * MAX_REL + 1 and MAX_REL == LANES and CHUNK == LANES // 2
    assert LEFT_CHUNKS * CHUNK - MAX_REL == 3 * LANES and WIN_SLABS == 5
    n_heads = rb_ref.shape[0]
    rb = rb_ref[...]
    lane = lax.broadcasted_iota(jnp.int32, (n_heads, LANES), 1)
    b0 = jnp.broadcast_to(rb[:, 0:1], (n_heads, LANES))
    t4 = jnp.where(lane < CHUNK, rb[:, LANES:2 * LANES], b0)
    t = jnp.concatenate([b0, b0, b0, rb[:, 0:LANES], t4, b0], axis=1)
    r_i = lax.broadcasted_iota(jnp.int32, (PAIR, LANES), 0)
    c_i = lax.broadcasted_iota(jnp.int32, (PAIR, LANES), 1)
    for h in range(n_heads):
        row = jnp.broadcast_to(t[h:h + 1, :], (PAIR, t.shape[1]))
        toe = pltpu.roll(row, 0, 1, stride=1, stride_axis=0)
        for c in range(WIN_SLABS):
            jj = c_i + c * LANES
            valid = ((r_i < CHUNK) & (jj < WIN - CHUNK)) | ((r_i >= CHUNK) & (jj >= CHUNK))
            o_ref[h, c] = jnp.where(valid, toe[:, c * LANES:(c + 1) * LANES] * LOG2_E, NEG)


def _dft_tables(n, hop):
    half = n // 2
    t = np.arange(n)[None, :]
    f = np.arange(half)[:, None]
    ang = 2.0 * np.pi * f * t / n
    fwd = np.concatenate([np.cos(ang), np.cos(np.pi * t), np.sin(ang[1:])], axis=0)
    rows = np.arange(n - hop, n)[:, None]
    ang_o = 2.0 * np.pi * rows * f.T / n
    scale = np.where(f.T == 0, 1.0, 2.0) / n
    inv = np.concatenate([scale * np.cos(ang_o), np.cos(np.pi * rows) / n, (2.0 / n) * np.sin(ang_o[:, 1:])], axis=1)
    delay = (CONV_WIDTH - 1) - np.arange(CONV_WIDTH)[None, :]
    ang_h = 2.0 * np.pi * f * delay / n
    p, q = np.cos(ang_h), np.sin(ang_h)
    r = p.copy()
    r[0] = np.cos(np.pi * delay[0])
    return (jnp.asarray(fwd, F32), jnp.asarray(inv, F32), jnp.asarray(np.stack([p, q, r]), F32))


def _filter_response(wdw_ref, resp_ref, h_scr):
    for k in range(3):
        acc = None
        for j in range(CONV_WIDTH):
            term = resp_ref[k, :, j:j + 1] * wdw_ref[j:j + 1, :]
            acc = term if acc is None else acc + term
        h_scr[k] = acc


def _conv_stage(u, n_seg, halo_ref, ubuf, cn_ref, h_scr, fwd_ref, inv_ref,
                bdw_ref, gln_ref, bln_ref, gco_ref):
    d_conv = u.shape[-1]
    seg = u.shape[0] // n_seg
    n, hop = fwd_ref.shape[0], inv_ref.shape[0]
    half = n // 2
    assert n - hop == HALO and HALO >= CONV_WIDTH - 1 and seg % hop == 0
    for s in range(n_seg):
        if halo_ref is not None:
            hr = halo_ref.shape[1]
            if hr < HALO:
                ubuf[s, 0:HALO - hr, :] = jnp.zeros((HALO - hr, d_conv), F32)
            ubuf[s, HALO - hr:HALO, :] = halo_ref[s]
        ubuf[s, HALO:HALO + seg, :] = u[s * seg:(s + 1) * seg, :]

    blocks = [(s, i) for s in range(n_seg) for i in range(seg // hop)]
    windows = jnp.concatenate([ubuf[s, hop * i:hop * i + n, :].astype(BF16) for s, i in blocks], axis=1)
    spec = jnp.dot(fwd_ref[...].astype(BF16), windows, preferred_element_type=F32)
    p, q, r = h_scr[0], h_scr[1], h_scr[2]
    prods = []
    for k in range(len(blocks)):
        top = spec[0:half, k * d_conv:(k + 1) * d_conv]
        bot = spec[half:n, k * d_conv:(k + 1) * d_conv]
        prods.append(jnp.concatenate([p * top - q * bot, r * bot + q * top], axis=0).astype(BF16))
    y = jnp.dot(inv_ref[...].astype(BF16), jnp.concatenate(prods, axis=1),
                preferred_element_type=F32)
    for k, (s, i) in enumerate(blocks):
        yk = y[:, k * d_conv:(k + 1) * d_conv] + bdw_ref[...]
        mu = jnp.mean(yk, axis=-1, keepdims=True)
        yc = yk - mu
        yn = yc * lax.rsqrt(jnp.mean(yc * yc, axis=-1, keepdims=True) + EPS) * gln_ref[...] + bln_ref[...]
        c = yn * jax.nn.sigmoid(yn)
        cn_ref[s * seg + i * hop:s * seg + (i + 1) * hop, :] = _rms(c, gco_ref[...]).astype(BF16)


def _inproj_kernel(*refs, d_conv, d_att, tail0, prompt, n_cast, n_once):
    if prompt:
        x_ref, gmix_ref, win_ref, gq_ref, gk_ref = refs[:5]
        halo_ref, conv_w = None, refs[5:13]
        cast_in, rb_ref = refs[13:13 + n_cast], refs[13 + n_cast]
        once_in = (win_ref,) + refs[14 + n_cast:14 + n_cast + n_once]
        outs = refs[14 + n_cast + n_once:]
        cn_ref, ut_ref, q_ref, kt_ref, vb_ref, k32_ref, v32_ref = outs[:7]
        cast_out, bias_ref = outs[7:7 + n_cast], outs[7 + n_cast]
        once_out = outs[8 + n_cast:9 + n_cast + n_once]
        ubuf, h_scr = outs[9 + n_cast + n_once:]
        win_ref = once_out[0]
        t = pl.program_id(1)
        tm = x_ref.shape[1]

        for src, dst in zip(cast_in, cast_out):
            dst[...] = src[...].astype(BF16)

        @pl.when((pl.program_id(0) == 0) & (t == 0))
        def _():
            for src, dst in zip(once_in, once_out):
                for lo in range(0, src.shape[1], MXU_TILE):
                    dst[:, lo:lo + MXU_TILE] = src[:, lo:lo + MXU_TILE].astype(BF16)
            _bias_kernel(rb_ref, bias_ref)
            _filter_response(conv_w[0], conv_w[7], h_scr)

        @pl.when(t == 0)
        def _():
            ubuf[0, 0:HALO, :] = jnp.zeros((HALO, d_conv), F32)

        @pl.when(t > 0)
        def _():
            ubuf[0, 0:HALO, :] = ubuf[0, tm:tm + HALO, :]
    else:
        x_ref, gmix_ref, win_ref, gq_ref, gk_ref, halo_ref = refs[:6]
        conv_w = refs[6:14]
        cn_ref, u_ref, q_ref, k32_ref, v32_ref, ubuf, h_scr = refs[14:]
        tm = x_ref.shape[1]
        _filter_response(conv_w[0], conv_w[7], h_scr)
    h = _rms(x_ref[0], gmix_ref[...]).astype(BF16)

    def proj(lo, width):
        return jnp.dot(h, win_ref[:, lo:lo + width], preferred_element_type=F32)

    u = proj(0, d_conv) * jax.nn.sigmoid(proj(d_conv, d_conv))
    if prompt:
        @pl.when(t == pl.num_programs(1) - 1)
        def _():
            ut_ref[0] = u[tm - HALO:tm, :]
    else:
        u_ref[0] = u
    wdw_ref, bdw_ref, gln_ref, bln_ref, gco_ref, fwd_ref, inv_ref, _ = conv_w
    _conv_stage(u, ubuf.shape[0], halo_ref, ubuf, cn_ref.at[0], h_scr, fwd_ref, inv_ref,
                bdw_ref, gln_ref, bln_ref, gco_ref)

    head_bits = HEAD_DIM.bit_length() - 1
    r_i = lax.shift_right_logical(lax.broadcasted_iota(jnp.int32, (MXU_TILE, MXU_TILE), 0), head_bits)
    c_i = lax.shift_right_logical(lax.broadcasted_iota(jnp.int32, (MXU_TILE, MXU_TILE), 1), head_bits)
    avg = jnp.where(r_i == c_i, 1.0 / HEAD_DIM, 0.0).astype(BF16)

    def head_rms(z, g):
        sq = (z * z).astype(BF16)
        ms = jnp.concatenate([jnp.dot(sq[:, lo:lo + MXU_TILE], avg, preferred_element_type=F32)
                              for lo in range(0, d_att, MXU_TILE)], axis=1)
        return z * lax.rsqrt(ms + EPS) * g

    o = 2 * d_conv
    qn = head_rms(proj(o, d_att), gq_ref[...])
    q_ref[0] = (qn * (HEAD_DIM ** -0.5 * LOG2_E)).astype(BF16)
    kn = head_rms(proj(o + d_att, d_att), gk_ref[...])
    v = proj(o + 2 * d_att, d_att)
    if prompt:
        even_row = (lax.broadcasted_iota(jnp.int32, (d_att, LANES), 0) & HEAD_DIM) == 0
        for s in range(tm // LANES):
            kt = kn[s * LANES:(s + 1) * LANES, :].T
            kt_ref[0, 0, s] = jnp.where(even_row, kt, 0.0).astype(BF16)
            kt_ref[0, 1, s] = jnp.where(even_row, 0.0, kt).astype(BF16)
        even = (lax.broadcasted_iota(jnp.int32, (tm, d_att), 1) & HEAD_DIM) == 0
        vb_ref[0, 0] = jnp.where(even, v, 0.0).astype(BF16)
        vb_ref[0, 1] = jnp.where(even, 0.0, v).astype(BF16)

    @pl.when(pl.program_id(1) >= tail0)
    def _():
        k32_ref[0] = kn
        v32_ref[0] = v


def _in_proj(x, g_mix, w_in, g_q, g_k, conv_w, *, tm, keep, state=None, cast=(), cast_once=(), rel_bias=None):
    nb, t_len, d_model = x.shape
    d_att = g_q.shape[-1]
    d_in = w_in.shape[-1]
    d_conv = (d_in - 3 * d_att) // 2
    n_t = t_len // tm
    prompt = state is None
    assert t_len % tm == 0 and (t_len - keep) % tm == 0 and tm % LANES == 0 and tm >= HALO
    tail0 = (t_len - keep) // tm
    row = lambda b, t: (b, t, 0)
    tail = lambda b, t: (b, jnp.maximum(t - tail0, 0), 0)
    halves = pl.BlockSpec((1, 2, tm, d_att), lambda b, t: (b, 0, t, 0))
    out_shape = [jax.ShapeDtypeStruct((nb, t_len, d_conv), BF16)]
    out_specs = [pl.BlockSpec((1, tm, d_conv), row)]
    if prompt:
        n_seg = 1
        out_shape += [jax.ShapeDtypeStruct((nb, HALO, d_conv), F32)]
        out_specs += [pl.BlockSpec((1, HALO, d_conv), lambda b, t: (b, 0, 0))]
    else:
        n_seg = state.shape[0]
        assert nb == 1 and n_t == 1 and tm % n_seg == 0 and state.shape[1] == CONV_WIDTH - 1
        out_shape += [jax.ShapeDtypeStruct((nb, t_len, d_conv), F32)]
        out_specs += [pl.BlockSpec((1, tm, d_conv), row)]
    out_shape += [jax.ShapeDtypeStruct((nb, t_len, d_att), BF16)]
    out_specs += [pl.BlockSpec((1, tm, d_att), row)]
    if prompt:
        out_shape += [jax.ShapeDtypeStruct((nb, 2, t_len // LANES, d_att, LANES), BF16),
                      jax.ShapeDtypeStruct((nb, 2, t_len, d_att), BF16)]
        out_specs += [pl.BlockSpec((1, 2, tm // LANES, d_att, LANES), lambda b, t: (b, 0, t, 0, 0)), halves]
    out_shape += [jax.ShapeDtypeStruct((nb, keep, d_att), F32)] * 2
    out_specs += [pl.BlockSpec((1, tm, d_att), tail)] * 2
    cast_specs = []
    for w in cast:
        rb = w.shape[0] // (nb * n_t)
        assert prompt and w.ndim == 2 and rb * nb * n_t == w.shape[0] and rb % (2 * SUBLANES) == 0
        cast_specs.append(pl.BlockSpec((rb, w.shape[1]), lambda b, t: (b * n_t + t, 0)))
        out_shape.append(jax.ShapeDtypeStruct(w.shape, BF16))
    out_specs += cast_specs
    side_specs, side_in = list(cast_specs), list(cast)
    if prompt:
        bias_shape = (rel_bias.shape[0], WIN_SLABS, PAIR, LANES)
        side_specs.append(_const_spec(rel_bias.shape))
        side_in.append(rel_bias)
        out_shape.append(jax.ShapeDtypeStruct(bias_shape, F32))
        out_specs.append(pl.BlockSpec(bias_shape, lambda b, t: (0, 0, 0, 0)))
        for w in (w_in,) + tuple(cast_once):
            assert w.ndim == 2 and w.shape[1] % MXU_TILE == 0
            out_shape.append(jax.ShapeDtypeStruct(w.shape, BF16))
            out_specs.append(pl.BlockSpec(w.shape, lambda b, t: (0, 0)))
        side_specs += [_const_spec(w.shape) for w in cast_once]
        side_in += list(cast_once)
    else:
        assert not cast_once
    kern = functools.partial(_inproj_kernel, d_conv=d_conv, d_att=d_att, tail0=tail0, prompt=prompt,
                             n_cast=len(cast), n_once=len(cast_once))
    extra = [] if prompt else [state]
    hop = min(CONV_HOP, tm // n_seg)
    conv_w = tuple(conv_w) + _dft_tables(HALO + hop, hop)
    return pl.pallas_call(
        kern,
        grid=(nb, n_t),
        in_specs=[pl.BlockSpec((1, tm, d_model), row),
                  _const_spec((1, d_model)), _const_spec((d_model, d_in)),
                  _const_spec((1, d_att)), _const_spec((1, d_att))]
                 + [_const_spec(a.shape) for a in extra + list(conv_w)] + side_specs,
        out_specs=out_specs,
        out_shape=out_shape,
        scratch_shapes=[pltpu.VMEM((n_seg, HALO + tm // n_seg, d_conv), F32),
                        pltpu.VMEM((3, (HALO + hop) // 2, d_conv), F32)],
        compiler_params=pltpu.CompilerParams(
            dimension_semantics=("arbitrary", "arbitrary"), vmem_limit_bytes=VMEM_LIMIT),
        name="in_proj",
    )(x, g_mix, w_in, g_q, g_k, *extra, *conv_w, *side_in)


def _attn_scratch(n_heads, slots):
    return [pltpu.VMEM((slots, n_heads // 2, PAIR, 2 * WIN), F32),
            pltpu.VMEM((slots, n_heads // 2, PAIR, 2 * WIN), BF16)]


def _attend(rows, n_slabs, get_q, get_k, get_v, bias_ref, off, s_scr, p_scr):
    n_pairs = s_scr.shape[0]
    width = n_slabs * LANES
    for hp in range(n_pairs):
        s_scr[hp, 0:rows, 0:2 * width] = jnp.dot(get_q(hp), get_k(hp), preferred_element_type=F32)
    for hp in range(n_pairs):
        for sub in range(2):
            h = 2 * hp + sub
            for r0 in range(0, rows, SOFTMAX_ROWS):
                r = slice(r0, r0 + SOFTMAX_ROWS)
                s = s_scr[hp, r, sub * width:(sub + 1) * width]
                s = s + jnp.concatenate([bias_ref[h, off + c, r, :] for c in range(n_slabs)], axis=1)
                e = jnp.exp2(s - jnp.max(s, axis=-1, keepdims=True))
                p_scr[hp, r, sub * width:(sub + 1) * width] = e.astype(BF16)
    first = lax.broadcasted_iota(jnp.int32, (width, LANES), 1) < HEAD_DIM
    sums = jnp.concatenate([jnp.where(first, 1.0, 0.0), jnp.where(first, 0.0, 1.0)], axis=0).astype(BF16)
    outs = []
    for hp in range(n_pairs):
        o2 = jnp.dot(p_scr[hp, 0:rows, 0:2 * width], jnp.concatenate([get_v(hp), sums], axis=1),
                     preferred_element_type=F32)
        outs.append(o2[:, 0:LANES] * (1.0 / o2[:, LANES:2 * LANES]))
    return jnp.concatenate(outs, axis=1)


def _attn_prompt_kernel(q_ref, kt_ref, v_ref, bias_ref, g_ref, o_ref, s_scr, p_scr):
    pairs = q_ref.shape[1] // PAIR
    slots = s_scr.shape[0]
    t = pl.program_id(1)

    def run_pairs(first_step):
        for i in range(pairs):
            p = i if first_step else t * pairs + i
            n_slabs = min(i + 1, WIN_SLABS) if first_step else WIN_SLABS
            ws = max(p - (WIN_SLABS - 1), 0) if first_step else p - (WIN_SLABS - 1)
            off = ws + (WIN_SLABS - 1) - p
            k0 = ws * LANES if first_step else pl.multiple_of(ws * LANES, LANES)
            width = n_slabs * LANES

            def get_k(hp, ws=ws, n_slabs=n_slabs):
                return jnp.concatenate(
                    [kt_ref[0, e, ws + c, hp * LANES:(hp + 1) * LANES, :]
                     for e in range(2) for c in range(n_slabs)], axis=1)

            def get_v(hp, k0=k0, width=width):
                lanes = slice(hp * LANES, (hp + 1) * LANES)
                return jnp.concatenate([v_ref[0, 0, pl.ds(k0, width), lanes],
                                        v_ref[0, 1, pl.ds(k0, width), lanes]], axis=0)

            rows = slice(i * PAIR, (i + 1) * PAIR)

            def get_q(hp, rows=rows):
                return q_ref[0, rows, hp * LANES:(hp + 1) * LANES]

            k = i % slots
            a = _attend(PAIR, n_slabs, get_q, get_k, get_v, bias_ref, off, s_scr.at[k], p_scr.at[k])
            o_ref[0, rows, :] = _rms(a, g_ref[...]).astype(BF16)

    pl.when(t == 0)(functools.partial(run_pairs, True))
    pl.when(t > 0)(functools.partial(run_pairs, False))


def _attn_prompt(q, kt, v, bias, g_att, *, tq):
    nb, t_len, d_att = q.shape
    n_heads = d_att // HEAD_DIM
    assert t_len % tq == 0 and tq % PAIR == 0 and tq // PAIR >= WIN_SLABS - 1
    return pl.pallas_call(
        _attn_prompt_kernel,
        grid=(nb, t_len // tq),
        in_specs=[pl.BlockSpec((1, tq, d_att), lambda b, t: (b, t, 0)),
                  pl.BlockSpec((1, 2, t_len // LANES, d_att, LANES), lambda b, t: (b, 0, 0, 0, 0)),
                  pl.BlockSpec((1, 2, t_len, d_att), lambda b, t: (b, 0, 0, 0)),
                  _const_spec(bias.shape), _const_spec((1, d_att))],
        out_specs=pl.BlockSpec((1, tq, d_att), lambda b, t: (b, t, 0)),
        out_shape=jax.ShapeDtypeStruct((nb, t_len, d_att), BF16),
        scratch_shapes=_attn_scratch(n_heads, slots=2),
        compiler_params=pltpu.CompilerParams(
            dimension_semantics=("arbitrary", "arbitrary"), vmem_limit_bytes=VMEM_LIMIT),
        name="attn_prompt",
    )(q, kt, v, bias, g_att)


def _attn_sample_kernel(q_ref, ck_ref, cv_ref, kn_ref, vn_ref, bias_ref, g_ref, o_ref, kt_s, v_s,
                        s_scr, p_scr):
    d_att = q_ref.shape[-1]
    pad = jnp.zeros((WIN - ck_ref.shape[1] - kn_ref.shape[1], d_att), F32)
    k_all = jnp.concatenate([ck_ref[0].astype(F32), kn_ref[0], pad], axis=0)
    even_row = (lax.broadcasted_iota(jnp.int32, (d_att, LANES), 0) & HEAD_DIM) == 0
    for c in range(WIN_SLABS):
        kt = k_all[c * LANES:(c + 1) * LANES, :].T
        kt_s[0, c] = jnp.where(even_row, kt, 0.0).astype(BF16)
        kt_s[1, c] = jnp.where(even_row, 0.0, kt).astype(BF16)
    v_all = jnp.concatenate([cv_ref[0].astype(F32), vn_ref[0], pad], axis=0)
    even = (lax.broadcasted_iota(jnp.int32, v_all.shape, 1) & HEAD_DIM) == 0
    v_s[0] = jnp.where(even, v_all, 0.0).astype(BF16)
    v_s[1] = jnp.where(even, 0.0, v_all).astype(BF16)

    def get_k(hp):
        return jnp.concatenate(
            [kt_s[e, c, hp * LANES:(hp + 1) * LANES, :] for e in range(2) for c in range(WIN_SLABS)], axis=1)

    def get_v(hp):
        lanes = slice(hp * LANES, (hp + 1) * LANES)
        return jnp.concatenate([v_s[0, :, lanes], v_s[1, :, lanes]], axis=0)

    def get_q(hp):
        return q_ref[0, :, hp * LANES:(hp + 1) * LANES]

    a = _attend(q_ref.shape[1], WIN_SLABS, get_q, get_k, get_v, bias_ref, 0, s_scr.at[0], p_scr.at[0])
    o_ref[0] = _rms(a, g_ref[...]).astype(BF16)


def _attn_sample(q, cache_k, cache_v, k_new, v_new, bias, g_att):
    nb, s_len, d_att = k_new.shape
    n_heads = d_att // HEAD_DIM
    l_cache = cache_k.shape[1]
    assert s_len == CHUNK and l_cache == LEFT_CHUNKS * CHUNK
    per_b = lambda b: (b, 0, 0)
    return pl.pallas_call(
        _attn_sample_kernel,
        grid=(nb,),
        in_specs=[pl.BlockSpec((1, s_len, d_att), per_b),
                  pl.BlockSpec((1, l_cache, d_att), per_b), pl.BlockSpec((1, l_cache, d_att), per_b),
                  pl.BlockSpec((1, s_len, d_att), per_b), pl.BlockSpec((1, s_len, d_att), per_b),
                  _const_spec(bias.shape), _const_spec((1, d_att))],
        out_specs=pl.BlockSpec((1, s_len, d_att), per_b),
        out_shape=jax.ShapeDtypeStruct((nb, s_len, d_att), BF16),
        scratch_shapes=[pltpu.VMEM((2, WIN_SLABS, d_att, LANES), BF16), pltpu.VMEM((2, WIN, d_att), BF16)]
                       + _attn_scratch(n_heads, slots=1),
        compiler_params=pltpu.CompilerParams(
            dimension_semantics=("arbitrary",), vmem_limit_bytes=VMEM_LIMIT),
        name="attn_sample",
    )(q, cache_k, cache_v, k_new, v_new, bias, g_att)


def _dense_stage(x, cn, an, p, wout_ref, gffn_ref, w1_ref, w2_ref, gple_ref, wg_ref, wple_ref, ff_chunk):
    d_conv = cn.shape[-1]
    x1 = (x + jnp.dot(cn, wout_ref[0:d_conv, :], preferred_element_type=F32)
          + jnp.dot(an, wout_ref[d_conv:, :], preferred_element_type=F32))
    hb = _rms(x1, gffn_ref[...]).astype(BF16)
    ffn = None
    for lo in range(0, w1_ref.shape[1], ff_chunk):
        a = jnp.maximum(jnp.dot(hb, w1_ref[:, lo:lo + ff_chunk], preferred_element_type=F32), 0.0)
        part = jnp.dot((a * a).astype(BF16), w2_ref[lo:lo + ff_chunk, :], preferred_element_type=F32)
        ffn = part if ffn is None else ffn + part
    x2 = x1 + ffn
    gate = jax.nn.sigmoid(jnp.dot(_rms(x2, gple_ref[...]).astype(BF16), wg_ref[...],
                                  preferred_element_type=F32))
    ple = jnp.dot(p.astype(BF16), wple_ref[...], preferred_element_type=F32)
    return x2 + ple * gate


def _mix_kernel(*refs, starts, ff_chunk):
    n_groups = len(starts) - 1
    ins, dense_w, outs = refs[:4 * n_groups], refs[4 * n_groups:-n_groups], refs[-n_groups:]
    i = pl.program_id(0)
    for g in range(n_groups):
        x_ref, p_ref, an_ref, cn_ref = ins[4 * g:4 * g + 4]

        @pl.when((i >= starts[g]) & (i < starts[g + 1]))
        def _():
            outs[g][0] = _dense_stage(x_ref[0], cn_ref[0], an_ref[0], p_ref[0], *dense_w, ff_chunk)


def _mix(groups, w_out, g_ffn, w1, w2, g_ple, w_gate, w_ple, *, tm):
    consts = [w_out, g_ffn, w1, w2, g_ple, w_gate, w_ple]
    starts, in_specs, out_specs, out_shape, operands = [0], [], [], [], []
    for x, p, an, cn in groups:
        ng, rows, d_model = x.shape
        assert rows % tm == 0
        n_t = rows // tm
        first, count = starts[-1], ng * n_t
        starts.append(first + count)

        def tile(i, first=first, count=count, n_t=n_t):
            j = jnp.clip(i - first, 0, count - 1)
            return (j // n_t, j % n_t, 0)

        in_specs += [pl.BlockSpec((1, tm, a.shape[-1]), tile) for a in (x, p, an, cn)]
        out_specs.append(pl.BlockSpec((1, tm, d_model), tile))
        out_shape.append(jax.ShapeDtypeStruct(x.shape, F32))
        operands += [x, p, an, cn]
    return pl.pallas_call(
        functools.partial(_mix_kernel, starts=tuple(starts), ff_chunk=1024),
        grid=(starts[-1],),
        in_specs=in_specs + [_const_spec(c.shape) for c in consts],
        out_specs=out_specs,
        out_shape=out_shape,
        compiler_params=pltpu.CompilerParams(
            dimension_semantics=("arbitrary",), vmem_limit_bytes=VMEM_LIMIT),
        name="mix",
    )(*operands, *consts)


def kernel(x_prompt, x_sample, p_prompt, p_sample, cache_att_k, cache_att_v, state_conv, g_mix, w_in, w_dw,
           b_dw, g_conv_ln, b_conv_ln, g_q, g_k, rel_bias, g_conv_out, g_att_out, w_out, g_ffn, w_ff1, w_ff2,
           g_ple, w_gate, w_ple):
    depth = w_in.shape[0]
    nb, t_len, d_model = x_prompt.shape
    sb, s_len, _ = x_sample.shape
    d_conv = w_dw.shape[-1]
    d_att = g_att_out.shape[-1]
    n_heads = d_att // HEAD_DIM
    keep = min(LEFT_CHUNKS * CHUNK, t_len)
    tile = 512

    xp = x_prompt
    xs = x_sample.reshape(1, sb * s_len, d_model)
    outs = [[] for _ in range(6)]
    for i in range(depth):
        vec = lambda a: a[i].reshape(1, -1)
        gq = jnp.tile(g_q[i], n_heads).reshape(1, d_att)
        gk = jnp.tile(g_k[i], n_heads).reshape(1, d_att)
        conv_w = (w_dw[i], vec(b_dw), vec(g_conv_ln), vec(b_conv_ln), vec(g_conv_out))

        (cn_p, u_tail, q_p, kt_p, vb_p, k_tail, v_tail, w_out_b, w_ff1_b, w_ff2_b, w_gate_b, bias,
         w_in_b, w_ple_b) = _in_proj(
            xp, vec(g_mix), w_in[i], gq, gk, conv_w, tm=tile, keep=keep,
            cast=(w_out[i], w_ff1[i], w_ff2[i], w_gate[i]), cast_once=(w_ple[i],), rel_bias=rel_bias[i])
        mix_w = (w_out_b, vec(g_ffn), w_ff1_b, w_ff2_b, vec(g_ple), w_gate_b, w_ple_b)
        an_p = _attn_prompt(q_p, kt_p, vb_p, bias, vec(g_att_out), tq=tile)
        outs[0].append(k_tail.reshape(nb, keep, n_heads, HEAD_DIM))
        outs[1].append(v_tail.reshape(nb, keep, n_heads, HEAD_DIM))
        outs[2].append(u_tail[:, HALO - (CONV_WIDTH - 1):])

        rows = sb * s_len
        cn_s, u_s, q_s, k_s, v_s = _in_proj(xs, vec(g_mix), w_in_b, gq, gk, conv_w, tm=rows, keep=rows,
                                            state=state_conv[i])
        per_b = lambda a: a.reshape(sb, s_len, a.shape[-1])
        ck = cache_att_k[i].astype(BF16).reshape(sb, -1, d_att)
        cv = cache_att_v[i].astype(BF16).reshape(sb, -1, d_att)
        an_s = _attn_sample(per_b(q_s), ck, cv, per_b(k_s), per_b(v_s), bias, vec(g_att_out))
        outs[3].append(k_s.reshape(sb, s_len, n_heads, HEAD_DIM))
        outs[4].append(v_s.reshape(sb, s_len, n_heads, HEAD_DIM))
        conv_in_tail = jnp.concatenate([state_conv[i], per_b(u_s)], axis=1)[:, -(CONV_WIDTH - 1):]
        outs[5].append(conv_in_tail)

        xp, xs = _mix([(xp, p_prompt[i], an_p, cn_p),
                       (xs, p_sample[i].reshape(1, rows, -1), an_s.reshape(1, rows, d_att), cn_s)],
                      *mix_w, tm=tile)

    return (xp, xs.reshape(sb, s_len, d_model)) + tuple(jnp.stack(o) for o in outs)
```

```python
import functools

import jax
import jax.numpy as jnp
import numpy as np
from jax import lax
from jax.experimental import pallas as pl
from jax.experimental.pallas import tpu as pltpu

F32 = jnp.float32
BF16 = jnp.bfloat16

CHUNK = 64
LEFT_CHUNKS = 8
HEAD_DIM = 64
CONV_WIDTH = 31
MAX_REL = 128
EPS = 1e-6
NEG = -1e30
LOG2_E = 1.4426950408889634

LANES = 128
SUBLANES = 8
MXU_TILE = 256
PAIR = 2 * CHUNK
WIN = (LEFT_CHUNKS + 2) * CHUNK
WIN_SLABS = WIN // LANES
HALO = 32
CONV_HOP = 128
SOFTMAX_ROWS = 16
VMEM_LIMIT = 56 * 1024 * 1024


def _rms(x, g):
    return x * lax.rsqrt(jnp.mean(x * x, axis=-1, keepdims=True) + EPS) * g


def _const_spec(shape):
    zeros = (0,) * len(shape)
    return pl.BlockSpec(shape, lambda *_: zeros, pipeline_mode=pl.Buffered(1))


def _bias_kernel(rb_ref, o_ref):
    assert rb_ref.shape[1] == 2 * MAX_REL + 1 and MAX_REL == LANES and CHUNK == LANES // 2
    assert LEFT_CHUNKS * CHUNK - MAX_REL == 3 * LANES and WIN_SLABS == 5
    n_heads = rb_ref.shape[0]
    rb = rb_ref[...]
    lane = lax.broadcasted_iota(jnp.int32, (n_heads, LANES), 1)
    b0 = jnp.broadcast_to(rb[:, 0:1], (n_heads, LANES))
    t4 = jnp.where(lane < CHUNK, rb[:, LANES:2 * LANES], b0)
    t = jnp.concatenate([b0, b0, b0, rb[:, 0:LANES], t4, b0], axis=1)
    r_i = lax.broadcasted_iota(jnp.int32, (PAIR, LANES), 0)
    c_i = lax.broadcasted_iota(jnp.int32, (PAIR, LANES), 1)
    for h in range(n_heads):
        row = jnp.broadcast_to(t[h:h + 1, :], (PAIR, t.shape[1]))
        toe = pltpu.roll(row, 0, 1, stride=1, stride_axis=0)
        for c in range(WIN_SLABS):
            jj = c_i + c * LANES
            valid = ((r_i < CHUNK) & (jj < WIN - CHUNK)) | ((r_i >= CHUNK) & (jj >= CHUNK))
            o_ref[h, c] = jnp.where(valid, toe[:, c * LANES:(c + 1) * LANES] * LOG2_E, NEG)


def _dft_tables(n, hop):
    half = n // 2
    t = np.arange(n)[None, :]
    f = np.arange(half)[:, None]
    ang = 2.0 * np.pi * f * t / n
    fwd = np.concatenate([np.cos(ang), np.cos(np.pi * t), np.sin(ang[1:])], axis=0)
    rows = np.arange(n - hop, n)[:, None]
    ang_o = 2.0 * np.pi * rows * f.T / n
    scale = np.where(f.T == 0, 1.0, 2.0) / n
    inv = np.concatenate([scale * np.cos(ang_o), np.cos(np.pi * rows) / n, (2.0 / n) * np.sin(ang_o[:, 1:])], axis=1)
    delay = (CONV_WIDTH - 1) - np.arange(CONV_WIDTH)[None, :]
    ang_h = 2.0 * np.pi * f * delay / n
    p, q = np.cos(ang_h), np.sin(ang_h)
    r = p.copy()
    r[0] = np.cos(np.pi * delay[0])
    return (jnp.asarray(fwd, F32), jnp.asarray(inv, F32), jnp.asarray(np.stack([p, q, r]), F32))


def _filter_response(wdw_ref, resp_ref, h_scr):
    for k in range(3):
        acc = None
        for j in range(CONV_WIDTH):
            term = resp_ref[k, :, j:j + 1] * wdw_ref[j:j + 1, :]
            acc = term if acc is None else acc + term
        h_scr[k] = acc


def _conv_stage(u, n_seg, halo_ref, ubuf, cn_ref, h_scr, fwd_ref, inv_ref,
                bdw_ref, gln_ref, bln_ref, gco_ref):
    d_conv = u.shape[-1]
    seg = u.shape[0] // n_seg
    n, hop = fwd_ref.shape[0], inv_ref.shape[0]
    half = n // 2
    assert n - hop == HALO and HALO >= CONV_WIDTH - 1 and seg % hop == 0
    for s in range(n_seg):
        if halo_ref is not None:
            hr = halo_ref.shape[1]
            if hr < HALO:
                ubuf[s, 0:HALO - hr, :] = jnp.zeros((HALO - hr, d_conv), F32)
            ubuf[s, HALO - hr:HALO, :] = halo_ref[s]
        ubuf[s, HALO:HALO + seg, :] = u[s * seg:(s + 1) * seg, :]

    blocks = [(s, i) for s in range(n_seg) for i in range(seg // hop)]
    windows = jnp.concatenate([ubuf[s, hop * i:hop * i + n, :].astype(BF16) for s, i in blocks], axis=1)
    spec = jnp.dot(fwd_ref[...].astype(BF16), windows, preferred_element_type=F32)
    p, q, r = h_scr[0], h_scr[1], h_scr[2]
    prods = []
    for k in range(len(blocks)):
        top = spec[0:half, k * d_conv:(k + 1) * d_conv]
        bot = spec[half:n, k * d_conv:(k + 1) * d_conv]
        prods.append(jnp.concatenate([p * top - q * bot, r * bot + q * top], axis=0).astype(BF16))
    y = jnp.dot(inv_ref[...].astype(BF16), jnp.concatenate(prods, axis=1),
                preferred_element_type=F32)
    for k, (s, i) in enumerate(blocks):
        yk = y[:, k * d_conv:(k + 1) * d_conv] + bdw_ref[...]
        mu = jnp.mean(yk, axis=-1, keepdims=True)
        yc = yk - mu
        yn = yc * lax.rsqrt(jnp.mean(yc * yc, axis=-1, keepdims=True) + EPS) * gln_ref[...] + bln_ref[...]
        c = yn * jax.nn.sigmoid(yn)
        cn_ref[s * seg + i * hop:s * seg + (i + 1) * hop, :] = _rms(c, gco_ref[...]).astype(BF16)


def _inproj_kernel(*refs, d_conv, d_att, tail0, prompt, n_cast, n_once):
    if prompt:
        x_ref, gmix_ref, win_ref, gq_ref, gk_ref = refs[:5]
        halo_ref, conv_w = None, refs[5:13]
        cast_in, rb_ref = refs[13:13 + n_cast], refs[13 + n_cast]
        once_in = (win_ref,) + refs[14 + n_cast:14 + n_cast + n_once]
        outs = refs[14 + n_cast + n_once:]
        cn_ref, ut_ref, q_ref, kt_ref, vb_ref, k32_ref, v32_ref = outs[:7]
        cast_out, bias_ref = outs[7:7 + n_cast], outs[7 + n_cast]
        once_out = outs[8 + n_cast:9 + n_cast + n_once]
        ubuf, h_scr = outs[9 + n_cast + n_once:]
        win_ref = once_out[0]
        t = pl.program_id(1)
        tm = x_ref.shape[1]

        for src, dst in zip(cast_in, cast_out):
            dst[...] = src[...].astype(BF16)

        @pl.when((pl.program_id(0) == 0) & (t == 0))
        def _():
            for src, dst in zip(once_in, once_out):
                for lo in range(0, src.shape[1], MXU_TILE):
                    dst[:, lo:lo + MXU_TILE] = src[:, lo:lo + MXU_TILE].astype(BF16)
            _bias_kernel(rb_ref, bias_ref)
            _filter_response(conv_w[0], conv_w[7], h_scr)

        @pl.when(t == 0)
        def _():
            ubuf[0, 0:HALO, :] = jnp.zeros((HALO, d_conv), F32)

        @pl.when(t > 0)
        def _():
            ubuf[0, 0:HALO, :] = ubuf[0, tm:tm + HALO, :]
    else:
        x_ref, gmix_ref, win_ref, gq_ref, gk_ref, halo_ref = refs[:6]
        conv_w = refs[6:14]
        cn_ref, u_ref, q_ref, k32_ref, v32_ref, ubuf, h_scr = refs[14:]
        tm = x_ref.shape[1]
        _filter_response(conv_w[0], conv_w[7], h_scr)
    h = _rms(x_ref[0], gmix_ref[...]).astype(BF16)

    def proj(lo, width):
        return jnp.dot(h, win_ref[:, lo:lo + width], preferred_element_type=F32)

    u = proj(0, d_conv) * jax.nn.sigmoid(proj(d_conv, d_conv))
    if prompt:
        @pl.when(t == pl.num_programs(1) - 1)
        def _():
            ut_ref[0] = u[tm - HALO:tm, :]
    else:
        u_ref[0] = u
    wdw_ref, bdw_ref, gln_ref, bln_ref, gco_ref, fwd_ref, inv_ref, _ = conv_w
    _conv_stage(u, ubuf.shape[0], halo_ref, ubuf, cn_ref.at[0], h_scr, fwd_ref, inv_ref,
                bdw_ref, gln_ref, bln_ref, gco_ref)

    head_bits = HEAD_DIM.bit_length() - 1
    r_i = lax.shift_right_logical(lax.broadcasted_iota(jnp.int32, (MXU_TILE, MXU_TILE), 0), head_bits)
    c_i = lax.shift_right_logical(lax.broadcasted_iota(jnp.int32, (MXU_TILE, MXU_TILE), 1), head_bits)
    avg = jnp.where(r_i == c_i, 1.0 / HEAD_DIM, 0.0).astype(BF16)

    def head_rms(z, g):
        sq = (z * z).astype(BF16)
        ms = jnp.concatenate([jnp.dot(sq[:, lo:lo + MXU_TILE], avg, preferred_element_type=F32)
                              for lo in range(0, d_att, MXU_TILE)], axis=1)
        return z * lax.rsqrt(ms + EPS) * g

    o = 2 * d_conv
    qn = head_rms(proj(o, d_att), gq_ref[...])
    q_ref[0] = (qn * (HEAD_DIM ** -0.5 * LOG2_E)).astype(BF16)
    kn = head_rms(proj(o + d_att, d_att), gk_ref[...])
    v = proj(o + 2 * d_att, d_att)
    if prompt:
        even_row = (lax.broadcasted_iota(jnp.int32, (d_att, LANES), 0) & HEAD_DIM) == 0
        for s in range(tm // LANES):
            kt = kn[s * LANES:(s + 1) * LANES, :].T
            kt_ref[0, 0, s] = jnp.where(even_row, kt, 0.0).astype(BF16)
            kt_ref[0, 1, s] = jnp.where(even_row, 0.0, kt).astype(BF16)
        even = (lax.broadcasted_iota(jnp.int32, (tm, d_att), 1) & HEAD_DIM) == 0
        vb_ref[0, 0] = jnp.where(even, v, 0.0).astype(BF16)
        vb_ref[0, 1] = jnp.where(even, 0.0, v).astype(BF16)

    @pl.when(pl.program_id(1) >= tail0)
    def _():
        k32_ref[0] = kn
        v32_ref[0] = v


def _in_proj(x, g_mix, w_in, g_q, g_k, conv_w, *, tm, keep, state=None, cast=(), cast_once=(), rel_bias=None):
    nb, t_len, d_model = x.shape
    d_att = g_q.shape[-1]
    d_in = w_in.shape[-1]
    d_conv = (d_in - 3 * d_att) // 2
    n_t = t_len // tm
    prompt = state is None
    assert t_len % tm == 0 and (t_len - keep) % tm == 0 and tm % LANES == 0 and tm >= HALO
    tail0 = (t_len - keep) // tm
    row = lambda b, t: (b, t, 0)
    tail = lambda b, t: (b, jnp.maximum(t - tail0, 0), 0)
    halves = pl.BlockSpec((1, 2, tm, d_att), lambda b, t: (b, 0, t, 0))
    out_shape = [jax.ShapeDtypeStruct((nb, t_len, d_conv), BF16)]
    out_specs = [pl.BlockSpec((1, tm, d_conv), row)]
    if prompt:
        n_seg = 1
        out_shape += [jax.ShapeDtypeStruct((nb, HALO, d_conv), F32)]
        out_specs += [pl.BlockSpec((1, HALO, d_conv), lambda b, t: (b, 0, 0))]
    else:
        n_seg = state.shape[0]
        assert nb == 1 and n_t == 1 and tm % n_seg == 0 and state.shape[1] == CONV_WIDTH - 1
        out_shape += [jax.ShapeDtypeStruct((nb, t_len, d_conv), F32)]
        out_specs += [pl.BlockSpec((1, tm, d_conv), row)]
    out_shape += [jax.ShapeDtypeStruct((nb, t_len, d_att), BF16)]
    out_specs += [pl.BlockSpec((1, tm, d_att), row)]
    if prompt:
        out_shape += [jax.ShapeDtypeStruct((nb, 2, t_len // LANES, d_att, LANES), BF16),
                      jax.ShapeDtypeStruct((nb, 2, t_len, d_att), BF16)]
        out_specs += [pl.BlockSpec((1, 2, tm // LANES, d_att, LANES), lambda b, t: (b, 0, t, 0, 0)), halves]
    out_shape += [jax.ShapeDtypeStruct((nb, keep, d_att), F32)] * 2
    out_specs += [pl.BlockSpec((1, tm, d_att), tail)] * 2
    cast_specs = []
    for w in cast:
        rb = w.shape[0] // (nb * n_t)
        assert prompt and w.ndim == 2 and rb * nb * n_t == w.shape[0] and rb % (2 * SUBLANES) == 0
        cast_specs.append(pl.BlockSpec((rb, w.shape[1]), lambda b, t: (b * n_t + t, 0)))
        out_shape.append(jax.ShapeDtypeStruct(w.shape, BF16))
    out_specs += cast_specs
    side_specs, side_in = list(cast_specs), list(cast)
    if prompt:
        bias_shape = (rel_bias.shape[0], WIN_SLABS, PAIR, LANES)
        side_specs.append(_const_spec(rel_bias.shape))
        side_in.append(rel_bias)
        out_shape.append(jax.ShapeDtypeStruct(bias_shape, F32))
        out_specs.append(pl.BlockSpec(bias_shape, lambda b, t: (0, 0, 0, 0)))
        for w in (w_in,) + tuple(cast_once):
            assert w.ndim == 2 and w.shape[1] % MXU_TILE == 0
            out_shape.append(jax.ShapeDtypeStruct(w.shape, BF16))
            out_specs.append(pl.BlockSpec(w.shape, lambda b, t: (0, 0)))
        side_specs += [_const_spec(w.shape) for w in cast_once]
        side_in += list(cast_once)
    else:
        assert not cast_once
    kern = functools.partial(_inproj_kernel, d_conv=d_conv, d_att=d_att, tail0=tail0, prompt=prompt,
                             n_cast=len(cast), n_once=len(cast_once))
    extra = [] if prompt else [state]
    hop = min(CONV_HOP, tm // n_seg)
    conv_w = tuple(conv_w) + _dft_tables(HALO + hop, hop)
    return pl.pallas_call(
        kern,
        grid=(nb, n_t),
        in_specs=[pl.BlockSpec((1, tm, d_model), row),
                  _const_spec((1, d_model)), _const_spec((d_model, d_in)),
                  _const_spec((1, d_att)), _const_spec((1, d_att))]
                 + [_const_spec(a.shape) for a in extra + list(conv_w)] + side_specs,
        out_specs=out_specs,
        out_shape=out_shape,
        scratch_shapes=[pltpu.VMEM((n_seg, HALO + tm // n_seg, d_conv), F32),
                        pltpu.VMEM((3, (HALO + hop) // 2, d_conv), F32)],
        compiler_params=pltpu.CompilerParams(
            dimension_semantics=("arbitrary", "arbitrary"), vmem_limit_bytes=VMEM_LIMIT),
        name="in_proj",
    )(x, g_mix, w_in, g_q, g_k, *extra, *conv_w, *side_in)


def _attn_scratch(n_heads, slots):
    return [pltpu.VMEM((slots, n_heads // 2, PAIR, 2 * WIN), F32),
            pltpu.VMEM((slots, n_heads // 2, PAIR, 2 * WIN), BF16)]


def _attend(rows, n_slabs, get_q, get_k, get_v, bias_ref, off, s_scr, p_scr):
    n_pairs = s_scr.shape[0]
    width = n_slabs * LANES
    for hp in range(n_pairs):
        s_scr[hp, 0:rows, 0:2 * width] = jnp.dot(get_q(hp), get_k(hp), preferred_element_type=F32)
    for hp in range(n_pairs):
        for sub in range(2):
            h = 2 * hp + sub
            for r0 in range(0, rows, SOFTMAX_ROWS):
                r = slice(r0, r0 + SOFTMAX_ROWS)
                s = s_scr[hp, r, sub * width:(sub + 1) * width]
                s = s + jnp.concatenate([bias_ref[h, off + c, r, :] for c in range(n_slabs)], axis=1)
                e = jnp.exp2(s - jnp.max(s, axis=-1, keepdims=True))
                p_scr[hp, r, sub * width:(sub + 1) * width] = e.astype(BF16)
    first = lax.broadcasted_iota(jnp.int32, (width, LANES), 1) < HEAD_DIM
    sums = jnp.concatenate([jnp.where(first, 1.0, 0.0), jnp.where(first, 0.0, 1.0)], axis=0).astype(BF16)
    outs = []
    for hp in range(n_pairs):
        o2 = jnp.dot(p_scr[hp, 0:rows, 0:2 * width], jnp.concatenate([get_v(hp), sums], axis=1),
                     preferred_element_type=F32)
        outs.append(o2[:, 0:LANES] * (1.0 / o2[:, LANES:2 * LANES]))
    return jnp.concatenate(outs, axis=1)


def _attn_prompt_kernel(q_ref, kt_ref, v_ref, bias_ref, g_ref, o_ref, s_scr, p_scr):
    pairs = q_ref.shape[1] // PAIR
    slots = s_scr.shape[0]
    t = pl.program_id(1)

    def run_pairs(first_step):
        for i in range(pairs):
            p = i if first_step else t * pairs + i
            n_slabs = min(i + 1, WIN_SLABS) if first_step else WIN_SLABS
            ws = max(p - (WIN_SLABS - 1), 0) if first_step else p - (WIN_SLABS - 1)
            off = ws + (WIN_SLABS - 1) - p
            k0 = ws * LANES if first_step else pl.multiple_of(ws * LANES, LANES)
            width = n_slabs * LANES

            def get_k(hp, ws=ws, n_slabs=n_slabs):
                return jnp.concatenate(
                    [kt_ref[0, e, ws + c, hp * LANES:(hp + 1) * LANES, :]
                     for e in range(2) for c in range(n_slabs)], axis=1)

            def get_v(hp, k0=k0, width=width):
                lanes = slice(hp * LANES, (hp + 1) * LANES)
                return jnp.concatenate([v_ref[0, 0, pl.ds(k0, width), lanes],
                                        v_ref[0, 1, pl.ds(k0, width), lanes]], axis=0)

            rows = slice(i * PAIR, (i + 1) * PAIR)

            def get_q(hp, rows=rows):
                return q_ref[0, rows, hp * LANES:(hp + 1) * LANES]

            k = i % slots
            a = _attend(PAIR, n_slabs, get_q, get_k, get_v, bias_ref, off, s_scr.at[k], p_scr.at[k])
            o_ref[0, rows, :] = _rms(a, g_ref[...]).astype(BF16)

    pl.when(t == 0)(functools.partial(run_pairs, True))
    pl.when(t > 0)(functools.partial(run_pairs, False))


def _attn_prompt(q, kt, v, bias, g_att, *, tq):
    nb, t_len, d_att = q.shape
    n_heads = d_att // HEAD_DIM
    assert t_len % tq == 0 and tq % PAIR == 0 and tq // PAIR >= WIN_SLABS - 1
    return pl.pallas_call(
        _attn_prompt_kernel,
        grid=(nb, t_len // tq),
        in_specs=[pl.BlockSpec((1, tq, d_att), lambda b, t: (b, t, 0)),
                  pl.BlockSpec((1, 2, t_len // LANES, d_att, LANES), lambda b, t: (b, 0, 0, 0, 0)),
                  pl.BlockSpec((1, 2, t_len, d_att), lambda b, t: (b, 0, 0, 0)),
                  _const_spec(bias.shape), _const_spec((1, d_att))],
        out_specs=pl.BlockSpec((1, tq, d_att), lambda b, t: (b, t, 0)),
        out_shape=jax.ShapeDtypeStruct((nb, t_len, d_att), BF16),
        scratch_shapes=_attn_scratch(n_heads, slots=2),
        compiler_params=pltpu.CompilerParams(
            dimension_semantics=("arbitrary", "arbitrary"), vmem_limit_bytes=VMEM_LIMIT),
        name="attn_prompt",
    )(q, kt, v, bias, g_att)


def _attn_sample_kernel(q_ref, ck_ref, cv_ref, kn_ref, vn_ref, bias_ref, g_ref, o_ref, kt_s, v_s,
                        s_scr, p_scr):
    d_att = q_ref.shape[-1]
    pad = jnp.zeros((WIN - ck_ref.shape[1] - kn_ref.shape[1], d_att), F32)
    k_all = jnp.concatenate([ck_ref[0], kn_ref[0], pad], axis=0)
    even_row = (lax.broadcasted_iota(jnp.int32, (d_att, LANES), 0) & HEAD_DIM) == 0
    for c in range(WIN_SLABS):
        kt = k_all[c * LANES:(c + 1) * LANES, :].T
        kt_s[0, c] = jnp.where(even_row, kt, 0.0).astype(BF16)
        kt_s[1, c] = jnp.where(even_row, 0.0, kt).astype(BF16)
    v_all = jnp.concatenate([cv_ref[0], vn_ref[0], pad], axis=0)
    even = (lax.broadcasted_iota(jnp.int32, v_all.shape, 1) & HEAD_DIM) == 0
    v_s[0] = jnp.where(even, v_all, 0.0).astype(BF16)
    v_s[1] = jnp.where(even, 0.0, v_all).astype(BF16)

    def get_k(hp):
        return jnp.concatenate(
            [kt_s[e, c, hp * LANES:(hp + 1) * LANES, :] for e in range(2) for c in range(WIN_SLABS)], axis=1)

    def get_v(hp):
        lanes = slice(hp * LANES, (hp + 1) * LANES)
        return jnp.concatenate([v_s[0, :, lanes], v_s[1, :, lanes]], axis=0)

    def get_q(hp):
        return q_ref[0, :, hp * LANES:(hp + 1) * LANES]

    a = _attend(q_ref.shape[1], WIN_SLABS, get_q, get_k, get_v, bias_ref, 0, s_scr.at[0], p_scr.at[0])
    o_ref[0] = _rms(a, g_ref[...]).astype(BF16)


def _attn_sample(q, cache_k, cache_v, k_new, v_new, bias, g_att):
    nb, s_len, d_att = k_new.shape
    n_heads = d_att // HEAD_DIM
    l_cache = cache_k.shape[1]
    assert s_len == CHUNK and l_cache == LEFT_CHUNKS * CHUNK
    per_b = lambda b: (b, 0, 0)
    return pl.pallas_call(
        _attn_sample_kernel,
        grid=(nb,),
        in_specs=[pl.BlockSpec((1, s_len, d_att), per_b),
                  pl.BlockSpec((1, l_cache, d_att), per_b), pl.BlockSpec((1, l_cache, d_att), per_b),
                  pl.BlockSpec((1, s_len, d_att), per_b), pl.BlockSpec((1, s_len, d_att), per_b),
                  _const_spec(bias.shape), _const_spec((1, d_att))],
        out_specs=pl.BlockSpec((1, s_len, d_att), per_b),
        out_shape=jax.ShapeDtypeStruct((nb, s_len, d_att), BF16),
        scratch_shapes=[pltpu.VMEM((2, WIN_SLABS, d_att, LANES), BF16), pltpu.VMEM((2, WIN, d_att), BF16)]
                       + _attn_scratch(n_heads, slots=1),
        compiler_params=pltpu.CompilerParams(
            dimension_semantics=("arbitrary",), vmem_limit_bytes=VMEM_LIMIT),
        name="attn_sample",
    )(q, cache_k, cache_v, k_new, v_new, bias, g_att)


def _dense_stage(x, cn, an, p, wout_ref, gffn_ref, w1_ref, w2_ref, gple_ref, wg_ref, wple_ref, ff_chunk):
    d_conv = cn.shape[-1]
    x1 = (x + jnp.dot(cn, wout_ref[0:d_conv, :], preferred_element_type=F32)
          + jnp.dot(an, wout_ref[d_conv:, :], preferred_element_type=F32))
    hb = _rms(x1, gffn_ref[...]).astype(BF16)
    ffn = None
    for lo in range(0, w1_ref.shape[1], ff_chunk):
        a = jnp.maximum(jnp.dot(hb, w1_ref[:, lo:lo + ff_chunk], preferred_element_type=F32), 0.0)
        part = jnp.dot((a * a).astype(BF16), w2_ref[lo:lo + ff_chunk, :], preferred_element_type=F32)
        ffn = part if ffn is None else ffn + part
    x2 = x1 + ffn
    gate = jax.nn.sigmoid(jnp.dot(_rms(x2, gple_ref[...]).astype(BF16), wg_ref[...],
                                  preferred_element_type=F32))
    ple = jnp.dot(p.astype(BF16), wple_ref[...], preferred_element_type=F32)
    return x2 + ple * gate


def _mix_kernel(*refs, starts, ff_chunk):
    n_groups = len(starts) - 1
    ins, dense_w, outs = refs[:4 * n_groups], refs[4 * n_groups:-n_groups], refs[-n_groups:]
    i = pl.program_id(0)
    for g in range(n_groups):
        x_ref, p_ref, an_ref, cn_ref = ins[4 * g:4 * g + 4]

        @pl.when((i >= starts[g]) & (i < starts[g + 1]))
        def _():
            outs[g][0] = _dense_stage(x_ref[0], cn_ref[0], an_ref[0], p_ref[0], *dense_w, ff_chunk)


def _mix(groups, w_out, g_ffn, w1, w2, g_ple, w_gate, w_ple, *, tm):
    consts = [w_out, g_ffn, w1, w2, g_ple, w_gate, w_ple]
    starts, in_specs, out_specs, out_shape, operands = [0], [], [], [], []
    for x, p, an, cn in groups:
        ng, rows, d_model = x.shape
        assert rows % tm == 0
        n_t = rows // tm
        first, count = starts[-1], ng * n_t
        starts.append(first + count)

        def tile(i, first=first, count=count, n_t=n_t):
            j = jnp.clip(i - first, 0, count - 1)
            return (j // n_t, j % n_t, 0)

        in_specs += [pl.BlockSpec((1, tm, a.shape[-1]), tile) for a in (x, p, an, cn)]
        out_specs.append(pl.BlockSpec((1, tm, d_model), tile))
        out_shape.append(jax.ShapeDtypeStruct(x.shape, F32))
        operands += [x, p, an, cn]
    return pl.pallas_call(
        functools.partial(_mix_kernel, starts=tuple(starts), ff_chunk=1024),
        grid=(starts[-1],),
        in_specs=in_specs + [_const_spec(c.shape) for c in consts],
        out_specs=out_specs,
        out_shape=out_shape,
        compiler_params=pltpu.CompilerParams(
            dimension_semantics=("arbitrary",), vmem_limit_bytes=VMEM_LIMIT),
        name="mix",
    )(*operands, *consts)


def kernel(x_prompt, x_sample, p_prompt, p_sample, cache_att_k, cache_att_v, state_conv, g_mix, w_in, w_dw,
           b_dw, g_conv_ln, b_conv_ln, g_q, g_k, rel_bias, g_conv_out, g_att_out, w_out, g_ffn, w_ff1, w_ff2,
           g_ple, w_gate, w_ple):
    depth = w_in.shape[0]
    nb, t_len, d_model = x_prompt.shape
    sb, s_len, _ = x_sample.shape
    d_conv = w_dw.shape[-1]
    d_att = g_att_out.shape[-1]
    n_heads = d_att // HEAD_DIM
    keep = min(LEFT_CHUNKS * CHUNK, t_len)
    tile = 512

    xp = x_prompt
    xs = x_sample.reshape(1, sb * s_len, d_model)
    outs = [[] for _ in range(6)]
    for i in range(depth):
        vec = lambda a: a[i].reshape(1, -1)
        gq = jnp.tile(g_q[i], n_heads).reshape(1, d_att)
        gk = jnp.tile(g_k[i], n_heads).reshape(1, d_att)
        conv_w = (w_dw[i], vec(b_dw), vec(g_conv_ln), vec(b_conv_ln), vec(g_conv_out))

        (cn_p, u_tail, q_p, kt_p, vb_p, k_tail, v_tail, w_out_b, w_ff1_b, w_ff2_b, w_gate_b, bias,
         w_in_b, w_ple_b) = _in_proj(
            xp, vec(g_mix), w_in[i], gq, gk, conv_w, tm=tile, keep=keep,
            cast=(w_out[i], w_ff1[i], w_ff2[i], w_gate[i]), cast_once=(w_ple[i],), rel_bias=rel_bias[i])
        mix_w = (w_out_b, vec(g_ffn), w_ff1_b, w_ff2_b, vec(g_ple), w_gate_b, w_ple_b)
        an_p = _attn_prompt(q_p, kt_p, vb_p, bias, vec(g_att_out), tq=tile)
        outs[0].append(k_tail.reshape(nb, keep, n_heads, HEAD_DIM))
        outs[1].append(v_tail.reshape(nb, keep, n_heads, HEAD_DIM))
        outs[2].append(u_tail[:, HALO - (CONV_WIDTH - 1):])

        rows = sb * s_len
        cn_s, u_s, q_s, k_s, v_s = _in_proj(xs, vec(g_mix), w_in_b, gq, gk, conv_w, tm=rows, keep=rows,
                                            state=state_conv[i])
        per_b = lambda a: a.reshape(sb, s_len, a.shape[-1])
        ck = cache_att_k[i].reshape(sb, -1, d_att)
        cv = cache_att_v[i].reshape(sb, -1, d_att)
        an_s = _attn_sample(per_b(q_s), ck, cv, per_b(k_s), per_b(v_s), bias, vec(g_att_out))
        outs[3].append(k_s.reshape(sb, s_len, n_heads, HEAD_DIM))
        outs[4].append(v_s.reshape(sb, s_len, n_heads, HEAD_DIM))
        conv_in_tail = jnp.concatenate([state_conv[i], per_b(u_s)], axis=1)[:, -(CONV_WIDTH - 1):]
        outs[5].append(conv_in_tail)

        xp, xs = _mix([(xp, p_prompt[i], an_p, cn_p),
                       (xs, p_sample[i].reshape(1, rows, -1), an_s.reshape(1, rows, d_att), cn_s)],
                      *mix_w, tm=tile)

    return (xp, xs.reshape(sb, s_len, d_model)) + tuple(jnp.stack(o) for o in outs)
```

```python
import functools

import jax
import jax.numpy as jnp
import numpy as np
from jax import lax
from jax.experimental import pallas as pl
from jax.experimental.pallas import tpu as pltpu

F32 = jnp.float32
BF16 = jnp.bfloat16

CHUNK = 64
LEFT_CHUNKS = 8
HEAD_DIM = 64
CONV_WIDTH = 31
MAX_REL = 128
EPS = 1e-6
NEG = -1e30
LOG2_E = 1.4426950408889634

LANES = 128
SUBLANES = 8
MXU_TILE = 256
PAIR = 2 * CHUNK
WIN = (LEFT_CHUNKS + 2) * CHUNK
WIN_SLABS = WIN // LANES
HALO = 32
CONV_HOP = 128
SOFTMAX_ROWS = 32
VMEM_LIMIT = 56 * 1024 * 1024


def _rms(x, g):
    return x * lax.rsqrt(jnp.mean(x * x, axis=-1, keepdims=True) + EPS) * g


def _const_spec(shape):
    zeros = (0,) * len(shape)
    return pl.BlockSpec(shape, lambda *_: zeros, pipeline_mode=pl.Buffered(1))


def _bias_kernel(rb_ref, o_ref):
    assert rb_ref.shape[1] == 2 * MAX_REL + 1 and MAX_REL == LANES and CHUNK == LANES // 2
    assert LEFT_CHUNKS * CHUNK - MAX_REL == 3 * LANES and WIN_SLABS == 5
    n_heads = rb_ref.shape[0]
    rb = rb_ref[...]
    lane = lax.broadcasted_iota(jnp.int32, (n_heads, LANES), 1)
    b0 = jnp.broadcast_to(rb[:, 0:1], (n_heads, LANES))
    t4 = jnp.where(lane < CHUNK, rb[:, LANES:2 * LANES], b0)
    t = jnp.concatenate([b0, b0, b0, rb[:, 0:LANES], t4, b0], axis=1)
    r_i = lax.broadcasted_iota(jnp.int32, (PAIR, LANES), 0)
    c_i = lax.broadcasted_iota(jnp.int32, (PAIR, LANES), 1)
    for h in range(n_heads):
        row = jnp.broadcast_to(t[h:h + 1, :], (PAIR, t.shape[1]))
        toe = pltpu.roll(row, 0, 1, stride=1, stride_axis=0)
        for c in range(WIN_SLABS):
            jj = c_i + c * LANES
            valid = ((r_i < CHUNK) & (jj < WIN - CHUNK)) | ((r_i >= CHUNK) & (jj >= CHUNK))
            o_ref[h, c] = jnp.where(valid, toe[:, c * LANES:(c + 1) * LANES] * LOG2_E, NEG)


def _dft_tables(n, hop):
    half = n // 2
    t = np.arange(n)[None, :]
    f = np.arange(half)[:, None]
    ang = 2.0 * np.pi * f * t / n
    fwd = np.concatenate([np.cos(ang), np.cos(np.pi * t), np.sin(ang[1:])], axis=0)
    rows = np.arange(n - hop, n)[:, None]
    ang_o = 2.0 * np.pi * rows * f.T / n
    scale = np.where(f.T == 0, 1.0, 2.0) / n
    inv = np.concatenate([scale * np.cos(ang_o), np.cos(np.pi * rows) / n, (2.0 / n) * np.sin(ang_o[:, 1:])], axis=1)
    delay = (CONV_WIDTH - 1) - np.arange(CONV_WIDTH)[None, :]
    ang_h = 2.0 * np.pi * f * delay / n
    p, q = np.cos(ang_h), np.sin(ang_h)
    r = p.copy()
    r[0] = np.cos(np.pi * delay[0])
    return (jnp.asarray(fwd, F32), jnp.asarray(inv, F32), jnp.asarray(np.stack([p, q, r]), F32))


def _filter_response(wdw_ref, resp_ref, h_scr):
    for k in range(3):
        acc = None
        for j in range(CONV_WIDTH):
            term = resp_ref[k, :, j:j + 1] * wdw_ref[j:j + 1, :]
            acc = term if acc is None else acc + term
        h_scr[k] = acc


def _conv_stage(u, n_seg, halo_ref, ubuf, cn_ref, h_scr, fwd_ref, inv_ref,
                bdw_ref, gln_ref, bln_ref, gco_ref):
    d_conv = u.shape[-1]
    seg = u.shape[0] // n_seg
    n, hop = fwd_ref.shape[0], inv_ref.shape[0]
    half = n // 2
    assert n - hop == HALO and HALO >= CONV_WIDTH - 1 and seg % hop == 0
    for s in range(n_seg):
        if halo_ref is not None:
            hr = halo_ref.shape[1]
            if hr < HALO:
                ubuf[s, 0:HALO - hr, :] = jnp.zeros((HALO - hr, d_conv), F32)
            ubuf[s, HALO - hr:HALO, :] = halo_ref[s]
        ubuf[s, HALO:HALO + seg, :] = u[s * seg:(s + 1) * seg, :]

    blocks = [(s, i) for s in range(n_seg) for i in range(seg // hop)]
    windows = jnp.concatenate([ubuf[s, hop * i:hop * i + n, :].astype(BF16) for s, i in blocks], axis=1)
    spec = jnp.dot(fwd_ref[...].astype(BF16), windows, preferred_element_type=F32)
    p, q, r = h_scr[0], h_scr[1], h_scr[2]
    prods = []
    for k in range(len(blocks)):
        top = spec[0:half, k * d_conv:(k + 1) * d_conv]
        bot = spec[half:n, k * d_conv:(k + 1) * d_conv]
        prods.append(jnp.concatenate([p * top - q * bot, r * bot + q * top], axis=0).astype(BF16))
    y = jnp.dot(inv_ref[...].astype(BF16), jnp.concatenate(prods, axis=1),
                preferred_element_type=F32)
    for k, (s, i) in enumerate(blocks):
        yk = y[:, k * d_conv:(k + 1) * d_conv] + bdw_ref[...]
        mu = jnp.mean(yk, axis=-1, keepdims=True)
        yc = yk - mu
        yn = yc * lax.rsqrt(jnp.mean(yc * yc, axis=-1, keepdims=True) + EPS) * gln_ref[...] + bln_ref[...]
        c = yn * jax.nn.sigmoid(yn)
        cn_ref[s * seg + i * hop:s * seg + (i + 1) * hop, :] = _rms(c, gco_ref[...]).astype(BF16)


def _inproj_kernel(*refs, d_conv, d_att, tail0, prompt, n_cast, n_once):
    if prompt:
        x_ref, gmix_ref, win_ref, gq_ref, gk_ref = refs[:5]
        halo_ref, conv_w = None, refs[5:13]
        cast_in, rb_ref = refs[13:13 + n_cast], refs[13 + n_cast]
        once_in = (win_ref,) + refs[14 + n_cast:14 + n_cast + n_once]
        outs = refs[14 + n_cast + n_once:]
        cn_ref, ut_ref, q_ref, kt_ref, vb_ref, k32_ref, v32_ref = outs[:7]
        cast_out, bias_ref = outs[7:7 + n_cast], outs[7 + n_cast]
        once_out = outs[8 + n_cast:9 + n_cast + n_once]
        ubuf, h_scr = outs[9 + n_cast + n_once:]
        win_ref = once_out[0]
        t = pl.program_id(1)
        tm = x_ref.shape[1]

        for src, dst in zip(cast_in, cast_out):
            dst[...] = src[...].astype(BF16)

        @pl.when((pl.program_id(0) == 0) & (t == 0))
        def _():
            for src, dst in zip(once_in, once_out):
                for lo in range(0, src.shape[1], MXU_TILE):
                    dst[:, lo:lo + MXU_TILE] = src[:, lo:lo + MXU_TILE].astype(BF16)
            _bias_kernel(rb_ref, bias_ref)
            _filter_response(conv_w[0], conv_w[7], h_scr)

        @pl.when(t == 0)
        def _():
            ubuf[0, 0:HALO, :] = jnp.zeros((HALO, d_conv), F32)

        @pl.when(t > 0)
        def _():
            ubuf[0, 0:HALO, :] = ubuf[0, tm:tm + HALO, :]
    else:
        x_ref, gmix_ref, win_ref, gq_ref, gk_ref, halo_ref = refs[:6]
        conv_w = refs[6:14]
        cn_ref, u_ref, q_ref, k32_ref, v32_ref, ubuf, h_scr = refs[14:]
        tm = x_ref.shape[1]
        _filter_response(conv_w[0], conv_w[7], h_scr)
    h = _rms(x_ref[0], gmix_ref[...]).astype(BF16)

    def proj(lo, width):
        return jnp.dot(h, win_ref[:, lo:lo + width], preferred_element_type=F32)

    u = proj(0, d_conv) * jax.nn.sigmoid(proj(d_conv, d_conv))
    if prompt:
        @pl.when(t == pl.num_programs(1) - 1)
        def _():
            ut_ref[0] = u[tm - HALO:tm, :]
    else:
        u_ref[0] = u
    wdw_ref, bdw_ref, gln_ref, bln_ref, gco_ref, fwd_ref, inv_ref, _ = conv_w
    _conv_stage(u, ubuf.shape[0], halo_ref, ubuf, cn_ref.at[0], h_scr, fwd_ref, inv_ref,
                bdw_ref, gln_ref, bln_ref, gco_ref)

    head_bits = HEAD_DIM.bit_length() - 1
    r_i = lax.shift_right_logical(lax.broadcasted_iota(jnp.int32, (MXU_TILE, MXU_TILE), 0), head_bits)
    c_i = lax.shift_right_logical(lax.broadcasted_iota(jnp.int32, (MXU_TILE, MXU_TILE), 1), head_bits)
    avg = jnp.where(r_i == c_i, 1.0 / HEAD_DIM, 0.0).astype(BF16)

    def head_rms(z, g):
        sq = (z * z).astype(BF16)
        ms = jnp.concatenate([jnp.dot(sq[:, lo:lo + MXU_TILE], avg, preferred_element_type=F32)
                              for lo in range(0, d_att, MXU_TILE)], axis=1)
        return z * lax.rsqrt(ms + EPS) * g

    o = 2 * d_conv
    qn = head_rms(proj(o, d_att), gq_ref[...])
    q_ref[0] = (qn * (HEAD_DIM ** -0.5 * LOG2_E)).astype(BF16)
    kn = head_rms(proj(o + d_att, d_att), gk_ref[...])
    v = proj(o + 2 * d_att, d_att)
    if prompt:
        even_row = (lax.broadcasted_iota(jnp.int32, (d_att, LANES), 0) & HEAD_DIM) == 0
        for s in range(tm // LANES):
            kt = kn[s * LANES:(s + 1) * LANES, :].T
            kt_ref[0, 0, s] = jnp.where(even_row, kt, 0.0).astype(BF16)
            kt_ref[0, 1, s] = jnp.where(even_row, 0.0, kt).astype(BF16)
        even = (lax.broadcasted_iota(jnp.int32, (tm, d_att), 1) & HEAD_DIM) == 0
        vb_ref[0, 0] = jnp.where(even, v, 0.0).astype(BF16)
        vb_ref[0, 1] = jnp.where(even, 0.0, v).astype(BF16)

    @pl.when(pl.program_id(1) >= tail0)
    def _():
        k32_ref[0] = kn
        v32_ref[0] = v


def _in_proj(x, g_mix, w_in, g_q, g_k, conv_w, *, tm, keep, state=None, cast=(), cast_once=(), rel_bias=None):
    nb, t_len, d_model = x.shape
    d_att = g_q.shape[-1]
    d_in = w_in.shape[-1]
    d_conv = (d_in - 3 * d_att) // 2
    n_t = t_len // tm
    prompt = state is None
    assert t_len % tm == 0 and (t_len - keep) % tm == 0 and tm % LANES == 0 and tm >= HALO
    tail0 = (t_len - keep) // tm
    row = lambda b, t: (b, t, 0)
    tail = lambda b, t: (b, jnp.maximum(t - tail0, 0), 0)
    halves = pl.BlockSpec((1, 2, tm, d_att), lambda b, t: (b, 0, t, 0))
    out_shape = [jax.ShapeDtypeStruct((nb, t_len, d_conv), BF16)]
    out_specs = [pl.BlockSpec((1, tm, d_conv), row)]
    if prompt:
        n_seg = 1
        out_shape += [jax.ShapeDtypeStruct((nb, HALO, d_conv), F32)]
        out_specs += [pl.BlockSpec((1, HALO, d_conv), lambda b, t: (b, 0, 0))]
    else:
        n_seg = state.shape[0]
        assert nb == 1 and n_t == 1 and tm % n_seg == 0 and state.shape[1] == CONV_WIDTH - 1
        out_shape += [jax.ShapeDtypeStruct((nb, t_len, d_conv), F32)]
        out_specs += [pl.BlockSpec((1, tm, d_conv), row)]
    out_shape += [jax.ShapeDtypeStruct((nb, t_len, d_att), BF16)]
    out_specs += [pl.BlockSpec((1, tm, d_att), row)]
    if prompt:
        out_shape += [jax.ShapeDtypeStruct((nb, 2, t_len // LANES, d_att, LANES), BF16),
                      jax.ShapeDtypeStruct((nb, 2, t_len, d_att), BF16)]
        out_specs += [pl.BlockSpec((1, 2, tm // LANES, d_att, LANES), lambda b, t: (b, 0, t, 0, 0)), halves]
    out_shape += [jax.ShapeDtypeStruct((nb, keep, d_att), F32)] * 2
    out_specs += [pl.BlockSpec((1, tm, d_att), tail)] * 2
    cast_specs = []
    for w in cast:
        rb = w.shape[0] // (nb * n_t)
        assert prompt and w.ndim == 2 and rb * nb * n_t == w.shape[0] and rb % (2 * SUBLANES) == 0
        cast_specs.append(pl.BlockSpec((rb, w.shape[1]), lambda b, t: (b * n_t + t, 0)))
        out_shape.append(jax.ShapeDtypeStruct(w.shape, BF16))
    out_specs += cast_specs
    side_specs, side_in = list(cast_specs), list(cast)
    if prompt:
        bias_shape = (rel_bias.shape[0], WIN_SLABS, PAIR, LANES)
        side_specs.append(_const_spec(rel_bias.shape))
        side_in.append(rel_bias)
        out_shape.append(jax.ShapeDtypeStruct(bias_shape, F32))
        out_specs.append(pl.BlockSpec(bias_shape, lambda b, t: (0, 0, 0, 0)))
        for w in (w_in,) + tuple(cast_once):
            assert w.ndim == 2 and w.shape[1] % MXU_TILE == 0
            out_shape.append(jax.ShapeDtypeStruct(w.shape, BF16))
            out_specs.append(pl.BlockSpec(w.shape, lambda b, t: (0, 0)))
        side_specs += [_const_spec(w.shape) for w in cast_once]
        side_in += list(cast_once)
    else:
        assert not cast_once
    kern = functools.partial(_inproj_kernel, d_conv=d_conv, d_att=d_att, tail0=tail0, prompt=prompt,
                             n_cast=len(cast), n_once=len(cast_once))
    extra = [] if prompt else [state]
    hop = min(CONV_HOP, tm // n_seg)
    conv_w = tuple(conv_w) + _dft_tables(HALO + hop, hop)
    return pl.pallas_call(
        kern,
        grid=(nb, n_t),
        in_specs=[pl.BlockSpec((1, tm, d_model), row),
                  _const_spec((1, d_model)), _const_spec((d_model, d_in)),
                  _const_spec((1, d_att)), _const_spec((1, d_att))]
                 + [_const_spec(a.shape) for a in extra + list(conv_w)] + side_specs,
        out_specs=out_specs,
        out_shape=out_shape,
        scratch_shapes=[pltpu.VMEM((n_seg, HALO + tm // n_seg, d_conv), F32),
                        pltpu.VMEM((3, (HALO + hop) // 2, d_conv), F32)],
        compiler_params=pltpu.CompilerParams(
            dimension_semantics=("arbitrary", "arbitrary"), vmem_limit_bytes=VMEM_LIMIT),
        name="in_proj",
    )(x, g_mix, w_in, g_q, g_k, *extra, *conv_w, *side_in)


def _attn_scratch(n_heads, slots):
    return [pltpu.VMEM((slots, n_heads // 2, PAIR, 2 * WIN), F32),
            pltpu.VMEM((slots, n_heads // 2, PAIR, 2 * WIN), BF16)]


def _attend(rows, n_slabs, get_q, get_k, get_v, bias_ref, off, s_scr, p_scr):
    n_pairs = s_scr.shape[0]
    width = n_slabs * LANES
    for hp in range(n_pairs):
        s_scr[hp, 0:rows, 0:2 * width] = jnp.dot(get_q(hp), get_k(hp), preferred_element_type=F32)
    for hp in range(n_pairs):
        for sub in range(2):
            h = 2 * hp + sub
            for r0 in range(0, rows, SOFTMAX_ROWS):
                r = slice(r0, r0 + SOFTMAX_ROWS)
                s = s_scr[hp, r, sub * width:(sub + 1) * width]
                s = s + jnp.concatenate([bias_ref[h, off + c, r, :] for c in range(n_slabs)], axis=1)
                e = jnp.exp2(s - jnp.max(s, axis=-1, keepdims=True))
                p_scr[hp, r, sub * width:(sub + 1) * width] = e.astype(BF16)
    first = lax.broadcasted_iota(jnp.int32, (width, LANES), 1) < HEAD_DIM
    sums = jnp.concatenate([jnp.where(first, 1.0, 0.0), jnp.where(first, 0.0, 1.0)], axis=0).astype(BF16)
    outs = []
    for hp in range(n_pairs):
        o2 = jnp.dot(p_scr[hp, 0:rows, 0:2 * width], jnp.concatenate([get_v(hp), sums], axis=1),
                     preferred_element_type=F32)
        outs.append(o2[:, 0:LANES] * (1.0 / o2[:, LANES:2 * LANES]))
    return jnp.concatenate(outs, axis=1)


def _attn_prompt_kernel(q_ref, kt_ref, v_ref, bias_ref, g_ref, o_ref, s_scr, p_scr):
    pairs = q_ref.shape[1] // PAIR
    slots = s_scr.shape[0]
    t = pl.program_id(1)

    def run_pairs(first_step):
        for i in range(pairs):
            p = i if first_step else t * pairs + i
            n_slabs = min(i + 1, WIN_SLABS) if first_step else WIN_SLABS
            ws = max(p - (WIN_SLABS - 1), 0) if first_step else p - (WIN_SLABS - 1)
            off = ws + (WIN_SLABS - 1) - p
            k0 = ws * LANES if first_step else pl.multiple_of(ws * LANES, LANES)
            width = n_slabs * LANES

            def get_k(hp, ws=ws, n_slabs=n_slabs):
                return jnp.concatenate(
                    [kt_ref[0, e, ws + c, hp * LANES:(hp + 1) * LANES, :]
                     for e in range(2) for c in range(n_slabs)], axis=1)

            def get_v(hp, k0=k0, width=width):
                lanes = slice(hp * LANES, (hp + 1) * LANES)
                return jnp.concatenate([v_ref[0, 0, pl.ds(k0, width), lanes],
                                        v_ref[0, 1, pl.ds(k0, width), lanes]], axis=0)

            rows = slice(i * PAIR, (i + 1) * PAIR)

            def get_q(hp, rows=rows):
                return q_ref[0, rows, hp * LANES:(hp + 1) * LANES]

            k = i % slots
            a = _attend(PAIR, n_slabs, get_q, get_k, get_v, bias_ref, off, s_scr.at[k], p_scr.at[k])
            o_ref[0, rows, :] = _rms(a, g_ref[...]).astype(BF16)

    pl.when(t == 0)(functools.partial(run_pairs, True))
    pl.when(t > 0)(functools.partial(run_pairs, False))


def _attn_prompt(q, kt, v, bias, g_att, *, tq):
    nb, t_len, d_att = q.shape
    n_heads = d_att // HEAD_DIM
    assert t_len % tq == 0 and tq % PAIR == 0 and tq // PAIR >= WIN_SLABS - 1
    return pl.pallas_call(
        _attn_prompt_kernel,
        grid=(nb, t_len // tq),
        in_specs=[pl.BlockSpec((1, tq, d_att), lambda b, t: (b, t, 0)),
                  pl.BlockSpec((1, 2, t_len // LANES, d_att, LANES), lambda b, t: (b, 0, 0, 0, 0)),
                  pl.BlockSpec((1, 2, t_len, d_att), lambda b, t: (b, 0, 0, 0)),
                  _const_spec(bias.shape), _const_spec((1, d_att))],
        out_specs=pl.BlockSpec((1, tq, d_att), lambda b, t: (b, t, 0)),
        out_shape=jax.ShapeDtypeStruct((nb, t_len, d_att), BF16),
        scratch_shapes=_attn_scratch(n_heads, slots=2),
        compiler_params=pltpu.CompilerParams(
            dimension_semantics=("arbitrary", "arbitrary"), vmem_limit_bytes=VMEM_LIMIT),
        name="attn_prompt",
    )(q, kt, v, bias, g_att)


def _attn_sample_kernel(q_ref, ck_ref, cv_ref, kn_ref, vn_ref, bias_ref, g_ref, o_ref, kt_s, v_s,
                        s_scr, p_scr):
    d_att = q_ref.shape[-1]
    pad = jnp.zeros((WIN - ck_ref.shape[1] - kn_ref.shape[1], d_att), F32)
    k_all = jnp.concatenate([ck_ref[0].astype(F32), kn_ref[0], pad], axis=0)
    even_row = (lax.broadcasted_iota(jnp.int32, (d_att, LANES), 0) & HEAD_DIM) == 0
    for c in range(WIN_SLABS):
        kt = k_all[c * LANES:(c + 1) * LANES, :].T
        kt_s[0, c] = jnp.where(even_row, kt, 0.0).astype(BF16)
        kt_s[1, c] = jnp.where(even_row, 0.0, kt).astype(BF16)
    v_all = jnp.concatenate([cv_ref[0].astype(F32), vn_ref[0], pad], axis=0)
    even = (lax.broadcasted_iota(jnp.int32, v_all.shape, 1) & HEAD_DIM) == 0
    v_s[0] = jnp.where(even, v_all, 0.0).astype(BF16)
    v_s[1] = jnp.where(even, 0.0, v_all).astype(BF16)

    def get_k(hp):
        return jnp.concatenate(
            [kt_s[e, c, hp * LANES:(hp + 1) * LANES, :] for e in range(2) for c in range(WIN_SLABS)], axis=1)

    def get_v(hp):
        lanes = slice(hp * LANES, (hp + 1) * LANES)
        return jnp.concatenate([v_s[0, :, lanes], v_s[1, :, lanes]], axis=0)

    def get_q(hp):
        return q_ref[0, :, hp * LANES:(hp + 1) * LANES]

    a = _attend(q_ref.shape[1], WIN_SLABS, get_q, get_k, get_v, bias_ref, 0, s_scr.at[0], p_scr.at[0])
    o_ref[0] = _rms(a, g_ref[...]).astype(BF16)


def _attn_sample(q, cache_k, cache_v, k_new, v_new, bias, g_att):
    nb, s_len, d_att = k_new.shape
    n_heads = d_att // HEAD_DIM
    l_cache = cache_k.shape[1]
    assert s_len == CHUNK and l_cache == LEFT_CHUNKS * CHUNK
    per_b = lambda b: (b, 0, 0)
    return pl.pallas_call(
        _attn_sample_kernel,
        grid=(nb,),
        in_specs=[pl.BlockSpec((1, s_len, d_att), per_b),
                  pl.BlockSpec((1, l_cache, d_att), per_b), pl.BlockSpec((1, l_cache, d_att), per_b),
                  pl.BlockSpec((1, s_len, d_att), per_b), pl.BlockSpec((1, s_len, d_att), per_b),
                  _const_spec(bias.shape), _const_spec((1, d_att))],
        out_specs=pl.BlockSpec((1, s_len, d_att), per_b),
        out_shape=jax.ShapeDtypeStruct((nb, s_len, d_att), BF16),
        scratch_shapes=[pltpu.VMEM((2, WIN_SLABS, d_att, LANES), BF16), pltpu.VMEM((2, WIN, d_att), BF16)]
                       + _attn_scratch(n_heads, slots=1),
        compiler_params=pltpu.CompilerParams(
            dimension_semantics=("arbitrary",), vmem_limit_bytes=VMEM_LIMIT),
        name="attn_sample",
    )(q, cache_k, cache_v, k_new, v_new, bias, g_att)


def _dense_stage(x, cn, an, p, wout_ref, gffn_ref, w1_ref, w2_ref, gple_ref, wg_ref, wple_ref, ff_chunk):
    d_conv = cn.shape[-1]
    x1 = (x + jnp.dot(cn, wout_ref[0:d_conv, :], preferred_element_type=F32)
          + jnp.dot(an, wout_ref[d_conv:, :], preferred_element_type=F32))
    hb = _rms(x1, gffn_ref[...]).astype(BF16)
    ffn = None
    for lo in range(0, w1_ref.shape[1], ff_chunk):
        a = jnp.maximum(jnp.dot(hb, w1_ref[:, lo:lo + ff_chunk], preferred_element_type=F32), 0.0)
        part = jnp.dot((a * a).astype(BF16), w2_ref[lo:lo + ff_chunk, :], preferred_element_type=F32)
        ffn = part if ffn is None else ffn + part
    x2 = x1 + ffn
    gate = jax.nn.sigmoid(jnp.dot(_rms(x2, gple_ref[...]).astype(BF16), wg_ref[...],
                                  preferred_element_type=F32))
    ple = jnp.dot(p.astype(BF16), wple_ref[...], preferred_element_type=F32)
    return x2 + ple * gate


def _mix_kernel(*refs, starts, ff_chunk):
    n_groups = len(starts) - 1
    ins, dense_w, outs = refs[:4 * n_groups], refs[4 * n_groups:-n_groups], refs[-n_groups:]
    i = pl.program_id(0)
    for g in range(n_groups):
        x_ref, p_ref, an_ref, cn_ref = ins[4 * g:4 * g + 4]

        @pl.when((i >= starts[g]) & (i < starts[g + 1]))
        def _():
            outs[g][0] = _dense_stage(x_ref[0], cn_ref[0], an_ref[0], p_ref[0], *dense_w, ff_chunk)


def _mix(groups, w_out, g_ffn, w1, w2, g_ple, w_gate, w_ple, *, tm):
    consts = [w_out, g_ffn, w1, w2, g_ple, w_gate, w_ple]
    starts, in_specs, out_specs, out_shape, operands = [0], [], [], [], []
    for x, p, an, cn in groups:
        ng, rows, d_model = x.shape
        assert rows % tm == 0
        n_t = rows // tm
        first, count = starts[-1], ng * n_t
        starts.append(first + count)

        def tile(i, first=first, count=count, n_t=n_t):
            j = jnp.clip(i - first, 0, count - 1)
            return (j // n_t, j % n_t, 0)

        in_specs += [pl.BlockSpec((1, tm, a.shape[-1]), tile) for a in (x, p, an, cn)]
        out_specs.append(pl.BlockSpec((1, tm, d_model), tile))
        out_shape.append(jax.ShapeDtypeStruct(x.shape, F32))
        operands += [x, p, an, cn]
    return pl.pallas_call(
        functools.partial(_mix_kernel, starts=tuple(starts), ff_chunk=2048),
        grid=(starts[-1],),
        in_specs=in_specs + [_const_spec(c.shape) for c in consts],
        out_specs=out_specs,
        out_shape=out_shape,
        compiler_params=pltpu.CompilerParams(
            dimension_semantics=("arbitrary",), vmem_limit_bytes=VMEM_LIMIT),
        name="mix",
    )(*operands, *consts)


def kernel(x_prompt, x_sample, p_prompt, p_sample, cache_att_k, cache_att_v, state_conv, g_mix, w_in, w_dw,
           b_dw, g_conv_ln, b_conv_ln, g_q, g_k, rel_bias, g_conv_out, g_att_out, w_out, g_ffn, w_ff1, w_ff2,
           g_ple, w_gate, w_ple):
    depth = w_in.shape[0]
    nb, t_len, d_model = x_prompt.shape
    sb, s_len, _ = x_sample.shape
    d_conv = w_dw.shape[-1]
    d_att = g_att_out.shape[-1]
    n_heads = d_att // HEAD_DIM
    keep = min(LEFT_CHUNKS * CHUNK, t_len)
    tile = 512

    xp = x_prompt
    xs = x_sample.reshape(1, sb * s_len, d_model)
    outs = [[] for _ in range(6)]
    for i in range(depth):
        vec = lambda a: a[i].reshape(1, -1)
        gq = jnp.tile(g_q[i], n_heads).reshape(1, d_att)
        gk = jnp.tile(g_k[i], n_heads).reshape(1, d_att)
        conv_w = (w_dw[i], vec(b_dw), vec(g_conv_ln), vec(b_conv_ln), vec(g_conv_out))

        (cn_p, u_tail, q_p, kt_p, vb_p, k_tail, v_tail, w_out_b, w_ff1_b, w_ff2_b, w_gate_b, bias,
         w_in_b, w_ple_b) = _in_proj(
            xp, vec(g_mix), w_in[i], gq, gk, conv_w, tm=tile, keep=keep,
            cast=(w_out[i], w_ff1[i], w_ff2[i], w_gate[i]), cast_once=(w_ple[i],), rel_bias=rel_bias[i])
        mix_w = (w_out_b, vec(g_ffn), w_ff1_b, w_ff2_b, vec(g_ple), w_gate_b, w_ple_b)
        an_p = _attn_prompt(q_p, kt_p, vb_p, bias, vec(g_att_out), tq=tile)
        outs[0].append(k_tail.reshape(nb, keep, n_heads, HEAD_DIM))
        outs[1].append(v_tail.reshape(nb, keep, n_heads, HEAD_DIM))
        outs[2].append(u_tail[:, HALO - (CONV_WIDTH - 1):])

        rows = sb * s_len
        cn_s, u_s, q_s, k_s, v_s = _in_proj(xs, vec(g_mix), w_in_b, gq, gk, conv_w, tm=rows, keep=rows,
                                            state=state_conv[i])
        per_b = lambda a: a.reshape(sb, s_len, a.shape[-1])
        ck = cache_att_k[i].astype(BF16).reshape(sb, -1, d_att)
        cv = cache_att_v[i].astype(BF16).reshape(sb, -1, d_att)
        an_s = _attn_sample(per_b(q_s), ck, cv, per_b(k_s), per_b(v_s), bias, vec(g_att_out))
        outs[3].append(k_s.reshape(sb, s_len, n_heads, HEAD_DIM))
        outs[4].append(v_s.reshape(sb, s_len, n_heads, HEAD_DIM))
        conv_in_tail = jnp.concatenate([state_conv[i], per_b(u_s)], axis=1)[:, -(CONV_WIDTH - 1):]
        outs[5].append(conv_in_tail)

        xp, xs = _mix([(xp, p_prompt[i], an_p, cn_p),
                       (xs, p_sample[i].reshape(1, rows, -1), an_s.reshape(1, rows, d_att), cn_s)],
                      *mix_w, tm=tile)

    return (xp, xs.reshape(sb, s_len, d_model)) + tuple(jnp.stack(o) for o in outs)
```

```python
import functools

import jax
import jax.numpy as jnp
import numpy as np
from jax import lax
from jax.experimental import pallas as pl
from jax.experimental.pallas import tpu as pltpu

F32 = jnp.float32
BF16 = jnp.bfloat16

CHUNK = 64
LEFT_CHUNKS = 8
HEAD_DIM = 64
CONV_WIDTH = 31
MAX_REL = 128
EPS = 1e-6
NEG = -1e30
LOG2_E = 1.4426950408889634

LANES = 128
SUBLANES = 8
MXU_TILE = 256
PAIR = 2 * CHUNK
WIN = (LEFT_CHUNKS + 2) * CHUNK
WIN_SLABS = WIN // LANES
HALO = 32
CONV_HOP = 128
SOFTMAX_ROWS = 32
VMEM_LIMIT = 56 * 1024 * 1024


def _rms(x, g):
    return x * lax.rsqrt(jnp.mean(x * x, axis=-1, keepdims=True) + EPS) * g


def _const_spec(shape):
    zeros = (0,) * len(shape)
    return pl.BlockSpec(shape, lambda *_: zeros, pipeline_mode=pl.Buffered(1))


def _bias_kernel(rb_ref, o_ref):
    assert rb_ref.shape[1] == 2 * MAX_REL + 1 and MAX_REL == LANES and CHUNK == LANES // 2
    assert LEFT_CHUNKS * CHUNK - MAX_REL == 3 * LANES and WIN_SLABS == 5
    n_heads = rb_ref.shape[0]
    rb = rb_ref[...]
    lane = lax.broadcasted_iota(jnp.int32, (n_heads, LANES), 1)
    b0 = jnp.broadcast_to(rb[:, 0:1], (n_heads, LANES))
    t4 = jnp.where(lane < CHUNK, rb[:, LANES:2 * LANES], b0)
    t = jnp.concatenate([b0, b0, b0, rb[:, 0:LANES], t4, b0], axis=1)
    r_i = lax.broadcasted_iota(jnp.int32, (PAIR, LANES), 0)
    c_i = lax.broadcasted_iota(jnp.int32, (PAIR, LANES), 1)
    for h in range(n_heads):
        row = jnp.broadcast_to(t[h:h + 1, :], (PAIR, t.shape[1]))
        toe = pltpu.roll(row, 0, 1, stride=1, stride_axis=0)
        for c in range(WIN_SLABS):
            jj = c_i + c * LANES
            valid = ((r_i < CHUNK) & (jj < WIN - CHUNK)) | ((r_i >= CHUNK) & (jj >= CHUNK))
            o_ref[h, c] = jnp.where(valid, toe[:, c * LANES:(c + 1) * LANES] * LOG2_E, NEG)


def _dft_tables(n, hop):
    half = n // 2
    t = np.arange(n)[None, :]
    f = np.arange(half)[:, None]
    ang = 2.0 * np.pi * f * t / n
    fwd = np.concatenate([np.cos(ang), np.cos(np.pi * t), np.sin(ang[1:])], axis=0)
    rows = np.arange(n - hop, n)[:, None]
    ang_o = 2.0 * np.pi * rows * f.T / n
    scale = np.where(f.T == 0, 1.0, 2.0) / n
    inv = np.concatenate([scale * np.cos(ang_o), np.cos(np.pi * rows) / n, (2.0 / n) * np.sin(ang_o[:, 1:])], axis=1)
    delay = (CONV_WIDTH - 1) - np.arange(CONV_WIDTH)[None, :]
    ang_h = 2.0 * np.pi * f * delay / n
    p, q = np.cos(ang_h), np.sin(ang_h)
    r = p.copy()
    r[0] = np.cos(np.pi * delay[0])
    return (jnp.asarray(fwd, F32), jnp.asarray(inv, F32), jnp.asarray(np.stack([p, q, r]), F32))


def _filter_response(wdw_ref, resp_ref, h_scr):
    for k in range(3):
        acc = None
        for j in range(CONV_WIDTH):
            term = resp_ref[k, :, j:j + 1] * wdw_ref[j:j + 1, :]
            acc = term if acc is None else acc + term
        h_scr[k] = acc


def _conv_stage(u, n_seg, halo_ref, ubuf, cn_ref, h_scr, fwd_ref, inv_ref,
                bdw_ref, gln_ref, bln_ref, gco_ref):
    d_conv = u.shape[-1]
    seg = u.shape[0] // n_seg
    n, hop = fwd_ref.shape[0], inv_ref.shape[0]
    half = n // 2
    assert n - hop == HALO and HALO >= CONV_WIDTH - 1 and seg % hop == 0
    for s in range(n_seg):
        if halo_ref is not None:
            hr = halo_ref.shape[1]
            if hr < HALO:
                ubuf[s, 0:HALO - hr, :] = jnp.zeros((HALO - hr, d_conv), F32)
            ubuf[s, HALO - hr:HALO, :] = halo_ref[s]
        ubuf[s, HALO:HALO + seg, :] = u[s * seg:(s + 1) * seg, :]

    blocks = [(s, i) for s in range(n_seg) for i in range(seg // hop)]
    windows = jnp.concatenate([ubuf[s, hop * i:hop * i + n, :].astype(BF16) for s, i in blocks], axis=1)
    spec = jnp.dot(fwd_ref[...].astype(BF16), windows, preferred_element_type=F32)
    p, q, r = h_scr[0], h_scr[1], h_scr[2]
    prods = []
    for k in range(len(blocks)):
        top = spec[0:half, k * d_conv:(k + 1) * d_conv]
        bot = spec[half:n, k * d_conv:(k + 1) * d_conv]
        prods.append(jnp.concatenate([p * top - q * bot, r * bot + q * top], axis=0).astype(BF16))
    y = jnp.dot(inv_ref[...].astype(BF16), jnp.concatenate(prods, axis=1),
                preferred_element_type=F32)
    for k, (s, i) in enumerate(blocks):
        yk = y[:, k * d_conv:(k + 1) * d_conv] + bdw_ref[...]
        mu = jnp.mean(yk, axis=-1, keepdims=True)
        yc = yk - mu
        yn = yc * lax.rsqrt(jnp.mean(yc * yc, axis=-1, keepdims=True) + EPS) * gln_ref[...] + bln_ref[...]
        c = yn * jax.nn.sigmoid(yn)
        cn_ref[s * seg + i * hop:s * seg + (i + 1) * hop, :] = _rms(c, gco_ref[...]).astype(BF16)


def _inproj_kernel(*refs, d_conv, d_att, tail0, prompt, n_cast, n_once):
    if prompt:
        x_ref, gmix_ref, win_ref, gq_ref, gk_ref = refs[:5]
        halo_ref, conv_w = None, refs[5:13]
        cast_in, rb_ref = refs[13:13 + n_cast], refs[13 + n_cast]
        once_in = (win_ref,) + refs[14 + n_cast:14 + n_cast + n_once]
        outs = refs[14 + n_cast + n_once:]
        cn_ref, ut_ref, q_ref, kt_ref, vb_ref, k32_ref, v32_ref = outs[:7]
        cast_out, bias_ref = outs[7:7 + n_cast], outs[7 + n_cast]
        once_out = outs[8 + n_cast:9 + n_cast + n_once]
        ubuf, h_scr = outs[9 + n_cast + n_once:]
        win_ref = once_out[0]
        t = pl.program_id(1)
        tm = x_ref.shape[1]

        for src, dst in zip(cast_in, cast_out):
            dst[...] = src[...].astype(BF16)

        @pl.when((pl.program_id(0) == 0) & (t == 0))
        def _():
            for src, dst in zip(once_in, once_out):
                for lo in range(0, src.shape[1], MXU_TILE):
                    dst[:, lo:lo + MXU_TILE] = src[:, lo:lo + MXU_TILE].astype(BF16)
            _bias_kernel(rb_ref, bias_ref)
            _filter_response(conv_w[0], conv_w[7], h_scr)

        @pl.when(t == 0)
        def _():
            ubuf[0, 0:HALO, :] = jnp.zeros((HALO, d_conv), F32)

        @pl.when(t > 0)
        def _():
            ubuf[0, 0:HALO, :] = ubuf[0, tm:tm + HALO, :]
    else:
        x_ref, gmix_ref, win_ref, gq_ref, gk_ref, halo_ref = refs[:6]
        conv_w = refs[6:14]
        cn_ref, u_ref, q_ref, k32_ref, v32_ref, ubuf, h_scr = refs[14:]
        tm = x_ref.shape[1]
        _filter_response(conv_w[0], conv_w[7], h_scr)
    h = _rms(x_ref[0], gmix_ref[...]).astype(BF16)

    def proj(lo, width):
        return jnp.dot(h, win_ref[:, lo:lo + width], preferred_element_type=F32)

    u = proj(0, d_conv) * jax.nn.sigmoid(proj(d_conv, d_conv))
    if prompt:
        @pl.when(t == pl.num_programs(1) - 1)
        def _():
            ut_ref[0] = u[tm - HALO:tm, :]
    else:
        u_ref[0] = u
    wdw_ref, bdw_ref, gln_ref, bln_ref, gco_ref, fwd_ref, inv_ref, _ = conv_w
    _conv_stage(u, ubuf.shape[0], halo_ref, ubuf, cn_ref.at[0], h_scr, fwd_ref, inv_ref,
                bdw_ref, gln_ref, bln_ref, gco_ref)

    head_bits = HEAD_DIM.bit_length() - 1
    r_i = lax.shift_right_logical(lax.broadcasted_iota(jnp.int32, (MXU_TILE, MXU_TILE), 0), head_bits)
    c_i = lax.shift_right_logical(lax.broadcasted_iota(jnp.int32, (MXU_TILE, MXU_TILE), 1), head_bits)
    avg = jnp.where(r_i == c_i, 1.0 / HEAD_DIM, 0.0).astype(BF16)

    def head_rms(z, g):
        sq = (z * z).astype(BF16)
        ms = jnp.concatenate([jnp.dot(sq[:, lo:lo + MXU_TILE], avg, preferred_element_type=F32)
                              for lo in range(0, d_att, MXU_TILE)], axis=1)
        return z * lax.rsqrt(ms + EPS) * g

    o = 2 * d_conv
    qn = head_rms(proj(o, d_att), gq_ref[...])
    q_ref[0] = (qn * (HEAD_DIM ** -0.5 * LOG2_E)).astype(BF16)
    kn = head_rms(proj(o + d_att, d_att), gk_ref[...])
    v = proj(o + 2 * d_att, d_att)
    if prompt:
        even_row = (lax.broadcasted_iota(jnp.int32, (d_att, LANES), 0) & HEAD_DIM) == 0
        for s in range(tm // LANES):
            kt = kn[s * LANES:(s + 1) * LANES, :].T
            kt_ref[0, 0, s] = jnp.where(even_row, kt, 0.0).astype(BF16)
            kt_ref[0, 1, s] = jnp.where(even_row, 0.0, kt).astype(BF16)
        even = (lax.broadcasted_iota(jnp.int32, (tm, d_att), 1) & HEAD_DIM) == 0
        vb_ref[0, 0] = jnp.where(even, v, 0.0).astype(BF16)
        vb_ref[0, 1] = jnp.where(even, 0.0, v).astype(BF16)

    @pl.when(pl.program_id(1) >= tail0)
    def _():
        k32_ref[0] = kn
        v32_ref[0] = v


def _in_proj(x, g_mix, w_in, g_q, g_k, conv_w, *, tm, keep, state=None, cast=(), cast_once=(), rel_bias=None):
    nb, t_len, d_model = x.shape
    d_att = g_q.shape[-1]
    d_in = w_in.shape[-1]
    d_conv = (d_in - 3 * d_att) // 2
    n_t = t_len // tm
    prompt = state is None
    assert t_len % tm == 0 and (t_len - keep) % tm == 0 and tm % LANES == 0 and tm >= HALO
    tail0 = (t_len - keep) // tm
    row = lambda b, t: (b, t, 0)
    tail = lambda b, t: (b, jnp.maximum(t - tail0, 0), 0)
    halves = pl.BlockSpec((1, 2, tm, d_att), lambda b, t: (b, 0, t, 0))
    out_shape = [jax.ShapeDtypeStruct((nb, t_len, d_conv), BF16)]
    out_specs = [pl.BlockSpec((1, tm, d_conv), row)]
    if prompt:
        n_seg = 1
        out_shape += [jax.ShapeDtypeStruct((nb, HALO, d_conv), F32)]
        out_specs += [pl.BlockSpec((1, HALO, d_conv), lambda b, t: (b, 0, 0))]
    else:
        n_seg = state.shape[0]
        assert nb == 1 and n_t == 1 and tm % n_seg == 0 and state.shape[1] == CONV_WIDTH - 1
        out_shape += [jax.ShapeDtypeStruct((nb, t_len, d_conv), F32)]
        out_specs += [pl.BlockSpec((1, tm, d_conv), row)]
    out_shape += [jax.ShapeDtypeStruct((nb, t_len, d_att), BF16)]
    out_specs += [pl.BlockSpec((1, tm, d_att), row)]
    if prompt:
        out_shape += [jax.ShapeDtypeStruct((nb, 2, t_len // LANES, d_att, LANES), BF16),
                      jax.ShapeDtypeStruct((nb, 2, t_len, d_att), BF16)]
        out_specs += [pl.BlockSpec((1, 2, tm // LANES, d_att, LANES), lambda b, t: (b, 0, t, 0, 0)), halves]
    out_shape += [jax.ShapeDtypeStruct((nb, keep, d_att), F32)] * 2
    out_specs += [pl.BlockSpec((1, tm, d_att), tail)] * 2
    cast_specs = []
    for w in cast:
        rb = w.shape[0] // (nb * n_t)
        assert prompt and w.ndim == 2 and rb * nb * n_t == w.shape[0] and rb % (2 * SUBLANES) == 0
        cast_specs.append(pl.BlockSpec((rb, w.shape[1]), lambda b, t: (b * n_t + t, 0)))
        out_shape.append(jax.ShapeDtypeStruct(w.shape, BF16))
    out_specs += cast_specs
    side_specs, side_in = list(cast_specs), list(cast)
    if prompt:
        bias_shape = (rel_bias.shape[0], WIN_SLABS, PAIR, LANES)
        side_specs.append(_const_spec(rel_bias.shape))
        side_in.append(rel_bias)
        out_shape.append(jax.ShapeDtypeStruct(bias_shape, F32))
        out_specs.append(pl.BlockSpec(bias_shape, lambda b, t: (0, 0, 0, 0)))
        for w in (w_in,) + tuple(cast_once):
            assert w.ndim == 2 and w.shape[1] % MXU_TILE == 0
            out_shape.append(jax.ShapeDtypeStruct(w.shape, BF16))
            out_specs.append(pl.BlockSpec(w.shape, lambda b, t: (0, 0)))
        side_specs += [_const_spec(w.shape) for w in cast_once]
        side_in += list(cast_once)
    else:
        assert not cast_once
    kern = functools.partial(_inproj_kernel, d_conv=d_conv, d_att=d_att, tail0=tail0, prompt=prompt,
                             n_cast=len(cast), n_once=len(cast_once))
    extra = [] if prompt else [state]
    hop = min(CONV_HOP, tm // n_seg)
    conv_w = tuple(conv_w) + _dft_tables(HALO + hop, hop)
    return pl.pallas_call(
        kern,
        grid=(nb, n_t),
        in_specs=[pl.BlockSpec((1, tm, d_model), row),
                  _const_spec((1, d_model)), _const_spec((d_model, d_in)),
                  _const_spec((1, d_att)), _const_spec((1, d_att))]
                 + [_const_spec(a.shape) for a in extra + list(conv_w)] + side_specs,
        out_specs=out_specs,
        out_shape=out_shape,
        scratch_shapes=[pltpu.VMEM((n_seg, HALO + tm // n_seg, d_conv), F32),
                        pltpu.VMEM((3, (HALO + hop) // 2, d_conv), F32)],
        compiler_params=pltpu.CompilerParams(
            dimension_semantics=("arbitrary", "arbitrary"), vmem_limit_bytes=VMEM_LIMIT),
        name="in_proj",
    )(x, g_mix, w_in, g_q, g_k, *extra, *conv_w, *side_in)


def _attn_scratch(n_heads, slots):
    return [pltpu.VMEM((slots, n_heads // 2, PAIR, 2 * WIN), F32),
            pltpu.VMEM((slots, n_heads // 2, PAIR, 2 * WIN), BF16)]


def _attend(rows, n_slabs, get_q, get_k, get_v, bias_ref, off, s_scr, p_scr):
    n_pairs = s_scr.shape[0]
    width = n_slabs * LANES
    for hp in range(n_pairs):
        s_scr[hp, 0:rows, 0:2 * width] = jnp.dot(get_q(hp), get_k(hp), preferred_element_type=F32)
    for hp in range(n_pairs):
        for sub in range(2):
            h = 2 * hp + sub
            for r0 in range(0, rows, SOFTMAX_ROWS):
                r = slice(r0, r0 + SOFTMAX_ROWS)
                s = s_scr[hp, r, sub * width:(sub + 1) * width]
                s = s + jnp.concatenate([bias_ref[h, off + c, r, :] for c in range(n_slabs)], axis=1)
                e = jnp.exp2(s - jnp.max(s, axis=-1, keepdims=True))
                p_scr[hp, r, sub * width:(sub + 1) * width] = e.astype(BF16)
    first = lax.broadcasted_iota(jnp.int32, (width, LANES), 1) < HEAD_DIM
    sums = jnp.concatenate([jnp.where(first, 1.0, 0.0), jnp.where(first, 0.0, 1.0)], axis=0).astype(BF16)
    outs = []
    for hp in range(n_pairs):
        o2 = jnp.dot(p_scr[hp, 0:rows, 0:2 * width], jnp.concatenate([get_v(hp), sums], axis=1),
                     preferred_element_type=F32)
        outs.append(o2[:, 0:LANES] * (1.0 / o2[:, LANES:2 * LANES]))
    return jnp.concatenate(outs, axis=1)


def _attn_prompt_kernel(q_ref, kt_ref, v_ref, bias_ref, g_ref, o_ref, s_scr, p_scr):
    pairs = q_ref.shape[1] // PAIR
    slots = s_scr.shape[0]
    t = pl.program_id(1)

    def run_pairs(first_step):
        for i in range(pairs):
            p = i if first_step else t * pairs + i
            n_slabs = min(i + 1, WIN_SLABS) if first_step else WIN_SLABS
            ws = max(p - (WIN_SLABS - 1), 0) if first_step else p - (WIN_SLABS - 1)
            off = ws + (WIN_SLABS - 1) - p
            k0 = ws * LANES if first_step else pl.multiple_of(ws * LANES, LANES)
            width = n_slabs * LANES

            def get_k(hp, ws=ws, n_slabs=n_slabs):
                return jnp.concatenate(
                    [kt_ref[0, e, ws + c, hp * LANES:(hp + 1) * LANES, :]
                     for e in range(2) for c in range(n_slabs)], axis=1)

            def get_v(hp, k0=k0, width=width):
                lanes = slice(hp * LANES, (hp + 1) * LANES)
                return jnp.concatenate([v_ref[0, 0, pl.ds(k0, width), lanes],
                                        v_ref[0, 1, pl.ds(k0, width), lanes]], axis=0)

            rows = slice(i * PAIR, (i + 1) * PAIR)

            def get_q(hp, rows=rows):
                return q_ref[0, rows, hp * LANES:(hp + 1) * LANES]

            k = i % slots
            a = _attend(PAIR, n_slabs, get_q, get_k, get_v, bias_ref, off, s_scr.at[k], p_scr.at[k])
            o_ref[0, rows, :] = _rms(a, g_ref[...]).astype(BF16)

    pl.when(t == 0)(functools.partial(run_pairs, True))
    pl.when(t > 0)(functools.partial(run_pairs, False))


def _attn_prompt(q, kt, v, bias, g_att, *, tq):
    nb, t_len, d_att = q.shape
    n_heads = d_att // HEAD_DIM
    assert t_len % tq == 0 and tq % PAIR == 0 and tq // PAIR >= WIN_SLABS - 1
    return pl.pallas_call(
        _attn_prompt_kernel,
        grid=(nb, t_len // tq),
        in_specs=[pl.BlockSpec((1, tq, d_att), lambda b, t: (b, t, 0)),
                  pl.BlockSpec((1, 2, t_len // LANES, d_att, LANES), lambda b, t: (b, 0, 0, 0, 0)),
                  pl.BlockSpec((1, 2, t_len, d_att), lambda b, t: (b, 0, 0, 0)),
                  _const_spec(bias.shape), _const_spec((1, d_att))],
        out_specs=pl.BlockSpec((1, tq, d_att), lambda b, t: (b, t, 0)),
        out_shape=jax.ShapeDtypeStruct((nb, t_len, d_att), BF16),
        scratch_shapes=_attn_scratch(n_heads, slots=2),
        compiler_params=pltpu.CompilerParams(
            dimension_semantics=("arbitrary", "arbitrary"), vmem_limit_bytes=VMEM_LIMIT),
        name="attn_prompt",
    )(q, kt, v, bias, g_att)


def _attn_sample_kernel(q_ref, ck_ref, cv_ref, kn_ref, vn_ref, bias_ref, g_ref, o_ref, kt_s, v_s,
                        s_scr, p_scr):
    d_att = q_ref.shape[-1]
    pad = jnp.zeros((WIN - ck_ref.shape[1] - kn_ref.shape[1], d_att), F32)
    k_all = jnp.concatenate([ck_ref[0].astype(F32), kn_ref[0], pad], axis=0)
    even_row = (lax.broadcasted_iota(jnp.int32, (d_att, LANES), 0) & HEAD_DIM) == 0
    for c in range(WIN_SLABS):
        kt = k_all[c * LANES:(c + 1) * LANES, :].T
        kt_s[0, c] = jnp.where(even_row, kt, 0.0).astype(BF16)
        kt_s[1, c] = jnp.where(even_row, 0.0, kt).astype(BF16)
    v_all = jnp.concatenate([cv_ref[0].astype(F32), vn_ref[0], pad], axis=0)
    even = (lax.broadcasted_iota(jnp.int32, v_all.shape, 1) & HEAD_DIM) == 0
    v_s[0] = jnp.where(even, v_all, 0.0).astype(BF16)
    v_s[1] = jnp.where(even, 0.0, v_all).astype(BF16)

    def get_k(hp):
        return jnp.concatenate(
            [kt_s[e, c, hp * LANES:(hp + 1) * LANES, :] for e in range(2) for c in range(WIN_SLABS)], axis=1)

    def get_v(hp):
        lanes = slice(hp * LANES, (hp + 1) * LANES)
        return jnp.concatenate([v_s[0, :, lanes], v_s[1, :, lanes]], axis=0)

    def get_q(hp):
        return q_ref[0, :, hp * LANES:(hp + 1) * LANES]

    a = _attend(q_ref.shape[1], WIN_SLABS, get_q, get_k, get_v, bias_ref, 0, s_scr.at[0], p_scr.at[0])
    o_ref[0] = _rms(a, g_ref[...]).astype(BF16)


def _attn_sample(q, cache_k, cache_v, k_new, v_new, bias, g_att):
    nb, s_len, d_att = k_new.shape
    n_heads = d_att // HEAD_DIM
    l_cache = cache_k.shape[1]
    assert s_len == CHUNK and l_cache == LEFT_CHUNKS * CHUNK
    per_b = lambda b: (b, 0, 0)
    return pl.pallas_call(
        _attn_sample_kernel,
        grid=(nb,),
        in_specs=[pl.BlockSpec((1, s_len, d_att), per_b),
                  pl.BlockSpec((1, l_cache, d_att), per_b), pl.BlockSpec((1, l_cache, d_att), per_b),
                  pl.BlockSpec((1, s_len, d_att), per_b), pl.BlockSpec((1, s_len, d_att), per_b),
                  _const_spec(bias.shape), _const_spec((1, d_att))],
        out_specs=pl.BlockSpec((1, s_len, d_att), per_b),
        out_shape=jax.ShapeDtypeStruct((nb, s_len, d_att), BF16),
        scratch_shapes=[pltpu.VMEM((2, WIN_SLABS, d_att, LANES), BF16), pltpu.VMEM((2, WIN, d_att), BF16)]
                       + _attn_scratch(n_heads, slots=1),
        compiler_params=pltpu.CompilerParams(
            dimension_semantics=("arbitrary",), vmem_limit_bytes=VMEM_LIMIT),
        name="attn_sample",
    )(q, cache_k, cache_v, k_new, v_new, bias, g_att)


def _dense_stage(x, cn, an, p, wout_ref, gffn_ref, w1_ref, w2_ref, gple_ref, wg_ref, wple_ref, ff_chunk):
    d_conv = cn.shape[-1]
    x1 = (x + jnp.dot(cn, wout_ref[0:d_conv, :], preferred_element_type=F32)
          + jnp.dot(an, wout_ref[d_conv:, :], preferred_element_type=F32))
    hb = _rms(x1, gffn_ref[...]).astype(BF16)
    ffn = None
    for lo in range(0, w1_ref.shape[1], ff_chunk):
        a = jnp.maximum(jnp.dot(hb, w1_ref[:, lo:lo + ff_chunk], preferred_element_type=F32), 0.0)
        part = jnp.dot((a * a).astype(BF16), w2_ref[lo:lo + ff_chunk, :], preferred_element_type=F32)
        ffn = part if ffn is None else ffn + part
    x2 = x1 + ffn
    gate = jax.nn.sigmoid(jnp.dot(_rms(x2, gple_ref[...]).astype(BF16), wg_ref[...],
                                  preferred_element_type=F32))
    ple = jnp.dot(p.astype(BF16), wple_ref[...], preferred_element_type=F32)
    return x2 + ple * gate


def _mix_kernel(*refs, starts, ff_chunk):
    n_groups = len(starts) - 1
    ins, dense_w, outs = refs[:4 * n_groups], refs[4 * n_groups:-n_groups], refs[-n_groups:]
    i = pl.program_id(0)
    for g in range(n_groups):
        x_ref, p_ref, an_ref, cn_ref = ins[4 * g:4 * g + 4]

        @pl.when((i >= starts[g]) & (i < starts[g + 1]))
        def _():
            outs[g][0] = _dense_stage(x_ref[0], cn_ref[0], an_ref[0], p_ref[0], *dense_w, ff_chunk)


def _mix(groups, w_out, g_ffn, w1, w2, g_ple, w_gate, w_ple, *, tm):
    consts = [w_out, g_ffn, w1, w2, g_ple, w_gate, w_ple]
    starts, in_specs, out_specs, out_shape, operands = [0], [], [], [], []
    for x, p, an, cn in groups:
        ng, rows, d_model = x.shape
        assert rows % tm == 0
        n_t = rows // tm
        first, count = starts[-1], ng * n_t
        starts.append(first + count)

        def tile(i, first=first, count=count, n_t=n_t):
            j = jnp.clip(i - first, 0, count - 1)
            return (j // n_t, j % n_t, 0)

        in_specs += [pl.BlockSpec((1, tm, a.shape[-1]), tile) for a in (x, p, an, cn)]
        out_specs.append(pl.BlockSpec((1, tm, d_model), tile))
        out_shape.append(jax.ShapeDtypeStruct(x.shape, F32))
        operands += [x, p, an, cn]
    return pl.pallas_call(
        functools.partial(_mix_kernel, starts=tuple(starts), ff_chunk=2048),
        grid=(starts[-1],),
        in_specs=in_specs + [_const_spec(c.shape) for c in consts],
        out_specs=out_specs,
        out_shape=out_shape,
        compiler_params=pltpu.CompilerParams(
            dimension_semantics=("arbitrary",), vmem_limit_bytes=VMEM_LIMIT),
        name="mix",
    )(*operands, *consts)


def kernel(x_prompt, x_sample, p_prompt, p_sample, cache_att_k, cache_att_v, state_conv, g_mix, w_in, w_dw,
           b_dw, g_conv_ln, b_conv_ln, g_q, g_k, rel_bias, g_conv_out, g_att_out, w_out, g_ffn, w_ff1, w_ff2,
           g_ple, w_gate, w_ple):
    depth = w_in.shape[0]
    nb, t_len, d_model = x_prompt.shape
    sb, s_len, _ = x_sample.shape
    d_conv = w_dw.shape[-1]
    d_att = g_att_out.shape[-1]
    n_heads = d_att // HEAD_DIM
    keep = min(LEFT_CHUNKS * CHUNK, t_len)
    tile = 512

    xp = x_prompt
    xs = x_sample.reshape(1, sb * s_len, d_model)
    outs = [[] for _ in range(6)]
    for i in range(depth):
        vec = lambda a: a[i].reshape(1, -1)
        gq = jnp.tile(g_q[i], n_heads).reshape(1, d_att)
        gk = jnp.tile(g_k[i], n_heads).reshape(1, d_att)
        conv_w = (w_dw[i], vec(b_dw), vec(g_conv_ln), vec(b_conv_ln), vec(g_conv_out))

        (cn_p, u_tail, q_p, kt_p, vb_p, k_tail, v_tail, w_out_b, w_ff1_b, w_ff2_b, w_gate_b, bias,
         w_in_b, w_ple_b) = _in_proj(
            xp, vec(g_mix), w_in[i], gq, gk, conv_w, tm=tile, keep=keep,
            cast=(w_out[i], w_ff1[i], w_ff2[i], w_gate[i]), cast_once=(w_ple[i],), rel_bias=rel_bias[i])
        mix_w = (w_out_b, vec(g_ffn), w_ff1_b, w_ff2_b, vec(g_ple), w_gate_b, w_ple_b)
        an_p = _attn_prompt(q_p, kt_p, vb_p, bias, vec(g_att_out), tq=tile)
        outs[0].append(k_tail.reshape(nb, keep, n_heads, HEAD_DIM))
        outs[1].append(v_tail.reshape(nb, keep, n_heads, HEAD_DIM))
        outs[2].append(u_tail[:, HALO - (CONV_WIDTH - 1):])

        rows = sb * s_len
        cn_s, u_s, q_s, k_s, v_s = _in_proj(xs, vec(g_mix), w_in_b, gq, gk, conv_w, tm=rows, keep=rows,
                                            state=state_conv[i])
        per_b = lambda a: a.reshape(sb, s_len, a.shape[-1])
        ck = cache_att_k[i].reshape(sb, -1, d_att).astype(BF16)
        cv = cache_att_v[i].reshape(sb, -1, d_att).astype(BF16)
        an_s = _attn_sample(per_b(q_s), ck, cv, per_b(k_s), per_b(v_s), bias, vec(g_att_out))
        outs[3].append(k_s.reshape(sb, s_len, n_heads, HEAD_DIM))
        outs[4].append(v_s.reshape(sb, s_len, n_heads, HEAD_DIM))
        conv_in_tail = jnp.concatenate([state_conv[i], per_b(u_s)], axis=1)[:, -(CONV_WIDTH - 1):]
        outs[5].append(conv_in_tail)

        xp, xs = _mix([(xp, p_prompt[i], an_p, cn_p),
                       (xs, p_sample[i].reshape(1, rows, -1), an_s.reshape(1, rows, d_att), cn_s)],
                      *mix_w, tm=tile)

    return (xp, xs.reshape(sb, s_len, d_model)) + tuple(jnp.stack(o) for o in outs)
```

```python
import functools

import jax
import jax.numpy as jnp
import numpy as np
from jax import lax
from jax.experimental import pallas as pl
from jax.experimental.pallas import tpu as pltpu

F32 = jnp.float32
BF16 = jnp.bfloat16

CHUNK = 64
LEFT_CHUNKS = 8
HEAD_DIM = 64
CONV_WIDTH = 31
MAX_REL = 128
EPS = 1e-6
NEG = -1e30
LOG2_E = 1.4426950408889634

LANES = 128
SUBLANES = 8
MXU_TILE = 256
PAIR = 2 * CHUNK
WIN = (LEFT_CHUNKS + 2) * CHUNK
WIN_SLABS = WIN // LANES
HALO = 32
CONV_HOP = 128
SOFTMAX_ROWS = 32
VMEM_LIMIT = 56 * 1024 * 1024


def _rms(x, g):
    return x * lax.rsqrt(jnp.mean(x * x, axis=-1, keepdims=True) + EPS) * g


def _const_spec(shape):
    zeros = (0,) * len(shape)
    return pl.BlockSpec(shape, lambda *_: zeros, pipeline_mode=pl.Buffered(1))


def _bias_kernel(rb_ref, o_ref):
    assert rb_ref.shape[1] == 2 * MAX_REL + 1 and MAX_REL == LANES and CHUNK == LANES // 2
    assert LEFT_CHUNKS * CHUNK - MAX_REL == 3 * LANES and WIN_SLABS == 5
    n_heads = rb_ref.shape[0]
    rb = rb_ref[...]
    lane = lax.broadcasted_iota(jnp.int32, (n_heads, LANES), 1)
    b0 = jnp.broadcast_to(rb[:, 0:1], (n_heads, LANES))
    t4 = jnp.where(lane < CHUNK, rb[:, LANES:2 * LANES], b0)
    t = jnp.concatenate([b0, b0, b0, rb[:, 0:LANES], t4, b0], axis=1)
    r_i = lax.broadcasted_iota(jnp.int32, (PAIR, LANES), 0)
    c_i = lax.broadcasted_iota(jnp.int32, (PAIR, LANES), 1)
    for h in range(n_heads):
        row = jnp.broadcast_to(t[h:h + 1, :], (PAIR, t.shape[1]))
        toe = pltpu.roll(row, 0, 1, stride=1, stride_axis=0)
        for c in range(WIN_SLABS):
            jj = c_i + c * LANES
            valid = ((r_i < CHUNK) & (jj < WIN - CHUNK)) | ((r_i >= CHUNK) & (jj >= CHUNK))
            o_ref[h, c] = jnp.where(valid, toe[:, c * LANES:(c + 1) * LANES] * LOG2_E, NEG)


def _dft_tables(n, hop):
    half = n // 2
    t = np.arange(n)[None, :]
    f = np.arange(half)[:, None]
    ang = 2.0 * np.pi * f * t / n
    fwd = np.concatenate([np.cos(ang), np.cos(np.pi * t), np.sin(ang[1:])], axis=0)
    rows = np.arange(n - hop, n)[:, None]
    ang_o = 2.0 * np.pi * rows * f.T / n
    scale = np.where(f.T == 0, 1.0, 2.0) / n
    inv = np.concatenate([scale * np.cos(ang_o), np.cos(np.pi * rows) / n, (2.0 / n) * np.sin(ang_o[:, 1:])], axis=1)
    delay = (CONV_WIDTH - 1) - np.arange(CONV_WIDTH)[None, :]
    ang_h = 2.0 * np.pi * f * delay / n
    p, q = np.cos(ang_h), np.sin(ang_h)
    r = p.copy()
    r[0] = np.cos(np.pi * delay[0])
    return (jnp.asarray(fwd, F32), jnp.asarray(inv, F32), jnp.asarray(np.stack([p, q, r]), F32))


def _filter_response(wdw_ref, resp_ref, h_scr):
    for k in range(3):
        acc = None
        for j in range(CONV_WIDTH):
            term = resp_ref[k, :, j:j + 1] * wdw_ref[j:j + 1, :]
            acc = term if acc is None else acc + term
        h_scr[k] = acc


def _conv_stage(u, n_seg, halo_ref, ubuf, cn_ref, h_scr, fwd_ref, inv_ref,
                bdw_ref, gln_ref, bln_ref, gco_ref):
    d_conv = u.shape[-1]
    seg = u.shape[0] // n_seg
    n, hop = fwd_ref.shape[0], inv_ref.shape[0]
    half = n // 2
    assert n - hop == HALO and HALO >= CONV_WIDTH - 1 and seg % hop == 0
    for s in range(n_seg):
        if halo_ref is not None:
            hr = halo_ref.shape[1]
            if hr < HALO:
                ubuf[s, 0:HALO - hr, :] = jnp.zeros((HALO - hr, d_conv), F32)
            ubuf[s, HALO - hr:HALO, :] = halo_ref[s]
        ubuf[s, HALO:HALO + seg, :] = u[s * seg:(s + 1) * seg, :]

    blocks = [(s, i) for s in range(n_seg) for i in range(seg // hop)]
    windows = jnp.concatenate([ubuf[s, hop * i:hop * i + n, :].astype(BF16) for s, i in blocks], axis=1)
    spec = jnp.dot(fwd_ref[...].astype(BF16), windows, preferred_element_type=F32)
    p, q, r = h_scr[0], h_scr[1], h_scr[2]
    prods = []
    for k in range(len(blocks)):
        top = spec[0:half, k * d_conv:(k + 1) * d_conv]
        bot = spec[half:n, k * d_conv:(k + 1) * d_conv]
        prods.append(jnp.concatenate([p * top - q * bot, r * bot + q * top], axis=0).astype(BF16))
    y = jnp.dot(inv_ref[...].astype(BF16), jnp.concatenate(prods, axis=1),
                preferred_element_type=F32)
    for k, (s, i) in enumerate(blocks):
        yk = y[:, k * d_conv:(k + 1) * d_conv] + bdw_ref[...]
        mu = jnp.mean(yk, axis=-1, keepdims=True)
        yc = yk - mu
        yn = yc * lax.rsqrt(jnp.mean(yc * yc, axis=-1, keepdims=True) + EPS) * gln_ref[...] + bln_ref[...]
        c = yn * jax.nn.sigmoid(yn)
        cn_ref[s * seg + i * hop:s * seg + (i + 1) * hop, :] = _rms(c, gco_ref[...]).astype(BF16)


def _inproj_kernel(*refs, d_conv, d_att, tail0, prompt, n_cast, n_once):
    if prompt:
        x_ref, gmix_ref, win_ref, gq_ref, gk_ref = refs[:5]
        halo_ref, conv_w = None, refs[5:13]
        cast_in, rb_ref = refs[13:13 + n_cast], refs[13 + n_cast]
        once_in = (win_ref,) + refs[14 + n_cast:14 + n_cast + n_once]
        outs = refs[14 + n_cast + n_once:]
        cn_ref, ut_ref, q_ref, kt_ref, vb_ref, k32_ref, v32_ref = outs[:7]
        cast_out, bias_ref = outs[7:7 + n_cast], outs[7 + n_cast]
        once_out = outs[8 + n_cast:9 + n_cast + n_once]
        ubuf, h_scr = outs[9 + n_cast + n_once:]
        win_ref = once_out[0]
        t = pl.program_id(1)
        tm = x_ref.shape[1]

        for src, dst in zip(cast_in, cast_out):
            dst[...] = src[...].astype(BF16)

        @pl.when((pl.program_id(0) == 0) & (t == 0))
        def _():
            for src, dst in zip(once_in, once_out):
                for lo in range(0, src.shape[1], MXU_TILE):
                    dst[:, lo:lo + MXU_TILE] = src[:, lo:lo + MXU_TILE].astype(BF16)
            _bias_kernel(rb_ref, bias_ref)
            _filter_response(conv_w[0], conv_w[7], h_scr)

        @pl.when(t == 0)
        def _():
            ubuf[0, 0:HALO, :] = jnp.zeros((HALO, d_conv), F32)

        @pl.when(t > 0)
        def _():
            ubuf[0, 0:HALO, :] = ubuf[0, tm:tm + HALO, :]
    else:
        x_ref, gmix_ref, win_ref, gq_ref, gk_ref, halo_ref = refs[:6]
        conv_w = refs[6:14]
        cn_ref, ut_ref, q_ref, k32_ref, v32_ref, ubuf, h_scr = refs[14:]
        tm = x_ref.shape[1]
        _filter_response(conv_w[0], conv_w[7], h_scr)
    h = _rms(x_ref[0], gmix_ref[...]).astype(BF16)

    def proj(lo, width):
        return jnp.dot(h, win_ref[:, lo:lo + width], preferred_element_type=F32)

    u = proj(0, d_conv) * jax.nn.sigmoid(proj(d_conv, d_conv))
    taps = CONV_WIDTH - 1
    if prompt:
        @pl.when(t == pl.num_programs(1) - 1)
        def _():
            ut_ref[0] = u[tm - taps:tm, :]
    else:
        seg = tm // ubuf.shape[0]
        for s in range(ubuf.shape[0]):
            ut_ref[s] = u[(s + 1) * seg - taps:(s + 1) * seg, :]
    wdw_ref, bdw_ref, gln_ref, bln_ref, gco_ref, fwd_ref, inv_ref, _ = conv_w
    _conv_stage(u, ubuf.shape[0], halo_ref, ubuf, cn_ref.at[0], h_scr, fwd_ref, inv_ref,
                bdw_ref, gln_ref, bln_ref, gco_ref)

    head_bits = HEAD_DIM.bit_length() - 1
    r_i = lax.shift_right_logical(lax.broadcasted_iota(jnp.int32, (MXU_TILE, MXU_TILE), 0), head_bits)
    c_i = lax.shift_right_logical(lax.broadcasted_iota(jnp.int32, (MXU_TILE, MXU_TILE), 1), head_bits)
    avg = jnp.where(r_i == c_i, 1.0 / HEAD_DIM, 0.0).astype(BF16)

    def head_rms(z, g):
        sq = (z * z).astype(BF16)
        ms = jnp.concatenate([jnp.dot(sq[:, lo:lo + MXU_TILE], avg, preferred_element_type=F32)
                              for lo in range(0, d_att, MXU_TILE)], axis=1)
        return z * lax.rsqrt(ms + EPS) * g

    o = 2 * d_conv
    qn = head_rms(proj(o, d_att), gq_ref[...])
    q_ref[0] = (qn * (HEAD_DIM ** -0.5 * LOG2_E)).astype(BF16)
    kn = head_rms(proj(o + d_att, d_att), gk_ref[...])
    v = proj(o + 2 * d_att, d_att)
    if prompt:
        even_row = (lax.broadcasted_iota(jnp.int32, (d_att, LANES), 0) & HEAD_DIM) == 0
        for s in range(tm // LANES):
            kt = kn[s * LANES:(s + 1) * LANES, :].T
            kt_ref[0, 0, s] = jnp.where(even_row, kt, 0.0).astype(BF16)
            kt_ref[0, 1, s] = jnp.where(even_row, 0.0, kt).astype(BF16)
        even = (lax.broadcasted_iota(jnp.int32, (tm, d_att), 1) & HEAD_DIM) == 0
        vb_ref[0, 0] = jnp.where(even, v, 0.0).astype(BF16)
        vb_ref[0, 1] = jnp.where(even, 0.0, v).astype(BF16)

    @pl.when(pl.program_id(1) >= tail0)
    def _():
        k32_ref[0] = kn
        v32_ref[0] = v


def _in_proj(x, g_mix, w_in, g_q, g_k, conv_w, *, tm, keep, state=None, cast=(), cast_once=(), rel_bias=None):
    nb, t_len, d_model = x.shape
    d_att = g_q.shape[-1]
    d_in = w_in.shape[-1]
    d_conv = (d_in - 3 * d_att) // 2
    n_t = t_len // tm
    prompt = state is None
    assert t_len % tm == 0 and (t_len - keep) % tm == 0 and tm % LANES == 0 and tm >= HALO
    tail0 = (t_len - keep) // tm
    row = lambda b, t: (b, t, 0)
    tail = lambda b, t: (b, jnp.maximum(t - tail0, 0), 0)
    halves = pl.BlockSpec((1, 2, tm, d_att), lambda b, t: (b, 0, t, 0))
    out_shape = [jax.ShapeDtypeStruct((nb, t_len, d_conv), BF16)]
    out_specs = [pl.BlockSpec((1, tm, d_conv), row)]
    taps = CONV_WIDTH - 1
    if prompt:
        n_seg = 1
        out_shape += [jax.ShapeDtypeStruct((nb, taps, d_conv), F32)]
        out_specs += [pl.BlockSpec((1, taps, d_conv), lambda b, t: (b, 0, 0))]
    else:
        n_seg = state.shape[0]
        assert nb == 1 and n_t == 1 and tm % n_seg == 0 and state.shape[1] == taps and tm // n_seg >= taps
        out_shape += [jax.ShapeDtypeStruct((n_seg, taps, d_conv), F32)]
        out_specs += [pl.BlockSpec((n_seg, taps, d_conv), lambda b, t: (0, 0, 0))]
    out_shape += [jax.ShapeDtypeStruct((nb, t_len, d_att), BF16)]
    out_specs += [pl.BlockSpec((1, tm, d_att), row)]
    if prompt:
        out_shape += [jax.ShapeDtypeStruct((nb, 2, t_len // LANES, d_att, LANES), BF16),
                      jax.ShapeDtypeStruct((nb, 2, t_len, d_att), BF16)]
        out_specs += [pl.BlockSpec((1, 2, tm // LANES, d_att, LANES), lambda b, t: (b, 0, t, 0, 0)), halves]
    out_shape += [jax.ShapeDtypeStruct((nb, keep, d_att), F32)] * 2
    out_specs += [pl.BlockSpec((1, tm, d_att), tail)] * 2
    cast_specs = []
    for w in cast:
        rb = w.shape[0] // (nb * n_t)
        assert prompt and w.ndim == 2 and rb * nb * n_t == w.shape[0] and rb % (2 * SUBLANES) == 0
        cast_specs.append(pl.BlockSpec((rb, w.shape[1]), lambda b, t: (b * n_t + t, 0)))
        out_shape.append(jax.ShapeDtypeStruct(w.shape, BF16))
    out_specs += cast_specs
    side_specs, side_in = list(cast_specs), list(cast)
    if prompt:
        bias_shape = (rel_bias.shape[0], WIN_SLABS, PAIR, LANES)
        side_specs.append(_const_spec(rel_bias.shape))
        side_in.append(rel_bias)
        out_shape.append(jax.ShapeDtypeStruct(bias_shape, F32))
        out_specs.append(pl.BlockSpec(bias_shape, lambda b, t: (0, 0, 0, 0)))
        for w in (w_in,) + tuple(cast_once):
            assert w.ndim == 2 and w.shape[1] % MXU_TILE == 0
            out_shape.append(jax.ShapeDtypeStruct(w.shape, BF16))
            out_specs.append(pl.BlockSpec(w.shape, lambda b, t: (0, 0)))
        side_specs += [_const_spec(w.shape) for w in cast_once]
        side_in += list(cast_once)
    else:
        assert not cast_once
    kern = functools.partial(_inproj_kernel, d_conv=d_conv, d_att=d_att, tail0=tail0, prompt=prompt,
                             n_cast=len(cast), n_once=len(cast_once))
    extra = [] if prompt else [state]
    hop = min(CONV_HOP, tm // n_seg)
    conv_w = tuple(conv_w) + _dft_tables(HALO + hop, hop)
    return pl.pallas_call(
        kern,
        grid=(nb, n_t),
        in_specs=[pl.BlockSpec((1, tm, d_model), row),
                  _const_spec((1, d_model)), _const_spec((d_model, d_in)),
                  _const_spec((1, d_att)), _const_spec((1, d_att))]
                 + [_const_spec(a.shape) for a in extra + list(conv_w)] + side_specs,
        out_specs=out_specs,
        out_shape=out_shape,
        scratch_shapes=[pltpu.VMEM((n_seg, HALO + tm // n_seg, d_conv), F32),
                        pltpu.VMEM((3, (HALO + hop) // 2, d_conv), F32)],
        compiler_params=pltpu.CompilerParams(
            dimension_semantics=("arbitrary", "arbitrary"), vmem_limit_bytes=VMEM_LIMIT),
        name="in_proj",
    )(x, g_mix, w_in, g_q, g_k, *extra, *conv_w, *side_in)


def _attn_scratch(n_heads, slots):
    return [pltpu.VMEM((slots, n_heads // 2, PAIR, 2 * WIN), F32),
            pltpu.VMEM((slots, n_heads // 2, PAIR, 2 * WIN), BF16)]


def _attend(rows, n_slabs, get_q, get_k, get_v, bias_ref, off, s_scr, p_scr):
    n_pairs = s_scr.shape[0]
    width = n_slabs * LANES
    for hp in range(n_pairs):
        s_scr[hp, 0:rows, 0:2 * width] = jnp.dot(get_q(hp), get_k(hp), preferred_element_type=F32)
    for hp in range(n_pairs):
        for sub in range(2):
            h = 2 * hp + sub
            for r0 in range(0, rows, SOFTMAX_ROWS):
                r = slice(r0, r0 + SOFTMAX_ROWS)
                s = s_scr[hp, r, sub * width:(sub + 1) * width]
                s = s + jnp.concatenate([bias_ref[h, off + c, r, :] for c in range(n_slabs)], axis=1)
                e = jnp.exp2(s - jnp.max(s, axis=-1, keepdims=True))
                p_scr[hp, r, sub * width:(sub + 1) * width] = e.astype(BF16)
    first = lax.broadcasted_iota(jnp.int32, (width, LANES), 1) < HEAD_DIM
    sums = jnp.concatenate([jnp.where(first, 1.0, 0.0), jnp.where(first, 0.0, 1.0)], axis=0).astype(BF16)
    outs = []
    for hp in range(n_pairs):
        o2 = jnp.dot(p_scr[hp, 0:rows, 0:2 * width], jnp.concatenate([get_v(hp), sums], axis=1),
                     preferred_element_type=F32)
        outs.append(o2[:, 0:LANES] * (1.0 / o2[:, LANES:2 * LANES]))
    return jnp.concatenate(outs, axis=1)


def _attn_prompt_kernel(q_ref, kt_ref, v_ref, bias_ref, g_ref, o_ref, s_scr, p_scr):
    pairs = q_ref.shape[1] // PAIR
    slots = s_scr.shape[0]
    t = pl.program_id(1)

    def run_pairs(first_step):
        for i in range(pairs):
            p = i if first_step else t * pairs + i
            n_slabs = min(i + 1, WIN_SLABS) if first_step else WIN_SLABS
            ws = max(p - (WIN_SLABS - 1), 0) if first_step else p - (WIN_SLABS - 1)
            off = ws + (WIN_SLABS - 1) - p
            k0 = ws * LANES if first_step else pl.multiple_of(ws * LANES, LANES)
            width = n_slabs * LANES

            def get_k(hp, ws=ws, n_slabs=n_slabs):
                return jnp.concatenate(
                    [kt_ref[0, e, ws + c, hp * LANES:(hp + 1) * LANES, :]
                     for e in range(2) for c in range(n_slabs)], axis=1)

            def get_v(hp, k0=k0, width=width):
                lanes = slice(hp * LANES, (hp + 1) * LANES)
                return jnp.concatenate([v_ref[0, 0, pl.ds(k0, width), lanes],
                                        v_ref[0, 1, pl.ds(k0, width), lanes]], axis=0)

            rows = slice(i * PAIR, (i + 1) * PAIR)

            def get_q(hp, rows=rows):
                return q_ref[0, rows, hp * LANES:(hp + 1) * LANES]

            k = i % slots
            a = _attend(PAIR, n_slabs, get_q, get_k, get_v, bias_ref, off, s_scr.at[k], p_scr.at[k])
            o_ref[0, rows, :] = _rms(a, g_ref[...]).astype(BF16)

    pl.when(t == 0)(functools.partial(run_pairs, True))
    pl.when(t > 0)(functools.partial(run_pairs, False))


def _attn_prompt(q, kt, v, bias, g_att, *, tq):
    nb, t_len, d_att = q.shape
    n_heads = d_att // HEAD_DIM
    assert t_len % tq == 0 and tq % PAIR == 0 and tq // PAIR >= WIN_SLABS - 1
    return pl.pallas_call(
        _attn_prompt_kernel,
        grid=(nb, t_len // tq),
        in_specs=[pl.BlockSpec((1, tq, d_att), lambda b, t: (b, t, 0)),
                  pl.BlockSpec((1, 2, t_len // LANES, d_att, LANES), lambda b, t: (b, 0, 0, 0, 0)),
                  pl.BlockSpec((1, 2, t_len, d_att), lambda b, t: (b, 0, 0, 0)),
                  _const_spec(bias.shape), _const_spec((1, d_att))],
        out_specs=pl.BlockSpec((1, tq, d_att), lambda b, t: (b, t, 0)),
        out_shape=jax.ShapeDtypeStruct((nb, t_len, d_att), BF16),
        scratch_shapes=_attn_scratch(n_heads, slots=2),
        compiler_params=pltpu.CompilerParams(
            dimension_semantics=("arbitrary", "arbitrary"), vmem_limit_bytes=VMEM_LIMIT),
        name="attn_prompt",
    )(q, kt, v, bias, g_att)


def _attn_sample_kernel(q_ref, ck_ref, cv_ref, kn_ref, vn_ref, bias_ref, g_ref, o_ref, kt_s, v_s,
                        s_scr, p_scr):
    d_att = q_ref.shape[-1]
    pad = jnp.zeros((WIN - ck_ref.shape[1] - kn_ref.shape[1], d_att), F32)
    k_all = jnp.concatenate([ck_ref[0].astype(F32), kn_ref[0], pad], axis=0)
    even_row = (lax.broadcasted_iota(jnp.int32, (d_att, LANES), 0) & HEAD_DIM) == 0
    for c in range(WIN_SLABS):
        kt = k_all[c * LANES:(c + 1) * LANES, :].T
        kt_s[0, c] = jnp.where(even_row, kt, 0.0).astype(BF16)
        kt_s[1, c] = jnp.where(even_row, 0.0, kt).astype(BF16)
    v_all = jnp.concatenate([cv_ref[0].astype(F32), vn_ref[0], pad], axis=0)
    even = (lax.broadcasted_iota(jnp.int32, v_all.shape, 1) & HEAD_DIM) == 0
    v_s[0] = jnp.where(even, v_all, 0.0).astype(BF16)
    v_s[1] = jnp.where(even, 0.0, v_all).astype(BF16)

    def get_k(hp):
        return jnp.concatenate(
            [kt_s[e, c, hp * LANES:(hp + 1) * LANES, :] for e in range(2) for c in range(WIN_SLABS)], axis=1)

    def get_v(hp):
        lanes = slice(hp * LANES, (hp + 1) * LANES)
        return jnp.concatenate([v_s[0, :, lanes], v_s[1, :, lanes]], axis=0)

    def get_q(hp):
        return q_ref[0, :, hp * LANES:(hp + 1) * LANES]

    a = _attend(q_ref.shape[1], WIN_SLABS, get_q, get_k, get_v, bias_ref, 0, s_scr.at[0], p_scr.at[0])
    o_ref[0] = _rms(a, g_ref[...]).astype(BF16)


def _attn_sample(q, cache_k, cache_v, k_new, v_new, bias, g_att):
    nb, s_len, d_att = k_new.shape
    n_heads = d_att // HEAD_DIM
    l_cache = cache_k.shape[1]
    assert s_len == CHUNK and l_cache == LEFT_CHUNKS * CHUNK
    per_b = lambda b: (b, 0, 0)
    return pl.pallas_call(
        _attn_sample_kernel,
        grid=(nb,),
        in_specs=[pl.BlockSpec((1, s_len, d_att), per_b),
                  pl.BlockSpec((1, l_cache, d_att), per_b), pl.BlockSpec((1, l_cache, d_att), per_b),
                  pl.BlockSpec((1, s_len, d_att), per_b), pl.BlockSpec((1, s_len, d_att), per_b),
                  _const_spec(bias.shape), _const_spec((1, d_att))],
        out_specs=pl.BlockSpec((1, s_len, d_att), per_b),
        out_shape=jax.ShapeDtypeStruct((nb, s_len, d_att), BF16),
        scratch_shapes=[pltpu.VMEM((2, WIN_SLABS, d_att, LANES), BF16), pltpu.VMEM((2, WIN, d_att), BF16)]
                       + _attn_scratch(n_heads, slots=1),
        compiler_params=pltpu.CompilerParams(
            dimension_semantics=("arbitrary",), vmem_limit_bytes=VMEM_LIMIT),
        name="attn_sample",
    )(q, cache_k, cache_v, k_new, v_new, bias, g_att)


def _dense_stage(x, cn, an, p, wout_ref, gffn_ref, w1_ref, w2_ref, gple_ref, wg_ref, wple_ref, ff_chunk):
    d_conv = cn.shape[-1]
    x1 = (x + jnp.dot(cn, wout_ref[0:d_conv, :], preferred_element_type=F32)
          + jnp.dot(an, wout_ref[d_conv:, :], preferred_element_type=F32))
    hb = _rms(x1, gffn_ref[...]).astype(BF16)
    ffn = None
    for lo in range(0, w1_ref.shape[1], ff_chunk):
        a = jnp.maximum(jnp.dot(hb, w1_ref[:, lo:lo + ff_chunk], preferred_element_type=F32), 0.0)
        part = jnp.dot((a * a).astype(BF16), w2_ref[lo:lo + ff_chunk, :], preferred_element_type=F32)
        ffn = part if ffn is None else ffn + part
    x2 = x1 + ffn
    gate = jax.nn.sigmoid(jnp.dot(_rms(x2, gple_ref[...]).astype(BF16), wg_ref[...],
                                  preferred_element_type=F32))
    ple = jnp.dot(p.astype(BF16), wple_ref[...], preferred_element_type=F32)
    return x2 + ple * gate


def _mix_kernel(*refs, starts, ff_chunk):
    n_groups = len(starts) - 1
    ins, dense_w, outs = refs[:4 * n_groups], refs[4 * n_groups:-n_groups], refs[-n_groups:]
    i = pl.program_id(0)
    for g in range(n_groups):
        x_ref, p_ref, an_ref, cn_ref = ins[4 * g:4 * g + 4]

        @pl.when((i >= starts[g]) & (i < starts[g + 1]))
        def _():
            outs[g][0] = _dense_stage(x_ref[0], cn_ref[0], an_ref[0], p_ref[0], *dense_w, ff_chunk)


def _mix(groups, w_out, g_ffn, w1, w2, g_ple, w_gate, w_ple, *, tm):
    consts = [w_out, g_ffn, w1, w2, g_ple, w_gate, w_ple]
    starts, in_specs, out_specs, out_shape, operands = [0], [], [], [], []
    for x, p, an, cn in groups:
        ng, rows, d_model = x.shape
        assert rows % tm == 0
        n_t = rows // tm
        first, count = starts[-1], ng * n_t
        starts.append(first + count)

        def tile(i, first=first, count=count, n_t=n_t):
            j = jnp.clip(i - first, 0, count - 1)
            return (j // n_t, j % n_t, 0)

        in_specs += [pl.BlockSpec((1, tm, a.shape[-1]), tile) for a in (x, p, an, cn)]
        out_specs.append(pl.BlockSpec((1, tm, d_model), tile))
        out_shape.append(jax.ShapeDtypeStruct(x.shape, F32))
        operands += [x, p, an, cn]
    return pl.pallas_call(
        functools.partial(_mix_kernel, starts=tuple(starts), ff_chunk=2048),
        grid=(starts[-1],),
        in_specs=in_specs + [_const_spec(c.shape) for c in consts],
        out_specs=out_specs,
        out_shape=out_shape,
        compiler_params=pltpu.CompilerParams(
            dimension_semantics=("arbitrary",), vmem_limit_bytes=VMEM_LIMIT),
        name="mix",
    )(*operands, *consts)


def kernel(x_prompt, x_sample, p_prompt, p_sample, cache_att_k, cache_att_v, state_conv, g_mix, w_in, w_dw,
           b_dw, g_conv_ln, b_conv_ln, g_q, g_k, rel_bias, g_conv_out, g_att_out, w_out, g_ffn, w_ff1, w_ff2,
           g_ple, w_gate, w_ple):
    depth = w_in.shape[0]
    nb, t_len, d_model = x_prompt.shape
    sb, s_len, _ = x_sample.shape
    d_conv = w_dw.shape[-1]
    d_att = g_att_out.shape[-1]
    n_heads = d_att // HEAD_DIM
    keep = min(LEFT_CHUNKS * CHUNK, t_len)
    tile = 512

    xp = x_prompt
    xs = x_sample.reshape(1, sb * s_len, d_model)
    outs = [[] for _ in range(6)]
    for i in range(depth):
        vec = lambda a: a[i].reshape(1, -1)
        gq = jnp.tile(g_q[i], n_heads).reshape(1, d_att)
        gk = jnp.tile(g_k[i], n_heads).reshape(1, d_att)
        conv_w = (w_dw[i], vec(b_dw), vec(g_conv_ln), vec(b_conv_ln), vec(g_conv_out))

        (cn_p, u_tail, q_p, kt_p, vb_p, k_tail, v_tail, w_out_b, w_ff1_b, w_ff2_b, w_gate_b, bias,
         w_in_b, w_ple_b) = _in_proj(
            xp, vec(g_mix), w_in[i], gq, gk, conv_w, tm=tile, keep=keep,
            cast=(w_out[i], w_ff1[i], w_ff2[i], w_gate[i]), cast_once=(w_ple[i],), rel_bias=rel_bias[i])
        mix_w = (w_out_b, vec(g_ffn), w_ff1_b, w_ff2_b, vec(g_ple), w_gate_b, w_ple_b)
        an_p = _attn_prompt(q_p, kt_p, vb_p, bias, vec(g_att_out), tq=tile)
        outs[0].append(k_tail.reshape(nb, keep, n_heads, HEAD_DIM))
        outs[1].append(v_tail.reshape(nb, keep, n_heads, HEAD_DIM))
        outs[2].append(u_tail)

        rows = sb * s_len
        cn_s, u_s, q_s, k_s, v_s = _in_proj(xs, vec(g_mix), w_in_b, gq, gk, conv_w, tm=rows, keep=rows,
                                            state=state_conv[i])
        per_b = lambda a: a.reshape(sb, s_len, a.shape[-1])
        ck = cache_att_k[i].reshape(sb, -1, d_att).astype(BF16)
        cv = cache_att_v[i].reshape(sb, -1, d_att).astype(BF16)
        an_s = _attn_sample(per_b(q_s), ck, cv, per_b(k_s), per_b(v_s), bias, vec(g_att_out))
        outs[3].append(k_s.reshape(sb, s_len, n_heads, HEAD_DIM))
        outs[4].append(v_s.reshape(sb, s_len, n_heads, HEAD_DIM))
        outs[5].append(u_s)

        xp, xs = _mix([(xp, p_prompt[i], an_p, cn_p),
                       (xs, p_sample[i].reshape(1, rows, -1), an_s.reshape(1, rows, d_att), cn_s)],
                      *mix_w, tm=tile)

    return (xp, xs.reshape(sb, s_len, d_model)) + tuple(jnp.stack(o) for o in outs)
```

```python
import functools

import jax
import jax.numpy as jnp
import numpy as np
from jax import lax
from jax.experimental import pallas as pl
from jax.experimental.pallas import tpu as pltpu

F32 = jnp.float32
BF16 = jnp.bfloat16

CHUNK = 64
LEFT_CHUNKS = 8
HEAD_DIM = 64
CONV_WIDTH = 31
MAX_REL = 128
EPS = 1e-6
NEG = -1e30
LOG2_E = 1.4426950408889634

LANES = 128
SUBLANES = 8
MXU_TILE = 256
PAIR = 2 * CHUNK
WIN = (LEFT_CHUNKS + 2) * CHUNK
WIN_SLABS = WIN // LANES
HALO = 32
CONV_HOP = 128
SOFTMAX_ROWS = 32
VMEM_LIMIT = 56 * 1024 * 1024


def _rms(x, g):
    return x * lax.rsqrt(jnp.mean(x * x, axis=-1, keepdims=True) + EPS) * g


def _const_spec(shape):
    zeros = (0,) * len(shape)
    return pl.BlockSpec(shape, lambda *_: zeros, pipeline_mode=pl.Buffered(1))


def _bias_kernel(rb_ref, o_ref):
    assert rb_ref.shape[1] == 2 * MAX_REL + 1 and MAX_REL == LANES and CHUNK == LANES // 2
    assert LEFT_CHUNKS * CHUNK - MAX_REL == 3 * LANES and WIN_SLABS == 5
    n_heads = rb_ref.shape[0]
    rb = rb_ref[...]
    lane = lax.broadcasted_iota(jnp.int32, (n_heads, LANES), 1)
    b0 = jnp.broadcast_to(rb[:, 0:1], (n_heads, LANES))
    t4 = jnp.where(lane < CHUNK, rb[:, LANES:2 * LANES], b0)
    t = jnp.concatenate([b0, b0, b0, rb[:, 0:LANES], t4, b0], axis=1)
    r_i = lax.broadcasted_iota(jnp.int32, (PAIR, LANES), 0)
    c_i = lax.broadcasted_iota(jnp.int32, (PAIR, LANES), 1)
    for h in range(n_heads):
        row = jnp.broadcast_to(t[h:h + 1, :], (PAIR, t.shape[1]))
        toe = pltpu.roll(row, 0, 1, stride=1, stride_axis=0)
        for c in range(WIN_SLABS):
            jj = c_i + c * LANES
            valid = ((r_i < CHUNK) & (jj < WIN - CHUNK)) | ((r_i >= CHUNK) & (jj >= CHUNK))
            o_ref[h, c] = jnp.where(valid, toe[:, c * LANES:(c + 1) * LANES] * LOG2_E, NEG)


def _dft_tables(n, hop):
    half = n // 2
    t = np.arange(n)[None, :]
    f = np.arange(half)[:, None]
    ang = 2.0 * np.pi * f * t / n
    fwd = np.concatenate([np.cos(ang), np.cos(np.pi * t), np.sin(ang[1:])], axis=0)
    rows = np.arange(n - hop, n)[:, None]
    ang_o = 2.0 * np.pi * rows * f.T / n
    scale = np.where(f.T == 0, 1.0, 2.0) / n
    inv = np.concatenate([scale * np.cos(ang_o), np.cos(np.pi * rows) / n, (2.0 / n) * np.sin(ang_o[:, 1:])], axis=1)
    delay = (CONV_WIDTH - 1) - np.arange(CONV_WIDTH)[None, :]
    ang_h = 2.0 * np.pi * f * delay / n
    p, q = np.cos(ang_h), np.sin(ang_h)
    r = p.copy()
    r[0] = np.cos(np.pi * delay[0])
    return (jnp.asarray(fwd, F32), jnp.asarray(inv, F32), jnp.asarray(np.stack([p, q, r]), F32))


def _filter_response(wdw_ref, resp_ref, h_scr):
    for k in range(3):
        acc = None
        for j in range(CONV_WIDTH):
            term = resp_ref[k, :, j:j + 1] * wdw_ref[j:j + 1, :]
            acc = term if acc is None else acc + term
        h_scr[k] = acc


def _conv_stage(u, n_seg, halo_ref, ubuf, cn_ref, h_scr, fwd_ref, inv_ref,
                bdw_ref, gln_ref, bln_ref, gco_ref):
    d_conv = u.shape[-1]
    seg = u.shape[0] // n_seg
    n, hop = fwd_ref.shape[0], inv_ref.shape[0]
    half = n // 2
    assert n - hop == HALO and HALO >= CONV_WIDTH - 1 and seg % hop == 0
    for s in range(n_seg):
        if halo_ref is not None:
            hr = halo_ref.shape[1]
            if hr < HALO:
                ubuf[s, 0:HALO - hr, :] = jnp.zeros((HALO - hr, d_conv), F32)
            ubuf[s, HALO - hr:HALO, :] = halo_ref[s]
        ubuf[s, HALO:HALO + seg, :] = u[s * seg:(s + 1) * seg, :]

    blocks = [(s, i) for s in range(n_seg) for i in range(seg // hop)]
    windows = jnp.concatenate([ubuf[s, hop * i:hop * i + n, :].astype(BF16) for s, i in blocks], axis=1)
    spec = jnp.dot(fwd_ref[...].astype(BF16), windows, preferred_element_type=F32)
    p, q, r = h_scr[0], h_scr[1], h_scr[2]
    prods = []
    for k in range(len(blocks)):
        top = spec[0:half, k * d_conv:(k + 1) * d_conv]
        bot = spec[half:n, k * d_conv:(k + 1) * d_conv]
        prods.append(jnp.concatenate([p * top - q * bot, r * bot + q * top], axis=0).astype(BF16))
    y = jnp.dot(inv_ref[...].astype(BF16), jnp.concatenate(prods, axis=1),
                preferred_element_type=F32)
    for k, (s, i) in enumerate(blocks):
        yk = y[:, k * d_conv:(k + 1) * d_conv] + bdw_ref[...]
        mu = jnp.mean(yk, axis=-1, keepdims=True)
        yc = yk - mu
        yn = yc * lax.rsqrt(jnp.mean(yc * yc, axis=-1, keepdims=True) + EPS) * gln_ref[...] + bln_ref[...]
        c = yn * jax.nn.sigmoid(yn)
        cn_ref[s * seg + i * hop:s * seg + (i + 1) * hop, :] = _rms(c, gco_ref[...]).astype(BF16)


def _inproj_kernel(*refs, d_conv, d_att, tail0, prompt, n_cast, n_once):
    if prompt:
        x_ref, gmix_ref, win_ref, gq_ref, gk_ref = refs[:5]
        halo_ref, conv_w = None, refs[5:13]
        cast_in, rb_ref = refs[13:13 + n_cast], refs[13 + n_cast]
        once_in = (win_ref,) + refs[14 + n_cast:14 + n_cast + n_once]
        outs = refs[14 + n_cast + n_once:]
        cn_ref, ut_ref, q_ref, kt_ref, vb_ref, k32_ref, v32_ref = outs[:7]
        cast_out, bias_ref = outs[7:7 + n_cast], outs[7 + n_cast]
        once_out = outs[8 + n_cast:9 + n_cast + n_once]
        ubuf, h_scr = outs[9 + n_cast + n_once:]
        win_ref = once_out[0]
        t = pl.program_id(1)
        tm = x_ref.shape[1]

        for src, dst in zip(cast_in, cast_out):
            dst[...] = src[...].astype(BF16)

        @pl.when((pl.program_id(0) == 0) & (t == 0))
        def _():
            for src, dst in zip(once_in, once_out):
                for lo in range(0, src.shape[1], MXU_TILE):
                    dst[:, lo:lo + MXU_TILE] = src[:, lo:lo + MXU_TILE].astype(BF16)
            _bias_kernel(rb_ref, bias_ref)
            _filter_response(conv_w[0], conv_w[7], h_scr)

        @pl.when(t == 0)
        def _():
            ubuf[0, 0:HALO, :] = jnp.zeros((HALO, d_conv), F32)

        @pl.when(t > 0)
        def _():
            ubuf[0, 0:HALO, :] = ubuf[0, tm:tm + HALO, :]
    else:
        x_ref, gmix_ref, win_ref, gq_ref, gk_ref, halo_ref = refs[:6]
        conv_w = refs[6:14]
        cn_ref, u_ref, q_ref, k32_ref, v32_ref, ubuf, h_scr = refs[14:]
        tm = x_ref.shape[1]
        _filter_response(conv_w[0], conv_w[7], h_scr)
    h = _rms(x_ref[0], gmix_ref[...]).astype(BF16)

    def proj(lo, width):
        return jnp.dot(h, win_ref[:, lo:lo + width], preferred_element_type=F32)

    u = proj(0, d_conv) * jax.nn.sigmoid(proj(d_conv, d_conv))
    if prompt:
        @pl.when(t == pl.num_programs(1) - 1)
        def _():
            ut_ref[0] = u[tm - HALO:tm, :]
    else:
        u_ref[0] = u
    wdw_ref, bdw_ref, gln_ref, bln_ref, gco_ref, fwd_ref, inv_ref, _ = conv_w
    _conv_stage(u, ubuf.shape[0], halo_ref, ubuf, cn_ref.at[0], h_scr, fwd_ref, inv_ref,
                bdw_ref, gln_ref, bln_ref, gco_ref)

    head_bits = HEAD_DIM.bit_length() - 1
    r_i = lax.shift_right_logical(lax.broadcasted_iota(jnp.int32, (MXU_TILE, MXU_TILE), 0), head_bits)
    c_i = lax.shift_right_logical(lax.broadcasted_iota(jnp.int32, (MXU_TILE, MXU_TILE), 1), head_bits)
    avg = jnp.where(r_i == c_i, 1.0 / HEAD_DIM, 0.0).astype(BF16)

    def head_rms(z, g):
        sq = (z * z).astype(BF16)
        ms = jnp.concatenate([jnp.dot(sq[:, lo:lo + MXU_TILE], avg, preferred_element_type=F32)
                              for lo in range(0, d_att, MXU_TILE)], axis=1)
        return z * lax.rsqrt(ms + EPS) * g

    o = 2 * d_conv
    qn = head_rms(proj(o, d_att), gq_ref[...])
    q_ref[0] = (qn * (HEAD_DIM ** -0.5 * LOG2_E)).astype(BF16)
    kn = head_rms(proj(o + d_att, d_att), gk_ref[...])
    v = proj(o + 2 * d_att, d_att)
    if prompt:
        even_row = (lax.broadcasted_iota(jnp.int32, (d_att, LANES), 0) & HEAD_DIM) == 0
        for s in range(tm // LANES):
            kt = kn[s * LANES:(s + 1) * LANES, :].T
            kt_ref[0, 0, s] = jnp.where(even_row, kt, 0.0).astype(BF16)
            kt_ref[0, 1, s] = jnp.where(even_row, 0.0, kt).astype(BF16)
        even = (lax.broadcasted_iota(jnp.int32, (tm, d_att), 1) & HEAD_DIM) == 0
        vb_ref[0, 0] = jnp.where(even, v, 0.0).astype(BF16)
        vb_ref[0, 1] = jnp.where(even, 0.0, v).astype(BF16)

    @pl.when(pl.program_id(1) >= tail0)
    def _():
        k32_ref[0] = kn
        v32_ref[0] = v


def _in_proj(x, g_mix, w_in, g_q, g_k, conv_w, *, tm, keep, state=None, cast=(), cast_once=(), rel_bias=None):
    nb, t_len, d_model = x.shape
    d_att = g_q.shape[-1]
    d_in = w_in.shape[-1]
    d_conv = (d_in - 3 * d_att) // 2
    n_t = t_len // tm
    prompt = state is None
    assert t_len % tm == 0 and (t_len - keep) % tm == 0 and tm % LANES == 0 and tm >= HALO
    tail0 = (t_len - keep) // tm
    row = lambda b, t: (b, t, 0)
    tail = lambda b, t: (b, jnp.maximum(t - tail0, 0), 0)
    halves = pl.BlockSpec((1, 2, tm, d_att), lambda b, t: (b, 0, t, 0))
    out_shape = [jax.ShapeDtypeStruct((nb, t_len, d_conv), BF16)]
    out_specs = [pl.BlockSpec((1, tm, d_conv), row)]
    if prompt:
        n_seg = 1
        out_shape += [jax.ShapeDtypeStruct((nb, HALO, d_conv), F32)]
        out_specs += [pl.BlockSpec((1, HALO, d_conv), lambda b, t: (b, 0, 0))]
    else:
        n_seg = state.shape[0]
        assert nb == 1 and n_t == 1 and tm % n_seg == 0 and state.shape[1] == CONV_WIDTH - 1
        out_shape += [jax.ShapeDtypeStruct((nb, t_len, d_conv), F32)]
        out_specs += [pl.BlockSpec((1, tm, d_conv), row)]
    out_shape += [jax.ShapeDtypeStruct((nb, t_len, d_att), BF16)]
    out_specs += [pl.BlockSpec((1, tm, d_att), row)]
    if prompt:
        out_shape += [jax.ShapeDtypeStruct((nb, 2, t_len // LANES, d_att, LANES), BF16),
                      jax.ShapeDtypeStruct((nb, 2, t_len, d_att), BF16)]
        out_specs += [pl.BlockSpec((1, 2, tm // LANES, d_att, LANES), lambda b, t: (b, 0, t, 0, 0)), halves]
    out_shape += [jax.ShapeDtypeStruct((nb, keep, d_att), F32)] * 2
    out_specs += [pl.BlockSpec((1, tm, d_att), tail)] * 2
    cast_specs = []
    for w in cast:
        rb = w.shape[0] // (nb * n_t)
        assert prompt and w.ndim == 2 and rb * nb * n_t == w.shape[0] and rb % (2 * SUBLANES) == 0
        cast_specs.append(pl.BlockSpec((rb, w.shape[1]), lambda b, t: (b * n_t + t, 0)))
        out_shape.append(jax.ShapeDtypeStruct(w.shape, BF16))
    out_specs += cast_specs
    side_specs, side_in = list(cast_specs), list(cast)
    if prompt:
        bias_shape = (rel_bias.shape[0], WIN_SLABS, PAIR, LANES)
        side_specs.append(_const_spec(rel_bias.shape))
        side_in.append(rel_bias)
        out_shape.append(jax.ShapeDtypeStruct(bias_shape, F32))
        out_specs.append(pl.BlockSpec(bias_shape, lambda b, t: (0, 0, 0, 0)))
        for w in (w_in,) + tuple(cast_once):
            assert w.ndim == 2 and w.shape[1] % MXU_TILE == 0
            out_shape.append(jax.ShapeDtypeStruct(w.shape, BF16))
            out_specs.append(pl.BlockSpec(w.shape, lambda b, t: (0, 0)))
        side_specs += [_const_spec(w.shape) for w in cast_once]
        side_in += list(cast_once)
    else:
        assert not cast_once
    kern = functools.partial(_inproj_kernel, d_conv=d_conv, d_att=d_att, tail0=tail0, prompt=prompt,
                             n_cast=len(cast), n_once=len(cast_once))
    extra = [] if prompt else [state]
    hop = min(CONV_HOP, tm // n_seg)
    conv_w = tuple(conv_w) + _dft_tables(HALO + hop, hop)
    return pl.pallas_call(
        kern,
        grid=(nb, n_t),
        in_specs=[pl.BlockSpec((1, tm, d_model), row),
                  _const_spec((1, d_model)), _const_spec((d_model, d_in)),
                  _const_spec((1, d_att)), _const_spec((1, d_att))]
                 + [_const_spec(a.shape) for a in extra + list(conv_w)] + side_specs,
        out_specs=out_specs,
        out_shape=out_shape,
        scratch_shapes=[pltpu.VMEM((n_seg, HALO + tm // n_seg, d_conv), F32),
                        pltpu.VMEM((3, (HALO + hop) // 2, d_conv), F32)],
        compiler_params=pltpu.CompilerParams(
            dimension_semantics=("arbitrary", "arbitrary"), vmem_limit_bytes=VMEM_LIMIT),
        name="in_proj",
    )(x, g_mix, w_in, g_q, g_k, *extra, *conv_w, *side_in)


def _attn_scratch(n_heads, slots):
    return [pltpu.VMEM((slots, n_heads // 2, PAIR, 2 * WIN), F32),
            pltpu.VMEM((slots, n_heads // 2, PAIR, 2 * WIN), BF16)]


def _attend(rows, n_slabs, get_q, get_k, get_v, bias_ref, off, s_scr, p_scr):
    n_pairs = s_scr.shape[0]
    width = n_slabs * LANES
    for hp in range(n_pairs):
        s_scr[hp, 0:rows, 0:2 * width] = jnp.dot(get_q(hp), get_k(hp), preferred_element_type=F32)
    for hp in range(n_pairs):
        for sub in range(2):
            h = 2 * hp + sub
            for r0 in range(0, rows, SOFTMAX_ROWS):
                r = slice(r0, r0 + SOFTMAX_ROWS)
                s = s_scr[hp, r, sub * width:(sub + 1) * width]
                s = s + jnp.concatenate([bias_ref[h, off + c, r, :] for c in range(n_slabs)], axis=1)
                e = jnp.exp2(s - jnp.max(s, axis=-1, keepdims=True))
                p_scr[hp, r, sub * width:(sub + 1) * width] = e.astype(BF16)
    first = lax.broadcasted_iota(jnp.int32, (width, LANES), 1) < HEAD_DIM
    sums = jnp.concatenate([jnp.where(first, 1.0, 0.0), jnp.where(first, 0.0, 1.0)], axis=0).astype(BF16)
    outs = []
    for hp in range(n_pairs):
        o2 = jnp.dot(p_scr[hp, 0:rows, 0:2 * width], jnp.concatenate([get_v(hp), sums], axis=1),
                     preferred_element_type=F32)
        outs.append(o2[:, 0:LANES] * (1.0 / o2[:, LANES:2 * LANES]))
    return jnp.concatenate(outs, axis=1)


def _attn_prompt_kernel(q_ref, kt_ref, v_ref, bias_ref, g_ref, o_ref, s_scr, p_scr):
    pairs = q_ref.shape[1] // PAIR
    slots = s_scr.shape[0]
    t = pl.program_id(1)

    def run_pairs(first_step):
        for i in range(pairs):
            p = i if first_step else t * pairs + i
            n_slabs = min(i + 1, WIN_SLABS) if first_step else WIN_SLABS
            ws = max(p - (WIN_SLABS - 1), 0) if first_step else p - (WIN_SLABS - 1)
            off = ws + (WIN_SLABS - 1) - p
            k0 = ws * LANES if first_step else pl.multiple_of(ws * LANES, LANES)
            width = n_slabs * LANES

            def get_k(hp, ws=ws, n_slabs=n_slabs):
                return jnp.concatenate(
                    [kt_ref[0, e, ws + c, hp * LANES:(hp + 1) * LANES, :]
                     for e in range(2) for c in range(n_slabs)], axis=1)

            def get_v(hp, k0=k0, width=width):
                lanes = slice(hp * LANES, (hp + 1) * LANES)
                return jnp.concatenate([v_ref[0, 0, pl.ds(k0, width), lanes],
                                        v_ref[0, 1, pl.ds(k0, width), lanes]], axis=0)

            rows = slice(i * PAIR, (i + 1) * PAIR)

            def get_q(hp, rows=rows):
                return q_ref[0, rows, hp * LANES:(hp + 1) * LANES]

            k = i % slots
            a = _attend(PAIR, n_slabs, get_q, get_k, get_v, bias_ref, off, s_scr.at[k], p_scr.at[k])
            o_ref[0, rows, :] = _rms(a, g_ref[...]).astype(BF16)

    pl.when(t == 0)(functools.partial(run_pairs, True))
    pl.when(t > 0)(functools.partial(run_pairs, False))


def _attn_prompt(q, kt, v, bias, g_att, *, tq):
    nb, t_len, d_att = q.shape
    n_heads = d_att // HEAD_DIM
    assert t_len % tq == 0 and tq % PAIR == 0 and tq // PAIR >= WIN_SLABS - 1
    return pl.pallas_call(
        _attn_prompt_kernel,
        grid=(nb, t_len // tq),
        in_specs=[pl.BlockSpec((1, tq, d_att), lambda b, t: (b, t, 0)),
                  pl.BlockSpec((1, 2, t_len // LANES, d_att, LANES), lambda b, t: (b, 0, 0, 0, 0)),
                  pl.BlockSpec((1, 2, t_len, d_att), lambda b, t: (b, 0, 0, 0)),
                  _const_spec(bias.shape), _const_spec((1, d_att))],
        out_specs=pl.BlockSpec((1, tq, d_att), lambda b, t: (b, t, 0)),
        out_shape=jax.ShapeDtypeStruct((nb, t_len, d_att), BF16),
        scratch_shapes=_attn_scratch(n_heads, slots=2),
        compiler_params=pltpu.CompilerParams(
            dimension_semantics=("arbitrary", "arbitrary"), vmem_limit_bytes=VMEM_LIMIT),
        name="attn_prompt",
    )(q, kt, v, bias, g_att)


def _attn_sample_kernel(q_ref, ck_ref, cv_ref, kn_ref, vn_ref, bias_ref, g_ref, o_ref, kt_s, v_s,
                        s_scr, p_scr):
    d_att = q_ref.shape[-1]
    pad = jnp.zeros((WIN - ck_ref.shape[1] - kn_ref.shape[1], d_att), F32)
    k_all = jnp.concatenate([ck_ref[0].astype(F32), kn_ref[0], pad], axis=0)
    even_row = (lax.broadcasted_iota(jnp.int32, (d_att, LANES), 0) & HEAD_DIM) == 0
    for c in range(WIN_SLABS):
        kt = k_all[c * LANES:(c + 1) * LANES, :].T
        kt_s[0, c] = jnp.where(even_row, kt, 0.0).astype(BF16)
        kt_s[1, c] = jnp.where(even_row, 0.0, kt).astype(BF16)
    v_all = jnp.concatenate([cv_ref[0].astype(F32), vn_ref[0], pad], axis=0)
    even = (lax.broadcasted_iota(jnp.int32, v_all.shape, 1) & HEAD_DIM) == 0
    v_s[0] = jnp.where(even, v_all, 0.0).astype(BF16)
    v_s[1] = jnp.where(even, 0.0, v_all).astype(BF16)

    def get_k(hp):
        return jnp.concatenate(
            [kt_s[e, c, hp * LANES:(hp + 1) * LANES, :] for e in range(2) for c in range(WIN_SLABS)], axis=1)

    def get_v(hp):
        lanes = slice(hp * LANES, (hp + 1) * LANES)
        return jnp.concatenate([v_s[0, :, lanes], v_s[1, :, lanes]], axis=0)

    def get_q(hp):
        return q_ref[0, :, hp * LANES:(hp + 1) * LANES]

    a = _attend(q_ref.shape[1], WIN_SLABS, get_q, get_k, get_v, bias_ref, 0, s_scr.at[0], p_scr.at[0])
    o_ref[0] = _rms(a, g_ref[...]).astype(BF16)


def _attn_sample(q, cache_k, cache_v, k_new, v_new, bias, g_att):
    nb, s_len, d_att = k_new.shape
    n_heads = d_att // HEAD_DIM
    l_cache = cache_k.shape[1]
    assert s_len == CHUNK and l_cache == LEFT_CHUNKS * CHUNK
    per_b = lambda b: (b, 0, 0)
    return pl.pallas_call(
        _attn_sample_kernel,
        grid=(nb,),
        in_specs=[pl.BlockSpec((1, s_len, d_att), per_b),
                  pl.BlockSpec((1, l_cache, d_att), per_b), pl.BlockSpec((1, l_cache, d_att), per_b),
                  pl.BlockSpec((1, s_len, d_att), per_b), pl.BlockSpec((1, s_len, d_att), per_b),
                  _const_spec(bias.shape), _const_spec((1, d_att))],
        out_specs=pl.BlockSpec((1, s_len, d_att), per_b),
        out_shape=jax.ShapeDtypeStruct((nb, s_len, d_att), BF16),
        scratch_shapes=[pltpu.VMEM((2, WIN_SLABS, d_att, LANES), BF16), pltpu.VMEM((2, WIN, d_att), BF16)]
                       + _attn_scratch(n_heads, slots=1),
        compiler_params=pltpu.CompilerParams(
            dimension_semantics=("arbitrary",), vmem_limit_bytes=VMEM_LIMIT),
        name="attn_sample",
    )(q, cache_k, cache_v, k_new, v_new, bias, g_att)


def _dense_stage(x, cn, an, p, wout_ref, gffn_ref, w1_ref, w2_ref, gple_ref, wg_ref, wple_ref, ff_chunk,
                 before_chunk):
    d_conv = cn.shape[-1]
    x1 = (x + jnp.dot(cn, wout_ref[0:d_conv, :], preferred_element_type=F32)
          + jnp.dot(an, wout_ref[d_conv:, :], preferred_element_type=F32))
    hb = _rms(x1, gffn_ref[...]).astype(BF16)
    ffn = None
    for lo in range(0, w1_ref.shape[1], ff_chunk):
        before_chunk(lo // ff_chunk)
        a = jnp.maximum(jnp.dot(hb, w1_ref[:, lo:lo + ff_chunk], preferred_element_type=F32), 0.0)
        part = jnp.dot((a * a).astype(BF16), w2_ref[lo:lo + ff_chunk, :], preferred_element_type=F32)
        ffn = part if ffn is None else ffn + part
    x2 = x1 + ffn
    gate = jax.nn.sigmoid(jnp.dot(_rms(x2, gple_ref[...]).astype(BF16), wg_ref[...],
                                  preferred_element_type=F32))
    ple = jnp.dot(p.astype(BF16), wple_ref[...], preferred_element_type=F32)
    return x2 + ple * gate


def _mix_kernel(*refs, starts, ff_chunk):
    n_groups = len(starts) - 1
    ins, dense_w = refs[:4 * n_groups], list(refs[4 * n_groups:4 * n_groups + 7])
    outs, (w1_s, w2_s, sem) = refs[4 * n_groups + 7:5 * n_groups + 7], refs[5 * n_groups + 7:]
    w1_hbm, w2_hbm = dense_w[2], dense_w[3]
    dense_w[2], dense_w[3] = w1_s, w2_s
    i = pl.program_id(0)

    def copies(c):
        cols = slice(c * ff_chunk, (c + 1) * ff_chunk)
        return (pltpu.make_async_copy(w1_hbm.at[:, cols], w1_s.at[:, cols], sem.at[0, c]),
                pltpu.make_async_copy(w2_hbm.at[cols, :], w2_s.at[cols, :], sem.at[1, c]))

    @pl.when(i == 0)
    def _():
        for c in range(w1_s.shape[1] // ff_chunk):
            for cp in copies(c):
                cp.start()

    def wait_chunk(c):
        @pl.when(i == 0)
        def _():
            for cp in copies(c):
                cp.wait()

    for g in range(n_groups):
        x_ref, p_ref, an_ref, cn_ref = ins[4 * g:4 * g + 4]
        before_chunk = wait_chunk if starts[g] == 0 else (lambda c: None)

        @pl.when((i >= starts[g]) & (i < starts[g + 1]))
        def _():
            outs[g][0] = _dense_stage(x_ref[0], cn_ref[0], an_ref[0], p_ref[0], *dense_w, ff_chunk,
                                      before_chunk)


def _mix(groups, w_out, g_ffn, w1, w2, g_ple, w_gate, w_ple, *, tm):
    consts = [w_out, g_ffn, w1, w2, g_ple, w_gate, w_ple]
    starts, in_specs, out_specs, out_shape, operands = [0], [], [], [], []
    for x, p, an, cn in groups:
        ng, rows, d_model = x.shape
        assert rows % tm == 0
        n_t = rows // tm
        first, count = starts[-1], ng * n_t
        starts.append(first + count)

        def tile(i, first=first, count=count, n_t=n_t):
            j = jnp.clip(i - first, 0, count - 1)
            return (j // n_t, j % n_t, 0)

        in_specs += [pl.BlockSpec((1, tm, a.shape[-1]), tile) for a in (x, p, an, cn)]
        out_specs.append(pl.BlockSpec((1, tm, d_model), tile))
        out_shape.append(jax.ShapeDtypeStruct(x.shape, F32))
        operands += [x, p, an, cn]
    ff_chunk = 2048
    assert starts[1] > 0 and w1.shape[1] % ff_chunk == 0
    streamed = (w1, w2)
    return pl.pallas_call(
        functools.partial(_mix_kernel, starts=tuple(starts), ff_chunk=ff_chunk),
        grid=(starts[-1],),
        in_specs=in_specs + [pl.BlockSpec(memory_space=pl.ANY) if any(c is s for s in streamed)
                             else _const_spec(c.shape) for c in consts],
        out_specs=out_specs,
        out_shape=out_shape,
        scratch_shapes=[pltpu.VMEM(w1.shape, BF16), pltpu.VMEM(w2.shape, BF16),
                        pltpu.SemaphoreType.DMA((2, w1.shape[1] // ff_chunk))],
        compiler_params=pltpu.CompilerParams(
            dimension_semantics=("arbitrary",), vmem_limit_bytes=VMEM_LIMIT),
        name="mix",
    )(*operands, *consts)


def kernel(x_prompt, x_sample, p_prompt, p_sample, cache_att_k, cache_att_v, state_conv, g_mix, w_in, w_dw,
           b_dw, g_conv_ln, b_conv_ln, g_q, g_k, rel_bias, g_conv_out, g_att_out, w_out, g_ffn, w_ff1, w_ff2,
           g_ple, w_gate, w_ple):
    depth = w_in.shape[0]
    nb, t_len, d_model = x_prompt.shape
    sb, s_len, _ = x_sample.shape
    d_conv = w_dw.shape[-1]
    d_att = g_att_out.shape[-1]
    n_heads = d_att // HEAD_DIM
    keep = min(LEFT_CHUNKS * CHUNK, t_len)
    tile = 512

    xp = x_prompt
    xs = x_sample.reshape(1, sb * s_len, d_model)
    outs = [[] for _ in range(6)]
    for i in range(depth):
        vec = lambda a: a[i].reshape(1, -1)
        gq = jnp.tile(g_q[i], n_heads).reshape(1, d_att)
        gk = jnp.tile(g_k[i], n_heads).reshape(1, d_att)
        conv_w = (w_dw[i], vec(b_dw), vec(g_conv_ln), vec(b_conv_ln), vec(g_conv_out))

        (cn_p, u_tail, q_p, kt_p, vb_p, k_tail, v_tail, w_out_b, w_ff1_b, w_ff2_b, w_gate_b, bias,
         w_in_b, w_ple_b) = _in_proj(
            xp, vec(g_mix), w_in[i], gq, gk, conv_w, tm=tile, keep=keep,
            cast=(w_out[i], w_ff1[i], w_ff2[i], w_gate[i]), cast_once=(w_ple[i],), rel_bias=rel_bias[i])
        mix_w = (w_out_b, vec(g_ffn), w_ff1_b, w_ff2_b, vec(g_ple), w_gate_b, w_ple_b)
        an_p = _attn_prompt(q_p, kt_p, vb_p, bias, vec(g_att_out), tq=tile)
        outs[0].append(k_tail.reshape(nb, keep, n_heads, HEAD_DIM))
        outs[1].append(v_tail.reshape(nb, keep, n_heads, HEAD_DIM))
        outs[2].append(u_tail[:, HALO - (CONV_WIDTH - 1):])

        rows = sb * s_len
        cn_s, u_s, q_s, k_s, v_s = _in_proj(xs, vec(g_mix), w_in_b, gq, gk, conv_w, tm=rows, keep=rows,
                                            state=state_conv[i])
        per_b = lambda a: a.reshape(sb, s_len, a.shape[-1])
        ck = cache_att_k[i].reshape(sb, -1, d_att).astype(BF16)
        cv = cache_att_v[i].reshape(sb, -1, d_att).astype(BF16)
        an_s = _attn_sample(per_b(q_s), ck, cv, per_b(k_s), per_b(v_s), bias, vec(g_att_out))
        outs[3].append(k_s.reshape(sb, s_len, n_heads, HEAD_DIM))
        outs[4].append(v_s.reshape(sb, s_len, n_heads, HEAD_DIM))
        conv_in_tail = jnp.concatenate([state_conv[i], per_b(u_s)], axis=1)[:, -(CONV_WIDTH - 1):]
        outs[5].append(conv_in_tail)

        xp, xs = _mix([(xp, p_prompt[i], an_p, cn_p),
                       (xs, p_sample[i].reshape(1, rows, -1), an_s.reshape(1, rows, d_att), cn_s)],
                      *mix_w, tm=tile)

    return (xp, xs.reshape(sb, s_len, d_model)) + tuple(jnp.stack(o) for o in outs)
```

```python
import functools

import jax
import jax.numpy as jnp
import numpy as np
from jax import lax
from jax.experimental import pallas as pl
from jax.experimental.pallas import tpu as pltpu

F32 = jnp.float32
BF16 = jnp.bfloat16

CHUNK = 64
LEFT_CHUNKS = 8
HEAD_DIM = 64
CONV_WIDTH = 31
MAX_REL = 128
EPS = 1e-6
NEG = -1e30
LOG2_E = 1.4426950408889634

LANES = 128
SUBLANES = 8
MXU_TILE = 256
PAIR = 2 * CHUNK
WIN = (LEFT_CHUNKS + 2) * CHUNK
WIN_SLABS = WIN // LANES
HALO = 32
CONV_HOP = 128
SOFTMAX_ROWS = 32
VMEM_LIMIT = 56 * 1024 * 1024


def _rms(x, g):
    return x * lax.rsqrt(jnp.mean(x * x, axis=-1, keepdims=True) + EPS) * g


def _const_spec(shape):
    zeros = (0,) * len(shape)
    return pl.BlockSpec(shape, lambda *_: zeros, pipeline_mode=pl.Buffered(1))


def _bias_kernel(rb_ref, o_ref):
    assert rb_ref.shape[1] == 2 * MAX_REL + 1 and MAX_REL == LANES and CHUNK == LANES // 2
    assert LEFT_CHUNKS * CHUNK - MAX_REL == 3 * LANES and WIN_SLABS == 5
    n_heads = rb_ref.shape[0]
    rb = rb_ref[...]
    lane = lax.broadcasted_iota(jnp.int32, (n_heads, LANES), 1)
    b0 = jnp.broadcast_to(rb[:, 0:1], (n_heads, LANES))
    t4 = jnp.where(lane < CHUNK, rb[:, LANES:2 * LANES], b0)
    t = jnp.concatenate([b0, b0, b0, rb[:, 0:LANES], t4, b0], axis=1)
    r_i = lax.broadcasted_iota(jnp.int32, (PAIR, LANES), 0)
    c_i = lax.broadcasted_iota(jnp.int32, (PAIR, LANES), 1)
    for h in range(n_heads):
        row = jnp.broadcast_to(t[h:h + 1, :], (PAIR, t.shape[1]))
        toe = pltpu.roll(row, 0, 1, stride=1, stride_axis=0)
        for c in range(WIN_SLABS):
            jj = c_i + c * LANES
            valid = ((r_i < CHUNK) & (jj < WIN - CHUNK)) | ((r_i >= CHUNK) & (jj >= CHUNK))
            o_ref[h, c] = jnp.where(valid, toe[:, c * LANES:(c + 1) * LANES] * LOG2_E, NEG)


def _dft_tables(n, hop):
    half = n // 2
    t = np.arange(n)[None, :]
    f = np.arange(half)[:, None]
    ang = 2.0 * np.pi * f * t / n
    fwd = np.concatenate([np.cos(ang), np.cos(np.pi * t), np.sin(ang[1:])], axis=0)
    rows = np.arange(n - hop, n)[:, None]
    ang_o = 2.0 * np.pi * rows * f.T / n
    scale = np.where(f.T == 0, 1.0, 2.0) / n
    inv = np.concatenate([scale * np.cos(ang_o), np.cos(np.pi * rows) / n, (2.0 / n) * np.sin(ang_o[:, 1:])], axis=1)
    delay = (CONV_WIDTH - 1) - np.arange(CONV_WIDTH)[None, :]
    ang_h = 2.0 * np.pi * f * delay / n
    p, q = np.cos(ang_h), np.sin(ang_h)
    r = p.copy()
    r[0] = np.cos(np.pi * delay[0])
    return (jnp.asarray(fwd, F32), jnp.asarray(inv, F32), jnp.asarray(np.stack([p, q, r]), F32))


def _filter_response(wdw_ref, resp_ref, h_scr):
    for k in range(3):
        acc = None
        for j in range(CONV_WIDTH):
            term = resp_ref[k, :, j:j + 1] * wdw_ref[j:j + 1, :]
            acc = term if acc is None else acc + term
        h_scr[k] = acc


def _conv_stage(u, n_seg, halo_ref, ubuf, cn_ref, h_scr, fwd_ref, inv_ref,
                bdw_ref, gln_ref, bln_ref, gco_ref):
    d_conv = u.shape[-1]
    seg = u.shape[0] // n_seg
    n, hop = fwd_ref.shape[0], inv_ref.shape[0]
    half = n // 2
    assert n - hop == HALO and HALO >= CONV_WIDTH - 1 and seg % hop == 0
    for s in range(n_seg):
        if halo_ref is not None:
            hr = halo_ref.shape[1]
            if hr < HALO:
                ubuf[s, 0:HALO - hr, :] = jnp.zeros((HALO - hr, d_conv), F32)
            ubuf[s, HALO - hr:HALO, :] = halo_ref[s]
        ubuf[s, HALO:HALO + seg, :] = u[s * seg:(s + 1) * seg, :]

    blocks = [(s, i) for s in range(n_seg) for i in range(seg // hop)]
    windows = jnp.concatenate([ubuf[s, hop * i:hop * i + n, :].astype(BF16) for s, i in blocks], axis=1)
    spec = jnp.dot(fwd_ref[...].astype(BF16), windows, preferred_element_type=F32)
    p, q, r = h_scr[0], h_scr[1], h_scr[2]
    prods = []
    for k in range(len(blocks)):
        top = spec[0:half, k * d_conv:(k + 1) * d_conv]
        bot = spec[half:n, k * d_conv:(k + 1) * d_conv]
        prods.append(jnp.concatenate([p * top - q * bot, r * bot + q * top], axis=0).astype(BF16))
    y = jnp.dot(inv_ref[...].astype(BF16), jnp.concatenate(prods, axis=1),
                preferred_element_type=F32)
    for k, (s, i) in enumerate(blocks):
        yk = y[:, k * d_conv:(k + 1) * d_conv] + bdw_ref[...]
        mu = jnp.mean(yk, axis=-1, keepdims=True)
        yc = yk - mu
        yn = yc * lax.rsqrt(jnp.mean(yc * yc, axis=-1, keepdims=True) + EPS) * gln_ref[...] + bln_ref[...]
        c = yn * jax.nn.sigmoid(yn)
        cn_ref[s * seg + i * hop:s * seg + (i + 1) * hop, :] = _rms(c, gco_ref[...]).astype(BF16)


def _inproj_kernel(*refs, d_conv, d_att, tail0, prompt, n_cast, n_once):
    if prompt:
        x_ref, gmix_ref, win_ref, gq_ref, gk_ref = refs[:5]
        halo_ref, conv_w = None, refs[5:13]
        cast_in, rb_ref = refs[13:13 + n_cast], refs[13 + n_cast]
        once_in = (win_ref,) + refs[14 + n_cast:14 + n_cast + n_once]
        outs = refs[14 + n_cast + n_once:]
        cn_ref, ut_ref, q_ref, kt_ref, vb_ref, k32_ref, v32_ref = outs[:7]
        cast_out, bias_ref = outs[7:7 + n_cast], outs[7 + n_cast]
        once_out = outs[8 + n_cast:9 + n_cast + n_once]
        ubuf, h_scr = outs[9 + n_cast + n_once:]
        win_ref = once_out[0]
        t = pl.program_id(1)
        tm = x_ref.shape[1]

        for src, dst in zip(cast_in, cast_out):
            dst[...] = src[...].astype(BF16)

        @pl.when((pl.program_id(0) == 0) & (t == 0))
        def _():
            for src, dst in zip(once_in, once_out):
                for lo in range(0, src.shape[1], MXU_TILE):
                    dst[:, lo:lo + MXU_TILE] = src[:, lo:lo + MXU_TILE].astype(BF16)
            _bias_kernel(rb_ref, bias_ref)
            _filter_response(conv_w[0], conv_w[7], h_scr)

        @pl.when(t == 0)
        def _():
            ubuf[0, 0:HALO, :] = jnp.zeros((HALO, d_conv), F32)

        @pl.when(t > 0)
        def _():
            ubuf[0, 0:HALO, :] = ubuf[0, tm:tm + HALO, :]
    else:
        x_ref, gmix_ref, win_ref, gq_ref, gk_ref, halo_ref = refs[:6]
        conv_w = refs[6:14]
        cn_ref, u_ref, q_ref, k32_ref, v32_ref, ubuf, h_scr = refs[14:]
        tm = x_ref.shape[1]
        _filter_response(conv_w[0], conv_w[7], h_scr)
    h = _rms(x_ref[0], gmix_ref[...]).astype(BF16)

    def proj(lo, width):
        return jnp.dot(h, win_ref[:, lo:lo + width], preferred_element_type=F32)

    u = proj(0, d_conv) * jax.nn.sigmoid(proj(d_conv, d_conv))
    if prompt:
        @pl.when(t == pl.num_programs(1) - 1)
        def _():
            ut_ref[0] = u[tm - HALO:tm, :]
    else:
        u_ref[0] = u
    wdw_ref, bdw_ref, gln_ref, bln_ref, gco_ref, fwd_ref, inv_ref, _ = conv_w
    _conv_stage(u, ubuf.shape[0], halo_ref, ubuf, cn_ref.at[0], h_scr, fwd_ref, inv_ref,
                bdw_ref, gln_ref, bln_ref, gco_ref)

    head_bits = HEAD_DIM.bit_length() - 1
    r_i = lax.shift_right_logical(lax.broadcasted_iota(jnp.int32, (MXU_TILE, MXU_TILE), 0), head_bits)
    c_i = lax.shift_right_logical(lax.broadcasted_iota(jnp.int32, (MXU_TILE, MXU_TILE), 1), head_bits)
    avg = jnp.where(r_i == c_i, 1.0 / HEAD_DIM, 0.0).astype(BF16)

    def head_rms(z, g):
        sq = (z * z).astype(BF16)
        ms = jnp.concatenate([jnp.dot(sq[:, lo:lo + MXU_TILE], avg, preferred_element_type=F32)
                              for lo in range(0, d_att, MXU_TILE)], axis=1)
        return z * lax.rsqrt(ms + EPS) * g

    o = 2 * d_conv
    qn = head_rms(proj(o, d_att), gq_ref[...])
    q_ref[0] = (qn * (HEAD_DIM ** -0.5 * LOG2_E)).astype(BF16)
    kn = head_rms(proj(o + d_att, d_att), gk_ref[...])
    v = proj(o + 2 * d_att, d_att)
    if prompt:
        even_row = (lax.broadcasted_iota(jnp.int32, (d_att, LANES), 0) & HEAD_DIM) == 0
        for s in range(tm // LANES):
            kt = kn[s * LANES:(s + 1) * LANES, :].T
            kt_ref[0, 0, s] = jnp.where(even_row, kt, 0.0).astype(BF16)
            kt_ref[0, 1, s] = jnp.where(even_row, 0.0, kt).astype(BF16)
        even = (lax.broadcasted_iota(jnp.int32, (tm, d_att), 1) & HEAD_DIM) == 0
        vb_ref[0, 0] = jnp.where(even, v, 0.0).astype(BF16)
        vb_ref[0, 1] = jnp.where(even, 0.0, v).astype(BF16)

    @pl.when(pl.program_id(1) >= tail0)
    def _():
        k32_ref[0] = kn
        v32_ref[0] = v


def _in_proj(x, g_mix, w_in, g_q, g_k, conv_w, *, tm, keep, state=None, cast=(), cast_once=(), rel_bias=None):
    nb, t_len, d_model = x.shape
    d_att = g_q.shape[-1]
    d_in = w_in.shape[-1]
    d_conv = (d_in - 3 * d_att) // 2
    n_t = t_len // tm
    prompt = state is None
    assert t_len % tm == 0 and (t_len - keep) % tm == 0 and tm % LANES == 0 and tm >= HALO
    tail0 = (t_len - keep) // tm
    row = lambda b, t: (b, t, 0)
    tail = lambda b, t: (b, jnp.maximum(t - tail0, 0), 0)
    halves = pl.BlockSpec((1, 2, tm, d_att), lambda b, t: (b, 0, t, 0))
    out_shape = [jax.ShapeDtypeStruct((nb, t_len, d_conv), BF16)]
    out_specs = [pl.BlockSpec((1, tm, d_conv), row)]
    if prompt:
        n_seg = 1
        out_shape += [jax.ShapeDtypeStruct((nb, HALO, d_conv), F32)]
        out_specs += [pl.BlockSpec((1, HALO, d_conv), lambda b, t: (b, 0, 0))]
    else:
        n_seg = state.shape[0]
        assert nb == 1 and n_t == 1 and tm % n_seg == 0 and state.shape[1] == CONV_WIDTH - 1
        out_shape += [jax.ShapeDtypeStruct((nb, t_len, d_conv), F32)]
        out_specs += [pl.BlockSpec((1, tm, d_conv), row)]
    out_shape += [jax.ShapeDtypeStruct((nb, t_len, d_att), BF16)]
    out_specs += [pl.BlockSpec((1, tm, d_att), row)]
    if prompt:
        out_shape += [jax.ShapeDtypeStruct((nb, 2, t_len // LANES, d_att, LANES), BF16),
                      jax.ShapeDtypeStruct((nb, 2, t_len, d_att), BF16)]
        out_specs += [pl.BlockSpec((1, 2, tm // LANES, d_att, LANES), lambda b, t: (b, 0, t, 0, 0)), halves]
    out_shape += [jax.ShapeDtypeStruct((nb, keep, d_att), F32)] * 2
    out_specs += [pl.BlockSpec((1, tm, d_att), tail)] * 2
    cast_specs = []
    for w in cast:
        rb = w.shape[0] // (nb * n_t)
        assert prompt and w.ndim == 2 and rb * nb * n_t == w.shape[0] and rb % (2 * SUBLANES) == 0
        cast_specs.append(pl.BlockSpec((rb, w.shape[1]), lambda b, t: (b * n_t + t, 0)))
        out_shape.append(jax.ShapeDtypeStruct(w.shape, BF16))
    out_specs += cast_specs
    side_specs, side_in = list(cast_specs), list(cast)
    if prompt:
        bias_shape = (rel_bias.shape[0], WIN_SLABS, PAIR, LANES)
        side_specs.append(_const_spec(rel_bias.shape))
        side_in.append(rel_bias)
        out_shape.append(jax.ShapeDtypeStruct(bias_shape, F32))
        out_specs.append(pl.BlockSpec(bias_shape, lambda b, t: (0, 0, 0, 0)))
        for w in (w_in,) + tuple(cast_once):
            assert w.ndim == 2 and w.shape[1] % MXU_TILE == 0
            out_shape.append(jax.ShapeDtypeStruct(w.shape, BF16))
            out_specs.append(pl.BlockSpec(w.shape, lambda b, t: (0, 0)))
        side_specs += [_const_spec(w.shape) for w in cast_once]
        side_in += list(cast_once)
    else:
        assert not cast_once
    kern = functools.partial(_inproj_kernel, d_conv=d_conv, d_att=d_att, tail0=tail0, prompt=prompt,
                             n_cast=len(cast), n_once=len(cast_once))
    extra = [] if prompt else [state]
    hop = min(CONV_HOP, tm // n_seg)
    conv_w = tuple(conv_w) + _dft_tables(HALO + hop, hop)
    return pl.pallas_call(
        kern,
        grid=(nb, n_t),
        in_specs=[pl.BlockSpec((1, tm, d_model), row),
                  _const_spec((1, d_model)), _const_spec((d_model, d_in)),
                  _const_spec((1, d_att)), _const_spec((1, d_att))]
                 + [_const_spec(a.shape) for a in extra + list(conv_w)] + side_specs,
        out_specs=out_specs,
        out_shape=out_shape,
        scratch_shapes=[pltpu.VMEM((n_seg, HALO + tm // n_seg, d_conv), F32),
                        pltpu.VMEM((3, (HALO + hop) // 2, d_conv), F32)],
        compiler_params=pltpu.CompilerParams(
            dimension_semantics=("arbitrary", "arbitrary"), vmem_limit_bytes=VMEM_LIMIT),
        name="in_proj",
    )(x, g_mix, w_in, g_q, g_k, *extra, *conv_w, *side_in)


def _attn_scratch(n_heads, slots):
    return [pltpu.VMEM((slots, n_heads // 2, PAIR, 2 * WIN), F32),
            pltpu.VMEM((slots, n_heads // 2, PAIR, 2 * WIN), BF16)]


def _attend(rows, n_slabs, get_q, get_k, get_v, bias_ref, off, s_scr, p_scr):
    n_pairs = s_scr.shape[0]
    width = n_slabs * LANES
    for hp in range(n_pairs):
        s_scr[hp, 0:rows, 0:2 * width] = jnp.dot(get_q(hp), get_k(hp), preferred_element_type=F32)
    for hp in range(n_pairs):
        for sub in range(2):
            h = 2 * hp + sub
            for r0 in range(0, rows, SOFTMAX_ROWS):
                r = slice(r0, r0 + SOFTMAX_ROWS)
                s = s_scr[hp, r, sub * width:(sub + 1) * width]
                s = s + jnp.concatenate([bias_ref[h, off + c, r, :] for c in range(n_slabs)], axis=1)
                e = jnp.exp2(s - jnp.max(s, axis=-1, keepdims=True))
                p_scr[hp, r, sub * width:(sub + 1) * width] = e.astype(BF16)
    first = lax.broadcasted_iota(jnp.int32, (width, LANES), 1) < HEAD_DIM
    sums = jnp.concatenate([jnp.where(first, 1.0, 0.0), jnp.where(first, 0.0, 1.0)], axis=0).astype(BF16)
    outs = []
    for hp in range(n_pairs):
        o2 = jnp.dot(p_scr[hp, 0:rows, 0:2 * width], jnp.concatenate([get_v(hp), sums], axis=1),
                     preferred_element_type=F32)
        outs.append(o2[:, 0:LANES] * (1.0 / o2[:, LANES:2 * LANES]))
    return jnp.concatenate(outs, axis=1)


def _attn_prompt_kernel(q_ref, kt_ref, v_ref, bias_ref, g_ref, o_ref, s_scr, p_scr):
    pairs = q_ref.shape[1] // PAIR
    slots = s_scr.shape[0]
    t = pl.program_id(1)

    def run_pairs(first_step):
        for i in range(pairs):
            p = i if first_step else t * pairs + i
            n_slabs = min(i + 1, WIN_SLABS) if first_step else WIN_SLABS
            ws = max(p - (WIN_SLABS - 1), 0) if first_step else p - (WIN_SLABS - 1)
            off = ws + (WIN_SLABS - 1) - p
            k0 = ws * LANES if first_step else pl.multiple_of(ws * LANES, LANES)
            width = n_slabs * LANES

            def get_k(hp, ws=ws, n_slabs=n_slabs):
                return jnp.concatenate(
                    [kt_ref[0, e, ws + c, hp * LANES:(hp + 1) * LANES, :]
                     for e in range(2) for c in range(n_slabs)], axis=1)

            def get_v(hp, k0=k0, width=width):
                lanes = slice(hp * LANES, (hp + 1) * LANES)
                return jnp.concatenate([v_ref[0, 0, pl.ds(k0, width), lanes],
                                        v_ref[0, 1, pl.ds(k0, width), lanes]], axis=0)

            rows = slice(i * PAIR, (i + 1) * PAIR)

            def get_q(hp, rows=rows):
                return q_ref[0, rows, hp * LANES:(hp + 1) * LANES]

            k = i % slots
            a = _attend(PAIR, n_slabs, get_q, get_k, get_v, bias_ref, off, s_scr.at[k], p_scr.at[k])
            o_ref[0, rows, :] = _rms(a, g_ref[...]).astype(BF16)

    pl.when(t == 0)(functools.partial(run_pairs, True))
    pl.when(t > 0)(functools.partial(run_pairs, False))


def _attn_prompt(q, kt, v, bias, g_att, *, tq):
    nb, t_len, d_att = q.shape
    n_heads = d_att // HEAD_DIM
    assert t_len % tq == 0 and tq % PAIR == 0 and tq // PAIR >= WIN_SLABS - 1
    return pl.pallas_call(
        _attn_prompt_kernel,
        grid=(nb, t_len // tq),
        in_specs=[pl.BlockSpec((1, tq, d_att), lambda b, t: (b, t, 0)),
                  pl.BlockSpec((1, 2, t_len // LANES, d_att, LANES), lambda b, t: (b, 0, 0, 0, 0)),
                  pl.BlockSpec((1, 2, t_len, d_att), lambda b, t: (b, 0, 0, 0)),
                  _const_spec(bias.shape), _const_spec((1, d_att))],
        out_specs=pl.BlockSpec((1, tq, d_att), lambda b, t: (b, t, 0)),
        out_shape=jax.ShapeDtypeStruct((nb, t_len, d_att), BF16),
        scratch_shapes=_attn_scratch(n_heads, slots=2),
        compiler_params=pltpu.CompilerParams(
            dimension_semantics=("arbitrary", "arbitrary"), vmem_limit_bytes=VMEM_LIMIT),
        name="attn_prompt",
    )(q, kt, v, bias, g_att)


def _attn_sample_kernel(q_ref, ck_ref, cv_ref, kn_ref, vn_ref, bias_ref, g_ref, o_ref, kt_s, v_s,
                        s_scr, p_scr):
    d_att = q_ref.shape[-1]
    pad = jnp.zeros((WIN - ck_ref.shape[1] - kn_ref.shape[1], d_att), F32)
    k_all = jnp.concatenate([ck_ref[0].astype(F32), kn_ref[0], pad], axis=0)
    even_row = (lax.broadcasted_iota(jnp.int32, (d_att, LANES), 0) & HEAD_DIM) == 0
    for c in range(WIN_SLABS):
        kt = k_all[c * LANES:(c + 1) * LANES, :].T
        kt_s[0, c] = jnp.where(even_row, kt, 0.0).astype(BF16)
        kt_s[1, c] = jnp.where(even_row, 0.0, kt).astype(BF16)
    v_all = jnp.concatenate([cv_ref[0].astype(F32), vn_ref[0], pad], axis=0)
    even = (lax.broadcasted_iota(jnp.int32, v_all.shape, 1) & HEAD_DIM) == 0
    v_s[0] = jnp.where(even, v_all, 0.0).astype(BF16)
    v_s[1] = jnp.where(even, 0.0, v_all).astype(BF16)

    def get_k(hp):
        return jnp.concatenate(
            [kt_s[e, c, hp * LANES:(hp + 1) * LANES, :] for e in range(2) for c in range(WIN_SLABS)], axis=1)

    def get_v(hp):
        lanes = slice(hp * LANES, (hp + 1) * LANES)
        return jnp.concatenate([v_s[0, :, lanes], v_s[1, :, lanes]], axis=0)

    def get_q(hp):
        return q_ref[0, :, hp * LANES:(hp + 1) * LANES]

    a = _attend(q_ref.shape[1], WIN_SLABS, get_q, get_k, get_v, bias_ref, 0, s_scr.at[0], p_scr.at[0])
    o_ref[0] = _rms(a, g_ref[...]).astype(BF16)


def _attn_sample(q, cache_k, cache_v, k_new, v_new, bias, g_att):
    nb, s_len, d_att = k_new.shape
    n_heads = d_att // HEAD_DIM
    l_cache = cache_k.shape[1]
    assert s_len == CHUNK and l_cache == LEFT_CHUNKS * CHUNK
    per_b = lambda b: (b, 0, 0)
    return pl.pallas_call(
        _attn_sample_kernel,
        grid=(nb,),
        in_specs=[pl.BlockSpec((1, s_len, d_att), per_b),
                  pl.BlockSpec((1, l_cache, d_att), per_b), pl.BlockSpec((1, l_cache, d_att), per_b),
                  pl.BlockSpec((1, s_len, d_att), per_b), pl.BlockSpec((1, s_len, d_att), per_b),
                  _const_spec(bias.shape), _const_spec((1, d_att))],
        out_specs=pl.BlockSpec((1, s_len, d_att), per_b),
        out_shape=jax.ShapeDtypeStruct((nb, s_len, d_att), BF16),
        scratch_shapes=[pltpu.VMEM((2, WIN_SLABS, d_att, LANES), BF16), pltpu.VMEM((2, WIN, d_att), BF16)]
                       + _attn_scratch(n_heads, slots=1),
        compiler_params=pltpu.CompilerParams(
            dimension_semantics=("arbitrary",), vmem_limit_bytes=VMEM_LIMIT),
        name="attn_sample",
    )(q, cache_k, cache_v, k_new, v_new, bias, g_att)


def _dense_stage(x, cn, an, p, wout_ref, gffn_ref, w1_ref, w2_ref, gple_ref, wg_ref, wple_ref, ff_chunk,
                 before_chunk):
    d_conv = cn.shape[-1]
    x1 = (x + jnp.dot(cn, wout_ref[0:d_conv, :], preferred_element_type=F32)
          + jnp.dot(an, wout_ref[d_conv:, :], preferred_element_type=F32))
    hb = _rms(x1, gffn_ref[...]).astype(BF16)
    ffn = None
    for lo in range(0, w1_ref.shape[1], ff_chunk):
        before_chunk(lo // ff_chunk)
        a = jnp.maximum(jnp.dot(hb, w1_ref[:, lo:lo + ff_chunk], preferred_element_type=F32), 0.0)
        part = jnp.dot((a * a).astype(BF16), w2_ref[lo:lo + ff_chunk, :], preferred_element_type=F32)
        ffn = part if ffn is None else ffn + part
    x2 = x1 + ffn
    gate = jax.nn.sigmoid(jnp.dot(_rms(x2, gple_ref[...]).astype(BF16), wg_ref[...],
                                  preferred_element_type=F32))
    ple = jnp.dot(p.astype(BF16), wple_ref[...], preferred_element_type=F32)
    return x2 + ple * gate


def _mix_kernel(*refs, starts, ff_chunk):
    n_groups = len(starts) - 1
    ins, dense_w = refs[:4 * n_groups], list(refs[4 * n_groups:4 * n_groups + 7])
    outs, (w1_s, w2_s, sem) = refs[4 * n_groups + 7:5 * n_groups + 7], refs[5 * n_groups + 7:]
    w1_hbm, w2_hbm = dense_w[2], dense_w[3]
    dense_w[2], dense_w[3] = w1_s, w2_s
    i = pl.program_id(0)

    def copies(c):
        cols = slice(c * ff_chunk, (c + 1) * ff_chunk)
        return (pltpu.make_async_copy(w1_hbm.at[:, cols], w1_s.at[:, cols], sem.at[0, c]),
                pltpu.make_async_copy(w2_hbm.at[cols, :], w2_s.at[cols, :], sem.at[1, c]))

    def wait_chunk(c):
        for cp in copies(c):
            cp.wait()

    def run(g, before_chunk):
        x_ref, p_ref, an_ref, cn_ref = ins[4 * g:4 * g + 4]
        outs[g][0] = _dense_stage(x_ref[0], cn_ref[0], an_ref[0], p_ref[0], *dense_w, ff_chunk, before_chunk)

    @pl.when(i == 0)
    def _():
        for c in range(w1_s.shape[1] // ff_chunk):
            for cp in copies(c):
                cp.start()
        run(0, wait_chunk)

    for g in range(n_groups):
        @pl.when((i >= max(starts[g], 1)) & (i < starts[g + 1]))
        def _():
            run(g, lambda c: None)


def _mix(groups, w_out, g_ffn, w1, w2, g_ple, w_gate, w_ple, *, tm):
    consts = [w_out, g_ffn, w1, w2, g_ple, w_gate, w_ple]
    starts, in_specs, out_specs, out_shape, operands = [0], [], [], [], []
    for x, p, an, cn in groups:
        ng, rows, d_model = x.shape
        assert rows % tm == 0
        n_t = rows // tm
        first, count = starts[-1], ng * n_t
        starts.append(first + count)

        def tile(i, first=first, count=count, n_t=n_t):
            j = jnp.clip(i - first, 0, count - 1)
            return (j // n_t, j % n_t, 0)

        in_specs += [pl.BlockSpec((1, tm, a.shape[-1]), tile) for a in (x, p, an, cn)]
        out_specs.append(pl.BlockSpec((1, tm, d_model), tile))
        out_shape.append(jax.ShapeDtypeStruct(x.shape, F32))
        operands += [x, p, an, cn]
    ff_chunk = 2048
    assert starts[1] > 0 and w1.shape[1] % ff_chunk == 0
    streamed = (w1, w2)
    return pl.pallas_call(
        functools.partial(_mix_kernel, starts=tuple(starts), ff_chunk=ff_chunk),
        grid=(starts[-1],),
        in_specs=in_specs + [pl.BlockSpec(memory_space=pl.ANY) if any(c is s for s in streamed)
                             else _const_spec(c.shape) for c in consts],
        out_specs=out_specs,
        out_shape=out_shape,
        scratch_shapes=[pltpu.VMEM(w1.shape, BF16), pltpu.VMEM(w2.shape, BF16),
                        pltpu.SemaphoreType.DMA((2, w1.shape[1] // ff_chunk))],
        compiler_params=pltpu.CompilerParams(
            dimension_semantics=("arbitrary",), vmem_limit_bytes=VMEM_LIMIT),
        name="mix",
    )(*operands, *consts)


def kernel(x_prompt, x_sample, p_prompt, p_sample, cache_att_k, cache_att_v, state_conv, g_mix, w_in, w_dw,
           b_dw, g_conv_ln, b_conv_ln, g_q, g_k, rel_bias, g_conv_out, g_att_out, w_out, g_ffn, w_ff1, w_ff2,
           g_ple, w_gate, w_ple):
    depth = w_in.shape[0]
    nb, t_len, d_model = x_prompt.shape
    sb, s_len, _ = x_sample.shape
    d_conv = w_dw.shape[-1]
    d_att = g_att_out.shape[-1]
    n_heads = d_att // HEAD_DIM
    keep = min(LEFT_CHUNKS * CHUNK, t_len)
    tile = 512

    xp = x_prompt
    xs = x_sample.reshape(1, sb * s_len, d_model)
    outs = [[] for _ in range(6)]
    for i in range(depth):
        vec = lambda a: a[i].reshape(1, -1)
        gq = jnp.tile(g_q[i], n_heads).reshape(1, d_att)
        gk = jnp.tile(g_k[i], n_heads).reshape(1, d_att)
        conv_w = (w_dw[i], vec(b_dw), vec(g_conv_ln), vec(b_conv_ln), vec(g_conv_out))

        (cn_p, u_tail, q_p, kt_p, vb_p, k_tail, v_tail, w_out_b, w_ff1_b, w_ff2_b, w_gate_b, bias,
         w_in_b, w_ple_b) = _in_proj(
            xp, vec(g_mix), w_in[i], gq, gk, conv_w, tm=tile, keep=keep,
            cast=(w_out[i], w_ff1[i], w_ff2[i], w_gate[i]), cast_once=(w_ple[i],), rel_bias=rel_bias[i])
        mix_w = (w_out_b, vec(g_ffn), w_ff1_b, w_ff2_b, vec(g_ple), w_gate_b, w_ple_b)
        an_p = _attn_prompt(q_p, kt_p, vb_p, bias, vec(g_att_out), tq=tile)
        outs[0].append(k_tail.reshape(nb, keep, n_heads, HEAD_DIM))
        outs[1].append(v_tail.reshape(nb, keep, n_heads, HEAD_DIM))
        outs[2].append(u_tail[:, HALO - (CONV_WIDTH - 1):])

        rows = sb * s_len
        cn_s, u_s, q_s, k_s, v_s = _in_proj(xs, vec(g_mix), w_in_b, gq, gk, conv_w, tm=rows, keep=rows,
                                            state=state_conv[i])
        per_b = lambda a: a.reshape(sb, s_len, a.shape[-1])
        ck = cache_att_k[i].reshape(sb, -1, d_att).astype(BF16)
        cv = cache_att_v[i].reshape(sb, -1, d_att).astype(BF16)
        an_s = _attn_sample(per_b(q_s), ck, cv, per_b(k_s), per_b(v_s), bias, vec(g_att_out))
        outs[3].append(k_s.reshape(sb, s_len, n_heads, HEAD_DIM))
        outs[4].append(v_s.reshape(sb, s_len, n_heads, HEAD_DIM))
        conv_in_tail = jnp.concatenate([state_conv[i], per_b(u_s)], axis=1)[:, -(CONV_WIDTH - 1):]
        outs[5].append(conv_in_tail)

        xp, xs = _mix([(xp, p_prompt[i], an_p, cn_p),
                       (xs, p_sample[i].reshape(1, rows, -1), an_s.reshape(1, rows, d_att), cn_s)],
                      *mix_w, tm=tile)

    return (xp, xs.reshape(sb, s_len, d_model)) + tuple(jnp.stack(o) for o in outs)
```

```python
import functools

import jax
import jax.numpy as jnp
import numpy as np
from jax import lax
from jax.experimental import pallas as pl
from jax.experimental.pallas import tpu as pltpu

F32 = jnp.float32
BF16 = jnp.bfloat16

CHUNK = 64
LEFT_CHUNKS = 8
HEAD_DIM = 64
CONV_WIDTH = 31
MAX_REL = 128
EPS = 1e-6
NEG = -1e30
LOG2_E = 1.4426950408889634

LANES = 128
SUBLANES = 8
MXU_TILE = 256
PAIR = 2 * CHUNK
WIN = (LEFT_CHUNKS + 2) * CHUNK
WIN_SLABS = WIN // LANES
HALO = 32
CONV_HOP = 128
SOFTMAX_ROWS = 32
VMEM_LIMIT = 56 * 1024 * 1024


def _rms(x, g):
    return x * lax.rsqrt(jnp.mean(x * x, axis=-1, keepdims=True) + EPS) * g


def _const_spec(shape):
    zeros = (0,) * len(shape)
    return pl.BlockSpec(shape, lambda *_: zeros, pipeline_mode=pl.Buffered(1))


def _bias_kernel(rb_ref, o_ref):
    assert rb_ref.shape[1] == 2 * MAX_REL + 1 and MAX_REL == LANES and CHUNK == LANES // 2
    assert LEFT_CHUNKS * CHUNK - MAX_REL == 3 * LANES and WIN_SLABS == 5
    n_heads = rb_ref.shape[0]
    rb = rb_ref[...]
    lane = lax.broadcasted_iota(jnp.int32, (n_heads, LANES), 1)
    b0 = jnp.broadcast_to(rb[:, 0:1], (n_heads, LANES))
    t4 = jnp.where(lane < CHUNK, rb[:, LANES:2 * LANES], b0)
    t = jnp.concatenate([b0, b0, b0, rb[:, 0:LANES], t4, b0], axis=1)
    r_i = lax.broadcasted_iota(jnp.int32, (PAIR, LANES), 0)
    c_i = lax.broadcasted_iota(jnp.int32, (PAIR, LANES), 1)
    for h in range(n_heads):
        row = jnp.broadcast_to(t[h:h + 1, :], (PAIR, t.shape[1]))
        toe = pltpu.roll(row, 0, 1, stride=1, stride_axis=0)
        for c in range(WIN_SLABS):
            jj = c_i + c * LANES
            valid = ((r_i < CHUNK) & (jj < WIN - CHUNK)) | ((r_i >= CHUNK) & (jj >= CHUNK))
            o_ref[h, c] = jnp.where(valid, toe[:, c * LANES:(c + 1) * LANES] * LOG2_E, NEG)


def _dft_tables(n, hop):
    half = n // 2
    t = np.arange(n)[None, :]
    f = np.arange(half)[:, None]
    ang = 2.0 * np.pi * f * t / n
    fwd = np.concatenate([np.cos(ang), np.cos(np.pi * t), np.sin(ang[1:])], axis=0)
    rows = np.arange(n - hop, n)[:, None]
    ang_o = 2.0 * np.pi * rows * f.T / n
    scale = np.where(f.T == 0, 1.0, 2.0) / n
    inv = np.concatenate([scale * np.cos(ang_o), np.cos(np.pi * rows) / n, (2.0 / n) * np.sin(ang_o[:, 1:])], axis=1)
    delay = (CONV_WIDTH - 1) - np.arange(CONV_WIDTH)[None, :]
    ang_h = 2.0 * np.pi * f * delay / n
    p, q = np.cos(ang_h), np.sin(ang_h)
    r = p.copy()
    r[0] = np.cos(np.pi * delay[0])
    return (jnp.asarray(fwd, F32), jnp.asarray(inv, F32), jnp.asarray(np.stack([p, q, r]), F32))


def _filter_response(wdw_ref, resp_ref, h_scr):
    for k in range(3):
        acc = None
        for j in range(CONV_WIDTH):
            term = resp_ref[k, :, j:j + 1] * wdw_ref[j:j + 1, :]
            acc = term if acc is None else acc + term
        h_scr[k] = acc


def _conv_stage(u, n_seg, halo_ref, ubuf, cn_ref, h_scr, fwd_ref, inv_ref,
                bdw_ref, gln_ref, bln_ref, gco_ref):
    d_conv = u.shape[-1]
    seg = u.shape[0] // n_seg
    n, hop = fwd_ref.shape[0], inv_ref.shape[0]
    half = n // 2
    assert n - hop == HALO and HALO >= CONV_WIDTH - 1 and seg % hop == 0
    for s in range(n_seg):
        if halo_ref is not None:
            hr = halo_ref.shape[1]
            if hr < HALO:
                ubuf[s, 0:HALO - hr, :] = jnp.zeros((HALO - hr, d_conv), F32)
            ubuf[s, HALO - hr:HALO, :] = halo_ref[s]
        ubuf[s, HALO:HALO + seg, :] = u[s * seg:(s + 1) * seg, :]

    blocks = [(s, i) for s in range(n_seg) for i in range(seg // hop)]
    windows = jnp.concatenate([ubuf[s, hop * i:hop * i + n, :].astype(BF16) for s, i in blocks], axis=1)
    spec = jnp.dot(fwd_ref[...].astype(BF16), windows, preferred_element_type=F32)
    p, q, r = h_scr[0], h_scr[1], h_scr[2]
    prods = []
    for k in range(len(blocks)):
        top = spec[0:half, k * d_conv:(k + 1) * d_conv]
        bot = spec[half:n, k * d_conv:(k + 1) * d_conv]
        prods.append(jnp.concatenate([p * top - q * bot, r * bot + q * top], axis=0).astype(BF16))
    y = jnp.dot(inv_ref[...].astype(BF16), jnp.concatenate(prods, axis=1),
                preferred_element_type=F32)
    for k, (s, i) in enumerate(blocks):
        yk = y[:, k * d_conv:(k + 1) * d_conv] + bdw_ref[...]
        mu = jnp.mean(yk, axis=-1, keepdims=True)
        yc = yk - mu
        yn = yc * lax.rsqrt(jnp.mean(yc * yc, axis=-1, keepdims=True) + EPS) * gln_ref[...] + bln_ref[...]
        c = yn * jax.nn.sigmoid(yn)
        cn_ref[s * seg + i * hop:s * seg + (i + 1) * hop, :] = _rms(c, gco_ref[...]).astype(BF16)


def _inproj_kernel(*refs, d_conv, d_att, tail0, prompt, n_cast, n_once):
    if prompt:
        x_ref, gmix_ref, win_ref, gq_ref, gk_ref = refs[:5]
        halo_ref, conv_w = None, refs[5:13]
        cast_in, rb_ref = refs[13:13 + n_cast], refs[13 + n_cast]
        once_in = (win_ref,) + refs[14 + n_cast:14 + n_cast + n_once]
        outs = refs[14 + n_cast + n_once:]
        cn_ref, ut_ref, q_ref, kt_ref, vb_ref, k32_ref, v32_ref = outs[:7]
        cast_out, bias_ref = outs[7:7 + n_cast], outs[7 + n_cast]
        once_out = outs[8 + n_cast:9 + n_cast + n_once]
        ubuf, h_scr = outs[9 + n_cast + n_once:]
        win_ref = once_out[0]
        t = pl.program_id(1)
        tm = x_ref.shape[1]

        for src, dst in zip(cast_in, cast_out):
            dst[...] = src[...].astype(BF16)

        @pl.when((pl.program_id(0) == 0) & (t == 0))
        def _():
            for src, dst in zip(once_in, once_out):
                for lo in range(0, src.shape[1], MXU_TILE):
                    dst[:, lo:lo + MXU_TILE] = src[:, lo:lo + MXU_TILE].astype(BF16)
            _bias_kernel(rb_ref, bias_ref)
            _filter_response(conv_w[0], conv_w[7], h_scr)

        @pl.when(t == 0)
        def _():
            ubuf[0, 0:HALO, :] = jnp.zeros((HALO, d_conv), F32)

        @pl.when(t > 0)
        def _():
            ubuf[0, 0:HALO, :] = ubuf[0, tm:tm + HALO, :]
    else:
        x_ref, gmix_ref, win_ref, gq_ref, gk_ref, halo_ref = refs[:6]
        conv_w = refs[6:14]
        cn_ref, u_ref, q_ref, k32_ref, v32_ref, ubuf, h_scr = refs[14:]
        tm = x_ref.shape[1]
        _filter_response(conv_w[0], conv_w[7], h_scr)
    h = _rms(x_ref[0], gmix_ref[...]).astype(BF16)

    def proj(lo, width):
        return jnp.dot(h, win_ref[:, lo:lo + width], preferred_element_type=F32)

    u = proj(0, d_conv) * jax.nn.sigmoid(proj(d_conv, d_conv))
    if prompt:
        @pl.when(t == pl.num_programs(1) - 1)
        def _():
            ut_ref[0] = u[tm - HALO:tm, :]
    else:
        u_ref[0] = u
    wdw_ref, bdw_ref, gln_ref, bln_ref, gco_ref, fwd_ref, inv_ref, _ = conv_w
    _conv_stage(u, ubuf.shape[0], halo_ref, ubuf, cn_ref.at[0], h_scr, fwd_ref, inv_ref,
                bdw_ref, gln_ref, bln_ref, gco_ref)

    head_bits = HEAD_DIM.bit_length() - 1
    r_i = lax.shift_right_logical(lax.broadcasted_iota(jnp.int32, (MXU_TILE, MXU_TILE), 0), head_bits)
    c_i = lax.shift_right_logical(lax.broadcasted_iota(jnp.int32, (MXU_TILE, MXU_TILE), 1), head_bits)
    avg = jnp.where(r_i == c_i, 1.0 / HEAD_DIM, 0.0).astype(BF16)

    def head_rms(z, g):
        sq = (z * z).astype(BF16)
        ms = jnp.concatenate([jnp.dot(sq[:, lo:lo + MXU_TILE], avg, preferred_element_type=F32)
                              for lo in range(0, d_att, MXU_TILE)], axis=1)
        return z * lax.rsqrt(ms + EPS) * jnp.tile(g, (1, d_att // HEAD_DIM))

    o = 2 * d_conv
    qn = head_rms(proj(o, d_att), gq_ref[...])
    q_ref[0] = (qn * (HEAD_DIM ** -0.5 * LOG2_E)).astype(BF16)
    kn = head_rms(proj(o + d_att, d_att), gk_ref[...])
    v = proj(o + 2 * d_att, d_att)
    if prompt:
        even_row = (lax.broadcasted_iota(jnp.int32, (d_att, LANES), 0) & HEAD_DIM) == 0
        for s in range(tm // LANES):
            kt = kn[s * LANES:(s + 1) * LANES, :].T
            kt_ref[0, 0, s] = jnp.where(even_row, kt, 0.0).astype(BF16)
            kt_ref[0, 1, s] = jnp.where(even_row, 0.0, kt).astype(BF16)
        even = (lax.broadcasted_iota(jnp.int32, (tm, d_att), 1) & HEAD_DIM) == 0
        vb_ref[0, 0] = jnp.where(even, v, 0.0).astype(BF16)
        vb_ref[0, 1] = jnp.where(even, 0.0, v).astype(BF16)

    @pl.when(pl.program_id(1) >= tail0)
    def _():
        k32_ref[0] = kn
        v32_ref[0] = v


def _in_proj(x, g_mix, w_in, g_q, g_k, conv_w, *, tm, keep, state=None, cast=(), cast_once=(), rel_bias=None):
    nb, t_len, d_model = x.shape
    d_in = w_in.shape[-1]
    d_conv = conv_w[0].shape[-1]
    d_att = (d_in - 2 * d_conv) // 3
    assert g_q.shape == g_k.shape == (1, HEAD_DIM) and d_att % HEAD_DIM == 0
    n_t = t_len // tm
    prompt = state is None
    assert t_len % tm == 0 and (t_len - keep) % tm == 0 and tm % LANES == 0 and tm >= HALO
    tail0 = (t_len - keep) // tm
    row = lambda b, t: (b, t, 0)
    tail = lambda b, t: (b, jnp.maximum(t - tail0, 0), 0)
    halves = pl.BlockSpec((1, 2, tm, d_att), lambda b, t: (b, 0, t, 0))
    out_shape = [jax.ShapeDtypeStruct((nb, t_len, d_conv), BF16)]
    out_specs = [pl.BlockSpec((1, tm, d_conv), row)]
    if prompt:
        n_seg = 1
        out_shape += [jax.ShapeDtypeStruct((nb, HALO, d_conv), F32)]
        out_specs += [pl.BlockSpec((1, HALO, d_conv), lambda b, t: (b, 0, 0))]
    else:
        n_seg = state.shape[0]
        assert nb == 1 and n_t == 1 and tm % n_seg == 0 and state.shape[1] == CONV_WIDTH - 1
        out_shape += [jax.ShapeDtypeStruct((nb, t_len, d_conv), F32)]
        out_specs += [pl.BlockSpec((1, tm, d_conv), row)]
    out_shape += [jax.ShapeDtypeStruct((nb, t_len, d_att), BF16)]
    out_specs += [pl.BlockSpec((1, tm, d_att), row)]
    if prompt:
        out_shape += [jax.ShapeDtypeStruct((nb, 2, t_len // LANES, d_att, LANES), BF16),
                      jax.ShapeDtypeStruct((nb, 2, t_len, d_att), BF16)]
        out_specs += [pl.BlockSpec((1, 2, tm // LANES, d_att, LANES), lambda b, t: (b, 0, t, 0, 0)), halves]
    out_shape += [jax.ShapeDtypeStruct((nb, keep, d_att), F32)] * 2
    out_specs += [pl.BlockSpec((1, tm, d_att), tail)] * 2
    cast_specs = []
    for w in cast:
        rb = w.shape[0] // (nb * n_t)
        assert prompt and w.ndim == 2 and rb * nb * n_t == w.shape[0] and rb % (2 * SUBLANES) == 0
        cast_specs.append(pl.BlockSpec((rb, w.shape[1]), lambda b, t: (b * n_t + t, 0)))
        out_shape.append(jax.ShapeDtypeStruct(w.shape, BF16))
    out_specs += cast_specs
    side_specs, side_in = list(cast_specs), list(cast)
    if prompt:
        bias_shape = (rel_bias.shape[0], WIN_SLABS, PAIR, LANES)
        side_specs.append(_const_spec(rel_bias.shape))
        side_in.append(rel_bias)
        out_shape.append(jax.ShapeDtypeStruct(bias_shape, F32))
        out_specs.append(pl.BlockSpec(bias_shape, lambda b, t: (0, 0, 0, 0)))
        for w in (w_in,) + tuple(cast_once):
            assert w.ndim == 2 and w.shape[1] % MXU_TILE == 0
            out_shape.append(jax.ShapeDtypeStruct(w.shape, BF16))
            out_specs.append(pl.BlockSpec(w.shape, lambda b, t: (0, 0)))
        side_specs += [_const_spec(w.shape) for w in cast_once]
        side_in += list(cast_once)
    else:
        assert not cast_once
    kern = functools.partial(_inproj_kernel, d_conv=d_conv, d_att=d_att, tail0=tail0, prompt=prompt,
                             n_cast=len(cast), n_once=len(cast_once))
    extra = [] if prompt else [state]
    hop = min(CONV_HOP, tm // n_seg)
    conv_w = tuple(conv_w) + _dft_tables(HALO + hop, hop)
    return pl.pallas_call(
        kern,
        grid=(nb, n_t),
        in_specs=[pl.BlockSpec((1, tm, d_model), row),
                  _const_spec((1, d_model)), _const_spec((d_model, d_in)),
                  _const_spec((1, HEAD_DIM)), _const_spec((1, HEAD_DIM))]
                 + [_const_spec(a.shape) for a in extra + list(conv_w)] + side_specs,
        out_specs=out_specs,
        out_shape=out_shape,
        scratch_shapes=[pltpu.VMEM((n_seg, HALO + tm // n_seg, d_conv), F32),
                        pltpu.VMEM((3, (HALO + hop) // 2, d_conv), F32)],
        compiler_params=pltpu.CompilerParams(
            dimension_semantics=("arbitrary", "arbitrary"), vmem_limit_bytes=VMEM_LIMIT),
        name="in_proj",
    )(x, g_mix, w_in, g_q, g_k, *extra, *conv_w, *side_in)


def _attn_scratch(n_heads, slots):
    return [pltpu.VMEM((slots, n_heads // 2, PAIR, 2 * WIN), F32),
            pltpu.VMEM((slots, n_heads // 2, PAIR, 2 * WIN), BF16)]


def _attend(rows, n_slabs, get_q, get_k, get_v, bias_ref, off, s_scr, p_scr):
    n_pairs = s_scr.shape[0]
    width = n_slabs * LANES
    for hp in range(n_pairs):
        s_scr[hp, 0:rows, 0:2 * width] = jnp.dot(get_q(hp), get_k(hp), preferred_element_type=F32)
    for hp in range(n_pairs):
        for sub in range(2):
            h = 2 * hp + sub
            for r0 in range(0, rows, SOFTMAX_ROWS):
                r = slice(r0, r0 + SOFTMAX_ROWS)
                s = s_scr[hp, r, sub * width:(sub + 1) * width]
                s = s + jnp.concatenate([bias_ref[h, off + c, r, :] for c in range(n_slabs)], axis=1)
                e = jnp.exp2(s - jnp.max(s, axis=-1, keepdims=True))
                p_scr[hp, r, sub * width:(sub + 1) * width] = e.astype(BF16)
    first = lax.broadcasted_iota(jnp.int32, (width, LANES), 1) < HEAD_DIM
    sums = jnp.concatenate([jnp.where(first, 1.0, 0.0), jnp.where(first, 0.0, 1.0)], axis=0).astype(BF16)
    outs = []
    for hp in range(n_pairs):
        o2 = jnp.dot(p_scr[hp, 0:rows, 0:2 * width], jnp.concatenate([get_v(hp), sums], axis=1),
                     preferred_element_type=F32)
        outs.append(o2[:, 0:LANES] * (1.0 / o2[:, LANES:2 * LANES]))
    return jnp.concatenate(outs, axis=1)


def _attn_prompt_kernel(q_ref, kt_ref, v_ref, bias_ref, g_ref, o_ref, s_scr, p_scr):
    pairs = q_ref.shape[1] // PAIR
    slots = s_scr.shape[0]
    t = pl.program_id(1)

    def run_pairs(first_step):
        for i in range(pairs):
            p = i if first_step else t * pairs + i
            n_slabs = min(i + 1, WIN_SLABS) if first_step else WIN_SLABS
            ws = max(p - (WIN_SLABS - 1), 0) if first_step else p - (WIN_SLABS - 1)
            off = ws + (WIN_SLABS - 1) - p
            k0 = ws * LANES if first_step else pl.multiple_of(ws * LANES, LANES)
            width = n_slabs * LANES

            def get_k(hp, ws=ws, n_slabs=n_slabs):
                return jnp.concatenate(
                    [kt_ref[0, e, ws + c, hp * LANES:(hp + 1) * LANES, :]
                     for e in range(2) for c in range(n_slabs)], axis=1)

            def get_v(hp, k0=k0, width=width):
                lanes = slice(hp * LANES, (hp + 1) * LANES)
                return jnp.concatenate([v_ref[0, 0, pl.ds(k0, width), lanes],
                                        v_ref[0, 1, pl.ds(k0, width), lanes]], axis=0)

            rows = slice(i * PAIR, (i + 1) * PAIR)

            def get_q(hp, rows=rows):
                return q_ref[0, rows, hp * LANES:(hp + 1) * LANES]

            k = i % slots
            a = _attend(PAIR, n_slabs, get_q, get_k, get_v, bias_ref, off, s_scr.at[k], p_scr.at[k])
            o_ref[0, rows, :] = _rms(a, g_ref[...]).astype(BF16)

    pl.when(t == 0)(functools.partial(run_pairs, True))
    pl.when(t > 0)(functools.partial(run_pairs, False))


def _attn_prompt(q, kt, v, bias, g_att, *, tq):
    nb, t_len, d_att = q.shape
    n_heads = d_att // HEAD_DIM
    assert t_len % tq == 0 and tq % PAIR == 0 and tq // PAIR >= WIN_SLABS - 1
    return pl.pallas_call(
        _attn_prompt_kernel,
        grid=(nb, t_len // tq),
        in_specs=[pl.BlockSpec((1, tq, d_att), lambda b, t: (b, t, 0)),
                  pl.BlockSpec((1, 2, t_len // LANES, d_att, LANES), lambda b, t: (b, 0, 0, 0, 0)),
                  pl.BlockSpec((1, 2, t_len, d_att), lambda b, t: (b, 0, 0, 0)),
                  _const_spec(bias.shape), _const_spec((1, d_att))],
        out_specs=pl.BlockSpec((1, tq, d_att), lambda b, t: (b, t, 0)),
        out_shape=jax.ShapeDtypeStruct((nb, t_len, d_att), BF16),
        scratch_shapes=_attn_scratch(n_heads, slots=2),
        compiler_params=pltpu.CompilerParams(
            dimension_semantics=("arbitrary", "arbitrary"), vmem_limit_bytes=VMEM_LIMIT),
        name="attn_prompt",
    )(q, kt, v, bias, g_att)


def _attn_sample_kernel(q_ref, ck_ref, cv_ref, kn_ref, vn_ref, bias_ref, g_ref, o_ref, kt_s, v_s,
                        s_scr, p_scr):
    d_att = q_ref.shape[-1]
    pad = jnp.zeros((WIN - ck_ref.shape[1] - kn_ref.shape[1], d_att), F32)
    k_all = jnp.concatenate([ck_ref[0].astype(F32), kn_ref[0], pad], axis=0)
    even_row = (lax.broadcasted_iota(jnp.int32, (d_att, LANES), 0) & HEAD_DIM) == 0
    for c in range(WIN_SLABS):
        kt = k_all[c * LANES:(c + 1) * LANES, :].T
        kt_s[0, c] = jnp.where(even_row, kt, 0.0).astype(BF16)
        kt_s[1, c] = jnp.where(even_row, 0.0, kt).astype(BF16)
    v_all = jnp.concatenate([cv_ref[0].astype(F32), vn_ref[0], pad], axis=0)
    even = (lax.broadcasted_iota(jnp.int32, v_all.shape, 1) & HEAD_DIM) == 0
    v_s[0] = jnp.where(even, v_all, 0.0).astype(BF16)
    v_s[1] = jnp.where(even, 0.0, v_all).astype(BF16)

    def get_k(hp):
        return jnp.concatenate(
            [kt_s[e, c, hp * LANES:(hp + 1) * LANES, :] for e in range(2) for c in range(WIN_SLABS)], axis=1)

    def get_v(hp):
        lanes = slice(hp * LANES, (hp + 1) * LANES)
        return jnp.concatenate([v_s[0, :, lanes], v_s[1, :, lanes]], axis=0)

    def get_q(hp):
        return q_ref[0, :, hp * LANES:(hp + 1) * LANES]

    a = _attend(q_ref.shape[1], WIN_SLABS, get_q, get_k, get_v, bias_ref, 0, s_scr.at[0], p_scr.at[0])
    o_ref[0] = _rms(a, g_ref[...]).astype(BF16)


def _attn_sample(q, cache_k, cache_v, k_new, v_new, bias, g_att):
    nb, s_len, d_att = k_new.shape
    n_heads = d_att // HEAD_DIM
    l_cache = cache_k.shape[1]
    assert s_len == CHUNK and l_cache == LEFT_CHUNKS * CHUNK
    per_b = lambda b: (b, 0, 0)
    return pl.pallas_call(
        _attn_sample_kernel,
        grid=(nb,),
        in_specs=[pl.BlockSpec((1, s_len, d_att), per_b),
                  pl.BlockSpec((1, l_cache, d_att), per_b), pl.BlockSpec((1, l_cache, d_att), per_b),
                  pl.BlockSpec((1, s_len, d_att), per_b), pl.BlockSpec((1, s_len, d_att), per_b),
                  _const_spec(bias.shape), _const_spec((1, d_att))],
        out_specs=pl.BlockSpec((1, s_len, d_att), per_b),
        out_shape=jax.ShapeDtypeStruct((nb, s_len, d_att), BF16),
        scratch_shapes=[pltpu.VMEM((2, WIN_SLABS, d_att, LANES), BF16), pltpu.VMEM((2, WIN, d_att), BF16)]
                       + _attn_scratch(n_heads, slots=1),
        compiler_params=pltpu.CompilerParams(
            dimension_semantics=("arbitrary",), vmem_limit_bytes=VMEM_LIMIT),
        name="attn_sample",
    )(q, cache_k, cache_v, k_new, v_new, bias, g_att)


def _dense_stage(x, cn, an, p, wout_ref, gffn_ref, w1_ref, w2_ref, gple_ref, wg_ref, wple_ref, ff_chunk):
    d_conv = cn.shape[-1]
    x1 = (x + jnp.dot(cn, wout_ref[0:d_conv, :], preferred_element_type=F32)
          + jnp.dot(an, wout_ref[d_conv:, :], preferred_element_type=F32))
    hb = _rms(x1, gffn_ref[...]).astype(BF16)
    ffn = None
    for lo in range(0, w1_ref.shape[1], ff_chunk):
        a = jnp.maximum(jnp.dot(hb, w1_ref[:, lo:lo + ff_chunk], preferred_element_type=F32), 0.0)
        part = jnp.dot((a * a).astype(BF16), w2_ref[lo:lo + ff_chunk, :], preferred_element_type=F32)
        ffn = part if ffn is None else ffn + part
    x2 = x1 + ffn
    gate = jax.nn.sigmoid(jnp.dot(_rms(x2, gple_ref[...]).astype(BF16), wg_ref[...],
                                  preferred_element_type=F32))
    ple = jnp.dot(p.astype(BF16), wple_ref[...], preferred_element_type=F32)
    return x2 + ple * gate


def _mix_kernel(*refs, starts, ff_chunk):
    n_groups = len(starts) - 1
    ins, dense_w, outs = refs[:4 * n_groups], refs[4 * n_groups:-n_groups], refs[-n_groups:]
    i = pl.program_id(0)
    for g in range(n_groups):
        x_ref, p_ref, an_ref, cn_ref = ins[4 * g:4 * g + 4]

        @pl.when((i >= starts[g]) & (i < starts[g + 1]))
        def _():
            outs[g][0] = _dense_stage(x_ref[0], cn_ref[0], an_ref[0], p_ref[0], *dense_w, ff_chunk)


def _mix(groups, w_out, g_ffn, w1, w2, g_ple, w_gate, w_ple, *, tm):
    consts = [w_out, g_ffn, w1, w2, g_ple, w_gate, w_ple]
    starts, in_specs, out_specs, out_shape, operands = [0], [], [], [], []
    for x, p, an, cn in groups:
        ng, rows, d_model = x.shape
        assert rows % tm == 0
        n_t = rows // tm
        first, count = starts[-1], ng * n_t
        starts.append(first + count)

        def tile(i, first=first, count=count, n_t=n_t):
            j = jnp.clip(i - first, 0, count - 1)
            return (j // n_t, j % n_t, 0)

        in_specs += [pl.BlockSpec((1, tm, a.shape[-1]), tile) for a in (x, p, an, cn)]
        out_specs.append(pl.BlockSpec((1, tm, d_model), tile))
        out_shape.append(jax.ShapeDtypeStruct(x.shape, F32))
        operands += [x, p, an, cn]
    return pl.pallas_call(
        functools.partial(_mix_kernel, starts=tuple(starts), ff_chunk=2048),
        grid=(starts[-1],),
        in_specs=in_specs + [_const_spec(c.shape) for c in consts],
        out_specs=out_specs,
        out_shape=out_shape,
        compiler_params=pltpu.CompilerParams(
            dimension_semantics=("arbitrary",), vmem_limit_bytes=VMEM_LIMIT),
        name="mix",
    )(*operands, *consts)


def kernel(x_prompt, x_sample, p_prompt, p_sample, cache_att_k, cache_att_v, state_conv, g_mix, w_in, w_dw,
           b_dw, g_conv_ln, b_conv_ln, g_q, g_k, rel_bias, g_conv_out, g_att_out, w_out, g_ffn, w_ff1, w_ff2,
           g_ple, w_gate, w_ple):
    depth = w_in.shape[0]
    nb, t_len, d_model = x_prompt.shape
    sb, s_len, _ = x_sample.shape
    d_conv = w_dw.shape[-1]
    d_att = g_att_out.shape[-1]
    n_heads = d_att // HEAD_DIM
    keep = min(LEFT_CHUNKS * CHUNK, t_len)
    tile = 512

    xp = x_prompt
    xs = x_sample.reshape(1, sb * s_len, d_model)
    outs = [[] for _ in range(6)]
    for i in range(depth):
        vec = lambda a: a[i].reshape(1, -1)
        gq, gk = vec(g_q), vec(g_k)
        conv_w = (w_dw[i], vec(b_dw), vec(g_conv_ln), vec(b_conv_ln), vec(g_conv_out))

        (cn_p, u_tail, q_p, kt_p, vb_p, k_tail, v_tail, w_out_b, w_ff1_b, w_ff2_b, w_gate_b, bias,
         w_in_b, w_ple_b) = _in_proj(
            xp, vec(g_mix), w_in[i], gq, gk, conv_w, tm=tile, keep=keep,
            cast=(w_out[i], w_ff1[i], w_ff2[i], w_gate[i]), cast_once=(w_ple[i],), rel_bias=rel_bias[i])
        mix_w = (w_out_b, vec(g_ffn), w_ff1_b, w_ff2_b, vec(g_ple), w_gate_b, w_ple_b)
        an_p = _attn_prompt(q_p, kt_p, vb_p, bias, vec(g_att_out), tq=tile)
        outs[0].append(k_tail.reshape(nb, keep, n_heads, HEAD_DIM))
        outs[1].append(v_tail.reshape(nb, keep, n_heads, HEAD_DIM))
        outs[2].append(u_tail[:, HALO - (CONV_WIDTH - 1):])

        rows = sb * s_len
        cn_s, u_s, q_s, k_s, v_s = _in_proj(xs, vec(g_mix), w_in_b, gq, gk, conv_w, tm=rows, keep=rows,
                                            state=state_conv[i])
        per_b = lambda a: a.reshape(sb, s_len, a.shape[-1])
        ck = cache_att_k[i].reshape(sb, -1, d_att).astype(BF16)
        cv = cache_att_v[i].reshape(sb, -1, d_att).astype(BF16)
        an_s = _attn_sample(per_b(q_s), ck, cv, per_b(k_s), per_b(v_s), bias, vec(g_att_out))
        outs[3].append(k_s.reshape(sb, s_len, n_heads, HEAD_DIM))
        outs[4].append(v_s.reshape(sb, s_len, n_heads, HEAD_DIM))
        conv_in_tail = jnp.concatenate([state_conv[i], per_b(u_s)], axis=1)[:, -(CONV_WIDTH - 1):]
        outs[5].append(conv_in_tail)

        xp, xs = _mix([(xp, p_prompt[i], an_p, cn_p),
                       (xs, p_sample[i].reshape(1, rows, -1), an_s.reshape(1, rows, d_att), cn_s)],
                      *mix_w, tm=tile)

    return (xp, xs.reshape(sb, s_len, d_model)) + tuple(jnp.stack(o) for o in outs)
```

```python
import functools

import jax
import jax.numpy as jnp
import numpy as np
from jax import lax
from jax.experimental import pallas as pl
from jax.experimental.pallas import tpu as pltpu

F32 = jnp.float32
BF16 = jnp.bfloat16

CHUNK = 64
LEFT_CHUNKS = 8
HEAD_DIM = 64
CONV_WIDTH = 31
MAX_REL = 128
EPS = 1e-6
NEG = -1e30
LOG2_E = 1.4426950408889634

LANES = 128
SUBLANES = 8
MXU_TILE = 256
PAIR = 2 * CHUNK
WIN = (LEFT_CHUNKS + 2) * CHUNK
WIN_SLABS = WIN // LANES
HALO = 32
CONV_HOP = 128
SOFTMAX_ROWS = 32
VMEM_LIMIT = 56 * 1024 * 1024


def _rms(x, g):
    return x * lax.rsqrt(jnp.mean(x * x, axis=-1, keepdims=True) + EPS) * g


def _const_spec(shape):
    zeros = (0,) * len(shape)
    return pl.BlockSpec(shape, lambda *_: zeros, pipeline_mode=pl.Buffered(1))


def _bias_kernel(rb_ref, o_ref):
    assert rb_ref.shape[1] == 2 * MAX_REL + 1 and MAX_REL == LANES and CHUNK == LANES // 2
    assert LEFT_CHUNKS * CHUNK - MAX_REL == 3 * LANES and WIN_SLABS == 5
    n_heads = rb_ref.shape[0]
    rb = rb_ref[...]
    lane = lax.broadcasted_iota(jnp.int32, (n_heads, LANES), 1)
    b0 = jnp.broadcast_to(rb[:, 0:1], (n_heads, LANES))
    t4 = jnp.where(lane < CHUNK, rb[:, LANES:2 * LANES], b0)
    t = jnp.concatenate([b0, b0, b0, rb[:, 0:LANES], t4, b0], axis=1)
    r_i = lax.broadcasted_iota(jnp.int32, (PAIR, LANES), 0)
    c_i = lax.broadcasted_iota(jnp.int32, (PAIR, LANES), 1)
    for h in range(n_heads):
        row = jnp.broadcast_to(t[h:h + 1, :], (PAIR, t.shape[1]))
        toe = pltpu.roll(row, 0, 1, stride=1, stride_axis=0)
        for c in range(WIN_SLABS):
            jj = c_i + c * LANES
            valid = ((r_i < CHUNK) & (jj < WIN - CHUNK)) | ((r_i >= CHUNK) & (jj >= CHUNK))
            o_ref[h, c] = jnp.where(valid, toe[:, c * LANES:(c + 1) * LANES] * LOG2_E, NEG)


def _dft_tables(n, hop):
    half = n // 2
    t = np.arange(n)[None, :]
    f = np.arange(half)[:, None]
    ang = 2.0 * np.pi * f * t / n
    fwd = np.concatenate([np.cos(ang), np.cos(np.pi * t), np.sin(ang[1:])], axis=0)
    rows = np.arange(n - hop, n)[:, None]
    ang_o = 2.0 * np.pi * rows * f.T / n
    scale = np.where(f.T == 0, 1.0, 2.0) / n
    inv = np.concatenate([scale * np.cos(ang_o), np.cos(np.pi * rows) / n, (2.0 / n) * np.sin(ang_o[:, 1:])], axis=1)
    delay = (CONV_WIDTH - 1) - np.arange(CONV_WIDTH)[None, :]
    ang_h = 2.0 * np.pi * f * delay / n
    p, q = np.cos(ang_h), np.sin(ang_h)
    r = p.copy()
    r[0] = np.cos(np.pi * delay[0])
    return (jnp.asarray(fwd, F32), jnp.asarray(inv, F32), jnp.asarray(np.stack([p, q, r]), F32))


def _filter_response(wdw_ref, resp_ref, h_scr):
    for k in range(3):
        acc = None
        for j in range(CONV_WIDTH):
            term = resp_ref[k, :, j:j + 1] * wdw_ref[0, j:j + 1, :]
            acc = term if acc is None else acc + term
        h_scr[k] = acc


def _conv_stage(u, n_seg, halo_ref, ubuf, cn_ref, h_scr, fwd_ref, inv_ref,
                bdw_ref, gln_ref, bln_ref, gco_ref):
    d_conv = u.shape[-1]
    seg = u.shape[0] // n_seg
    n, hop = fwd_ref.shape[0], inv_ref.shape[0]
    half = n // 2
    assert n - hop == HALO and HALO >= CONV_WIDTH - 1 and seg % hop == 0
    for s in range(n_seg):
        if halo_ref is not None:
            hr = halo_ref.shape[1]
            if hr < HALO:
                ubuf[s, 0:HALO - hr, :] = jnp.zeros((HALO - hr, d_conv), F32)
            ubuf[s, HALO - hr:HALO, :] = halo_ref[s]
        ubuf[s, HALO:HALO + seg, :] = u[s * seg:(s + 1) * seg, :]

    blocks = [(s, i) for s in range(n_seg) for i in range(seg // hop)]
    windows = jnp.concatenate([ubuf[s, hop * i:hop * i + n, :].astype(BF16) for s, i in blocks], axis=1)
    spec = jnp.dot(fwd_ref[...].astype(BF16), windows, preferred_element_type=F32)
    p, q, r = h_scr[0], h_scr[1], h_scr[2]
    prods = []
    for k in range(len(blocks)):
        top = spec[0:half, k * d_conv:(k + 1) * d_conv]
        bot = spec[half:n, k * d_conv:(k + 1) * d_conv]
        prods.append(jnp.concatenate([p * top - q * bot, r * bot + q * top], axis=0).astype(BF16))
    y = jnp.dot(inv_ref[...].astype(BF16), jnp.concatenate(prods, axis=1),
                preferred_element_type=F32)
    for k, (s, i) in enumerate(blocks):
        yk = y[:, k * d_conv:(k + 1) * d_conv] + bdw_ref[...]
        mu = jnp.mean(yk, axis=-1, keepdims=True)
        yc = yk - mu
        yn = yc * lax.rsqrt(jnp.mean(yc * yc, axis=-1, keepdims=True) + EPS) * gln_ref[...] + bln_ref[...]
        c = yn * jax.nn.sigmoid(yn)
        cn_ref[s * seg + i * hop:s * seg + (i + 1) * hop, :] = _rms(c, gco_ref[...]).astype(BF16)


def _inproj_kernel(*refs, d_conv, d_att, tail0, prompt, n_cast, n_once):
    if prompt:
        x_ref, gmix_ref, win_ref, gq_ref, gk_ref = refs[:5]
        halo_ref, conv_w = None, refs[5:13]
        cast_in, rb_ref = refs[13:13 + n_cast], refs[13 + n_cast]
        once_in = (win_ref,) + refs[14 + n_cast:14 + n_cast + n_once]
        outs = refs[14 + n_cast + n_once:]
        cn_ref, ut_ref, q_ref, kt_ref, vb_ref, k32_ref, v32_ref = outs[:7]
        cast_out, bias_ref = outs[7:7 + n_cast], outs[7 + n_cast]
        once_out = outs[8 + n_cast:9 + n_cast + n_once]
        ubuf, h_scr = outs[9 + n_cast + n_once:]
        win_ref = once_out[0]
        t = pl.program_id(1)
        tm = x_ref.shape[1]

        for src, dst in zip(cast_in, cast_out):
            dst[...] = src[...].astype(BF16)

        @pl.when((pl.program_id(0) == 0) & (t == 0))
        def _():
            for src, dst in zip(once_in, once_out):
                for lo in range(0, src.shape[1], MXU_TILE):
                    dst[:, lo:lo + MXU_TILE] = src[:, lo:lo + MXU_TILE].astype(BF16)
            _bias_kernel(rb_ref, bias_ref)
            _filter_response(conv_w[0], conv_w[7], h_scr)

        @pl.when(t == 0)
        def _():
            ubuf[0, 0:HALO, :] = jnp.zeros((HALO, d_conv), F32)

        @pl.when(t > 0)
        def _():
            ubuf[0, 0:HALO, :] = ubuf[0, tm:tm + HALO, :]
    else:
        x_ref, gmix_ref, win_ref, gq_ref, gk_ref, halo_ref = refs[:6]
        conv_w = refs[6:14]
        cn_ref, u_ref, q_ref, k32_ref, v32_ref, ubuf, h_scr = refs[14:]
        tm = x_ref.shape[1]
        _filter_response(conv_w[0], conv_w[7], h_scr)
    h = _rms(x_ref[0], gmix_ref[...]).astype(BF16)

    def proj(lo, width):
        return jnp.dot(h, win_ref[:, lo:lo + width], preferred_element_type=F32)

    u = proj(0, d_conv) * jax.nn.sigmoid(proj(d_conv, d_conv))
    if prompt:
        @pl.when(t == pl.num_programs(1) - 1)
        def _():
            ut_ref[0] = u[tm - HALO:tm, :]
    else:
        u_ref[0] = u
    wdw_ref, bdw_ref, gln_ref, bln_ref, gco_ref, fwd_ref, inv_ref, _ = conv_w
    _conv_stage(u, ubuf.shape[0], halo_ref, ubuf, cn_ref.at[0], h_scr, fwd_ref, inv_ref,
                bdw_ref, gln_ref, bln_ref, gco_ref)

    head_bits = HEAD_DIM.bit_length() - 1
    r_i = lax.shift_right_logical(lax.broadcasted_iota(jnp.int32, (MXU_TILE, MXU_TILE), 0), head_bits)
    c_i = lax.shift_right_logical(lax.broadcasted_iota(jnp.int32, (MXU_TILE, MXU_TILE), 1), head_bits)
    avg = jnp.where(r_i == c_i, 1.0 / HEAD_DIM, 0.0).astype(BF16)

    def head_rms(z, g):
        sq = (z * z).astype(BF16)
        ms = jnp.concatenate([jnp.dot(sq[:, lo:lo + MXU_TILE], avg, preferred_element_type=F32)
                              for lo in range(0, d_att, MXU_TILE)], axis=1)
        return z * lax.rsqrt(ms + EPS) * jnp.tile(g, (1, d_att // HEAD_DIM))

    o = 2 * d_conv
    qn = head_rms(proj(o, d_att), gq_ref[...])
    q_ref[0] = (qn * (HEAD_DIM ** -0.5 * LOG2_E)).astype(BF16)
    kn = head_rms(proj(o + d_att, d_att), gk_ref[...])
    v = proj(o + 2 * d_att, d_att)
    if prompt:
        even_row = (lax.broadcasted_iota(jnp.int32, (d_att, LANES), 0) & HEAD_DIM) == 0
        for s in range(tm // LANES):
            kt = kn[s * LANES:(s + 1) * LANES, :].T
            kt_ref[0, 0, s] = jnp.where(even_row, kt, 0.0).astype(BF16)
            kt_ref[0, 1, s] = jnp.where(even_row, 0.0, kt).astype(BF16)
        even = (lax.broadcasted_iota(jnp.int32, (tm, d_att), 1) & HEAD_DIM) == 0
        vb_ref[0, 0] = jnp.where(even, v, 0.0).astype(BF16)
        vb_ref[0, 1] = jnp.where(even, 0.0, v).astype(BF16)

    @pl.when(pl.program_id(1) >= tail0)
    def _():
        k32_ref[0] = kn
        v32_ref[0] = v


def _in_proj(x, g_mix, w_in, g_q, g_k, conv_w, *, tm, keep, state=None, cast=(), cast_once=(), rel_bias=None):
    nb, t_len, d_model = x.shape
    d_in = w_in.shape[-1]
    d_conv = conv_w[0].shape[-1]
    d_att = (d_in - 2 * d_conv) // 3
    assert g_q.shape == g_k.shape == (1, HEAD_DIM) and d_att % HEAD_DIM == 0
    n_t = t_len // tm
    prompt = state is None
    assert t_len % tm == 0 and (t_len - keep) % tm == 0 and tm % LANES == 0 and tm >= HALO
    tail0 = (t_len - keep) // tm
    row = lambda b, t: (b, t, 0)
    tail = lambda b, t: (b, jnp.maximum(t - tail0, 0), 0)
    halves = pl.BlockSpec((1, 2, tm, d_att), lambda b, t: (b, 0, t, 0))
    out_shape = [jax.ShapeDtypeStruct((nb, t_len, d_conv), BF16)]
    out_specs = [pl.BlockSpec((1, tm, d_conv), row)]
    if prompt:
        n_seg = 1
        out_shape += [jax.ShapeDtypeStruct((nb, HALO, d_conv), F32)]
        out_specs += [pl.BlockSpec((1, HALO, d_conv), lambda b, t: (b, 0, 0))]
    else:
        n_seg = state.shape[0]
        assert nb == 1 and n_t == 1 and tm % n_seg == 0 and state.shape[1] == CONV_WIDTH - 1
        out_shape += [jax.ShapeDtypeStruct((nb, t_len, d_conv), F32)]
        out_specs += [pl.BlockSpec((1, tm, d_conv), row)]
    out_shape += [jax.ShapeDtypeStruct((nb, t_len, d_att), BF16)]
    out_specs += [pl.BlockSpec((1, tm, d_att), row)]
    if prompt:
        out_shape += [jax.ShapeDtypeStruct((nb, 2, t_len // LANES, d_att, LANES), BF16),
                      jax.ShapeDtypeStruct((nb, 2, t_len, d_att), BF16)]
        out_specs += [pl.BlockSpec((1, 2, tm // LANES, d_att, LANES), lambda b, t: (b, 0, t, 0, 0)), halves]
    out_shape += [jax.ShapeDtypeStruct((nb, keep, d_att), F32)] * 2
    out_specs += [pl.BlockSpec((1, tm, d_att), tail)] * 2
    cast_specs = []
    for w in cast:
        rb = w.shape[0] // (nb * n_t)
        assert prompt and w.ndim == 2 and rb * nb * n_t == w.shape[0] and rb % (2 * SUBLANES) == 0
        cast_specs.append(pl.BlockSpec((rb, w.shape[1]), lambda b, t: (b * n_t + t, 0)))
        out_shape.append(jax.ShapeDtypeStruct(w.shape, BF16))
    out_specs += cast_specs
    side_specs, side_in = list(cast_specs), list(cast)
    if prompt:
        bias_shape = (rel_bias.shape[0], WIN_SLABS, PAIR, LANES)
        side_specs.append(_const_spec(rel_bias.shape))
        side_in.append(rel_bias)
        out_shape.append(jax.ShapeDtypeStruct(bias_shape, F32))
        out_specs.append(pl.BlockSpec(bias_shape, lambda b, t: (0, 0, 0, 0)))
        for w in (w_in,) + tuple(cast_once):
            assert w.ndim == 2 and w.shape[1] % MXU_TILE == 0
            out_shape.append(jax.ShapeDtypeStruct(w.shape, BF16))
            out_specs.append(pl.BlockSpec(w.shape, lambda b, t: (0, 0)))
        side_specs += [_const_spec(w.shape) for w in cast_once]
        side_in += list(cast_once)
    else:
        assert not cast_once
    kern = functools.partial(_inproj_kernel, d_conv=d_conv, d_att=d_att, tail0=tail0, prompt=prompt,
                             n_cast=len(cast), n_once=len(cast_once))
    extra = [] if prompt else [state]
    hop = min(CONV_HOP, tm // n_seg)
    conv_w = tuple(conv_w) + _dft_tables(HALO + hop, hop)
    return pl.pallas_call(
        kern,
        grid=(nb, n_t),
        in_specs=[pl.BlockSpec((1, tm, d_model), row),
                  _const_spec((1, d_model)), _const_spec((d_model, d_in)),
                  _const_spec((1, HEAD_DIM)), _const_spec((1, HEAD_DIM))]
                 + [_const_spec(a.shape) for a in extra + list(conv_w)] + side_specs,
        out_specs=out_specs,
        out_shape=out_shape,
        scratch_shapes=[pltpu.VMEM((n_seg, HALO + tm // n_seg, d_conv), F32),
                        pltpu.VMEM((3, (HALO + hop) // 2, d_conv), F32)],
        compiler_params=pltpu.CompilerParams(
            dimension_semantics=("arbitrary", "arbitrary"), vmem_limit_bytes=VMEM_LIMIT),
        name="in_proj",
    )(x, g_mix, w_in, g_q, g_k, *extra, *conv_w, *side_in)


def _attn_scratch(n_heads, slots):
    return [pltpu.VMEM((slots, n_heads // 2, PAIR, 2 * WIN), F32),
            pltpu.VMEM((slots, n_heads // 2, PAIR, 2 * WIN), BF16)]


def _attend(rows, n_slabs, get_q, get_k, get_v, bias_ref, off, s_scr, p_scr):
    n_pairs = s_scr.shape[0]
    width = n_slabs * LANES
    for hp in range(n_pairs):
        s_scr[hp, 0:rows, 0:2 * width] = jnp.dot(get_q(hp), get_k(hp), preferred_element_type=F32)
    for hp in range(n_pairs):
        for sub in range(2):
            h = 2 * hp + sub
            for r0 in range(0, rows, SOFTMAX_ROWS):
                r = slice(r0, r0 + SOFTMAX_ROWS)
                s = s_scr[hp, r, sub * width:(sub + 1) * width]
                s = s + jnp.concatenate([bias_ref[h, off + c, r, :] for c in range(n_slabs)], axis=1)
                e = jnp.exp2(s - jnp.max(s, axis=-1, keepdims=True))
                p_scr[hp, r, sub * width:(sub + 1) * width] = e.astype(BF16)
    first = lax.broadcasted_iota(jnp.int32, (width, LANES), 1) < HEAD_DIM
    sums = jnp.concatenate([jnp.where(first, 1.0, 0.0), jnp.where(first, 0.0, 1.0)], axis=0).astype(BF16)
    outs = []
    for hp in range(n_pairs):
        o2 = jnp.dot(p_scr[hp, 0:rows, 0:2 * width], jnp.concatenate([get_v(hp), sums], axis=1),
                     preferred_element_type=F32)
        outs.append(o2[:, 0:LANES] * (1.0 / o2[:, LANES:2 * LANES]))
    return jnp.concatenate(outs, axis=1)


def _attn_prompt_kernel(q_ref, kt_ref, v_ref, bias_ref, g_ref, o_ref, s_scr, p_scr):
    pairs = q_ref.shape[1] // PAIR
    slots = s_scr.shape[0]
    t = pl.program_id(1)

    def run_pairs(first_step):
        for i in range(pairs):
            p = i if first_step else t * pairs + i
            n_slabs = min(i + 1, WIN_SLABS) if first_step else WIN_SLABS
            ws = max(p - (WIN_SLABS - 1), 0) if first_step else p - (WIN_SLABS - 1)
            off = ws + (WIN_SLABS - 1) - p
            k0 = ws * LANES if first_step else pl.multiple_of(ws * LANES, LANES)
            width = n_slabs * LANES

            def get_k(hp, ws=ws, n_slabs=n_slabs):
                return jnp.concatenate(
                    [kt_ref[0, e, ws + c, hp * LANES:(hp + 1) * LANES, :]
                     for e in range(2) for c in range(n_slabs)], axis=1)

            def get_v(hp, k0=k0, width=width):
                lanes = slice(hp * LANES, (hp + 1) * LANES)
                return jnp.concatenate([v_ref[0, 0, pl.ds(k0, width), lanes],
                                        v_ref[0, 1, pl.ds(k0, width), lanes]], axis=0)

            rows = slice(i * PAIR, (i + 1) * PAIR)

            def get_q(hp, rows=rows):
                return q_ref[0, rows, hp * LANES:(hp + 1) * LANES]

            k = i % slots
            a = _attend(PAIR, n_slabs, get_q, get_k, get_v, bias_ref, off, s_scr.at[k], p_scr.at[k])
            o_ref[0, rows, :] = _rms(a, g_ref[...]).astype(BF16)

    pl.when(t == 0)(functools.partial(run_pairs, True))
    pl.when(t > 0)(functools.partial(run_pairs, False))


def _attn_prompt(q, kt, v, bias, g_att, *, tq):
    nb, t_len, d_att = q.shape
    n_heads = d_att // HEAD_DIM
    assert t_len % tq == 0 and tq % PAIR == 0 and tq // PAIR >= WIN_SLABS - 1
    return pl.pallas_call(
        _attn_prompt_kernel,
        grid=(nb, t_len // tq),
        in_specs=[pl.BlockSpec((1, tq, d_att), lambda b, t: (b, t, 0)),
                  pl.BlockSpec((1, 2, t_len // LANES, d_att, LANES), lambda b, t: (b, 0, 0, 0, 0)),
                  pl.BlockSpec((1, 2, t_len, d_att), lambda b, t: (b, 0, 0, 0)),
                  _const_spec(bias.shape), _const_spec((1, d_att))],
        out_specs=pl.BlockSpec((1, tq, d_att), lambda b, t: (b, t, 0)),
        out_shape=jax.ShapeDtypeStruct((nb, t_len, d_att), BF16),
        scratch_shapes=_attn_scratch(n_heads, slots=2),
        compiler_params=pltpu.CompilerParams(
            dimension_semantics=("arbitrary", "arbitrary"), vmem_limit_bytes=VMEM_LIMIT),
        name="attn_prompt",
    )(q, kt, v, bias, g_att)


def _attn_sample_kernel(q_ref, ck_ref, cv_ref, kn_ref, vn_ref, bias_ref, g_ref, o_ref, kt_s, v_s,
                        s_scr, p_scr):
    d_att = q_ref.shape[-1]
    pad = jnp.zeros((WIN - ck_ref.shape[1] - kn_ref.shape[1], d_att), F32)
    k_all = jnp.concatenate([ck_ref[0].astype(F32), kn_ref[0], pad], axis=0)
    even_row = (lax.broadcasted_iota(jnp.int32, (d_att, LANES), 0) & HEAD_DIM) == 0
    for c in range(WIN_SLABS):
        kt = k_all[c * LANES:(c + 1) * LANES, :].T
        kt_s[0, c] = jnp.where(even_row, kt, 0.0).astype(BF16)
        kt_s[1, c] = jnp.where(even_row, 0.0, kt).astype(BF16)
    v_all = jnp.concatenate([cv_ref[0].astype(F32), vn_ref[0], pad], axis=0)
    even = (lax.broadcasted_iota(jnp.int32, v_all.shape, 1) & HEAD_DIM) == 0
    v_s[0] = jnp.where(even, v_all, 0.0).astype(BF16)
    v_s[1] = jnp.where(even, 0.0, v_all).astype(BF16)

    def get_k(hp):
        return jnp.concatenate(
            [kt_s[e, c, hp * LANES:(hp + 1) * LANES, :] for e in range(2) for c in range(WIN_SLABS)], axis=1)

    def get_v(hp):
        lanes = slice(hp * LANES, (hp + 1) * LANES)
        return jnp.concatenate([v_s[0, :, lanes], v_s[1, :, lanes]], axis=0)

    def get_q(hp):
        return q_ref[0, :, hp * LANES:(hp + 1) * LANES]

    a = _attend(q_ref.shape[1], WIN_SLABS, get_q, get_k, get_v, bias_ref, 0, s_scr.at[0], p_scr.at[0])
    o_ref[0] = _rms(a, g_ref[...]).astype(BF16)


def _attn_sample(q, cache_k, cache_v, k_new, v_new, bias, g_att):
    nb, s_len, d_att = k_new.shape
    n_heads = d_att // HEAD_DIM
    l_cache = cache_k.shape[1]
    assert s_len == CHUNK and l_cache == LEFT_CHUNKS * CHUNK
    per_b = lambda b: (b, 0, 0)
    return pl.pallas_call(
        _attn_sample_kernel,
        grid=(nb,),
        in_specs=[pl.BlockSpec((1, s_len, d_att), per_b),
                  pl.BlockSpec((1, l_cache, d_att), per_b), pl.BlockSpec((1, l_cache, d_att), per_b),
                  pl.BlockSpec((1, s_len, d_att), per_b), pl.BlockSpec((1, s_len, d_att), per_b),
                  _const_spec(bias.shape), _const_spec((1, d_att))],
        out_specs=pl.BlockSpec((1, s_len, d_att), per_b),
        out_shape=jax.ShapeDtypeStruct((nb, s_len, d_att), BF16),
        scratch_shapes=[pltpu.VMEM((2, WIN_SLABS, d_att, LANES), BF16), pltpu.VMEM((2, WIN, d_att), BF16)]
                       + _attn_scratch(n_heads, slots=1),
        compiler_params=pltpu.CompilerParams(
            dimension_semantics=("arbitrary",), vmem_limit_bytes=VMEM_LIMIT),
        name="attn_sample",
    )(q, cache_k, cache_v, k_new, v_new, bias, g_att)


def _dense_stage(x, cn, an, p, wout_ref, gffn_ref, w1_ref, w2_ref, gple_ref, wg_ref, wple_ref, ff_chunk):
    d_conv = cn.shape[-1]
    x1 = (x + jnp.dot(cn, wout_ref[0:d_conv, :], preferred_element_type=F32)
          + jnp.dot(an, wout_ref[d_conv:, :], preferred_element_type=F32))
    hb = _rms(x1, gffn_ref[...]).astype(BF16)
    ffn = None
    for lo in range(0, w1_ref.shape[1], ff_chunk):
        a = jnp.maximum(jnp.dot(hb, w1_ref[:, lo:lo + ff_chunk], preferred_element_type=F32), 0.0)
        part = jnp.dot((a * a).astype(BF16), w2_ref[lo:lo + ff_chunk, :], preferred_element_type=F32)
        ffn = part if ffn is None else ffn + part
    x2 = x1 + ffn
    gate = jax.nn.sigmoid(jnp.dot(_rms(x2, gple_ref[...]).astype(BF16), wg_ref[...],
                                  preferred_element_type=F32))
    ple = jnp.dot(p.astype(BF16), wple_ref[...], preferred_element_type=F32)
    return x2 + ple * gate


def _mix_kernel(*refs, starts, ff_chunk):
    n_groups = len(starts) - 1
    ins, dense_w, outs = refs[:4 * n_groups], refs[4 * n_groups:-n_groups], refs[-n_groups:]
    i = pl.program_id(0)
    for g in range(n_groups):
        x_ref, p_ref, an_ref, cn_ref = ins[4 * g:4 * g + 4]

        @pl.when((i >= starts[g]) & (i < starts[g + 1]))
        def _():
            outs[g][0] = _dense_stage(x_ref[0], cn_ref[0], an_ref[0], p_ref[0], *dense_w, ff_chunk)


def _mix(groups, w_out, g_ffn, w1, w2, g_ple, w_gate, w_ple, *, tm):
    consts = [w_out, g_ffn, w1, w2, g_ple, w_gate, w_ple]
    starts, in_specs, out_specs, out_shape, operands = [0], [], [], [], []
    for x, p, an, cn in groups:
        ng, rows, d_model = x.shape
        assert rows % tm == 0
        n_t = rows // tm
        first, count = starts[-1], ng * n_t
        starts.append(first + count)

        def tile(i, first=first, count=count, n_t=n_t):
            j = jnp.clip(i - first, 0, count - 1)
            return (j // n_t, j % n_t, 0)

        in_specs += [pl.BlockSpec((1, tm, a.shape[-1]), tile) for a in (x, p, an, cn)]
        out_specs.append(pl.BlockSpec((1, tm, d_model), tile))
        out_shape.append(jax.ShapeDtypeStruct(x.shape, F32))
        operands += [x, p, an, cn]
    return pl.pallas_call(
        functools.partial(_mix_kernel, starts=tuple(starts), ff_chunk=2048),
        grid=(starts[-1],),
        in_specs=in_specs + [_const_spec(c.shape) for c in consts],
        out_specs=out_specs,
        out_shape=out_shape,
        compiler_params=pltpu.CompilerParams(
            dimension_semantics=("arbitrary",), vmem_limit_bytes=VMEM_LIMIT),
        name="mix",
    )(*operands, *consts)


def kernel(x_prompt, x_sample, p_prompt, p_sample, cache_att_k, cache_att_v, state_conv, g_mix, w_in, w_dw,
           b_dw, g_conv_ln, b_conv_ln, g_q, g_k, rel_bias, g_conv_out, g_att_out, w_out, g_ffn, w_ff1, w_ff2,
           g_ple, w_gate, w_ple):
    depth = w_in.shape[0]
    nb, t_len, d_model = x_prompt.shape
    sb, s_len, _ = x_sample.shape
    d_conv = w_dw.shape[-1]
    d_att = g_att_out.shape[-1]
    n_heads = d_att // HEAD_DIM
    keep = min(LEFT_CHUNKS * CHUNK, t_len)
    tile = 512

    xp = x_prompt
    xs = x_sample.reshape(1, sb * s_len, d_model)
    outs = [[] for _ in range(6)]
    for i in range(depth):
        vec = lambda a: a[i].reshape(1, -1)
        gq, gk = vec(g_q), vec(g_k)
        conv_w = (w_dw[i:i + 1], vec(b_dw), vec(g_conv_ln), vec(b_conv_ln), vec(g_conv_out))

        (cn_p, u_tail, q_p, kt_p, vb_p, k_tail, v_tail, w_out_b, w_ff1_b, w_ff2_b, w_gate_b, bias,
         w_in_b, w_ple_b) = _in_proj(
            xp, vec(g_mix), w_in[i], gq, gk, conv_w, tm=tile, keep=keep,
            cast=(w_out[i], w_ff1[i], w_ff2[i], w_gate[i]), cast_once=(w_ple[i],), rel_bias=rel_bias[i])
        mix_w = (w_out_b, vec(g_ffn), w_ff1_b, w_ff2_b, vec(g_ple), w_gate_b, w_ple_b)
        an_p = _attn_prompt(q_p, kt_p, vb_p, bias, vec(g_att_out), tq=tile)
        outs[0].append(k_tail.reshape(nb, keep, n_heads, HEAD_DIM))
        outs[1].append(v_tail.reshape(nb, keep, n_heads, HEAD_DIM))
        outs[2].append(u_tail[:, HALO - (CONV_WIDTH - 1):])

        rows = sb * s_len
        cn_s, u_s, q_s, k_s, v_s = _in_proj(xs, vec(g_mix), w_in_b, gq, gk, conv_w, tm=rows, keep=rows,
                                            state=state_conv[i])
        per_b = lambda a: a.reshape(sb, s_len, a.shape[-1])
        ck = cache_att_k[i].reshape(sb, -1, d_att).astype(BF16)
        cv = cache_att_v[i].reshape(sb, -1, d_att).astype(BF16)
        an_s = _attn_sample(per_b(q_s), ck, cv, per_b(k_s), per_b(v_s), bias, vec(g_att_out))
        outs[3].append(k_s.reshape(sb, s_len, n_heads, HEAD_DIM))
        outs[4].append(v_s.reshape(sb, s_len, n_heads, HEAD_DIM))
        conv_in_tail = jnp.concatenate([state_conv[i], per_b(u_s)], axis=1)[:, -(CONV_WIDTH - 1):]
        outs[5].append(conv_in_tail)

        xp, xs = _mix([(xp, p_prompt[i], an_p, cn_p),
                       (xs, p_sample[i].reshape(1, rows, -1), an_s.reshape(1, rows, d_att), cn_s)],
                      *mix_w, tm=tile)

    return (xp, xs.reshape(sb, s_len, d_model)) + tuple(jnp.stack(o) for o in outs)
```

```python
import functools

import jax
import jax.numpy as jnp
import numpy as np
from jax import lax
from jax.experimental import pallas as pl
from jax.experimental.pallas import tpu as pltpu

F32 = jnp.float32
BF16 = jnp.bfloat16

CHUNK = 64
LEFT_CHUNKS = 8
HEAD_DIM = 64
CONV_WIDTH = 31
MAX_REL = 128
EPS = 1e-6
NEG = -1e30
LOG2_E = 1.4426950408889634

LANES = 128
SUBLANES = 8
MXU_TILE = 256
PAIR = 2 * CHUNK
WIN = (LEFT_CHUNKS + 2) * CHUNK
WIN_SLABS = WIN // LANES
HALO = 32
CONV_HOP = 128
SOFTMAX_ROWS = 32
VMEM_LIMIT = 56 * 1024 * 1024


def _rms(x, g):
    return x * lax.rsqrt(jnp.mean(x * x, axis=-1, keepdims=True) + EPS) * g


def _const_spec(shape):
    zeros = (0,) * len(shape)
    return pl.BlockSpec(shape, lambda *_: zeros, pipeline_mode=pl.Buffered(1))


def _bias_kernel(rb_ref, o_ref):
    assert rb_ref.shape[1] == 2 * MAX_REL + 1 and MAX_REL == LANES and CHUNK == LANES // 2
    assert LEFT_CHUNKS * CHUNK - MAX_REL == 3 * LANES and WIN_SLABS == 5
    n_heads = rb_ref.shape[0]
    rb = rb_ref[...]
    lane = lax.broadcasted_iota(jnp.int32, (n_heads, LANES), 1)
    b0 = jnp.broadcast_to(rb[:, 0:1], (n_heads, LANES))
    t4 = jnp.where(lane < CHUNK, rb[:, LANES:2 * LANES], b0)
    t = jnp.concatenate([b0, b0, b0, rb[:, 0:LANES], t4, b0], axis=1)
    r_i = lax.broadcasted_iota(jnp.int32, (PAIR, LANES), 0)
    c_i = lax.broadcasted_iota(jnp.int32, (PAIR, LANES), 1)
    for h in range(n_heads):
        row = jnp.broadcast_to(t[h:h + 1, :], (PAIR, t.shape[1]))
        toe = pltpu.roll(row, 0, 1, stride=1, stride_axis=0)
        for c in range(WIN_SLABS):
            jj = c_i + c * LANES
            valid = ((r_i < CHUNK) & (jj < WIN - CHUNK)) | ((r_i >= CHUNK) & (jj >= CHUNK))
            o_ref[h, c] = jnp.where(valid, toe[:, c * LANES:(c + 1) * LANES] * LOG2_E, NEG)


def _dft_tables(n, hop):
    half = n // 2
    t = np.arange(n)[None, :]
    f = np.arange(half)[:, None]
    ang = 2.0 * np.pi * f * t / n
    fwd = np.concatenate([np.cos(ang), np.cos(np.pi * t), np.sin(ang[1:])], axis=0)
    rows = np.arange(n - hop, n)[:, None]
    ang_o = 2.0 * np.pi * rows * f.T / n
    scale = np.where(f.T == 0, 1.0, 2.0) / n
    inv = np.concatenate([scale * np.cos(ang_o), np.cos(np.pi * rows) / n, (2.0 / n) * np.sin(ang_o[:, 1:])], axis=1)
    delay = (CONV_WIDTH - 1) - np.arange(CONV_WIDTH)[None, :]
    ang_h = 2.0 * np.pi * f * delay / n
    p, q = np.cos(ang_h), np.sin(ang_h)
    r = p.copy()
    r[0] = np.cos(np.pi * delay[0])
    return (jnp.asarray(fwd, F32), jnp.asarray(inv, F32), jnp.asarray(np.stack([p, q, r]), F32))


def _filter_response(wdw_ref, resp_ref, h_scr):
    for k in range(3):
        acc = None
        for j in range(CONV_WIDTH):
            term = resp_ref[k, :, j:j + 1] * wdw_ref[j:j + 1, :]
            acc = term if acc is None else acc + term
        h_scr[k] = acc


def _conv_stage(u, n_seg, halo_ref, ubuf, cn_ref, h_scr, fwd_ref, inv_ref,
                bdw_ref, gln_ref, bln_ref, gco_ref):
    d_conv = u.shape[-1]
    seg = u.shape[0] // n_seg
    n, hop = fwd_ref.shape[0], inv_ref.shape[0]
    half = n // 2
    assert n - hop == HALO and HALO >= CONV_WIDTH - 1 and seg % hop == 0
    for s in range(n_seg):
        if halo_ref is not None:
            hr = halo_ref.shape[0]
            if hr < HALO:
                ubuf[s, 0:HALO - hr, :] = jnp.zeros((HALO - hr, d_conv), F32)
            ubuf[s, HALO - hr:HALO, :] = halo_ref[:, s, :]
        ubuf[s, HALO:HALO + seg, :] = u[s * seg:(s + 1) * seg, :]

    blocks = [(s, i) for s in range(n_seg) for i in range(seg // hop)]
    windows = jnp.concatenate([ubuf[s, hop * i:hop * i + n, :].astype(BF16) for s, i in blocks], axis=1)
    spec = jnp.dot(fwd_ref[...].astype(BF16), windows, preferred_element_type=F32)
    p, q, r = h_scr[0], h_scr[1], h_scr[2]
    prods = []
    for k in range(len(blocks)):
        top = spec[0:half, k * d_conv:(k + 1) * d_conv]
        bot = spec[half:n, k * d_conv:(k + 1) * d_conv]
        prods.append(jnp.concatenate([p * top - q * bot, r * bot + q * top], axis=0).astype(BF16))
    y = jnp.dot(inv_ref[...].astype(BF16), jnp.concatenate(prods, axis=1),
                preferred_element_type=F32)
    for k, (s, i) in enumerate(blocks):
        yk = y[:, k * d_conv:(k + 1) * d_conv] + bdw_ref[...]
        mu = jnp.mean(yk, axis=-1, keepdims=True)
        yc = yk - mu
        yn = yc * lax.rsqrt(jnp.mean(yc * yc, axis=-1, keepdims=True) + EPS) * gln_ref[...] + bln_ref[...]
        c = yn * jax.nn.sigmoid(yn)
        cn_ref[s * seg + i * hop:s * seg + (i + 1) * hop, :] = _rms(c, gco_ref[...]).astype(BF16)


def _inproj_kernel(*refs, d_conv, d_att, tail0, prompt, n_cast, n_once):
    if prompt:
        x_ref, gmix_ref, win_ref, gq_ref, gk_ref = refs[:5]
        halo_ref, conv_w = None, refs[5:13]
        cast_in, rb_ref = refs[13:13 + n_cast], refs[13 + n_cast]
        once_in = (win_ref,) + refs[14 + n_cast:14 + n_cast + n_once]
        outs = refs[14 + n_cast + n_once:]
        cn_ref, ut_ref, q_ref, kt_ref, vb_ref, k32_ref, v32_ref = outs[:7]
        cast_out, bias_ref = outs[7:7 + n_cast], outs[7 + n_cast]
        once_out = outs[8 + n_cast:9 + n_cast + n_once]
        ubuf, h_scr = outs[9 + n_cast + n_once:]
        win_ref = once_out[0]
        t = pl.program_id(1)
        tm = x_ref.shape[1]

        for src, dst in zip(cast_in, cast_out):
            dst[...] = src[...].astype(BF16)

        @pl.when((pl.program_id(0) == 0) & (t == 0))
        def _():
            for src, dst in zip(once_in, once_out):
                for lo in range(0, src.shape[1], MXU_TILE):
                    dst[:, lo:lo + MXU_TILE] = src[:, lo:lo + MXU_TILE].astype(BF16)
            _bias_kernel(rb_ref, bias_ref)
            _filter_response(conv_w[0], conv_w[7], h_scr)

        @pl.when(t == 0)
        def _():
            ubuf[0, 0:HALO, :] = jnp.zeros((HALO, d_conv), F32)

        @pl.when(t > 0)
        def _():
            ubuf[0, 0:HALO, :] = ubuf[0, tm:tm + HALO, :]
    else:
        x_ref, gmix_ref, win_ref, gq_ref, gk_ref, halo_ref = refs[:6]
        conv_w = refs[6:14]
        cn_ref, u_ref, q_ref, k32_ref, v32_ref, ubuf, h_scr = refs[14:]
        tm = x_ref.shape[1]
        _filter_response(conv_w[0], conv_w[7], h_scr)
    h = _rms(x_ref[0], gmix_ref[...]).astype(BF16)

    def proj(lo, width):
        return jnp.dot(h, win_ref[:, lo:lo + width], preferred_element_type=F32)

    u = proj(0, d_conv) * jax.nn.sigmoid(proj(d_conv, d_conv))
    if prompt:
        @pl.when(t == pl.num_programs(1) - 1)
        def _():
            ut_ref[0] = u[tm - HALO:tm, :]
    else:
        u_ref[0] = u
    wdw_ref, bdw_ref, gln_ref, bln_ref, gco_ref, fwd_ref, inv_ref, _ = conv_w
    _conv_stage(u, ubuf.shape[0], halo_ref, ubuf, cn_ref.at[0], h_scr, fwd_ref, inv_ref,
                bdw_ref, gln_ref, bln_ref, gco_ref)

    head_bits = HEAD_DIM.bit_length() - 1
    r_i = lax.shift_right_logical(lax.broadcasted_iota(jnp.int32, (MXU_TILE, MXU_TILE), 0), head_bits)
    c_i = lax.shift_right_logical(lax.broadcasted_iota(jnp.int32, (MXU_TILE, MXU_TILE), 1), head_bits)
    avg = jnp.where(r_i == c_i, 1.0 / HEAD_DIM, 0.0).astype(BF16)

    def head_rms(z, g):
        sq = (z * z).astype(BF16)
        ms = jnp.concatenate([jnp.dot(sq[:, lo:lo + MXU_TILE], avg, preferred_element_type=F32)
                              for lo in range(0, d_att, MXU_TILE)], axis=1)
        return z * lax.rsqrt(ms + EPS) * jnp.tile(g, (1, d_att // HEAD_DIM))

    o = 2 * d_conv
    qn = head_rms(proj(o, d_att), gq_ref[...])
    q_ref[0] = (qn * (HEAD_DIM ** -0.5 * LOG2_E)).astype(BF16)
    kn = head_rms(proj(o + d_att, d_att), gk_ref[...])
    v = proj(o + 2 * d_att, d_att)
    if prompt:
        even_row = (lax.broadcasted_iota(jnp.int32, (d_att, LANES), 0) & HEAD_DIM) == 0
        for s in range(tm // LANES):
            kt = kn[s * LANES:(s + 1) * LANES, :].T
            kt_ref[0, 0, s] = jnp.where(even_row, kt, 0.0).astype(BF16)
            kt_ref[0, 1, s] = jnp.where(even_row, 0.0, kt).astype(BF16)
        even = (lax.broadcasted_iota(jnp.int32, (tm, d_att), 1) & HEAD_DIM) == 0
        vb_ref[0, 0] = jnp.where(even, v, 0.0).astype(BF16)
        vb_ref[0, 1] = jnp.where(even, 0.0, v).astype(BF16)

    @pl.when(pl.program_id(1) >= tail0)
    def _():
        k32_ref[0] = kn
        v32_ref[0] = v


def _in_proj(x, g_mix, w_in, g_q, g_k, conv_w, *, tm, keep, state=None, cast=(), cast_once=(), rel_bias=None):
    nb, t_len, d_model = x.shape
    d_in = w_in.shape[-1]
    d_conv = conv_w[0].shape[-1]
    d_att = (d_in - 2 * d_conv) // 3
    assert g_q.shape == g_k.shape == (1, HEAD_DIM) and d_att % HEAD_DIM == 0
    n_t = t_len // tm
    prompt = state is None
    assert t_len % tm == 0 and (t_len - keep) % tm == 0 and tm % LANES == 0 and tm >= HALO
    tail0 = (t_len - keep) // tm
    row = lambda b, t: (b, t, 0)
    tail = lambda b, t: (b, jnp.maximum(t - tail0, 0), 0)
    halves = pl.BlockSpec((1, 2, tm, d_att), lambda b, t: (b, 0, t, 0))
    out_shape = [jax.ShapeDtypeStruct((nb, t_len, d_conv), BF16)]
    out_specs = [pl.BlockSpec((1, tm, d_conv), row)]
    if prompt:
        n_seg = 1
        out_shape += [jax.ShapeDtypeStruct((nb, HALO, d_conv), F32)]
        out_specs += [pl.BlockSpec((1, HALO, d_conv), lambda b, t: (b, 0, 0))]
    else:
        n_seg = state.shape[1]
        assert nb == 1 and n_t == 1 and tm % n_seg == 0 and state.shape[0] == CONV_WIDTH - 1
        out_shape += [jax.ShapeDtypeStruct((nb, t_len, d_conv), F32)]
        out_specs += [pl.BlockSpec((1, tm, d_conv), row)]
    out_shape += [jax.ShapeDtypeStruct((nb, t_len, d_att), BF16)]
    out_specs += [pl.BlockSpec((1, tm, d_att), row)]
    if prompt:
        out_shape += [jax.ShapeDtypeStruct((nb, 2, t_len // LANES, d_att, LANES), BF16),
                      jax.ShapeDtypeStruct((nb, 2, t_len, d_att), BF16)]
        out_specs += [pl.BlockSpec((1, 2, tm // LANES, d_att, LANES), lambda b, t: (b, 0, t, 0, 0)), halves]
    out_shape += [jax.ShapeDtypeStruct((nb, keep, d_att), F32)] * 2
    out_specs += [pl.BlockSpec((1, tm, d_att), tail)] * 2
    cast_specs = []
    for w in cast:
        rb = w.shape[0] // (nb * n_t)
        assert prompt and w.ndim == 2 and rb * nb * n_t == w.shape[0] and rb % (2 * SUBLANES) == 0
        cast_specs.append(pl.BlockSpec((rb, w.shape[1]), lambda b, t: (b * n_t + t, 0)))
        out_shape.append(jax.ShapeDtypeStruct(w.shape, BF16))
    out_specs += cast_specs
    side_specs, side_in = list(cast_specs), list(cast)
    if prompt:
        bias_shape = (rel_bias.shape[0], WIN_SLABS, PAIR, LANES)
        side_specs.append(_const_spec(rel_bias.shape))
        side_in.append(rel_bias)
        out_shape.append(jax.ShapeDtypeStruct(bias_shape, F32))
        out_specs.append(pl.BlockSpec(bias_shape, lambda b, t: (0, 0, 0, 0)))
        for w in (w_in,) + tuple(cast_once):
            assert w.ndim == 2 and w.shape[1] % MXU_TILE == 0
            out_shape.append(jax.ShapeDtypeStruct(w.shape, BF16))
            out_specs.append(pl.BlockSpec(w.shape, lambda b, t: (0, 0)))
        side_specs += [_const_spec(w.shape) for w in cast_once]
        side_in += list(cast_once)
    else:
        assert not cast_once
    kern = functools.partial(_inproj_kernel, d_conv=d_conv, d_att=d_att, tail0=tail0, prompt=prompt,
                             n_cast=len(cast), n_once=len(cast_once))
    extra = [] if prompt else [state]
    hop = min(CONV_HOP, tm // n_seg)
    conv_w = tuple(conv_w) + _dft_tables(HALO + hop, hop)
    return pl.pallas_call(
        kern,
        grid=(nb, n_t),
        in_specs=[pl.BlockSpec((1, tm, d_model), row),
                  _const_spec((1, d_model)), _const_spec((d_model, d_in)),
                  _const_spec((1, HEAD_DIM)), _const_spec((1, HEAD_DIM))]
                 + [_const_spec(a.shape) for a in extra + list(conv_w)] + side_specs,
        out_specs=out_specs,
        out_shape=out_shape,
        scratch_shapes=[pltpu.VMEM((n_seg, HALO + tm // n_seg, d_conv), F32),
                        pltpu.VMEM((3, (HALO + hop) // 2, d_conv), F32)],
        compiler_params=pltpu.CompilerParams(
            dimension_semantics=("arbitrary", "arbitrary"), vmem_limit_bytes=VMEM_LIMIT),
        name="in_proj",
    )(x, g_mix, w_in, g_q, g_k, *extra, *conv_w, *side_in)


def _attn_scratch(n_heads, slots):
    return [pltpu.VMEM((slots, n_heads // 2, PAIR, 2 * WIN), F32),
            pltpu.VMEM((slots, n_heads // 2, PAIR, 2 * WIN), BF16)]


def _attend(rows, n_slabs, get_q, get_k, get_v, bias_ref, off, s_scr, p_scr):
    n_pairs = s_scr.shape[0]
    width = n_slabs * LANES
    for hp in range(n_pairs):
        s_scr[hp, 0:rows, 0:2 * width] = jnp.dot(get_q(hp), get_k(hp), preferred_element_type=F32)
    for hp in range(n_pairs):
        for sub in range(2):
            h = 2 * hp + sub
            for r0 in range(0, rows, SOFTMAX_ROWS):
                r = slice(r0, r0 + SOFTMAX_ROWS)
                s = s_scr[hp, r, sub * width:(sub + 1) * width]
                s = s + jnp.concatenate([bias_ref[h, off + c, r, :] for c in range(n_slabs)], axis=1)
                e = jnp.exp2(s - jnp.max(s, axis=-1, keepdims=True))
                p_scr[hp, r, sub * width:(sub + 1) * width] = e.astype(BF16)
    first = lax.broadcasted_iota(jnp.int32, (width, LANES), 1) < HEAD_DIM
    sums = jnp.concatenate([jnp.where(first, 1.0, 0.0), jnp.where(first, 0.0, 1.0)], axis=0).astype(BF16)
    outs = []
    for hp in range(n_pairs):
        o2 = jnp.dot(p_scr[hp, 0:rows, 0:2 * width], jnp.concatenate([get_v(hp), sums], axis=1),
                     preferred_element_type=F32)
        outs.append(o2[:, 0:LANES] * (1.0 / o2[:, LANES:2 * LANES]))
    return jnp.concatenate(outs, axis=1)


def _attn_prompt_kernel(q_ref, kt_ref, v_ref, bias_ref, g_ref, o_ref, s_scr, p_scr):
    pairs = q_ref.shape[1] // PAIR
    slots = s_scr.shape[0]
    t = pl.program_id(1)

    def run_pairs(first_step):
        for i in range(pairs):
            p = i if first_step else t * pairs + i
            n_slabs = min(i + 1, WIN_SLABS) if first_step else WIN_SLABS
            ws = max(p - (WIN_SLABS - 1), 0) if first_step else p - (WIN_SLABS - 1)
            off = ws + (WIN_SLABS - 1) - p
            k0 = ws * LANES if first_step else pl.multiple_of(ws * LANES, LANES)
            width = n_slabs * LANES

            def get_k(hp, ws=ws, n_slabs=n_slabs):
                return jnp.concatenate(
                    [kt_ref[0, e, ws + c, hp * LANES:(hp + 1) * LANES, :]
                     for e in range(2) for c in range(n_slabs)], axis=1)

            def get_v(hp, k0=k0, width=width):
                lanes = slice(hp * LANES, (hp + 1) * LANES)
                return jnp.concatenate([v_ref[0, 0, pl.ds(k0, width), lanes],
                                        v_ref[0, 1, pl.ds(k0, width), lanes]], axis=0)

            rows = slice(i * PAIR, (i + 1) * PAIR)

            def get_q(hp, rows=rows):
                return q_ref[0, rows, hp * LANES:(hp + 1) * LANES]

            k = i % slots
            a = _attend(PAIR, n_slabs, get_q, get_k, get_v, bias_ref, off, s_scr.at[k], p_scr.at[k])
            o_ref[0, rows, :] = _rms(a, g_ref[...]).astype(BF16)

    pl.when(t == 0)(functools.partial(run_pairs, True))
    pl.when(t > 0)(functools.partial(run_pairs, False))


def _attn_prompt(q, kt, v, bias, g_att, *, tq):
    nb, t_len, d_att = q.shape
    n_heads = d_att // HEAD_DIM
    assert t_len % tq == 0 and tq % PAIR == 0 and tq // PAIR >= WIN_SLABS - 1
    return pl.pallas_call(
        _attn_prompt_kernel,
        grid=(nb, t_len // tq),
        in_specs=[pl.BlockSpec((1, tq, d_att), lambda b, t: (b, t, 0)),
                  pl.BlockSpec((1, 2, t_len // LANES, d_att, LANES), lambda b, t: (b, 0, 0, 0, 0)),
                  pl.BlockSpec((1, 2, t_len, d_att), lambda b, t: (b, 0, 0, 0)),
                  _const_spec(bias.shape), _const_spec((1, d_att))],
        out_specs=pl.BlockSpec((1, tq, d_att), lambda b, t: (b, t, 0)),
        out_shape=jax.ShapeDtypeStruct((nb, t_len, d_att), BF16),
        scratch_shapes=_attn_scratch(n_heads, slots=2),
        compiler_params=pltpu.CompilerParams(
            dimension_semantics=("arbitrary", "arbitrary"), vmem_limit_bytes=VMEM_LIMIT),
        name="attn_prompt",
    )(q, kt, v, bias, g_att)


def _attn_sample_kernel(q_ref, ck_ref, cv_ref, kn_ref, vn_ref, bias_ref, g_ref, o_ref, kt_s, v_s,
                        s_scr, p_scr):
    d_att = q_ref.shape[-1]
    pad = jnp.zeros((WIN - ck_ref.shape[1] - kn_ref.shape[1], d_att), F32)
    k_all = jnp.concatenate([ck_ref[0].astype(F32), kn_ref[0], pad], axis=0)
    even_row = (lax.broadcasted_iota(jnp.int32, (d_att, LANES), 0) & HEAD_DIM) == 0
    for c in range(WIN_SLABS):
        kt = k_all[c * LANES:(c + 1) * LANES, :].T
        kt_s[0, c] = jnp.where(even_row, kt, 0.0).astype(BF16)
        kt_s[1, c] = jnp.where(even_row, 0.0, kt).astype(BF16)
    v_all = jnp.concatenate([cv_ref[0].astype(F32), vn_ref[0], pad], axis=0)
    even = (lax.broadcasted_iota(jnp.int32, v_all.shape, 1) & HEAD_DIM) == 0
    v_s[0] = jnp.where(even, v_all, 0.0).astype(BF16)
    v_s[1] = jnp.where(even, 0.0, v_all).astype(BF16)

    def get_k(hp):
        return jnp.concatenate(
            [kt_s[e, c, hp * LANES:(hp + 1) * LANES, :] for e in range(2) for c in range(WIN_SLABS)], axis=1)

    def get_v(hp):
        lanes = slice(hp * LANES, (hp + 1) * LANES)
        return jnp.concatenate([v_s[0, :, lanes], v_s[1, :, lanes]], axis=0)

    def get_q(hp):
        return q_ref[0, :, hp * LANES:(hp + 1) * LANES]

    a = _attend(q_ref.shape[1], WIN_SLABS, get_q, get_k, get_v, bias_ref, 0, s_scr.at[0], p_scr.at[0])
    o_ref[0] = _rms(a, g_ref[...]).astype(BF16)


def _attn_sample(q, cache_k, cache_v, k_new, v_new, bias, g_att):
    nb, s_len, d_att = k_new.shape
    n_heads = d_att // HEAD_DIM
    l_cache = cache_k.shape[1]
    assert s_len == CHUNK and l_cache == LEFT_CHUNKS * CHUNK
    per_b = lambda b: (b, 0, 0)
    return pl.pallas_call(
        _attn_sample_kernel,
        grid=(nb,),
        in_specs=[pl.BlockSpec((1, s_len, d_att), per_b),
                  pl.BlockSpec((1, l_cache, d_att), per_b), pl.BlockSpec((1, l_cache, d_att), per_b),
                  pl.BlockSpec((1, s_len, d_att), per_b), pl.BlockSpec((1, s_len, d_att), per_b),
                  _const_spec(bias.shape), _const_spec((1, d_att))],
        out_specs=pl.BlockSpec((1, s_len, d_att), per_b),
        out_shape=jax.ShapeDtypeStruct((nb, s_len, d_att), BF16),
        scratch_shapes=[pltpu.VMEM((2, WIN_SLABS, d_att, LANES), BF16), pltpu.VMEM((2, WIN, d_att), BF16)]
                       + _attn_scratch(n_heads, slots=1),
        compiler_params=pltpu.CompilerParams(
            dimension_semantics=("arbitrary",), vmem_limit_bytes=VMEM_LIMIT),
        name="attn_sample",
    )(q, cache_k, cache_v, k_new, v_new, bias, g_att)


def _dense_stage(x, cn, an, p, wout_ref, gffn_ref, w1_ref, w2_ref, gple_ref, wg_ref, wple_ref, ff_chunk):
    d_conv = cn.shape[-1]
    x1 = (x + jnp.dot(cn, wout_ref[0:d_conv, :], preferred_element_type=F32)
          + jnp.dot(an, wout_ref[d_conv:, :], preferred_element_type=F32))
    hb = _rms(x1, gffn_ref[...]).astype(BF16)
    ffn = None
    for lo in range(0, w1_ref.shape[1], ff_chunk):
        a = jnp.maximum(jnp.dot(hb, w1_ref[:, lo:lo + ff_chunk], preferred_element_type=F32), 0.0)
        part = jnp.dot((a * a).astype(BF16), w2_ref[lo:lo + ff_chunk, :], preferred_element_type=F32)
        ffn = part if ffn is None else ffn + part
    x2 = x1 + ffn
    gate = jax.nn.sigmoid(jnp.dot(_rms(x2, gple_ref[...]).astype(BF16), wg_ref[...],
                                  preferred_element_type=F32))
    ple = jnp.dot(p.astype(BF16), wple_ref[...], preferred_element_type=F32)
    return x2 + ple * gate


def _mix_kernel(*refs, starts, ff_chunk):
    n_groups = len(starts) - 1
    ins, dense_w, outs = refs[:4 * n_groups], refs[4 * n_groups:-n_groups], refs[-n_groups:]
    i = pl.program_id(0)
    for g in range(n_groups):
        x_ref, p_ref, an_ref, cn_ref = ins[4 * g:4 * g + 4]

        @pl.when((i >= starts[g]) & (i < starts[g + 1]))
        def _():
            outs[g][0] = _dense_stage(x_ref[0], cn_ref[0], an_ref[0], p_ref[0], *dense_w, ff_chunk)


def _mix(groups, w_out, g_ffn, w1, w2, g_ple, w_gate, w_ple, *, tm):
    consts = [w_out, g_ffn, w1, w2, g_ple, w_gate, w_ple]
    starts, in_specs, out_specs, out_shape, operands = [0], [], [], [], []
    for x, p, an, cn in groups:
        ng, rows, d_model = x.shape
        assert rows % tm == 0
        n_t = rows // tm
        first, count = starts[-1], ng * n_t
        starts.append(first + count)

        def tile(i, first=first, count=count, n_t=n_t):
            j = jnp.clip(i - first, 0, count - 1)
            return (j // n_t, j % n_t, 0)

        in_specs += [pl.BlockSpec((1, tm, a.shape[-1]), tile) for a in (x, p, an, cn)]
        out_specs.append(pl.BlockSpec((1, tm, d_model), tile))
        out_shape.append(jax.ShapeDtypeStruct(x.shape, F32))
        operands += [x, p, an, cn]
    return pl.pallas_call(
        functools.partial(_mix_kernel, starts=tuple(starts), ff_chunk=2048),
        grid=(starts[-1],),
        in_specs=in_specs + [_const_spec(c.shape) for c in consts],
        out_specs=out_specs,
        out_shape=out_shape,
        compiler_params=pltpu.CompilerParams(
            dimension_semantics=("arbitrary",), vmem_limit_bytes=VMEM_LIMIT),
        name="mix",
    )(*operands, *consts)


def kernel(x_prompt, x_sample, p_prompt, p_sample, cache_att_k, cache_att_v, state_conv, g_mix, w_in, w_dw,
           b_dw, g_conv_ln, b_conv_ln, g_q, g_k, rel_bias, g_conv_out, g_att_out, w_out, g_ffn, w_ff1, w_ff2,
           g_ple, w_gate, w_ple):
    depth = w_in.shape[0]
    nb, t_len, d_model = x_prompt.shape
    sb, s_len, _ = x_sample.shape
    d_conv = w_dw.shape[-1]
    d_att = g_att_out.shape[-1]
    n_heads = d_att // HEAD_DIM
    keep = min(LEFT_CHUNKS * CHUNK, t_len)
    tile = 512

    xp = x_prompt
    xs = x_sample.reshape(1, sb * s_len, d_model)
    outs = [[] for _ in range(6)]
    for i in range(depth):
        vec = lambda a: a[i].reshape(1, -1)
        gq, gk = vec(g_q), vec(g_k)
        conv_w = (w_dw[i], vec(b_dw), vec(g_conv_ln), vec(b_conv_ln), vec(g_conv_out))

        (cn_p, u_tail, q_p, kt_p, vb_p, k_tail, v_tail, w_out_b, w_ff1_b, w_ff2_b, w_gate_b, bias,
         w_in_b, w_ple_b) = _in_proj(
            xp, vec(g_mix), w_in[i], gq, gk, conv_w, tm=tile, keep=keep,
            cast=(w_out[i], w_ff1[i], w_ff2[i], w_gate[i]), cast_once=(w_ple[i],), rel_bias=rel_bias[i])
        mix_w = (w_out_b, vec(g_ffn), w_ff1_b, w_ff2_b, vec(g_ple), w_gate_b, w_ple_b)
        an_p = _attn_prompt(q_p, kt_p, vb_p, bias, vec(g_att_out), tq=tile)
        outs[0].append(k_tail.reshape(nb, keep, n_heads, HEAD_DIM))
        outs[1].append(v_tail.reshape(nb, keep, n_heads, HEAD_DIM))
        outs[2].append(u_tail[:, HALO - (CONV_WIDTH - 1):])

        rows = sb * s_len
        cn_s, u_s, q_s, k_s, v_s = _in_proj(xs, vec(g_mix), w_in_b, gq, gk, conv_w, tm=rows, keep=rows,
                                            state=state_conv[i].transpose(1, 0, 2))
        per_b = lambda a: a.reshape(sb, s_len, a.shape[-1])
        ck = cache_att_k[i].reshape(sb, -1, d_att).astype(BF16)
        cv = cache_att_v[i].reshape(sb, -1, d_att).astype(BF16)
        an_s = _attn_sample(per_b(q_s), ck, cv, per_b(k_s), per_b(v_s), bias, vec(g_att_out))
        outs[3].append(k_s.reshape(sb, s_len, n_heads, HEAD_DIM))
        outs[4].append(v_s.reshape(sb, s_len, n_heads, HEAD_DIM))
        conv_in_tail = jnp.concatenate([state_conv[i], per_b(u_s)], axis=1)[:, -(CONV_WIDTH - 1):]
        outs[5].append(conv_in_tail)

        xp, xs = _mix([(xp, p_prompt[i], an_p, cn_p),
                       (xs, p_sample[i].reshape(1, rows, -1), an_s.reshape(1, rows, d_att), cn_s)],
                      *mix_w, tm=tile)

    return (xp, xs.reshape(sb, s_len, d_model)) + tuple(jnp.stack(o) for o in outs)
```
